```python
import numpy as np
import jax
import jax.numpy as jnp
from jax import lax

D_MODEL = 2048
BATCH = 2
SEQ = 4096
DEPTH = 2

HEAD_DIM = 128
NSA_HEADS = 8
NSA_KV_GROUPS = 2
NSA_HPG = NSA_HEADS // NSA_KV_GROUPS
NSA_ROT = HEAD_DIM // 4
CMP_LEN = 32
CMP_STRIDE = 16
SLC_LEN = 64
SLC_TOPK = 16
N_LOCAL_SLC = 2
WINDOW = 512
WIN_BLK = 128
SLC_Q_CHUNK = 64
FORCE_SCORE = 1.0e4

MLA_HEADS = 8
Q_LORA = 768
KV_LORA = 512
QK_NOPE = 128
QK_ROPE = 64
V_HEAD = 128
Q_BLK = 128

ROPE_THETA = 500000.0
EPS = 1e-6

D_FF = 7168
N_EXPERTS = 8
TOP_K = 2
MOE_BLK = 256

IN_SPLITS = (NSA_HEADS * HEAD_DIM,
             3 * 2 * NSA_KV_GROUPS * HEAD_DIM,
             3 * NSA_HEADS,
             Q_LORA,
             KV_LORA,
             QK_ROPE,
             2 * D_MODEL)
D_IN = sum(IN_SPLITS)
IN_SPLIT_POINTS = tuple(int(v) for v in np.cumsum(IN_SPLITS)[:-1])

kernel_name = 'nsa_mla_gated_hybrid_moe'


def rms_norm(x, g):
    xf = x.astype(jnp.float32)
    y = xf * lax.rsqrt(jnp.mean(xf * xf, axis=-1, keepdims=True) + EPS)
    return (y * g.astype(jnp.float32)).astype(x.dtype)


def masked_softmax(s, mask):
    s = jnp.where(mask, s.astype(jnp.float32), -jnp.inf)
    m = jnp.max(s, axis=-1, keepdims=True)
    m = jnp.where(jnp.isfinite(m), m, 0.0)
    p = jnp.exp(s - m)
    den = jnp.sum(p, axis=-1, keepdims=True)
    return p / jnp.where(den > 0.0, den, 1.0)


def rope_tables(pos, rot_dim):
    inv = 1.0 / (ROPE_THETA ** (jnp.arange(0, rot_dim, 2, dtype=jnp.float32) / rot_dim))
    ang = jnp.asarray(pos).astype(jnp.float32)[:, None] * inv[None, :]
    return jnp.cos(ang), jnp.sin(ang)


def rope(x, cos, sin):
    r = cos.shape[-1]
    c = cos[:, None, :].astype(x.dtype)
    s = sin[:, None, :].astype(x.dtype)
    x1, x2 = x[..., :r], x[..., r:]
    return jnp.concatenate([x1 * c - x2 * s, x2 * c + x1 * s], axis=-1)


def partial_rope(x, cos, sin):
    rd = 2 * cos.shape[-1]
    return jnp.concatenate([rope(x[..., :rd], cos, sin), x[..., rd:]], axis=-1)


def compress_blocks(x_tok, pe, w):
    S_ = x_tok.shape[1]
    n_cmp = (S_ - CMP_LEN) // CMP_STRIDE + 1
    idx = np.arange(n_cmp)[:, None] * CMP_STRIDE + np.arange(CMP_LEN)[None, :]
    blocks = x_tok[:, idx] + pe[None, None, :, None, :].astype(x_tok.dtype)
    return jnp.einsum('bnlgd,lde->bnge', blocks, w)


def nsa_mixer(zq, zkv, zg, q_norm_g, k_norm_g, pe_k, pe_v, w_ck, w_cv, rope_tok, rope_cmp):
    B_, S_ = zq.shape[0], zq.shape[1]
    G, Hg, dk = NSA_KV_GROUPS, NSA_HPG, HEAD_DIM
    scale = dk ** -0.5
    t = jnp.arange(S_)
    q = partial_rope(rms_norm(zq.reshape(B_, S_, NSA_HEADS, dk), q_norm_g), *rope_tok)
    qg = q.reshape(B_, S_, G, Hg, dk)
    kv = zkv.reshape(B_, S_, 3, 2, G, dk)

    n_cmp = (S_ - CMP_LEN) // CMP_STRIDE + 1
    cmp_start = np.arange(n_cmp) * CMP_STRIDE
    cmp_end = cmp_start + CMP_LEN - 1
    k_c = partial_rope(rms_norm(compress_blocks(kv[:, :, 0, 0], pe_k, w_ck), k_norm_g[0]), *rope_cmp)
    v_c = compress_blocks(kv[:, :, 0, 1], pe_v, w_cv)
    s_c = jnp.einsum('bsghd,bngd->bghsn', qg, k_c, preferred_element_type=jnp.float32) * scale
    p_c = masked_softmax(s_c, t[:, None] >= jnp.asarray(cmp_end)[None, :])
    o_c = jnp.einsum('bghsn,bngd->bsghd', p_c.astype(v_c.dtype), v_c)

    n_slc = S_ // SLC_LEN
    n_sel = min(SLC_TOPK, n_slc)
    slc_start = np.arange(n_slc) * SLC_LEN
    overlap = ((cmp_start[:, None] <= slc_start[None, :] + SLC_LEN - 1)
               & (slc_start[None, :] <= cmp_end[:, None])).astype(np.float32)
    imp = jnp.einsum('bghsn,nj->bgsj', p_c, jnp.asarray(overlap))
    blk = jnp.arange(n_slc)[None, :]
    cur = (t // SLC_LEN)[:, None]
    forced = (blk == 0) | ((blk <= cur) & (blk > cur - N_LOCAL_SLC))
    imp = jnp.where(blk > cur, -jnp.inf, jnp.where(forced, FORCE_SCORE, imp))
    _, sel = lax.top_k(imp, n_sel)

    k_s = partial_rope(rms_norm(kv[:, :, 1, 0], k_norm_g[1]), *rope_tok).transpose(0, 2, 1, 3)
    v_s = kv[:, :, 1, 1].transpose(0, 2, 1, 3)
    C = SLC_Q_CHUNK
    n_chunk = S_ // C
    q_ch = qg.reshape(B_, n_chunk, C, G, Hg, dk).transpose(1, 0, 3, 2, 4, 5)
    sel_ch = sel.reshape(B_, G, n_chunk, C, n_sel).transpose(2, 0, 1, 3, 4)
    t_ch = t.reshape(n_chunk, C)
    b_ix = jnp.arange(B_)[:, None, None, None]
    g_ix = jnp.arange(G)[None, :, None, None]

    def slc_chunk(args):
        qc, ic, tc = args
        kpos = (ic[..., None] * SLC_LEN + jnp.arange(SLC_LEN)).reshape(B_, G, C, n_sel * SLC_LEN)
        kk = k_s[b_ix, g_ix, kpos]
        vv = v_s[b_ix, g_ix, kpos]
        s = jnp.einsum('bgchd,bgckd->bgchk', qc, kk, preferred_element_type=jnp.float32) * scale
        p = masked_softmax(s, (kpos <= tc[None, None, :, None])[:, :, :, None, :])
        return jnp.einsum('bgchk,bgckd->bgchd', p.astype(vv.dtype), vv)

    o_s = lax.map(slc_chunk, (q_ch, sel_ch, t_ch))
    o_s = o_s.transpose(1, 0, 3, 2, 4, 5).reshape(B_, S_, G, Hg, dk)

    k_w = partial_rope(rms_norm(kv[:, :, 2, 0], k_norm_g[2]), *rope_tok)
    v_w = kv[:, :, 2, 1]
    nb = S_ // WIN_BLK
    halo = WINDOW // WIN_BLK

    def band(a):
        a = jnp.pad(a, ((0, 0), (WINDOW, 0), (0, 0), (0, 0))).reshape(B_, nb + halo, WIN_BLK, G, dk)
        return jnp.concatenate([a[:, i:i + nb] for i in range(halo + 1)], axis=2)

    k_band, v_band = band(k_w), band(v_w)
    q_blk = qg.reshape(B_, nb, WIN_BLK, G, Hg, dk)
    qpos = jnp.arange(nb)[:, None] * WIN_BLK + jnp.arange(WIN_BLK)[None, :]
    kpos = jnp.arange(nb)[:, None] * WIN_BLK - WINDOW + jnp.arange((halo + 1) * WIN_BLK)[None, :]
    dist = qpos[:, :, None] - kpos[:, None, :]
    wmask = (dist >= 0) & (dist < WINDOW) & (kpos[:, None, :] >= 0)
    s_w = jnp.einsum('bnqghd,bnkgd->bnghqk', q_blk, k_band, preferred_element_type=jnp.float32) * scale
    p_w = masked_softmax(s_w, wmask[None, :, None, None])
    o_w = jnp.einsum('bnghqk,bnkgd->bnqghd', p_w.astype(v_band.dtype), v_band).reshape(B_, S_, G, Hg, dk)

    g = jax.nn.sigmoid(zg.astype(jnp.float32)).astype(zq.dtype).reshape(B_, S_, 3, NSA_HEADS, 1)
    o = (g[:, :, 0] * o_c.reshape(B_, S_, NSA_HEADS, dk)
         + g[:, :, 1] * o_s.reshape(B_, S_, NSA_HEADS, dk)
         + g[:, :, 2] * o_w.reshape(B_, S_, NSA_HEADS, dk))
    return o.reshape(B_, S_, NSA_HEADS * dk)


def mla_mixer(zqa, zkva, zkr, qa_norm_g, w_q_b, kva_norm_g, w_kv_b,
              q_norm_g, qr_norm_g, k_norm_g, kr_norm_g, rope_mla):
    B_, S_ = zqa.shape[0], zqa.shape[1]
    H = MLA_HEADS
    scale = (QK_NOPE + QK_ROPE) ** -0.5
    q = (rms_norm(zqa, qa_norm_g) @ w_q_b).reshape(B_, S_, H, QK_NOPE + QK_ROPE)
    q_nope = rms_norm(q[..., :QK_NOPE], q_norm_g)
    q_rope = rope(rms_norm(q[..., QK_NOPE:], qr_norm_g), *rope_mla)
    kv = (rms_norm(zkva, kva_norm_g) @ w_kv_b).reshape(B_, S_, H, QK_NOPE + V_HEAD)
    k_nope = rms_norm(kv[..., :QK_NOPE], k_norm_g)
    v = kv[..., QK_NOPE:]
    k_rope = rope(rms_norm(zkr[:, :, None, :], kr_norm_g), *rope_mla)[:, :, 0]
    nq = S_ // Q_BLK
    kpos = jnp.arange(S_)

    def q_block(args):
        qn, qr, qp = args
        s = (jnp.einsum('bqhd,bkhd->bhqk', qn, k_nope, preferred_element_type=jnp.float32)
             + jnp.einsum('bqhr,bkr->bhqk', qr, k_rope, preferred_element_type=jnp.float32)) * scale
        p = masked_softmax(s, qp[:, None] >= kpos[None, :])
        return jnp.einsum('bhqk,bkhd->bqhd', p.astype(v.dtype), v)

    qn_b = q_nope.reshape(B_, nq, Q_BLK, H, QK_NOPE).transpose(1, 0, 2, 3, 4)
    qr_b = q_rope.reshape(B_, nq, Q_BLK, H, QK_ROPE).transpose(1, 0, 2, 3, 4)
    o = lax.map(q_block, (qn_b, qr_b, kpos.reshape(nq, Q_BLK)))
    return o.transpose(1, 0, 2, 3, 4).reshape(B_, S_, H * V_HEAD)


def swiglu(h, w1, w3, w2):
    return (jax.nn.silu(h @ w1) * (h @ w3)) @ w2


def moe_ffn(h, w_router, w1, w3, w2):
    B_, S_, D = h.shape
    T = B_ * S_
    x = h.reshape(T, D)
    logits = jnp.einsum('td,de->te', x, w_router, preferred_element_type=jnp.float32)
    top_logit, top_idx = lax.top_k(logits, TOP_K)
    gate = jax.nn.softmax(top_logit, axis=-1).astype(x.dtype)
    A = T * TOP_K
    e_flat = top_idx.reshape(A)
    tok_flat = jnp.repeat(jnp.arange(T, dtype=jnp.int32), TOP_K)
    g_flat = gate.reshape(A)
    order = jnp.argsort(e_flat)
    e_sorted = e_flat[order]
    counts = jnp.bincount(e_flat, length=N_EXPERTS)
    padded = (counts + MOE_BLK - 1) // MOE_BLK * MOE_BLK
    start = jnp.cumsum(counts) - counts
    pad_end = jnp.cumsum(padded)
    pad_start = pad_end - padded
    dest = pad_start[e_sorted] + (jnp.arange(A) - start[e_sorted])
    n_blk = -(-A // MOE_BLK) + N_EXPERTS
    n_slot = n_blk * MOE_BLK
    slot_tok = jnp.full((n_slot,), T, jnp.int32).at[dest].set(tok_flat[order])
    slot_gate = jnp.zeros((n_slot,), x.dtype).at[dest].set(g_flat[order])
    blk_exp = jnp.clip(jnp.searchsorted(pad_end, jnp.arange(n_blk) * MOE_BLK, side='right'), 0, N_EXPERTS - 1)
    x_pad = jnp.concatenate([x, jnp.zeros((1, D), x.dtype)], axis=0)
    xb = x_pad[slot_tok].reshape(n_blk, MOE_BLK, D)

    def expert_block(args):
        xe, e = args
        return swiglu(xe, w1[e], w3[e], w2[e])

    yb = lax.map(expert_block, (xb, blk_exp)).reshape(n_slot, D)
    y = jnp.zeros((T + 1, D), x.dtype).at[slot_tok].add(yb * slot_gate[:, None])
    return y[:T].reshape(B_, S_, D)


def setup_inputs(seed: int = 0) -> dict:
    key = jax.random.key(seed)
    ks = iter(jax.random.split(key, 40))
    n_dense = (DEPTH + 1) // 2
    n_moe = DEPTH // 2

    def w(shape, fan_in):
        return jax.random.normal(next(ks), shape, jnp.float32) * (fan_in ** -0.5)

    def gain(shape):
        return 1.0 + 0.02 * jax.random.normal(next(ks), shape, jnp.float32)

    def small(shape, s):
        return s * jax.random.normal(next(ks), shape, jnp.float32)

    return {
        'x': jax.random.normal(next(ks), (BATCH, SEQ, D_MODEL), jnp.float32),
        'attn_norm_g': gain((DEPTH, D_MODEL)),
        'w_in': w((DEPTH, D_MODEL, D_IN), D_MODEL),
        'nsa_q_norm_g': gain((DEPTH, HEAD_DIM)),
        'nsa_k_norm_g': gain((DEPTH, 3, HEAD_DIM)),
        'cmp_pe_k': small((DEPTH, CMP_LEN, HEAD_DIM), 0.1),
        'cmp_pe_v': small((DEPTH, CMP_LEN, HEAD_DIM), 0.1),
        'w_cmp_k': w((DEPTH, CMP_LEN, HEAD_DIM, HEAD_DIM), CMP_LEN * HEAD_DIM),
        'w_cmp_v': w((DEPTH, CMP_LEN, HEAD_DIM, HEAD_DIM), CMP_LEN * HEAD_DIM),
        'mla_qa_norm_g': gain((DEPTH, Q_LORA)),
        'w_q_b': w((DEPTH, Q_LORA, MLA_HEADS * (QK_NOPE + QK_ROPE)), Q_LORA),
        'mla_kva_norm_g': gain((DEPTH, KV_LORA)),
        'w_kv_b': w((DEPTH, KV_LORA, MLA_HEADS * (QK_NOPE + V_HEAD)), KV_LORA),
        'mla_q_norm_g': gain((DEPTH, QK_NOPE)),
        'mla_qr_norm_g': gain((DEPTH, QK_ROPE)),
        'mla_k_norm_g': gain((DEPTH, QK_NOPE)),
        'mla_kr_norm_g': gain((DEPTH, QK_ROPE)),
        'w_proj_nsa': w((DEPTH, NSA_HEADS * HEAD_DIM, D_MODEL), NSA_HEADS * HEAD_DIM),
        'w_proj_mla': w((DEPTH, MLA_HEADS * V_HEAD, D_MODEL), MLA_HEADS * V_HEAD),
        'w_out': w((DEPTH, D_MODEL, D_MODEL), D_MODEL),
        'ffn_norm_g': gain((DEPTH, D_MODEL)),
        'w_ff1': w((n_dense, D_MODEL, D_FF), D_MODEL),
        'w_ff3': w((n_dense, D_MODEL, D_FF), D_MODEL),
        'w_ff2': w((n_dense, D_FF, D_MODEL), D_FF),
        'w_router': w((n_moe, D_MODEL, N_EXPERTS), D_MODEL),
        'w_e1': w((n_moe, N_EXPERTS, D_MODEL, D_FF), D_MODEL),
        'w_e3': w((n_moe, N_EXPERTS, D_MODEL, D_FF), D_MODEL),
        'w_e2': w((n_moe, N_EXPERTS, D_FF, D_MODEL), D_FF),
    }


def reference(x, attn_norm_g, w_in, nsa_q_norm_g, nsa_k_norm_g, cmp_pe_k, cmp_pe_v, w_cmp_k, w_cmp_v,
              mla_qa_norm_g, w_q_b, mla_kva_norm_g, w_kv_b, mla_q_norm_g, mla_qr_norm_g, mla_k_norm_g,
              mla_kr_norm_g, w_proj_nsa, w_proj_mla, w_out, ffn_norm_g, w_ff1, w_ff3, w_ff2,
              w_router, w_e1, w_e3, w_e2):
    S_ = x.shape[1]
    pos = jnp.arange(S_)
    n_cmp = (S_ - CMP_LEN) // CMP_STRIDE + 1
    rope_tok = rope_tables(pos, NSA_ROT)
    rope_cmp = rope_tables(jnp.arange(n_cmp) * CMP_STRIDE + CMP_LEN - 1, NSA_ROT)
    rope_mla = rope_tables(pos, QK_ROPE)
    for l in range(DEPTH):
        h = rms_norm(x, attn_norm_g[l])
        z = h @ w_in[l]
        zq, zkv, zg, zqa, zkva, zkr, zm = jnp.split(z, IN_SPLIT_POINTS, axis=-1)
        o_a = nsa_mixer(zq, zkv, zg, nsa_q_norm_g[l], nsa_k_norm_g[l], cmp_pe_k[l], cmp_pe_v[l],
                        w_cmp_k[l], w_cmp_v[l], rope_tok, rope_cmp)
        o_b = mla_mixer(zqa, zkva, zkr, mla_qa_norm_g[l], w_q_b[l], mla_kva_norm_g[l], w_kv_b[l],
                        mla_q_norm_g[l], mla_qr_norm_g[l], mla_k_norm_g[l], mla_kr_norm_g[l], rope_mla)
        gm = jax.nn.sigmoid(zm.astype(jnp.float32)).astype(x.dtype)
        mix = gm[..., :D_MODEL] * (o_a @ w_proj_nsa[l]) + gm[..., D_MODEL:] * (o_b @ w_proj_mla[l])
        x = x + mix @ w_out[l]
        h = rms_norm(x, ffn_norm_g[l])
        if l % 2 == 0:
            x = x + swiglu(h, w_ff1[l // 2], w_ff3[l // 2], w_ff2[l // 2])
        else:
            x = x + moe_ffn(h, w_router[l // 2], w_e1[l // 2], w_e3[l // 2], w_e2[l // 2])
    return x
```

```python
import functools

import numpy as np
import jax
import jax.numpy as jnp
from jax import lax
from jax.experimental import pallas as pl
from jax.experimental.pallas import tpu as pltpu

D_MODEL = 2048
HEAD_DIM = 128
NSA_HEADS = 8
NSA_KV_GROUPS = 2
NSA_HPG = NSA_HEADS // NSA_KV_GROUPS
NSA_ROT = HEAD_DIM // 4
CMP_LEN = 32
CMP_STRIDE = 16
SLC_LEN = 64
SLC_TOPK = 16
N_LOCAL_SLC = 2
WINDOW = 512
FORCE_SCORE = 1.0e4
MLA_HEADS = 8
Q_LORA = 768
KV_LORA = 512
QK_NOPE = 128
QK_ROPE = 64
V_HEAD = 128
ROPE_THETA = 500000.0
EPS = 1e-6
D_FF = 7168
N_EXPERTS = 8
TOP_K = 2

LANES = 128
MLA_QK_PAD = 256
NEG_INF = float("-inf")
BF16 = jnp.bfloat16
F32 = jnp.float32

OFF_QA = 0
OFF_G = 768
OFF_KR = 896
OFF_Q = 1024
OFF_KVA = 2048
OFF_KV = 2560
OFF_M = 4096
D_INP = 8192
NSA_KV_COLS = 3 * 2 * NSA_KV_GROUPS * HEAD_DIM

VMEM_LIMIT = 56 * 1024 * 1024


def _cparams(sem):
    return pltpu.CompilerParams(dimension_semantics=sem, vmem_limit_bytes=VMEM_LIMIT)


def _rms(x, g):
    ms = jnp.mean(x * x, axis=-1, keepdims=True)
    return x * lax.rsqrt(ms + EPS) * g


def _rope_lanes(y, ct, s1, s2, half):
    return y * ct + pltpu.roll(y, LANES - half, 1) * s1 + pltpu.roll(y, half, 1) * s2


def _sigmoid(x):
    return 1.0 / (1.0 + jnp.exp(-x))


def _nmm_kernel(x_ref, g_ref, w_ref, o_ref, h_scr):
    @pl.when(pl.program_id(1) == 0)
    def _():
        h_scr[...] = _rms(x_ref[...], g_ref[...]).astype(BF16)

    o_ref[...] = jnp.dot(h_scr[...], w_ref[...].astype(BF16),
                         preferred_element_type=F32).astype(o_ref.dtype)


def norm_matmul(x, g, w, *, tm, tn, out_dtype=F32):
    T, K = x.shape
    N = w.shape[1]
    return pl.pallas_call(
        _nmm_kernel,
        grid=(T // tm, N // tn),
        in_specs=[pl.BlockSpec((tm, K), lambda m, n: (m, 0)),
                  pl.BlockSpec((1, K), lambda m, n: (0, 0)),
                  pl.BlockSpec((K, tn), lambda m, n: (0, n))],
        out_specs=pl.BlockSpec((tm, tn), lambda m, n: (m, n)),
        out_shape=jax.ShapeDtypeStruct((T, N), out_dtype),
        scratch_shapes=[pltpu.VMEM((tm, K), BF16)],
        compiler_params=_cparams(("parallel", "arbitrary")),
        name="norm_matmul",
    )(x, g.reshape(1, K), w)


def _mmres_kernel(a_ref, w_ref, r_ref, o_ref):
    o_ref[...] = r_ref[...] + jnp.dot(a_ref[...], w_ref[...].astype(BF16), preferred_element_type=F32)


def matmul_residual(a, w, res, *, tm, tn):
    T, K = a.shape
    N = w.shape[1]
    return pl.pallas_call(
        _mmres_kernel,
        grid=(T // tm, N // tn),
        in_specs=[pl.BlockSpec((tm, K), lambda m, n: (m, 0)),
                  pl.BlockSpec((K, tn), lambda m, n: (0, n)),
                  pl.BlockSpec((tm, tn), lambda m, n: (m, n))],
        out_specs=pl.BlockSpec((tm, tn), lambda m, n: (m, n)),
        out_shape=jax.ShapeDtypeStruct((T, N), F32),
        compiler_params=_cparams(("parallel", "arbitrary")),
        name="matmul_residual",
    )(a, w, res)


def _prep_kernel(z_ref, ct_ref, s1_ref, s2_ref, gn_ref, qn_ref, kvp_ref, gate_ref):
    ct, s1, s2 = ct_ref[...], s1_ref[...], s2_ref[...]
    half = NSA_ROT // 2
    gq = gn_ref[0:1, :]
    for h in range(NSA_HEADS):
        c = OFF_Q + h * HEAD_DIM
        y = _rope_lanes(_rms(z_ref[:, c:c + HEAD_DIM], gq), ct, s1, s2, half)
        qn_ref[:, h * HEAD_DIM:(h + 1) * HEAD_DIM] = (y * (HEAD_DIM ** -0.5)).astype(BF16)
    for br in range(3):
        for kv in range(2):
            for g in range(NSA_KV_GROUPS):
                c = ((br * 2 + kv) * NSA_KV_GROUPS + g) * HEAD_DIM
                y = z_ref[:, OFF_KV + c:OFF_KV + c + HEAD_DIM]
                if kv == 0 and br > 0:
                    y = _rope_lanes(_rms(y, gn_ref[1 + br:2 + br, :]), ct, s1, s2, half)
                kvp_ref[:, c:c + HEAD_DIM] = y.astype(BF16)
    gate_ref[...] = _sigmoid(z_ref[:, OFF_G:OFF_G + LANES])


def nsa_prep(z, tabs, gn, S, *, tm):
    T = z.shape[0]
    ns = S // tm
    tab_spec = pl.BlockSpec((tm, LANES), lambda i: (i % ns, 0))
    return pl.pallas_call(
        _prep_kernel,
        grid=(T // tm,),
        in_specs=[pl.BlockSpec((tm, OFF_M), lambda i: (i, 0)), tab_spec, tab_spec, tab_spec,
                  pl.BlockSpec((8, LANES), lambda i: (0, 0))],
        out_specs=[pl.BlockSpec((tm, NSA_HEADS * HEAD_DIM), lambda i: (i, 0)),
                   pl.BlockSpec((tm, NSA_KV_COLS), lambda i: (i, 0)),
                   pl.BlockSpec((tm, LANES), lambda i: (i, 0))],
        out_shape=[jax.ShapeDtypeStruct((T, NSA_HEADS * HEAD_DIM), BF16),
                   jax.ShapeDtypeStruct((T, NSA_KV_COLS), BF16),
                   jax.ShapeDtypeStruct((T, LANES), F32)],
        compiler_params=_cparams(("parallel",)),
        name="nsa_prep",
    )(z, *tabs, gn)


def _cmp_kernel(xk_ref, xv_ref, wk_ref, wv_ref, pek_ref, pev_ref, gk_ref, ct_ref, s1_ref, s2_ref,
                kct_ref, vc_ref):
    nc = xk_ref.shape[0]
    row = lax.broadcasted_iota(jnp.int32, (nc, HEAD_DIM), 0)

    def comp(x_ref, w_ref, pe_ref):
        w = w_ref[...]
        y = jnp.dot(x_ref[...], w, preferred_element_type=F32)
        ype = jnp.dot(pe_ref[...], w, preferred_element_type=F32)
        bias = ype[0:1, :HEAD_DIM] + ype[1:2, HEAD_DIM:]
        out = y[:, :HEAD_DIM] + pltpu.roll(y[:, HEAD_DIM:], nc - 1, 0) + bias
        return jnp.where(row < nc - 1, out, 0.0)

    k = _rms(comp(xk_ref, wk_ref, pek_ref), gk_ref[...])
    k = _rope_lanes(k, ct_ref[...], s1_ref[...], s2_ref[...], NSA_ROT // 2)
    kct_ref[...] = k.T.astype(BF16)
    vc_ref[...] = comp(xv_ref, wv_ref, pev_ref).astype(BF16)


def nsa_compress(x2, wk2, wv2, pek2, pev2, gk, tabs_cmp):
    B, _, nc, kk = x2.shape
    G = NSA_KV_GROUPS
    full = lambda shape: pl.BlockSpec(shape, lambda b, g: (0,) * len(shape))
    return pl.pallas_call(
        _cmp_kernel,
        grid=(B, G),
        in_specs=[pl.BlockSpec((None, None, nc, kk), lambda b, g: (b, g, 0, 0)),
                  pl.BlockSpec((None, None, nc, kk), lambda b, g: (b, G + g, 0, 0)),
                  full((kk, 2 * HEAD_DIM)), full((kk, 2 * HEAD_DIM)),
                  full((8, kk)), full((8, kk)), full((1, HEAD_DIM)),
                  full((nc, LANES)), full((nc, LANES)), full((nc, LANES))],
        out_specs=[pl.BlockSpec((None, None, HEAD_DIM, nc), lambda b, g: (b, g, 0, 0)),
                   pl.BlockSpec((None, None, nc, HEAD_DIM), lambda b, g: (b, g, 0, 0))],
        out_shape=[jax.ShapeDtypeStruct((B, G, HEAD_DIM, nc), BF16),
                   jax.ShapeDtypeStruct((B, G, nc, HEAD_DIM), BF16)],
        compiler_params=_cparams(("parallel", "parallel")),
        name="nsa_compress",
    )(x2, x2, wk2, wv2, pek2, pev2, gk, *tabs_cmp)


def _cattn_kernel(q_ref, kct_ref, vc_ref, ov_ref, oc_ref, sel_ref, *, tq):
    nc = vc_ref.shape[0]
    ns = sel_ref.shape[1]
    t0 = pl.program_id(2) * tq
    t_pos = t0 + lax.broadcasted_iota(jnp.int32, (tq, nc), 0)
    n_idx = lax.broadcasted_iota(jnp.int32, (tq, nc), 1)
    vis = (t_pos >= n_idx * CMP_STRIDE + (CMP_LEN - 1)) & (n_idx < nc - 1)
    bias = jnp.where(vis, 0.0, NEG_INF)
    kct = kct_ref[...]
    vc = vc_ref[...]
    psum = jnp.zeros((tq, nc), F32)
    for hh in range(NSA_HPG):
        s = jnp.dot(q_ref[:, hh * HEAD_DIM:(hh + 1) * HEAD_DIM], kct, preferred_element_type=F32) + bias
        m = jnp.max(s, axis=-1, keepdims=True)
        m = jnp.where(m == NEG_INF, 0.0, m)
        p = jnp.exp(s - m)
        den = jnp.sum(p, axis=-1, keepdims=True)
        p = p * (1.0 / jnp.where(den > 0.0, den, 1.0))
        oc_ref[:, hh * HEAD_DIM:(hh + 1) * HEAD_DIM] = jnp.dot(
            p.astype(BF16), vc, preferred_element_type=F32).astype(oc_ref.dtype)
        psum = psum + p
    hi = psum.astype(BF16)
    lo = (psum - hi.astype(F32)).astype(BF16)
    ov = ov_ref[...]
    imp = jnp.dot(hi, ov, preferred_element_type=F32) + jnp.dot(lo, ov, preferred_element_type=F32)
    blk = lax.broadcasted_iota(jnp.int32, (tq, ns), 1)
    cur = (t0 + lax.broadcasted_iota(jnp.int32, (tq, ns), 0)) // SLC_LEN
    forced = (blk == 0) | ((blk <= cur) & (blk > cur - N_LOCAL_SLC))
    val = jnp.where(blk > cur, NEG_INF, jnp.where(forced, FORCE_SCORE, imp))
    rank = jnp.zeros((tq, ns), F32)
    for i in range(ns):
        col = val[:, i:i + 1]
        ahead = (col > val) | ((col == val) & (blk > i))
        rank = rank + jnp.where(ahead, 1.0, 0.0)
    sel_ref[...] = jnp.where(rank < float(min(SLC_TOPK, ns)), 1.0, 0.0).astype(sel_ref.dtype)


def nsa_cmp_attention(qn, kct, vc, ov, B, S, *, tq):
    G = NSA_KV_GROUPS
    nq = S // tq
    nc = vc.shape[2]
    ns = S // SLC_LEN
    gw = NSA_HPG * HEAD_DIM
    return pl.pallas_call(
        functools.partial(_cattn_kernel, tq=tq),
        grid=(B, G, nq),
        in_specs=[pl.BlockSpec((tq, gw), lambda b, g, i: (b * nq + i, g)),
                  pl.BlockSpec((None, None, HEAD_DIM, nc), lambda b, g, i: (b, g, 0, 0)),
                  pl.BlockSpec((None, None, nc, HEAD_DIM), lambda b, g, i: (b, g, 0, 0)),
                  pl.BlockSpec((nc, ns), lambda b, g, i: (0, 0))],
        out_specs=[pl.BlockSpec((tq, gw), lambda b, g, i: (b * nq + i, g)),
                   pl.BlockSpec((None, None, tq, ns), lambda b, g, i: (b, g, i, 0))],
        out_shape=[jax.ShapeDtypeStruct((B * S, NSA_HEADS * HEAD_DIM), BF16),
                   jax.ShapeDtypeStruct((B, G, S, ns), BF16)],
        compiler_params=_cparams(("parallel", "parallel", "parallel")),
        name="nsa_cmp_attention",
    )(qn, kct, vc, ov)


def _flash_kernel(*refs, mode, hg, tq, tk, dqk, dv, nk):
    if mode == "sel":
        q_ref, k_ref, v_ref, sel_ref, ex_ref, o_ref, qs, m_s, l_s, acc = refs
    else:
        q_ref, k_ref, v_ref, o_ref, qs, m_s, l_s, acc = refs
    qi = pl.program_id(2)
    kj = pl.program_id(3)
    if mode == "win":
        kt = qi * tq // tk - (nk - 1) + kj
        active = kt >= 0
    else:
        kt = kj
        active = kj <= ((qi + 1) * tq - 1) // tk

    @pl.when(kj == 0)
    def _():
        for hh in range(hg):
            qs[hh * tq:(hh + 1) * tq, :] = q_ref[:, hh * dqk:(hh + 1) * dqk]
        m_s[...] = jnp.full(m_s.shape, NEG_INF, F32)
        l_s[...] = jnp.zeros(l_s.shape, F32)
        acc[...] = jnp.zeros(acc.shape, F32)

    @pl.when(active)
    def _():
        q_pos = qi * tq + lax.broadcasted_iota(jnp.int32, (tq, tk), 0)
        k_pos = kt * tk + lax.broadcasted_iota(jnp.int32, (tq, tk), 1)
        ok = k_pos <= q_pos
        if mode == "win":
            ok = ok & (q_pos - k_pos < WINDOW)
        if mode == "sel":
            chosen = jnp.dot(sel_ref[...], ex_ref[...], preferred_element_type=F32)
            ok = ok & (chosen > 0.5)
        bias = jnp.where(ok, 0.0, NEG_INF)
        s = lax.dot_general(qs[...], k_ref[...], (((1,), (1,)), ((), ())), preferred_element_type=F32)
        s = (s.reshape(hg, tq, tk) + bias[None]).reshape(hg * tq, tk)
        m_prev = m_s[...]
        m_new = jnp.maximum(m_prev, jnp.max(s, axis=-1, keepdims=True))
        m_safe = jnp.where(m_new == NEG_INF, 0.0, m_new)
        p = jnp.exp(s - m_safe)
        alpha = jnp.exp(m_prev - m_safe)
        l_s[...] = alpha * l_s[...] + jnp.sum(p, axis=-1, keepdims=True)
        acc[...] = alpha * acc[...] + jnp.dot(p.astype(BF16), v_ref[...], preferred_element_type=F32)
        m_s[...] = m_new

    @pl.when(kj == nk - 1)
    def _():
        l = l_s[...]
        o = acc[...] * (1.0 / jnp.where(l > 0.0, l, 1.0))
        for hh in range(hg):
            o_ref[:, hh * dv:(hh + 1) * dv] = o[hh * tq:(hh + 1) * tq, :].astype(o_ref.dtype)


def flash_attention(q, k, v, B, S, *, mode, n_groups, hg, dqk, dv, kcol0, vcol0, tq, tk,
                    sel=None, expand=None):
    nq = S // tq
    nkt = S // tk
    if mode == "win":
        assert tq == tk
        nk = -(-(WINDOW - 1) // tk) + 1
        kmap = lambda b, g, i, j: (b * nkt + jnp.maximum(i * tq // tk - (nk - 1) + j, 0), kcol0 + g)
        vmap = lambda b, g, i, j: (b * nkt + jnp.maximum(i * tq // tk - (nk - 1) + j, 0), vcol0 + g)
    else:
        nk = nkt
        last = lambda i: ((i + 1) * tq - 1) // tk
        kmap = lambda b, g, i, j: (b * nkt + jnp.minimum(j, last(i)), kcol0 + g)
        vmap = lambda b, g, i, j: (b * nkt + jnp.minimum(j, last(i)), vcol0 + g)
    in_specs = [pl.BlockSpec((tq, hg * dqk), lambda b, g, i, j: (b * nq + i, g)),
                pl.BlockSpec((tk, dqk), kmap),
                pl.BlockSpec((tk, dv), vmap)]
    args = [q, k, v]
    if mode == "sel":
        ns = S // SLC_LEN
        in_specs += [pl.BlockSpec((None, None, tq, ns), lambda b, g, i, j: (b, g, i, 0)),
                     pl.BlockSpec((ns, tk), lambda b, g, i, j: (0, jnp.minimum(j, last(i))))]
        args += [sel, expand]
    return pl.pallas_call(
        functools.partial(_flash_kernel, mode=mode, hg=hg, tq=tq, tk=tk, dqk=dqk, dv=dv, nk=nk),
        grid=(B, n_groups, nq, nk),
        in_specs=in_specs,
        out_specs=pl.BlockSpec((tq, hg * dv), lambda b, g, i, j: (b * nq + i, g)),
        out_shape=jax.ShapeDtypeStruct((B * S, n_groups * hg * dv), BF16),
        scratch_shapes=[pltpu.VMEM((hg * tq, dqk), BF16),
                        pltpu.VMEM((hg * tq, 1), F32),
                        pltpu.VMEM((hg * tq, 1), F32),
                        pltpu.VMEM((hg * tq, dv), F32)],
        compiler_params=_cparams(("parallel", "parallel", "parallel", "arbitrary")),
        name="flash_" + mode,
    )(*args)


def _mla_q_kernel(z_ref, ga_ref, w_ref, gh_ref, ct_ref, s1_ref, s2_ref, o_ref):
    h = _rms(z_ref[...], ga_ref[...]).astype(BF16)
    y = jnp.dot(h, w_ref[...], preferred_element_type=F32)
    ct, s1, s2 = ct_ref[...], s1_ref[...], s2_ref[...]
    scale = (QK_NOPE + QK_ROPE) ** -0.5
    for hd in range(MLA_HEADS):
        c = hd * MLA_QK_PAD
        nope = _rms(y[:, c:c + QK_NOPE], gh_ref[0:1, :])
        r = y[:, c + QK_NOPE:c + MLA_QK_PAD]
        ms = jnp.sum(r * r, axis=-1, keepdims=True) * (1.0 / QK_ROPE)
        r = _rope_lanes(r * lax.rsqrt(ms + EPS) * gh_ref[1:2, :], ct, s1, s2, QK_ROPE // 2)
        o_ref[:, c:c + QK_NOPE] = (nope * scale).astype(BF16)
        o_ref[:, c + QK_NOPE:c + MLA_QK_PAD] = (r * scale).astype(BF16)


def mla_q_proj(z, ga, wq, gh, tabs, S, *, tm):
    T = z.shape[0]
    ns = S // tm
    tab_spec = pl.BlockSpec((tm, LANES), lambda i: (i % ns, 0))
    nout = MLA_HEADS * MLA_QK_PAD
    return pl.pallas_call(
        _mla_q_kernel,
        grid=(T // tm,),
        in_specs=[pl.BlockSpec((tm, Q_LORA), lambda i: (i, OFF_QA // Q_LORA)),
                  pl.BlockSpec((1, Q_LORA), lambda i: (0, 0)),
                  pl.BlockSpec((Q_LORA, nout), lambda i: (0, 0)),
                  pl.BlockSpec((8, LANES), lambda i: (0, 0)),
                  tab_spec, tab_spec, tab_spec],
        out_specs=pl.BlockSpec((tm, nout), lambda i: (i, 0)),
        out_shape=jax.ShapeDtypeStruct((T, nout), BF16),
        compiler_params=_cparams(("parallel",)),
        name="mla_q_proj",
    )(z, ga, wq, gh, *tabs)


def _mla_kv_kernel(z_ref, zr_ref, ga_ref, w_ref, gh_ref, ct_ref, s1_ref, s2_ref, k_ref, v_ref):
    h = _rms(z_ref[...], ga_ref[...]).astype(BF16)
    y = jnp.dot(h, w_ref[...], preferred_element_type=F32)
    r = zr_ref[...]
    ms = jnp.sum(r * r, axis=-1, keepdims=True) * (1.0 / QK_ROPE)
    r = _rope_lanes(r * lax.rsqrt(ms + EPS) * gh_ref[1:2, :], ct_ref[...], s1_ref[...], s2_ref[...],
                    QK_ROPE // 2).astype(BF16)
    nv = MLA_HEADS * QK_NOPE
    for hd in range(MLA_HEADS):
        c = hd * MLA_QK_PAD
        k_ref[:, c:c + QK_NOPE] = _rms(y[:, hd * QK_NOPE:(hd + 1) * QK_NOPE], gh_ref[0:1, :]).astype(BF16)
        k_ref[:, c + QK_NOPE:c + MLA_QK_PAD] = r
    v_ref[...] = y[:, nv:].astype(BF16)


def mla_kv_proj(z, ga, wkv, gh, tabs, S, *, tm):
    T = z.shape[0]
    ns = S // tm
    tab_spec = pl.BlockSpec((tm, LANES), lambda i: (i % ns, 0))
    nk = MLA_HEADS * MLA_QK_PAD
    nv = MLA_HEADS * V_HEAD
    return pl.pallas_call(
        _mla_kv_kernel,
        grid=(T // tm,),
        in_specs=[pl.BlockSpec((tm, KV_LORA), lambda i: (i, OFF_KVA // KV_LORA)),
                  pl.BlockSpec((tm, LANES), lambda i: (i, OFF_KR // LANES)),
                  pl.BlockSpec((1, KV_LORA), lambda i: (0, 0)),
                  pl.BlockSpec((KV_LORA, MLA_HEADS * (QK_NOPE + V_HEAD)), lambda i: (0, 0)),
                  pl.BlockSpec((8, LANES), lambda i: (0, 0)),
                  tab_spec, tab_spec, tab_spec],
        out_specs=[pl.BlockSpec((tm, nk), lambda i: (i, 0)),
                   pl.BlockSpec((tm, nv), lambda i: (i, 0))],
        out_shape=[jax.ShapeDtypeStruct((T, nk), BF16),
                   jax.ShapeDtypeStruct((T, nv), BF16)],
        compiler_params=_cparams(("parallel",)),
        name="mla_kv_proj",
    )(z, z, ga, wkv, gh, *tabs)


def _mix_kernel(oc_ref, os_ref, ow_ref, ob_ref, gate_ref, wa_ref, wb_ref, za_ref, zb_ref, o_ref, a_scr):
    @pl.when(pl.program_id(1) == 0)
    def _():
        for h in range(NSA_HEADS):
            sl = slice(h * HEAD_DIM, (h + 1) * HEAD_DIM)
            a = (gate_ref[:, h:h + 1] * oc_ref[:, sl].astype(F32)
                 + gate_ref[:, NSA_HEADS + h:NSA_HEADS + h + 1] * os_ref[:, sl].astype(F32)
                 + gate_ref[:, 2 * NSA_HEADS + h:2 * NSA_HEADS + h + 1] * ow_ref[:, sl].astype(F32))
            a_scr[:, sl] = a.astype(BF16)

    pa = jnp.dot(a_scr[...], wa_ref[...].astype(BF16), preferred_element_type=F32)
    pb = jnp.dot(ob_ref[...], wb_ref[...].astype(BF16), preferred_element_type=F32)
    o_ref[...] = (_sigmoid(za_ref[...]) * pa + _sigmoid(zb_ref[...]) * pb).astype(o_ref.dtype)


def gated_mix(oc, os_, ow, ob, gates, wa, wb, z, *, tm, tn):
    T, K = oc.shape
    N = wa.shape[1]
    row = lambda w: pl.BlockSpec((tm, w), lambda m, n: (m, 0))
    return pl.pallas_call(
        _mix_kernel,
        grid=(T // tm, N // tn),
        in_specs=[row(K), row(K), row(K), row(K), row(LANES),
                  pl.BlockSpec((K, tn), lambda m, n: (0, n)),
                  pl.BlockSpec((K, tn), lambda m, n: (0, n)),
                  pl.BlockSpec((tm, tn), lambda m, n: (m, OFF_M // tn + n)),
                  pl.BlockSpec((tm, tn), lambda m, n: (m, (OFF_M + D_MODEL) // tn + n))],
        out_specs=pl.BlockSpec((tm, tn), lambda m, n: (m, n)),
        out_shape=jax.ShapeDtypeStruct((T, N), BF16),
        scratch_shapes=[pltpu.VMEM((tm, K), BF16)],
        compiler_params=_cparams(("parallel", "arbitrary")),
        name="gated_mix",
    )(oc, os_, ow, ob, gates, wa, wb, z, z)


def _ffn_up_kernel(x_ref, g_ref, w1_ref, w3_ref, o_ref, h_scr):
    @pl.when(pl.program_id(1) == 0)
    def _():
        h_scr[...] = _rms(x_ref[...], g_ref[...]).astype(BF16)

    h = h_scr[...]
    a = jnp.dot(h, w1_ref[...].astype(BF16), preferred_element_type=F32)
    b = jnp.dot(h, w3_ref[...].astype(BF16), preferred_element_type=F32)
    o_ref[...] = (a * _sigmoid(a) * b).astype(o_ref.dtype)


def ffn_up(x, g, w1, w3, *, tm, tn):
    T, K = x.shape
    N = w1.shape[1]
    return pl.pallas_call(
        _ffn_up_kernel,
        grid=(T // tm, N // tn),
        in_specs=[pl.BlockSpec((tm, K), lambda m, n: (m, 0)),
                  pl.BlockSpec((1, K), lambda m, n: (0, 0)),
                  pl.BlockSpec((K, tn), lambda m, n: (0, n)),
                  pl.BlockSpec((K, tn), lambda m, n: (0, n))],
        out_specs=pl.BlockSpec((tm, tn), lambda m, n: (m, n)),
        out_shape=jax.ShapeDtypeStruct((T, N), BF16),
        scratch_shapes=[pltpu.VMEM((tm, K), BF16)],
        compiler_params=_cparams(("parallel", "arbitrary")),
        name="ffn_up",
    )(x, g.reshape(1, K), w1, w3)


def _router_kernel(x_ref, g_ref, wr_ref, h_ref, idx_ref, gate_ref):
    h = _rms(x_ref[...], g_ref[...])
    h_ref[...] = h.astype(BF16)
    logits = jnp.dot(h, wr_ref[...], preferred_element_type=F32, precision=lax.Precision.HIGHEST)
    lane = lax.broadcasted_iota(jnp.int32, logits.shape, 1).astype(F32)
    logits = jnp.where(lane < float(N_EXPERTS), logits, NEG_INF)
    m1 = jnp.max(logits, axis=-1, keepdims=True)
    i1 = jnp.min(jnp.where(logits == m1, lane, float(LANES)), axis=-1, keepdims=True)
    rest = jnp.where(lane == i1, NEG_INF, logits)
    m2 = jnp.max(rest, axis=-1, keepdims=True)
    i2 = jnp.min(jnp.where(rest == m2, lane, float(LANES)), axis=-1, keepdims=True)
    e = jnp.exp(m2 - m1)
    den = 1.0 + e
    idx_ref[...] = jnp.where(lane == 0.0, i1, jnp.where(lane == 1.0, i2, 0.0)).astype(jnp.int32)
    gate_ref[...] = jnp.where(lane == 0.0, 1.0 / den, jnp.where(lane == 1.0, e / den, 0.0))


def router(x, g, wr_pad, *, tm):
    T, K = x.shape
    return pl.pallas_call(
        _router_kernel,
        grid=(T // tm,),
        in_specs=[pl.BlockSpec((tm, K), lambda i: (i, 0)),
                  pl.BlockSpec((1, K), lambda i: (0, 0)),
                  pl.BlockSpec((K, LANES), lambda i: (0, 0))],
        out_specs=[pl.BlockSpec((tm, K), lambda i: (i, 0)),
                   pl.BlockSpec((tm, LANES), lambda i: (i, 0)),
                   pl.BlockSpec((tm, LANES), lambda i: (i, 0))],
        out_shape=[jax.ShapeDtypeStruct((T, K), BF16),
                   jax.ShapeDtypeStruct((T, LANES), jnp.int32),
                   jax.ShapeDtypeStruct((T, LANES), F32)],
        compiler_params=_cparams(("parallel",)),
        name="router",
    )(x, g.reshape(1, K), wr_pad)


def _moe_up_kernel(be_ref, x_ref, w1_ref, w3_ref, o_ref, w1_s, w3_s):
    r = pl.program_id(1)
    fresh = (r == 0) | (be_ref[r] != be_ref[jnp.maximum(r - 1, 0)])

    @pl.when(fresh)
    def _():
        w1_s[...] = w1_ref[...].astype(BF16)
        w3_s[...] = w3_ref[...].astype(BF16)

    x = x_ref[...]
    a = jnp.dot(x, w1_s[...], preferred_element_type=F32)
    b = jnp.dot(x, w3_s[...], preferred_element_type=F32)
    o_ref[...] = (a * _sigmoid(a) * b).astype(o_ref.dtype)


def moe_up(blk_exp, xb, w1, w3, *, tmb, tn):
    n_slot, K = xb.shape
    N = w1.shape[2]
    wspec = pl.BlockSpec((None, K, tn), lambda n, r, be: (be[r], 0, n))
    return pl.pallas_call(
        _moe_up_kernel,
        grid_spec=pltpu.PrefetchScalarGridSpec(
            num_scalar_prefetch=1,
            grid=(N // tn, n_slot // tmb),
            in_specs=[pl.BlockSpec((tmb, K), lambda n, r, be: (r, 0)), wspec, wspec],
            out_specs=pl.BlockSpec((tmb, tn), lambda n, r, be: (r, n)),
            scratch_shapes=[pltpu.VMEM((K, tn), BF16), pltpu.VMEM((K, tn), BF16)]),
        out_shape=jax.ShapeDtypeStruct((n_slot, N), BF16),
        compiler_params=_cparams(("arbitrary", "arbitrary")),
        name="moe_up",
    )(blk_exp, xb, w1, w3)


def _moe_down_kernel(be_ref, a_ref, w2_ref, g_ref, o_ref, w2_s):
    r = pl.program_id(1)
    fresh = (r == 0) | (be_ref[r] != be_ref[jnp.maximum(r - 1, 0)])

    @pl.when(fresh)
    def _():
        w2_s[...] = w2_ref[...].astype(BF16)

    o_ref[...] = jnp.dot(a_ref[...], w2_s[...], preferred_element_type=F32) * g_ref[...]


def moe_down(blk_exp, act, w2, slot_gate, *, tmb, tn):
    n_slot, K = act.shape
    N = w2.shape[2]
    return pl.pallas_call(
        _moe_down_kernel,
        grid_spec=pltpu.PrefetchScalarGridSpec(
            num_scalar_prefetch=1,
            grid=(N // tn, n_slot // tmb),
            in_specs=[pl.BlockSpec((tmb, K), lambda n, r, be: (r, 0)),
                      pl.BlockSpec((None, K, tn), lambda n, r, be: (be[r], 0, n)),
                      pl.BlockSpec((tmb, 1), lambda n, r, be: (r, 0))],
            out_specs=pl.BlockSpec((tmb, tn), lambda n, r, be: (r, n)),
            scratch_shapes=[pltpu.VMEM((K, tn), BF16)]),
        out_shape=jax.ShapeDtypeStruct((n_slot, N), F32),
        compiler_params=_cparams(("arbitrary", "arbitrary")),
        name="moe_down",
    )(blk_exp, act, w2, slot_gate)


def _rope_tabs(pos, rot_dim, n_rows):
    half = rot_dim // 2
    inv = 1.0 / (ROPE_THETA ** (jnp.arange(0, rot_dim, 2, dtype=F32) / rot_dim))
    ang = jnp.asarray(pos).astype(F32)[:, None] * inv[None, :]
    c, s = jnp.cos(ang), jnp.sin(ang)
    z = jnp.zeros_like(c)
    pad = lambda a, fill: jnp.pad(a, ((0, n_rows - a.shape[0]), (0, LANES - a.shape[1])), constant_values=fill)
    return pad(jnp.concatenate([c, c], 1), 1.0), pad(jnp.concatenate([-s, z], 1), 0.0), \
        pad(jnp.concatenate([z, s], 1), 0.0)


def _mla_tabs(S):
    ct, s1, s2 = _rope_tabs(jnp.arange(S), QK_ROPE, S)
    lane = jnp.arange(LANES)[None, :]
    return jnp.where(lane < QK_ROPE, ct, 0.0), s1, s2


def _pad_cols(w, n):
    return jnp.pad(w, ((0, 0), (0, n - w.shape[1])))


def _layout_w_in(w):
    sp = np.cumsum([0, NSA_HEADS * HEAD_DIM, NSA_KV_COLS, 3 * NSA_HEADS, Q_LORA, KV_LORA, QK_ROPE, 2 * D_MODEL])
    q, kv, g, qa, kva, kr, m = [w[:, sp[i]:sp[i + 1]] for i in range(7)]
    return jnp.concatenate([qa, _pad_cols(g, LANES), _pad_cols(kr, LANES), q, kva, kv, m], axis=1).astype(BF16)


def _layout_w_q_b(w):
    w = w.reshape(Q_LORA, MLA_HEADS, QK_NOPE + QK_ROPE)
    w = jnp.pad(w, ((0, 0), (0, 0), (0, MLA_QK_PAD - QK_NOPE - QK_ROPE)))
    return w.reshape(Q_LORA, MLA_HEADS * MLA_QK_PAD).astype(BF16)


def _layout_w_kv_b(w):
    w = w.reshape(KV_LORA, MLA_HEADS, QK_NOPE + V_HEAD)
    return jnp.concatenate([w[:, :, :QK_NOPE].reshape(KV_LORA, -1), w[:, :, QK_NOPE:].reshape(KV_LORA, -1)],
                           axis=1).astype(BF16)


def _layout_w_cmp(w):
    h = CMP_LEN // 2
    return jnp.concatenate([w[:h].reshape(h * HEAD_DIM, HEAD_DIM), w[h:].reshape(h * HEAD_DIM, HEAD_DIM)],
                           axis=1).astype(BF16)


def _layout_pe(pe):
    return jnp.pad(pe.reshape(2, (CMP_LEN // 2) * HEAD_DIM), ((0, 6), (0, 0))).astype(BF16)


def _overlap(nc, ns):
    n = np.arange(nc)[:, None] * CMP_STRIDE
    j = np.arange(ns)[None, :] * SLC_LEN
    ov = (n <= j + SLC_LEN - 1) & (j <= n + CMP_LEN - 1) & (np.arange(nc)[:, None] < nc - 1)
    return jnp.asarray(ov.astype(np.float32), BF16)


def _expand(ns, S):
    return jnp.asarray((np.arange(ns)[:, None] == (np.arange(S)[None, :] // SLC_LEN)).astype(np.float32), BF16)


def _attention_block(x2d, B, S, p, l, tabs):
    T = x2d.shape[0]
    z = norm_matmul(x2d, p['attn_norm_g'][l], _layout_w_in(p['w_in'][l]), tm=min(1024, T), tn=512)
    gn = jnp.concatenate([p['nsa_q_norm_g'][l][None], p['nsa_k_norm_g'][l], jnp.zeros((4, HEAD_DIM), F32)], 0)
    qn, kvp, gates = nsa_prep(z, tabs['tok'], gn, S, tm=256)

    nc = S // CMP_STRIDE
    x2 = kvp[:, :2 * NSA_KV_GROUPS * HEAD_DIM].reshape(B, S, 2 * NSA_KV_GROUPS, HEAD_DIM)
    x2 = x2.transpose(0, 2, 1, 3).reshape(B, 2 * NSA_KV_GROUPS, nc, CMP_STRIDE * HEAD_DIM)
    kct, vc = nsa_compress(x2, _layout_w_cmp(p['w_cmp_k'][l]), _layout_w_cmp(p['w_cmp_v'][l]),
                           _layout_pe(p['cmp_pe_k'][l]), _layout_pe(p['cmp_pe_v'][l]),
                           p['nsa_k_norm_g'][l][0:1], tabs['cmp'])
    ns = S // SLC_LEN
    oc, sel = nsa_cmp_attention(qn, kct, vc, _overlap(nc, ns), B, S, tq=min(256, S))
    nsa = dict(n_groups=NSA_KV_GROUPS, hg=NSA_HPG, dqk=HEAD_DIM, dv=HEAD_DIM)
    os_ = flash_attention(qn, kvp, kvp, B, S, mode="sel", kcol0=4, vcol0=6, tq=128, tk=min(512, S),
                          sel=sel, expand=_expand(ns, S), **nsa)
    ow = flash_attention(qn, kvp, kvp, B, S, mode="win", kcol0=8, vcol0=10, tq=256, tk=256, **nsa)

    gh_q = jnp.concatenate([p['mla_q_norm_g'][l][None], _pad_cols(p['mla_qr_norm_g'][l][None], LANES),
                            jnp.zeros((6, LANES), F32)], 0)
    gh_k = jnp.concatenate([p['mla_k_norm_g'][l][None], _pad_cols(p['mla_kr_norm_g'][l][None], LANES),
                            jnp.zeros((6, LANES), F32)], 0)
    qm = mla_q_proj(z, p['mla_qa_norm_g'][l][None], _layout_w_q_b(p['w_q_b'][l]), gh_q, tabs['mla'], S, tm=256)
    km, vm = mla_kv_proj(z, p['mla_kva_norm_g'][l][None], _layout_w_kv_b(p['w_kv_b'][l]), gh_k, tabs['mla'],
                         S, tm=256)
    ob = flash_attention(qm, km, vm, B, S, mode="causal", n_groups=MLA_HEADS, hg=1, dqk=MLA_QK_PAD, dv=V_HEAD,
                         kcol0=0, vcol0=0, tq=min(512, S), tk=min(512, S))

    mix = gated_mix(oc, os_, ow, ob, gates, p['w_proj_nsa'][l], p['w_proj_mla'][l], z, tm=512, tn=512)
    return matmul_residual(mix, p['w_out'][l], x2d, tm=512, tn=512)


def _dense_ffn(x2d, g, w1, w3, w2):
    T = x2d.shape[0]
    act = ffn_up(x2d, g, w1.astype(BF16), w3.astype(BF16), tm=min(1024, T), tn=512)
    return matmul_residual(act, w2.astype(BF16), x2d, tm=512, tn=512)


def _moe_ffn(x2d, g, w_router, w1, w3, w2, *, tmb=256):
    T = x2d.shape[0]
    hn, idx, gate = router(x2d, g, _pad_cols(w_router, LANES), tm=256)
    A = T * TOP_K
    e_flat = idx[:, :TOP_K].reshape(A)
    g_flat = gate[:, :TOP_K].reshape(A)
    tok_flat = jnp.repeat(jnp.arange(T, dtype=jnp.int32), TOP_K)
    oh = (e_flat[:, None] == jnp.arange(N_EXPERTS)[None, :]).astype(jnp.int32)
    csum = jnp.cumsum(oh, axis=0)
    rank = jnp.sum(oh * csum, axis=1) - 1
    counts = csum[-1]
    padded = (counts + tmb - 1) // tmb * tmb
    pad_end = jnp.cumsum(padded)
    dest = (pad_end - padded)[e_flat] + rank
    n_blk = -(-A // tmb) + N_EXPERTS
    n_slot = n_blk * tmb
    slot_tok = jnp.full((n_slot,), T, jnp.int32).at[dest].set(tok_flat)
    slot_gate = jnp.zeros((n_slot,), F32).at[dest].set(g_flat)
    blk_exp = jnp.clip(jnp.searchsorted(pad_end, jnp.arange(n_blk) * tmb, side='right'),
                       0, N_EXPERTS - 1).astype(jnp.int32)
    xb = jnp.concatenate([hn, jnp.zeros((1, D_MODEL), BF16)], 0)[slot_tok]
    act = moe_up(blk_exp, xb, w1, w3, tmb=tmb, tn=512)
    yb = moe_down(blk_exp, act, w2, slot_gate[:, None], tmb=tmb, tn=256)
    d2 = dest.reshape(T, TOP_K)
    return x2d + yb[d2[:, 0]] + yb[d2[:, 1]]


def kernel(x, attn_norm_g, w_in, nsa_q_norm_g, nsa_k_norm_g, cmp_pe_k, cmp_pe_v, w_cmp_k, w_cmp_v, mla_qa_norm_g, w_q_b, mla_kva_norm_g, w_kv_b, mla_q_norm_g, mla_qr_norm_g, mla_k_norm_g, mla_kr_norm_g, w_proj_nsa, w_proj_mla, w_out, ffn_norm_g, w_ff1, w_ff3, w_ff2, w_router, w_e1, w_e3, w_e2):
    p = dict(attn_norm_g=attn_norm_g, w_in=w_in, nsa_q_norm_g=nsa_q_norm_g, nsa_k_norm_g=nsa_k_norm_g,
             cmp_pe_k=cmp_pe_k, cmp_pe_v=cmp_pe_v, w_cmp_k=w_cmp_k, w_cmp_v=w_cmp_v,
             mla_qa_norm_g=mla_qa_norm_g, w_q_b=w_q_b, mla_kva_norm_g=mla_kva_norm_g, w_kv_b=w_kv_b,
             mla_q_norm_g=mla_q_norm_g, mla_qr_norm_g=mla_qr_norm_g, mla_k_norm_g=mla_k_norm_g,
             mla_kr_norm_g=mla_kr_norm_g, w_proj_nsa=w_proj_nsa, w_proj_mla=w_proj_mla, w_out=w_out)
    B, S, D = x.shape
    depth = w_in.shape[0]
    nc = S // CMP_STRIDE
    tabs = dict(tok=_rope_tabs(jnp.arange(S), NSA_ROT, S),
                cmp=_rope_tabs(jnp.arange(nc - 1) * CMP_STRIDE + CMP_LEN - 1, NSA_ROT, nc),
                mla=_mla_tabs(S))
    x2d = x.reshape(B * S, D)
    for l in range(depth):
        x2d = _attention_block(x2d, B, S, p, l, tabs)
        if l % 2 == 0:
            x2d = _dense_ffn(x2d, ffn_norm_g[l], w_ff1[l // 2], w_ff3[l // 2], w_ff2[l // 2])
        else:
            x2d = _moe_ffn(x2d, ffn_norm_g[l], w_router[l // 2], w_e1[l // 2], w_e3[l // 2], w_e2[l // 2])
    return x2d.reshape(B, S, D)
```

```python
import functools

import numpy as np
import jax
import jax.numpy as jnp
from jax import lax
from jax.experimental import pallas as pl
from jax.experimental.pallas import tpu as pltpu

D_MODEL = 2048
HEAD_DIM = 128
NSA_HEADS = 8
NSA_KV_GROUPS = 2
NSA_HPG = NSA_HEADS // NSA_KV_GROUPS
NSA_ROT = HEAD_DIM // 4
CMP_LEN = 32
CMP_STRIDE = 16
SLC_LEN = 64
SLC_TOPK = 16
N_LOCAL_SLC = 2
WINDOW = 512
FORCE_SCORE = 1.0e4
MLA_HEADS = 8
Q_LORA = 768
KV_LORA = 512
QK_NOPE = 128
QK_ROPE = 64
V_HEAD = 128
ROPE_THETA = 500000.0
EPS = 1e-6
D_FF = 7168
N_EXPERTS = 8
TOP_K = 2

LANES = 128
MLA_QK_PAD = 256
NEG_INF = float("-inf")
LOG2E = 1.4426950408889634
BF16 = jnp.bfloat16
F32 = jnp.float32

OFF_QA = 0
OFF_G = 768
OFF_KR = 896
OFF_Q = 1024
OFF_KVA = 2048
OFF_KV = 2560
OFF_M = 4096
D_INP = 8192
NSA_KV_COLS = 3 * 2 * NSA_KV_GROUPS * HEAD_DIM

VMEM_LIMIT = 56 * 1024 * 1024


def _cparams(sem):
    return pltpu.CompilerParams(dimension_semantics=sem, vmem_limit_bytes=VMEM_LIMIT)


def _rms(x, g):
    ms = jnp.mean(x * x, axis=-1, keepdims=True)
    return x * lax.rsqrt(ms + EPS) * g


def _rope_lanes(y, ct, s1, s2, half):
    return y * ct + pltpu.roll(y, LANES - half, 1) * s1 + pltpu.roll(y, half, 1) * s2


def _sigmoid(x):
    return 1.0 / (1.0 + jnp.exp(-x))


def _nmm_kernel(x_ref, g_ref, w_ref, o_ref, h_scr):
    @pl.when(pl.program_id(1) == 0)
    def _():
        h_scr[...] = _rms(x_ref[...], g_ref[...]).astype(BF16)

    o_ref[...] = jnp.dot(h_scr[...], w_ref[...].astype(BF16),
                         preferred_element_type=F32).astype(o_ref.dtype)


def norm_matmul(x, g, w, *, tm, tn, out_dtype=F32):
    T, K = x.shape
    N = w.shape[1]
    return pl.pallas_call(
        _nmm_kernel,
        grid=(T // tm, N // tn),
        in_specs=[pl.BlockSpec((tm, K), lambda m, n: (m, 0)),
                  pl.BlockSpec((1, K), lambda m, n: (0, 0)),
                  pl.BlockSpec((K, tn), lambda m, n: (0, n))],
        out_specs=pl.BlockSpec((tm, tn), lambda m, n: (m, n)),
        out_shape=jax.ShapeDtypeStruct((T, N), out_dtype),
        scratch_shapes=[pltpu.VMEM((tm, K), BF16)],
        compiler_params=_cparams(("parallel", "arbitrary")),
        name="norm_matmul",
    )(x, g.reshape(1, K), w)


def _mmres_kernel(a_ref, w_ref, r_ref, o_ref):
    o_ref[...] = r_ref[...] + jnp.dot(a_ref[...], w_ref[...].astype(BF16), preferred_element_type=F32)


def matmul_residual(a, w, res, *, tm, tn):
    T, K = a.shape
    N = w.shape[1]
    return pl.pallas_call(
        _mmres_kernel,
        grid=(T // tm, N // tn),
        in_specs=[pl.BlockSpec((tm, K), lambda m, n: (m, 0)),
                  pl.BlockSpec((K, tn), lambda m, n: (0, n)),
                  pl.BlockSpec((tm, tn), lambda m, n: (m, n))],
        out_specs=pl.BlockSpec((tm, tn), lambda m, n: (m, n)),
        out_shape=jax.ShapeDtypeStruct((T, N), F32),
        compiler_params=_cparams(("parallel", "arbitrary")),
        name="matmul_residual",
    )(a, w, res)


def _prep_kernel(z_ref, ct_ref, s1_ref, s2_ref, gn_ref, qn_ref, kvp_ref, gate_ref):
    ct, s1, s2 = ct_ref[...], s1_ref[...], s2_ref[...]
    half = NSA_ROT // 2
    gq = gn_ref[0:1, :]
    for h in range(NSA_HEADS):
        c = OFF_Q + h * HEAD_DIM
        y = _rope_lanes(_rms(z_ref[:, c:c + HEAD_DIM], gq), ct, s1, s2, half)
        qn_ref[:, h * HEAD_DIM:(h + 1) * HEAD_DIM] = (y * (LOG2E * HEAD_DIM ** -0.5)).astype(BF16)
    for br in range(3):
        for kv in range(2):
            for g in range(NSA_KV_GROUPS):
                c = ((br * 2 + kv) * NSA_KV_GROUPS + g) * HEAD_DIM
                y = z_ref[:, OFF_KV + c:OFF_KV + c + HEAD_DIM]
                if kv == 0 and br > 0:
                    y = _rope_lanes(_rms(y, gn_ref[1 + br:2 + br, :]), ct, s1, s2, half)
                kvp_ref[:, c:c + HEAD_DIM] = y.astype(BF16)
    gate_ref[...] = _sigmoid(z_ref[:, OFF_G:OFF_G + LANES])


def nsa_prep(z, tabs, gn, S, *, tm):
    T = z.shape[0]
    ns = S // tm
    tab_spec = pl.BlockSpec((tm, LANES), lambda i: (i % ns, 0))
    return pl.pallas_call(
        _prep_kernel,
        grid=(T // tm,),
        in_specs=[pl.BlockSpec((tm, OFF_M), lambda i: (i, 0)), tab_spec, tab_spec, tab_spec,
                  pl.BlockSpec((8, LANES), lambda i: (0, 0))],
        out_specs=[pl.BlockSpec((tm, NSA_HEADS * HEAD_DIM), lambda i: (i, 0)),
                   pl.BlockSpec((tm, NSA_KV_COLS), lambda i: (i, 0)),
                   pl.BlockSpec((tm, LANES), lambda i: (i, 0))],
        out_shape=[jax.ShapeDtypeStruct((T, NSA_HEADS * HEAD_DIM), BF16),
                   jax.ShapeDtypeStruct((T, NSA_KV_COLS), BF16),
                   jax.ShapeDtypeStruct((T, LANES), F32)],
        compiler_params=_cparams(("parallel",)),
        name="nsa_prep",
    )(z, *tabs, gn)


def _cmp_kernel(xk_ref, xv_ref, wk_ref, wv_ref, pek_ref, pev_ref, gk_ref, ct_ref, s1_ref, s2_ref,
                kct_ref, vc_ref):
    nc = xk_ref.shape[0]
    row = lax.broadcasted_iota(jnp.int32, (nc, HEAD_DIM), 0)

    def comp(x_ref, w_ref, pe_ref):
        w = w_ref[...]
        y = jnp.dot(x_ref[...], w, preferred_element_type=F32)
        ype = jnp.dot(pe_ref[...], w, preferred_element_type=F32)
        bias = ype[0:1, :HEAD_DIM] + ype[1:2, HEAD_DIM:]
        out = y[:, :HEAD_DIM] + pltpu.roll(y[:, HEAD_DIM:], nc - 1, 0) + bias
        return jnp.where(row < nc - 1, out, 0.0)

    k = _rms(comp(xk_ref, wk_ref, pek_ref), gk_ref[...])
    k = _rope_lanes(k, ct_ref[...], s1_ref[...], s2_ref[...], NSA_ROT // 2)
    kct_ref[...] = k.T.astype(BF16)
    vc_ref[...] = comp(xv_ref, wv_ref, pev_ref).astype(BF16)


def nsa_compress(x2, wk2, wv2, pek2, pev2, gk, tabs_cmp):
    B, _, nc, kk = x2.shape
    G = NSA_KV_GROUPS
    full = lambda shape: pl.BlockSpec(shape, lambda b, g: (0,) * len(shape))
    return pl.pallas_call(
        _cmp_kernel,
        grid=(B, G),
        in_specs=[pl.BlockSpec((None, None, nc, kk), lambda b, g: (b, g, 0, 0)),
                  pl.BlockSpec((None, None, nc, kk), lambda b, g: (b, G + g, 0, 0)),
                  full((kk, 2 * HEAD_DIM)), full((kk, 2 * HEAD_DIM)),
                  full((8, kk)), full((8, kk)), full((1, HEAD_DIM)),
                  full((nc, LANES)), full((nc, LANES)), full((nc, LANES))],
        out_specs=[pl.BlockSpec((None, None, HEAD_DIM, nc), lambda b, g: (b, g, 0, 0)),
                   pl.BlockSpec((None, None, nc, HEAD_DIM), lambda b, g: (b, g, 0, 0))],
        out_shape=[jax.ShapeDtypeStruct((B, G, HEAD_DIM, nc), BF16),
                   jax.ShapeDtypeStruct((B, G, nc, HEAD_DIM), BF16)],
        compiler_params=_cparams(("parallel", "parallel")),
        name="nsa_compress",
    )(x2, x2, wk2, wv2, pek2, pev2, gk, *tabs_cmp)


def _cattn_kernel(q_ref, kct_ref, vc_ref, ov_ref, oc_ref, sel_ref, *, tq):
    nc = vc_ref.shape[0]
    ns = sel_ref.shape[1]
    t0 = pl.program_id(2) * tq
    t_pos = t0 + lax.broadcasted_iota(jnp.int32, (tq, nc), 0)
    n_idx = lax.broadcasted_iota(jnp.int32, (tq, nc), 1)
    vis = (t_pos >= n_idx * CMP_STRIDE + (CMP_LEN - 1)) & (n_idx < nc - 1)
    bias = jnp.where(vis, 0.0, NEG_INF)
    kct = kct_ref[...]
    vc = vc_ref[...]
    psum = jnp.zeros((tq, nc), F32)
    for hh in range(NSA_HPG):
        s = jnp.dot(q_ref[:, hh * HEAD_DIM:(hh + 1) * HEAD_DIM], kct, preferred_element_type=F32) + bias
        m = jnp.max(s, axis=-1, keepdims=True)
        m = jnp.where(m == NEG_INF, 0.0, m)
        p = jnp.exp2(s - m)
        den = jnp.sum(p, axis=-1, keepdims=True)
        p = p * (1.0 / jnp.where(den > 0.0, den, 1.0))
        oc_ref[:, hh * HEAD_DIM:(hh + 1) * HEAD_DIM] = jnp.dot(
            p.astype(BF16), vc, preferred_element_type=F32).astype(oc_ref.dtype)
        psum = psum + p
    hi = psum.astype(BF16)
    lo = (psum - hi.astype(F32)).astype(BF16)
    ov = ov_ref[...]
    imp = jnp.dot(hi, ov, preferred_element_type=F32) + jnp.dot(lo, ov, preferred_element_type=F32)
    blk = lax.broadcasted_iota(jnp.int32, (tq, ns), 1)
    cur = (t0 + lax.broadcasted_iota(jnp.int32, (tq, ns), 0)) // SLC_LEN
    forced = (blk == 0) | ((blk <= cur) & (blk > cur - N_LOCAL_SLC))
    val = jnp.where(blk > cur, NEG_INF, jnp.where(forced, FORCE_SCORE, imp))
    rank = jnp.zeros((tq, ns), F32)
    for i in range(ns):
        col = val[:, i:i + 1]
        ahead = (col > val) | ((col == val) & (blk > i))
        rank = rank + jnp.where(ahead, 1.0, 0.0)
    sel_ref[...] = jnp.where(rank < float(min(SLC_TOPK, ns)), 1.0, 0.0).astype(sel_ref.dtype)


def nsa_cmp_attention(qn, kct, vc, ov, B, S, *, tq):
    G = NSA_KV_GROUPS
    nq = S // tq
    nc = vc.shape[2]
    ns = S // SLC_LEN
    gw = NSA_HPG * HEAD_DIM
    return pl.pallas_call(
        functools.partial(_cattn_kernel, tq=tq),
        grid=(B, G, nq),
        in_specs=[pl.BlockSpec((tq, gw), lambda b, g, i: (b * nq + i, g)),
                  pl.BlockSpec((None, None, HEAD_DIM, nc), lambda b, g, i: (b, g, 0, 0)),
                  pl.BlockSpec((None, None, nc, HEAD_DIM), lambda b, g, i: (b, g, 0, 0)),
                  pl.BlockSpec((nc, ns), lambda b, g, i: (0, 0))],
        out_specs=[pl.BlockSpec((tq, gw), lambda b, g, i: (b * nq + i, g)),
                   pl.BlockSpec((None, None, tq, ns), lambda b, g, i: (b, g, i, 0))],
        out_shape=[jax.ShapeDtypeStruct((B * S, NSA_HEADS * HEAD_DIM), BF16),
                   jax.ShapeDtypeStruct((B, G, S, ns), BF16)],
        compiler_params=_cparams(("parallel", "parallel", "parallel")),
        name="nsa_cmp_attention",
    )(qn, kct, vc, ov)


def _flash_kernel(*refs, mode, hg, tq, tk, dqk, dv):
    if mode == "sel":
        q_ref, k_ref, v_ref, sel_ref, ex_ref, o_ref, qs, m_s, l_s, acc_s = refs
    else:
        q_ref, k_ref, v_ref, o_ref, qs, m_s, l_s, acc_s = refs
    rows = hg * tq
    q0 = pl.program_id(2) * tq
    cd = q0 // tk
    for hh in range(hg):
        qs[hh * tq:(hh + 1) * tq, :] = q_ref[:, hh * dqk:(hh + 1) * dqk]

    def step(c, carry, kind):
        m, l, acc = carry
        start = pl.multiple_of(c * tk, tk)
        s = lax.dot_general(qs[...], k_ref[pl.ds(start, tk), :], (((1,), (1,)), ((), ())),
                            preferred_element_type=F32)
        ok = None
        if kind != "full":
            q_pos = q0 + lax.broadcasted_iota(jnp.int32, (tq, tk), 0)
            k_pos = start + lax.broadcasted_iota(jnp.int32, (tq, tk), 1)
            ok = (k_pos <= q_pos) if kind == "diag" else (q_pos - k_pos < WINDOW)
        if mode == "sel":
            chosen = jnp.dot(sel_ref[...], ex_ref[c], preferred_element_type=F32) > 0.5
            ok = chosen if ok is None else (ok & chosen)
        if ok is not None:
            bias = jnp.where(ok, 0.0, NEG_INF)
            s = (s.reshape(hg, tq, tk) + bias[None]).reshape(rows, tk)
        m_new = jnp.maximum(m, jnp.max(s, axis=-1, keepdims=True))
        p = jnp.exp2(s - m_new)
        alpha = jnp.exp2(m - m_new)
        l = alpha * l + jnp.sum(p, axis=-1, keepdims=True)
        acc = alpha * acc + jnp.dot(p.astype(BF16), v_ref[pl.ds(start, tk), :], preferred_element_type=F32)
        return m_new, l, acc

    def load():
        return m_s[...], l_s[...], acc_s[...]

    def store(carry):
        m_s[...], l_s[...], acc_s[...] = carry

    store(step(cd, (jnp.full((rows, 1), NEG_INF, F32), jnp.zeros((rows, 1), F32),
                    jnp.zeros((rows, dv), F32)), "diag"))

    if mode == "win":
        @pl.when(cd >= 1)
        def _():
            store(step(cd - 1, load(), "full"))

        @pl.when(cd >= 2)
        def _():
            store(step(cd - 2, load(), "far"))
    else:
        def pair(i, _):
            store(step(2 * i + 1, step(2 * i, load(), "full"), "full"))
            return 0

        lax.fori_loop(0, cd // 2, pair, 0)

        @pl.when(cd % 2 == 1)
        def _():
            store(step(cd - 1, load(), "full"))

    o = acc_s[...] * (1.0 / l_s[...])
    for hh in range(hg):
        o_ref[:, hh * dv:(hh + 1) * dv] = o[hh * tq:(hh + 1) * tq, :].astype(o_ref.dtype)


def flash_attention(q, k, v, B, S, *, mode, n_groups, hg, dqk, dv, kcol0, vcol0, tq, tk,
                    sel=None, expand=None):
    nq = S // tq
    assert tk % tq == 0 and S % tk == 0
    if mode == "win":
        assert tq == tk and WINDOW == 2 * tk
    in_specs = [pl.BlockSpec((tq, hg * dqk), lambda b, g, i: (b * nq + i, g)),
                pl.BlockSpec((S, dqk), lambda b, g, i: (b, kcol0 + g)),
                pl.BlockSpec((S, dv), lambda b, g, i: (b, vcol0 + g))]
    args = [q, k, v]
    if mode == "sel":
        ns = S // SLC_LEN
        in_specs += [pl.BlockSpec((None, None, tq, ns), lambda b, g, i: (b, g, i, 0)),
                     pl.BlockSpec((S // tk, ns, tk), lambda b, g, i: (0, 0, 0))]
        args += [sel, expand]
    return pl.pallas_call(
        functools.partial(_flash_kernel, mode=mode, hg=hg, tq=tq, tk=tk, dqk=dqk, dv=dv),
        grid=(B, n_groups, nq),
        in_specs=in_specs,
        out_specs=pl.BlockSpec((tq, hg * dv), lambda b, g, i: (b * nq + i, g)),
        out_shape=jax.ShapeDtypeStruct((B * S, n_groups * hg * dv), BF16),
        scratch_shapes=[pltpu.VMEM((hg * tq, dqk), BF16),
                        pltpu.VMEM((hg * tq, 1), F32),
                        pltpu.VMEM((hg * tq, 1), F32),
                        pltpu.VMEM((hg * tq, dv), F32)],
        compiler_params=_cparams(("parallel", "parallel", "arbitrary")),
        name="flash_" + mode,
    )(*args)


def _mla_q_kernel(z_ref, ga_ref, w_ref, gh_ref, ct_ref, s1_ref, s2_ref, o_ref):
    h = _rms(z_ref[...], ga_ref[...]).astype(BF16)
    y = jnp.dot(h, w_ref[...], preferred_element_type=F32)
    ct, s1, s2 = ct_ref[...], s1_ref[...], s2_ref[...]
    scale = LOG2E * (QK_NOPE + QK_ROPE) ** -0.5
    for hd in range(MLA_HEADS):
        c = hd * MLA_QK_PAD
        nope = _rms(y[:, c:c + QK_NOPE], gh_ref[0:1, :])
        r = y[:, c + QK_NOPE:c + MLA_QK_PAD]
        ms = jnp.sum(r * r, axis=-1, keepdims=True) * (1.0 / QK_ROPE)
        r = _rope_lanes(r * lax.rsqrt(ms + EPS) * gh_ref[1:2, :], ct, s1, s2, QK_ROPE // 2)
        o_ref[:, c:c + QK_NOPE] = (nope * scale).astype(BF16)
        o_ref[:, c + QK_NOPE:c + MLA_QK_PAD] = (r * scale).astype(BF16)


def mla_q_proj(z, ga, wq, gh, tabs, S, *, tm):
    T = z.shape[0]
    ns = S // tm
    tab_spec = pl.BlockSpec((tm, LANES), lambda i: (i % ns, 0))
    nout = MLA_HEADS * MLA_QK_PAD
    return pl.pallas_call(
        _mla_q_kernel,
        grid=(T // tm,),
        in_specs=[pl.BlockSpec((tm, Q_LORA), lambda i: (i, OFF_QA // Q_LORA)),
                  pl.BlockSpec((1, Q_LORA), lambda i: (0, 0)),
                  pl.BlockSpec((Q_LORA, nout), lambda i: (0, 0)),
                  pl.BlockSpec((8, LANES), lambda i: (0, 0)),
                  tab_spec, tab_spec, tab_spec],
        out_specs=pl.BlockSpec((tm, nout), lambda i: (i, 0)),
        out_shape=jax.ShapeDtypeStruct((T, nout), BF16),
        compiler_params=_cparams(("parallel",)),
        name="mla_q_proj",
    )(z, ga, wq, gh, *tabs)


def _mla_kv_kernel(z_ref, zr_ref, ga_ref, w_ref, gh_ref, ct_ref, s1_ref, s2_ref, k_ref, v_ref):
    h = _rms(z_ref[...], ga_ref[...]).astype(BF16)
    y = jnp.dot(h, w_ref[...], preferred_element_type=F32)
    r = zr_ref[...]
    ms = jnp.sum(r * r, axis=-1, keepdims=True) * (1.0 / QK_ROPE)
    r = _rope_lanes(r * lax.rsqrt(ms + EPS) * gh_ref[1:2, :], ct_ref[...], s1_ref[...], s2_ref[...],
                    QK_ROPE // 2).astype(BF16)
    nv = MLA_HEADS * QK_NOPE
    for hd in range(MLA_HEADS):
        c = hd * MLA_QK_PAD
        k_ref[:, c:c + QK_NOPE] = _rms(y[:, hd * QK_NOPE:(hd + 1) * QK_NOPE], gh_ref[0:1, :]).astype(BF16)
        k_ref[:, c + QK_NOPE:c + MLA_QK_PAD] = r
    v_ref[...] = y[:, nv:].astype(BF16)


def mla_kv_proj(z, ga, wkv, gh, tabs, S, *, tm):
    T = z.shape[0]
    ns = S // tm
    tab_spec = pl.BlockSpec((tm, LANES), lambda i: (i % ns, 0))
    nk = MLA_HEADS * MLA_QK_PAD
    nv = MLA_HEADS * V_HEAD
    return pl.pallas_call(
        _mla_kv_kernel,
        grid=(T // tm,),
        in_specs=[pl.BlockSpec((tm, KV_LORA), lambda i: (i, OFF_KVA // KV_LORA)),
                  pl.BlockSpec((tm, LANES), lambda i: (i, OFF_KR // LANES)),
                  pl.BlockSpec((1, KV_LORA), lambda i: (0, 0)),
                  pl.BlockSpec((KV_LORA, MLA_HEADS * (QK_NOPE + V_HEAD)), lambda i: (0, 0)),
                  pl.BlockSpec((8, LANES), lambda i: (0, 0)),
                  tab_spec, tab_spec, tab_spec],
        out_specs=[pl.BlockSpec((tm, nk), lambda i: (i, 0)),
                   pl.BlockSpec((tm, nv), lambda i: (i, 0))],
        out_shape=[jax.ShapeDtypeStruct((T, nk), BF16),
                   jax.ShapeDtypeStruct((T, nv), BF16)],
        compiler_params=_cparams(("parallel",)),
        name="mla_kv_proj",
    )(z, z, ga, wkv, gh, *tabs)


def _mix_kernel(oc_ref, os_ref, ow_ref, ob_ref, gate_ref, wa_ref, wb_ref, za_ref, zb_ref, o_ref, a_scr):
    @pl.when(pl.program_id(1) == 0)
    def _():
        for h in range(NSA_HEADS):
            sl = slice(h * HEAD_DIM, (h + 1) * HEAD_DIM)
            a = (gate_ref[:, h:h + 1] * oc_ref[:, sl].astype(F32)
                 + gate_ref[:, NSA_HEADS + h:NSA_HEADS + h + 1] * os_ref[:, sl].astype(F32)
                 + gate_ref[:, 2 * NSA_HEADS + h:2 * NSA_HEADS + h + 1] * ow_ref[:, sl].astype(F32))
            a_scr[:, sl] = a.astype(BF16)

    pa = jnp.dot(a_scr[...], wa_ref[...].astype(BF16), preferred_element_type=F32)
    pb = jnp.dot(ob_ref[...], wb_ref[...].astype(BF16), preferred_element_type=F32)
    o_ref[...] = (_sigmoid(za_ref[...]) * pa + _sigmoid(zb_ref[...]) * pb).astype(o_ref.dtype)


def gated_mix(oc, os_, ow, ob, gates, wa, wb, z, *, tm, tn):
    T, K = oc.shape
    N = wa.shape[1]
    row = lambda w: pl.BlockSpec((tm, w), lambda m, n: (m, 0))
    return pl.pallas_call(
        _mix_kernel,
        grid=(T // tm, N // tn),
        in_specs=[row(K), row(K), row(K), row(K), row(LANES),
                  pl.BlockSpec((K, tn), lambda m, n: (0, n)),
                  pl.BlockSpec((K, tn), lambda m, n: (0, n)),
                  pl.BlockSpec((tm, tn), lambda m, n: (m, OFF_M // tn + n)),
                  pl.BlockSpec((tm, tn), lambda m, n: (m, (OFF_M + D_MODEL) // tn + n))],
        out_specs=pl.BlockSpec((tm, tn), lambda m, n: (m, n)),
        out_shape=jax.ShapeDtypeStruct((T, N), BF16),
        scratch_shapes=[pltpu.VMEM((tm, K), BF16)],
        compiler_params=_cparams(("parallel", "arbitrary")),
        name="gated_mix",
    )(oc, os_, ow, ob, gates, wa, wb, z, z)


def _ffn_up_kernel(x_ref, g_ref, w1_ref, w3_ref, o_ref, h_scr):
    @pl.when(pl.program_id(1) == 0)
    def _():
        h_scr[...] = _rms(x_ref[...], g_ref[...]).astype(BF16)

    h = h_scr[...]
    a = jnp.dot(h, w1_ref[...].astype(BF16), preferred_element_type=F32)
    b = jnp.dot(h, w3_ref[...].astype(BF16), preferred_element_type=F32)
    o_ref[...] = (a * _sigmoid(a) * b).astype(o_ref.dtype)


def ffn_up(x, g, w1, w3, *, tm, tn):
    T, K = x.shape
    N = w1.shape[1]
    return pl.pallas_call(
        _ffn_up_kernel,
        grid=(T // tm, N // tn),
        in_specs=[pl.BlockSpec((tm, K), lambda m, n: (m, 0)),
                  pl.BlockSpec((1, K), lambda m, n: (0, 0)),
                  pl.BlockSpec((K, tn), lambda m, n: (0, n)),
                  pl.BlockSpec((K, tn), lambda m, n: (0, n))],
        out_specs=pl.BlockSpec((tm, tn), lambda m, n: (m, n)),
        out_shape=jax.ShapeDtypeStruct((T, N), BF16),
        scratch_shapes=[pltpu.VMEM((tm, K), BF16)],
        compiler_params=_cparams(("parallel", "arbitrary")),
        name="ffn_up",
    )(x, g.reshape(1, K), w1, w3)


def _pack_bf16_pairs(h):
    k = h.shape[1] // 2
    hi = lax.bitcast_convert_type(h[:, :k].astype(jnp.bfloat16).astype(F32), jnp.uint32)
    lo = lax.bitcast_convert_type(h[:, k:].astype(jnp.bfloat16).astype(F32), jnp.uint32)
    return hi | (lo >> 16)


def _unpack_bf16_pairs(xp):
    hi = lax.bitcast_convert_type(xp & jnp.uint32(0xFFFF0000), F32)
    lo = lax.bitcast_convert_type(xp << 16, F32)
    return hi.astype(BF16), lo.astype(BF16)


def _router_kernel(x_ref, g_ref, wr_ref, h_ref, idx_ref, gate_ref):
    h = _rms(x_ref[...], g_ref[...])
    h_ref[...] = _pack_bf16_pairs(h)
    logits = jnp.dot(h, wr_ref[...], preferred_element_type=F32, precision=lax.Precision.HIGHEST)
    lane = lax.broadcasted_iota(jnp.int32, logits.shape, 1).astype(F32)
    logits = jnp.where(lane < float(N_EXPERTS), logits, NEG_INF)
    m1 = jnp.max(logits, axis=-1, keepdims=True)
    i1 = jnp.min(jnp.where(logits == m1, lane, float(LANES)), axis=-1, keepdims=True)
    rest = jnp.where(lane == i1, NEG_INF, logits)
    m2 = jnp.max(rest, axis=-1, keepdims=True)
    i2 = jnp.min(jnp.where(rest == m2, lane, float(LANES)), axis=-1, keepdims=True)
    e = jnp.exp(m2 - m1)
    den = 1.0 + e
    idx_ref[...] = jnp.where(lane == 0.0, i1, jnp.where(lane == 1.0, i2, 0.0)).astype(jnp.int32)
    gate_ref[...] = jnp.where(lane == 0.0, 1.0 / den, jnp.where(lane == 1.0, e / den, 0.0))


def router(x, g, wr_pad, *, tm):
    T, K = x.shape
    return pl.pallas_call(
        _router_kernel,
        grid=(T // tm,),
        in_specs=[pl.BlockSpec((tm, K), lambda i: (i, 0)),
                  pl.BlockSpec((1, K), lambda i: (0, 0)),
                  pl.BlockSpec((K, LANES), lambda i: (0, 0))],
        out_specs=[pl.BlockSpec((tm, K // 2), lambda i: (i, 0)),
                   pl.BlockSpec((tm, LANES), lambda i: (i, 0)),
                   pl.BlockSpec((tm, LANES), lambda i: (i, 0))],
        out_shape=[jax.ShapeDtypeStruct((T, K // 2), jnp.uint32),
                   jax.ShapeDtypeStruct((T, LANES), jnp.int32),
                   jax.ShapeDtypeStruct((T, LANES), F32)],
        compiler_params=_cparams(("parallel",)),
        name="router",
    )(x, g.reshape(1, K), wr_pad)


def _moe_fresh(be_ref, r):
    return (r == 0) | (be_ref[r] != be_ref[jnp.maximum(r - 1, 0)])


def _moe_up_kernel(be_ref, nu_ref, x_ref, w1_ref, w3_ref, o_ref, w1_s, w3_s):
    r = pl.program_id(1)
    used = r < nu_ref[0]

    @pl.when(used & _moe_fresh(be_ref, r))
    def _():
        w1_s[...] = w1_ref[...].astype(BF16)
        w3_s[...] = w3_ref[...].astype(BF16)

    @pl.when(used)
    def _():
        xa, xb = _unpack_bf16_pairs(x_ref[...])
        k2 = xa.shape[1]
        a = (jnp.dot(xa, w1_s[:k2, :], preferred_element_type=F32)
             + jnp.dot(xb, w1_s[k2:, :], preferred_element_type=F32))
        b = (jnp.dot(xa, w3_s[:k2, :], preferred_element_type=F32)
             + jnp.dot(xb, w3_s[k2:, :], preferred_element_type=F32))
        o_ref[...] = (a * _sigmoid(a) * b).astype(o_ref.dtype)

    @pl.when(jnp.logical_not(used))
    def _():
        o_ref[...] = jnp.zeros(o_ref.shape, o_ref.dtype)


def moe_up(blk_exp, n_used, xb, w1, w3, *, tmb, tn):
    n_slot = xb.shape[0]
    K, N = w1.shape[1], w1.shape[2]
    row = lambda r, nu: jnp.minimum(r, nu[0] - 1)
    wspec = pl.BlockSpec((None, K, tn), lambda n, r, be, nu: (be[row(r, nu)], 0, n))
    return pl.pallas_call(
        _moe_up_kernel,
        grid_spec=pltpu.PrefetchScalarGridSpec(
            num_scalar_prefetch=2,
            grid=(N // tn, n_slot // tmb),
            in_specs=[pl.BlockSpec((tmb, K // 2), lambda n, r, be, nu: (row(r, nu), 0)), wspec, wspec],
            out_specs=pl.BlockSpec((tmb, tn), lambda n, r, be, nu: (r, n)),
            scratch_shapes=[pltpu.VMEM((K, tn), BF16), pltpu.VMEM((K, tn), BF16)]),
        out_shape=jax.ShapeDtypeStruct((n_slot, N), BF16),
        compiler_params=_cparams(("arbitrary", "arbitrary")),
        name="moe_up",
    )(blk_exp, n_used, xb, w1, w3)


def _moe_down_kernel(be_ref, nu_ref, a_ref, w2_ref, g_ref, o_ref, w2_s):
    r = pl.program_id(1)
    used = r < nu_ref[0]

    @pl.when(used & _moe_fresh(be_ref, r))
    def _():
        w2_s[...] = w2_ref[...].astype(BF16)

    @pl.when(used)
    def _():
        o_ref[...] = jnp.dot(a_ref[...], w2_s[...], preferred_element_type=F32) * g_ref[...]

    @pl.when(jnp.logical_not(used))
    def _():
        o_ref[...] = jnp.zeros(o_ref.shape, o_ref.dtype)


def moe_down(blk_exp, n_used, act, w2, slot_gate, *, tmb, tn):
    n_slot, K = act.shape
    N = w2.shape[2]
    row = lambda r, nu: jnp.minimum(r, nu[0] - 1)
    return pl.pallas_call(
        _moe_down_kernel,
        grid_spec=pltpu.PrefetchScalarGridSpec(
            num_scalar_prefetch=2,
            grid=(N // tn, n_slot // tmb),
            in_specs=[pl.BlockSpec((tmb, K), lambda n, r, be, nu: (row(r, nu), 0)),
                      pl.BlockSpec((None, K, tn), lambda n, r, be, nu: (be[row(r, nu)], 0, n)),
                      pl.BlockSpec((tmb, 1), lambda n, r, be, nu: (row(r, nu), 0))],
            out_specs=pl.BlockSpec((tmb, tn), lambda n, r, be, nu: (r, n)),
            scratch_shapes=[pltpu.VMEM((K, tn), BF16)]),
        out_shape=jax.ShapeDtypeStruct((n_slot, N), F32),
        compiler_params=_cparams(("arbitrary", "arbitrary")),
        name="moe_down",
    )(blk_exp, n_used, act, w2, slot_gate)


def _rope_tabs(pos, rot_dim, n_rows):
    half = rot_dim // 2
    inv = 1.0 / (ROPE_THETA ** (jnp.arange(0, rot_dim, 2, dtype=F32) / rot_dim))
    ang = jnp.asarray(pos).astype(F32)[:, None] * inv[None, :]
    c, s = jnp.cos(ang), jnp.sin(ang)
    z = jnp.zeros_like(c)
    pad = lambda a, fill: jnp.pad(a, ((0, n_rows - a.shape[0]), (0, LANES - a.shape[1])), constant_values=fill)
    return pad(jnp.concatenate([c, c], 1), 1.0), pad(jnp.concatenate([-s, z], 1), 0.0), \
        pad(jnp.concatenate([z, s], 1), 0.0)


def _mla_tabs(S):
    ct, s1, s2 = _rope_tabs(jnp.arange(S), QK_ROPE, S)
    lane = jnp.arange(LANES)[None, :]
    return jnp.where(lane < QK_ROPE, ct, 0.0), s1, s2


def _pad_cols(w, n):
    return jnp.pad(w, ((0, 0), (0, n - w.shape[1])))


def _layout_w_in(w):
    sp = np.cumsum([0, NSA_HEADS * HEAD_DIM, NSA_KV_COLS, 3 * NSA_HEADS, Q_LORA, KV_LORA, QK_ROPE, 2 * D_MODEL])
    q, kv, g, qa, kva, kr, m = [w[:, sp[i]:sp[i + 1]] for i in range(7)]
    return jnp.concatenate([qa, _pad_cols(g, LANES), _pad_cols(kr, LANES), q, kva, kv, m], axis=1).astype(BF16)


def _layout_w_q_b(w):
    w = w.reshape(Q_LORA, MLA_HEADS, QK_NOPE + QK_ROPE)
    w = jnp.pad(w, ((0, 0), (0, 0), (0, MLA_QK_PAD - QK_NOPE - QK_ROPE)))
    return w.reshape(Q_LORA, MLA_HEADS * MLA_QK_PAD).astype(BF16)


def _layout_w_kv_b(w):
    w = w.reshape(KV_LORA, MLA_HEADS, QK_NOPE + V_HEAD)
    return jnp.concatenate([w[:, :, :QK_NOPE].reshape(KV_LORA, -1), w[:, :, QK_NOPE:].reshape(KV_LORA, -1)],
                           axis=1).astype(BF16)


def _layout_w_cmp(w):
    h = CMP_LEN // 2
    return jnp.concatenate([w[:h].reshape(h * HEAD_DIM, HEAD_DIM), w[h:].reshape(h * HEAD_DIM, HEAD_DIM)],
                           axis=1).astype(BF16)


def _layout_pe(pe):
    return jnp.pad(pe.reshape(2, (CMP_LEN // 2) * HEAD_DIM), ((0, 6), (0, 0))).astype(BF16)


def _overlap(nc, ns):
    n = np.arange(nc)[:, None] * CMP_STRIDE
    j = np.arange(ns)[None, :] * SLC_LEN
    ov = (n <= j + SLC_LEN - 1) & (j <= n + CMP_LEN - 1) & (np.arange(nc)[:, None] < nc - 1)
    return jnp.asarray(ov.astype(np.float32), BF16)


def _expand(ns, S, tk):
    e = np.arange(ns)[None, :, None] == ((np.arange(S // tk)[:, None, None] * tk + np.arange(tk)[None, None, :])
                                         // SLC_LEN)
    return jnp.asarray(e.astype(np.float32), BF16)


def _attention_block(x2d, B, S, p, l, tabs):
    T = x2d.shape[0]
    z = norm_matmul(x2d, p['attn_norm_g'][l], _layout_w_in(p['w_in'][l]), tm=min(1024, T), tn=512)
    gn = jnp.concatenate([p['nsa_q_norm_g'][l][None], p['nsa_k_norm_g'][l], jnp.zeros((4, HEAD_DIM), F32)], 0)
    qn, kvp, gates = nsa_prep(z, tabs['tok'], gn, S, tm=256)

    nc = S // CMP_STRIDE
    x2 = kvp[:, :2 * NSA_KV_GROUPS * HEAD_DIM].reshape(B, S, 2 * NSA_KV_GROUPS, HEAD_DIM)
    x2 = x2.transpose(0, 2, 1, 3).reshape(B, 2 * NSA_KV_GROUPS, nc, CMP_STRIDE * HEAD_DIM)
    kct, vc = nsa_compress(x2, _layout_w_cmp(p['w_cmp_k'][l]), _layout_w_cmp(p['w_cmp_v'][l]),
                           _layout_pe(p['cmp_pe_k'][l]), _layout_pe(p['cmp_pe_v'][l]),
                           p['nsa_k_norm_g'][l][0:1], tabs['cmp'])
    ns = S // SLC_LEN
    oc, sel = nsa_cmp_attention(qn, kct, vc, _overlap(nc, ns), B, S, tq=min(256, S))
    nsa = dict(n_groups=NSA_KV_GROUPS, hg=NSA_HPG, dqk=HEAD_DIM, dv=HEAD_DIM)
    tks = min(512, S)
    os_ = flash_attention(qn, kvp, kvp, B, S, mode="sel", kcol0=4, vcol0=6, tq=128, tk=tks,
                          sel=sel, expand=_expand(ns, S, tks), **nsa)
    ow = flash_attention(qn, kvp, kvp, B, S, mode="win", kcol0=8, vcol0=10, tq=256, tk=256, **nsa)

    gh_q = jnp.concatenate([p['mla_q_norm_g'][l][None], _pad_cols(p['mla_qr_norm_g'][l][None], LANES),
                            jnp.zeros((6, LANES), F32)], 0)
    gh_k = jnp.concatenate([p['mla_k_norm_g'][l][None], _pad_cols(p['mla_kr_norm_g'][l][None], LANES),
                            jnp.zeros((6, LANES), F32)], 0)
    qm = mla_q_proj(z, p['mla_qa_norm_g'][l][None], _layout_w_q_b(p['w_q_b'][l]), gh_q, tabs['mla'], S, tm=256)
    km, vm = mla_kv_proj(z, p['mla_kva_norm_g'][l][None], _layout_w_kv_b(p['w_kv_b'][l]), gh_k, tabs['mla'],
                         S, tm=256)
    ob = flash_attention(qm, km, vm, B, S, mode="causal", n_groups=MLA_HEADS, hg=1, dqk=MLA_QK_PAD, dv=V_HEAD,
                         kcol0=0, vcol0=0, tq=min(512, S), tk=min(512, S))

    mix = gated_mix(oc, os_, ow, ob, gates, p['w_proj_nsa'][l], p['w_proj_mla'][l], z, tm=512, tn=512)
    return matmul_residual(mix, p['w_out'][l], x2d, tm=512, tn=512)


def _dense_ffn(x2d, g, w1, w3, w2):
    T = x2d.shape[0]
    act = ffn_up(x2d, g, w1.astype(BF16), w3.astype(BF16), tm=min(1024, T), tn=512)
    return matmul_residual(act, w2.astype(BF16), x2d, tm=512, tn=512)


def _moe_ffn(x2d, g, w_router, w1, w3, w2, *, tmb=512):
    T = x2d.shape[0]
    hp, idx, gate = router(x2d, g, _pad_cols(w_router, LANES), tm=256)
    A = T * TOP_K
    e_flat = idx[:, :TOP_K].reshape(A)
    g_flat = gate[:, :TOP_K].reshape(A)
    tok_flat = jnp.repeat(jnp.arange(T, dtype=jnp.int32), TOP_K)
    oh = (e_flat[:, None] == jnp.arange(N_EXPERTS)[None, :]).astype(jnp.int32)
    csum = jnp.cumsum(oh, axis=0)
    rank = jnp.sum(oh * csum, axis=1) - 1
    counts = csum[-1]
    padded = (counts + tmb - 1) // tmb * tmb
    pad_end = jnp.cumsum(padded)
    dest = (pad_end - padded)[e_flat] + rank
    n_blk = -(-A // tmb) + N_EXPERTS
    n_slot = n_blk * tmb
    slot_tok = jnp.full((n_slot,), T, jnp.int32).at[dest].set(tok_flat)
    slot_gate = jnp.zeros((n_slot,), F32).at[dest].set(g_flat)
    blk_start = jnp.arange(n_blk, dtype=jnp.int32) * tmb
    blk_exp = jnp.minimum(jnp.sum((pad_end[None, :] <= blk_start[:, None]).astype(jnp.int32), axis=1),
                          N_EXPERTS - 1).astype(jnp.int32)
    n_used = (pad_end[-1:] // tmb).astype(jnp.int32)
    xb = jnp.concatenate([hp, jnp.zeros((1, D_MODEL // 2), jnp.uint32)], 0)[slot_tok]
    act = moe_up(blk_exp, n_used, xb, w1, w3, tmb=tmb, tn=512)
    yb = moe_down(blk_exp, n_used, act, w2, slot_gate[:, None], tmb=tmb, tn=256)
    d2 = dest.reshape(T, TOP_K)
    return x2d + yb[d2[:, 0]] + yb[d2[:, 1]]


def kernel(x, attn_norm_g, w_in, nsa_q_norm_g, nsa_k_norm_g, cmp_pe_k, cmp_pe_v, w_cmp_k, w_cmp_v, mla_qa_norm_g, w_q_b, mla_kva_norm_g, w_kv_b, mla_q_norm_g, mla_qr_norm_g, mla_k_norm_g, mla_kr_norm_g, w_proj_nsa, w_proj_mla, w_out, ffn_norm_g, w_ff1, w_ff3, w_ff2, w_router, w_e1, w_e3, w_e2):
    p = dict(attn_norm_g=attn_norm_g, w_in=w_in, nsa_q_norm_g=nsa_q_norm_g, nsa_k_norm_g=nsa_k_norm_g,
             cmp_pe_k=cmp_pe_k, cmp_pe_v=cmp_pe_v, w_cmp_k=w_cmp_k, w_cmp_v=w_cmp_v,
             mla_qa_norm_g=mla_qa_norm_g, w_q_b=w_q_b, mla_kva_norm_g=mla_kva_norm_g, w_kv_b=w_kv_b,
             mla_q_norm_g=mla_q_norm_g, mla_qr_norm_g=mla_qr_norm_g, mla_k_norm_g=mla_k_norm_g,
             mla_kr_norm_g=mla_kr_norm_g, w_proj_nsa=w_proj_nsa, w_proj_mla=w_proj_mla, w_out=w_out)
    B, S, D = x.shape
    depth = w_in.shape[0]
    nc = S // CMP_STRIDE
    tabs = dict(tok=_rope_tabs(jnp.arange(S), NSA_ROT, S),
                cmp=_rope_tabs(jnp.arange(nc - 1) * CMP_STRIDE + CMP_LEN - 1, NSA_ROT, nc),
                mla=_mla_tabs(S))
    x2d = x.reshape(B * S, D)
    for l in range(depth):
        x2d = _attention_block(x2d, B, S, p, l, tabs)
        if l % 2 == 0:
            x2d = _dense_ffn(x2d, ffn_norm_g[l], w_ff1[l // 2], w_ff3[l // 2], w_ff2[l // 2])
        else:
            x2d = _moe_ffn(x2d, ffn_norm_g[l], w_router[l // 2], w_e1[l // 2], w_e3[l // 2], w_e2[l // 2])
    return x2d.reshape(B, S, D)
```

```python
import functools

import numpy as np
import jax
import jax.numpy as jnp
from jax import lax
from jax.experimental import pallas as pl
from jax.experimental.pallas import tpu as pltpu

D_MODEL = 2048
HEAD_DIM = 128
NSA_HEADS = 8
NSA_KV_GROUPS = 2
NSA_HPG = NSA_HEADS // NSA_KV_GROUPS
NSA_ROT = HEAD_DIM // 4
CMP_LEN = 32
CMP_STRIDE = 16
SLC_LEN = 64
SLC_TOPK = 16
N_LOCAL_SLC = 2
WINDOW = 512
FORCE_SCORE = 1.0e4
MLA_HEADS = 8
Q_LORA = 768
KV_LORA = 512
QK_NOPE = 128
QK_ROPE = 64
V_HEAD = 128
ROPE_THETA = 500000.0
EPS = 1e-6
D_FF = 7168
N_EXPERTS = 8
TOP_K = 2

LANES = 128
MLA_QK_PAD = 256
NEG_INF = float("-inf")
LOG2E = 1.4426950408889634
BF16 = jnp.bfloat16
F32 = jnp.float32

OFF_QA = 0
OFF_G = 768
OFF_KR = 896
OFF_Q = 1024
OFF_KVA = 2048
OFF_KV = 2560
OFF_M = 4096
D_INP = 8192
NSA_KV_COLS = 3 * 2 * NSA_KV_GROUPS * HEAD_DIM

VMEM_LIMIT = 56 * 1024 * 1024


def _cparams(sem):
    return pltpu.CompilerParams(dimension_semantics=sem, vmem_limit_bytes=VMEM_LIMIT)


def _rms(x, g):
    ms = jnp.mean(x * x, axis=-1, keepdims=True)
    return x * lax.rsqrt(ms + EPS) * g


def _rope_lanes(y, ct, s1, s2, half):
    return y * ct + pltpu.roll(y, LANES - half, 1) * s1 + pltpu.roll(y, half, 1) * s2


def _sigmoid(x):
    return 1.0 / (1.0 + jnp.exp(-x))


def _nmm_kernel(x_ref, g_ref, w_ref, o_ref, h_scr):
    @pl.when(pl.program_id(1) == 0)
    def _():
        h_scr[...] = _rms(x_ref[...], g_ref[...]).astype(BF16)

    o_ref[...] = jnp.dot(h_scr[...], w_ref[...].astype(BF16),
                         preferred_element_type=F32).astype(o_ref.dtype)


def norm_matmul(x, g, w, *, tm, tn, out_dtype=F32):
    T, K = x.shape
    N = w.shape[1]
    return pl.pallas_call(
        _nmm_kernel,
        grid=(T // tm, N // tn),
        in_specs=[pl.BlockSpec((tm, K), lambda m, n: (m, 0)),
                  pl.BlockSpec((1, K), lambda m, n: (0, 0)),
                  pl.BlockSpec((K, tn), lambda m, n: (0, n))],
        out_specs=pl.BlockSpec((tm, tn), lambda m, n: (m, n)),
        out_shape=jax.ShapeDtypeStruct((T, N), out_dtype),
        scratch_shapes=[pltpu.VMEM((tm, K), BF16)],
        compiler_params=_cparams(("parallel", "arbitrary")),
        name="norm_matmul",
    )(x, g.reshape(1, K), w)


def _mmres_kernel(a_ref, w_ref, r_ref, o_ref):
    o_ref[...] = r_ref[...] + jnp.dot(a_ref[...], w_ref[...].astype(BF16), preferred_element_type=F32)


def matmul_residual(a, w, res, *, tm, tn):
    T, K = a.shape
    N = w.shape[1]
    return pl.pallas_call(
        _mmres_kernel,
        grid=(T // tm, N // tn),
        in_specs=[pl.BlockSpec((tm, K), lambda m, n: (m, 0)),
                  pl.BlockSpec((K, tn), lambda m, n: (0, n)),
                  pl.BlockSpec((tm, tn), lambda m, n: (m, n))],
        out_specs=pl.BlockSpec((tm, tn), lambda m, n: (m, n)),
        out_shape=jax.ShapeDtypeStruct((T, N), F32),
        compiler_params=_cparams(("parallel", "arbitrary")),
        name="matmul_residual",
    )(a, w, res)


def _prep_kernel(z_ref, ct_ref, s1_ref, s2_ref, gn_ref, qn_ref, kvp_ref, gate_ref):
    ct, s1, s2 = ct_ref[...], s1_ref[...], s2_ref[...]
    half = NSA_ROT // 2
    gq = gn_ref[0:1, :]
    for h in range(NSA_HEADS):
        c = OFF_Q + h * HEAD_DIM
        y = _rope_lanes(_rms(z_ref[:, c:c + HEAD_DIM], gq), ct, s1, s2, half)
        qn_ref[:, h * HEAD_DIM:(h + 1) * HEAD_DIM] = (y * (LOG2E * HEAD_DIM ** -0.5)).astype(BF16)
    for br in range(3):
        for kv in range(2):
            for g in range(NSA_KV_GROUPS):
                c = ((br * 2 + kv) * NSA_KV_GROUPS + g) * HEAD_DIM
                y = z_ref[:, OFF_KV + c:OFF_KV + c + HEAD_DIM]
                if kv == 0 and br > 0:
                    y = _rope_lanes(_rms(y, gn_ref[1 + br:2 + br, :]), ct, s1, s2, half)
                kvp_ref[:, c:c + HEAD_DIM] = y.astype(BF16)
    gate_ref[...] = _sigmoid(z_ref[:, OFF_G:OFF_G + LANES])


def nsa_prep(z, tabs, gn, S, *, tm):
    T = z.shape[0]
    ns = S // tm
    tab_spec = pl.BlockSpec((tm, LANES), lambda i: (i % ns, 0))
    return pl.pallas_call(
        _prep_kernel,
        grid=(T // tm,),
        in_specs=[pl.BlockSpec((tm, OFF_M), lambda i: (i, 0)), tab_spec, tab_spec, tab_spec,
                  pl.BlockSpec((8, LANES), lambda i: (0, 0))],
        out_specs=[pl.BlockSpec((tm, NSA_HEADS * HEAD_DIM), lambda i: (i, 0)),
                   pl.BlockSpec((tm, NSA_KV_COLS), lambda i: (i, 0)),
                   pl.BlockSpec((tm, LANES), lambda i: (i, 0))],
        out_shape=[jax.ShapeDtypeStruct((T, NSA_HEADS * HEAD_DIM), BF16),
                   jax.ShapeDtypeStruct((T, NSA_KV_COLS), BF16),
                   jax.ShapeDtypeStruct((T, LANES), F32)],
        compiler_params=_cparams(("parallel",)),
        name="nsa_prep",
    )(z, *tabs, gn)


def _cmp_kernel(xk_ref, xv_ref, wk_ref, wv_ref, pek_ref, pev_ref, gk_ref, ct_ref, s1_ref, s2_ref,
                kct_ref, vc_ref):
    nc = xk_ref.shape[0]
    row = lax.broadcasted_iota(jnp.int32, (nc, HEAD_DIM), 0)

    def comp(x_ref, w_ref, pe_ref):
        w = w_ref[...]
        y = jnp.dot(x_ref[...], w, preferred_element_type=F32)
        ype = jnp.dot(pe_ref[...], w, preferred_element_type=F32)
        bias = ype[0:1, :HEAD_DIM] + ype[1:2, HEAD_DIM:]
        out = y[:, :HEAD_DIM] + pltpu.roll(y[:, HEAD_DIM:], nc - 1, 0) + bias
        return jnp.where(row < nc - 1, out, 0.0)

    k = _rms(comp(xk_ref, wk_ref, pek_ref), gk_ref[...])
    k = _rope_lanes(k, ct_ref[...], s1_ref[...], s2_ref[...], NSA_ROT // 2)
    kct_ref[...] = k.T.astype(BF16)
    vc_ref[...] = comp(xv_ref, wv_ref, pev_ref).astype(BF16)


def nsa_compress(x2, wk2, wv2, pek2, pev2, gk, tabs_cmp):
    B, _, nc, kk = x2.shape
    G = NSA_KV_GROUPS
    full = lambda shape: pl.BlockSpec(shape, lambda b, g: (0,) * len(shape))
    return pl.pallas_call(
        _cmp_kernel,
        grid=(B, G),
        in_specs=[pl.BlockSpec((None, None, nc, kk), lambda b, g: (b, g, 0, 0)),
                  pl.BlockSpec((None, None, nc, kk), lambda b, g: (b, G + g, 0, 0)),
                  full((kk, 2 * HEAD_DIM)), full((kk, 2 * HEAD_DIM)),
                  full((8, kk)), full((8, kk)), full((1, HEAD_DIM)),
                  full((nc, LANES)), full((nc, LANES)), full((nc, LANES))],
        out_specs=[pl.BlockSpec((None, None, HEAD_DIM, nc), lambda b, g: (b, g, 0, 0)),
                   pl.BlockSpec((None, None, nc, HEAD_DIM), lambda b, g: (b, g, 0, 0))],
        out_shape=[jax.ShapeDtypeStruct((B, G, HEAD_DIM, nc), BF16),
                   jax.ShapeDtypeStruct((B, G, nc, HEAD_DIM), BF16)],
        compiler_params=_cparams(("parallel", "parallel")),
        name="nsa_compress",
    )(x2, x2, wk2, wv2, pek2, pev2, gk, *tabs_cmp)


def _cattn_kernel(q_ref, kct_ref, vc_ref, ov_ref, oc_ref, sel_ref, *, tq):
    nc = vc_ref.shape[0]
    ns = sel_ref.shape[0]
    t0 = pl.program_id(2) * tq
    t_pos = t0 + lax.broadcasted_iota(jnp.int32, (tq, nc), 0)
    n_idx = lax.broadcasted_iota(jnp.int32, (tq, nc), 1)
    vis = (t_pos >= n_idx * CMP_STRIDE + (CMP_LEN - 1)) & (n_idx < nc - 1)
    bias = jnp.where(vis, 0.0, NEG_INF)
    kct = kct_ref[...]
    vc = vc_ref[...]
    psum = jnp.zeros((tq, nc), F32)
    for hh in range(NSA_HPG):
        s = jnp.dot(q_ref[:, hh * HEAD_DIM:(hh + 1) * HEAD_DIM], kct, preferred_element_type=F32) + bias
        m = jnp.max(s, axis=-1, keepdims=True)
        m = jnp.where(m == NEG_INF, 0.0, m)
        p = jnp.exp2(s - m)
        den = jnp.sum(p, axis=-1, keepdims=True)
        p = p * (1.0 / jnp.where(den > 0.0, den, 1.0))
        oc_ref[:, hh * HEAD_DIM:(hh + 1) * HEAD_DIM] = jnp.dot(
            p.astype(BF16), vc, preferred_element_type=F32).astype(oc_ref.dtype)
        psum = psum + p
    pt = psum.T
    hi = pt.astype(BF16)
    lo = (pt - hi.astype(F32)).astype(BF16)
    ov = ov_ref[...]
    imp = jnp.dot(ov, hi, preferred_element_type=F32) + jnp.dot(ov, lo, preferred_element_type=F32)
    blk = lax.broadcasted_iota(jnp.int32, (ns, tq), 0)
    cur = (t0 + lax.broadcasted_iota(jnp.int32, (ns, tq), 1)) // SLC_LEN
    forced = (blk == 0) | ((blk <= cur) & (blk > cur - N_LOCAL_SLC))
    val = jnp.where(blk > cur, NEG_INF, jnp.where(forced, FORCE_SCORE, imp))
    rank = jnp.zeros((ns, tq), F32)
    for i in range(ns):
        other = val[i:i + 1, :]
        ahead = (other > val) | ((other == val) & (blk > i))
        rank = rank + jnp.where(ahead, 1.0, 0.0)
    sel_ref[...] = jnp.where(rank < float(min(SLC_TOPK, ns)), 1.0, 0.0).astype(sel_ref.dtype)


def nsa_cmp_attention(qn, kct, vc, ov, B, S, *, tq):
    G = NSA_KV_GROUPS
    nq = S // tq
    nc = vc.shape[2]
    ns = S // SLC_LEN
    gw = NSA_HPG * HEAD_DIM
    return pl.pallas_call(
        functools.partial(_cattn_kernel, tq=tq),
        grid=(B, G, nq),
        in_specs=[pl.BlockSpec((tq, gw), lambda b, g, i: (b * nq + i, g)),
                  pl.BlockSpec((None, None, HEAD_DIM, nc), lambda b, g, i: (b, g, 0, 0)),
                  pl.BlockSpec((None, None, nc, HEAD_DIM), lambda b, g, i: (b, g, 0, 0)),
                  pl.BlockSpec((ns, nc), lambda b, g, i: (0, 0))],
        out_specs=[pl.BlockSpec((tq, gw), lambda b, g, i: (b * nq + i, g)),
                   pl.BlockSpec((None, None, ns, tq), lambda b, g, i: (b, g, 0, i))],
        out_shape=[jax.ShapeDtypeStruct((B * S, NSA_HEADS * HEAD_DIM), BF16),
                   jax.ShapeDtypeStruct((B, G, ns, S), BF16)],
        compiler_params=_cparams(("parallel", "parallel", "parallel")),
        name="nsa_cmp_attention",
    )(qn, kct, vc, ov)


def _flash_kernel(*refs, mode, hg, tq, tk, dqk, dv):
    if mode == "sel":
        q_ref, k_ref, vt_ref, sel_ref, ex_ref, o_ref, qs, m_s, l_s, acc_s = refs
    else:
        q_ref, k_ref, vt_ref, o_ref, qs, m_s, l_s, acc_s = refs
    rows = hg * tq
    q0 = pl.program_id(2) * tq
    cd = q0 // tk
    for hh in range(hg):
        qs[hh * tq:(hh + 1) * tq, :] = q_ref[:, hh * dqk:(hh + 1) * dqk]

    def step(c, carry, kind):
        m, l, acc = carry
        start = pl.multiple_of(c * tk, tk)
        s = lax.dot_general(k_ref[pl.ds(start, tk), :], qs[...], (((1,), (1,)), ((), ())),
                            preferred_element_type=F32)
        ok = None
        if kind != "full":
            k_pos = start + lax.broadcasted_iota(jnp.int32, (tk, tq), 0)
            q_pos = q0 + lax.broadcasted_iota(jnp.int32, (tk, tq), 1)
            ok = (k_pos <= q_pos) if kind == "diag" else (q_pos - k_pos < WINDOW)
        if mode == "sel":
            chosen = jnp.dot(ex_ref[c], sel_ref[...], preferred_element_type=F32) > 0.5
            ok = chosen if ok is None else (ok & chosen)
        if ok is not None:
            bias = jnp.where(ok, 0.0, NEG_INF)
            s = s + (jnp.concatenate([bias] * hg, axis=1) if hg > 1 else bias)
        m_new = jnp.maximum(m, jnp.max(s, axis=0, keepdims=True))
        p = jnp.exp2(s - m_new)
        alpha = jnp.exp2(m - m_new)
        l = alpha * l + jnp.sum(p, axis=0, keepdims=True)
        acc = alpha * acc + jnp.dot(vt_ref[c], p.astype(BF16), preferred_element_type=F32)
        return m_new, l, acc

    def load():
        return m_s[...], l_s[...], acc_s[...]

    def store(carry):
        m_s[...], l_s[...], acc_s[...] = carry

    store(step(cd, (jnp.full((1, rows), NEG_INF, F32), jnp.zeros((1, rows), F32),
                    jnp.zeros((dv, rows), F32)), "diag"))

    if mode == "win":
        @pl.when(cd >= 1)
        def _():
            store(step(cd - 1, load(), "full"))

        @pl.when(cd >= 2)
        def _():
            store(step(cd - 2, load(), "far"))
    else:
        def pair(i, _):
            store(step(2 * i + 1, step(2 * i, load(), "full"), "full"))
            return 0

        lax.fori_loop(0, cd // 2, pair, 0)

        @pl.when(cd % 2 == 1)
        def _():
            store(step(cd - 1, load(), "full"))

    o = acc_s[...] * (1.0 / l_s[...])
    for hh in range(hg):
        o_ref[:, hh * dv:(hh + 1) * dv] = o[:, hh * tq:(hh + 1) * tq].T.astype(o_ref.dtype)


def flash_attention(q, k, vt, B, S, *, mode, n_groups, hg, dqk, dv, kcol0, tq, tk, sel_t=None, expand=None):
    nq = S // tq
    nch = S // tk
    assert tk % tq == 0 and S % tk == 0 and tq % LANES == 0
    if mode == "win":
        assert tq == tk and WINDOW == 2 * tk
    in_specs = [pl.BlockSpec((tq, hg * dqk), lambda b, g, i: (b * nq + i, g)),
                pl.BlockSpec((S, dqk), lambda b, g, i: (b, kcol0 + g)),
                pl.BlockSpec((None, None, nch, dv, tk), lambda b, g, i: (b, g, 0, 0, 0))]
    args = [q, k, vt]
    if mode == "sel":
        ns = S // SLC_LEN
        in_specs += [pl.BlockSpec((None, None, ns, tq), lambda b, g, i: (b, g, 0, i)),
                     pl.BlockSpec((nch, tk, ns), lambda b, g, i: (0, 0, 0))]
        args += [sel_t, expand]
    return pl.pallas_call(
        functools.partial(_flash_kernel, mode=mode, hg=hg, tq=tq, tk=tk, dqk=dqk, dv=dv),
        grid=(B, n_groups, nq),
        in_specs=in_specs,
        out_specs=pl.BlockSpec((tq, hg * dv), lambda b, g, i: (b * nq + i, g)),
        out_shape=jax.ShapeDtypeStruct((B * S, n_groups * hg * dv), BF16),
        scratch_shapes=[pltpu.VMEM((hg * tq, dqk), BF16),
                        pltpu.VMEM((1, hg * tq), F32),
                        pltpu.VMEM((1, hg * tq), F32),
                        pltpu.VMEM((dv, hg * tq), F32)],
        compiler_params=_cparams(("parallel", "parallel", "arbitrary")),
        name="flash_" + mode,
    )(*args)


def _mla_q_kernel(z_ref, ga_ref, w_ref, gh_ref, ct_ref, s1_ref, s2_ref, o_ref):
    h = _rms(z_ref[...], ga_ref[...]).astype(BF16)
    y = jnp.dot(h, w_ref[...], preferred_element_type=F32)
    ct, s1, s2 = ct_ref[...], s1_ref[...], s2_ref[...]
    scale = LOG2E * (QK_NOPE + QK_ROPE) ** -0.5
    for hd in range(MLA_HEADS):
        c = hd * MLA_QK_PAD
        nope = _rms(y[:, c:c + QK_NOPE], gh_ref[0:1, :])
        r = y[:, c + QK_NOPE:c + MLA_QK_PAD]
        ms = jnp.sum(r * r, axis=-1, keepdims=True) * (1.0 / QK_ROPE)
        r = _rope_lanes(r * lax.rsqrt(ms + EPS) * gh_ref[1:2, :], ct, s1, s2, QK_ROPE // 2)
        o_ref[:, c:c + QK_NOPE] = (nope * scale).astype(BF16)
        o_ref[:, c + QK_NOPE:c + MLA_QK_PAD] = (r * scale).astype(BF16)


def mla_q_proj(z, ga, wq, gh, tabs, S, *, tm):
    T = z.shape[0]
    ns = S // tm
    tab_spec = pl.BlockSpec((tm, LANES), lambda i: (i % ns, 0))
    nout = MLA_HEADS * MLA_QK_PAD
    return pl.pallas_call(
        _mla_q_kernel,
        grid=(T // tm,),
        in_specs=[pl.BlockSpec((tm, Q_LORA), lambda i: (i, OFF_QA // Q_LORA)),
                  pl.BlockSpec((1, Q_LORA), lambda i: (0, 0)),
                  pl.BlockSpec((Q_LORA, nout), lambda i: (0, 0)),
                  pl.BlockSpec((8, LANES), lambda i: (0, 0)),
                  tab_spec, tab_spec, tab_spec],
        out_specs=pl.BlockSpec((tm, nout), lambda i: (i, 0)),
        out_shape=jax.ShapeDtypeStruct((T, nout), BF16),
        compiler_params=_cparams(("parallel",)),
        name="mla_q_proj",
    )(z, ga, wq, gh, *tabs)


def _mla_kv_kernel(z_ref, zr_ref, ga_ref, w_ref, gh_ref, ct_ref, s1_ref, s2_ref, k_ref, v_ref):
    h = _rms(z_ref[...], ga_ref[...]).astype(BF16)
    y = jnp.dot(h, w_ref[...], preferred_element_type=F32)
    r = zr_ref[...]
    ms = jnp.sum(r * r, axis=-1, keepdims=True) * (1.0 / QK_ROPE)
    r = _rope_lanes(r * lax.rsqrt(ms + EPS) * gh_ref[1:2, :], ct_ref[...], s1_ref[...], s2_ref[...],
                    QK_ROPE // 2).astype(BF16)
    nv = MLA_HEADS * QK_NOPE
    for hd in range(MLA_HEADS):
        c = hd * MLA_QK_PAD
        k_ref[:, c:c + QK_NOPE] = _rms(y[:, hd * QK_NOPE:(hd + 1) * QK_NOPE], gh_ref[0:1, :]).astype(BF16)
        k_ref[:, c + QK_NOPE:c + MLA_QK_PAD] = r
    v_ref[...] = y[:, nv:].astype(BF16)


def mla_kv_proj(z, ga, wkv, gh, tabs, S, *, tm):
    T = z.shape[0]
    ns = S // tm
    tab_spec = pl.BlockSpec((tm, LANES), lambda i: (i % ns, 0))
    nk = MLA_HEADS * MLA_QK_PAD
    nv = MLA_HEADS * V_HEAD
    return pl.pallas_call(
        _mla_kv_kernel,
        grid=(T // tm,),
        in_specs=[pl.BlockSpec((tm, KV_LORA), lambda i: (i, OFF_KVA // KV_LORA)),
                  pl.BlockSpec((tm, LANES), lambda i: (i, OFF_KR // LANES)),
                  pl.BlockSpec((1, KV_LORA), lambda i: (0, 0)),
                  pl.BlockSpec((KV_LORA, MLA_HEADS * (QK_NOPE + V_HEAD)), lambda i: (0, 0)),
                  pl.BlockSpec((8, LANES), lambda i: (0, 0)),
                  tab_spec, tab_spec, tab_spec],
        out_specs=[pl.BlockSpec((tm, nk), lambda i: (i, 0)),
                   pl.BlockSpec((tm, nv), lambda i: (i, 0))],
        out_shape=[jax.ShapeDtypeStruct((T, nk), BF16),
                   jax.ShapeDtypeStruct((T, nv), BF16)],
        compiler_params=_cparams(("parallel",)),
        name="mla_kv_proj",
    )(z, z, ga, wkv, gh, *tabs)


def _mix_kernel(oc_ref, os_ref, ow_ref, ob_ref, gate_ref, wa_ref, wb_ref, za_ref, zb_ref, o_ref, a_scr):
    @pl.when(pl.program_id(1) == 0)
    def _():
        for h in range(NSA_HEADS):
            sl = slice(h * HEAD_DIM, (h + 1) * HEAD_DIM)
            a = (gate_ref[:, h:h + 1] * oc_ref[:, sl].astype(F32)
                 + gate_ref[:, NSA_HEADS + h:NSA_HEADS + h + 1] * os_ref[:, sl].astype(F32)
                 + gate_ref[:, 2 * NSA_HEADS + h:2 * NSA_HEADS + h + 1] * ow_ref[:, sl].astype(F32))
            a_scr[:, sl] = a.astype(BF16)

    pa = jnp.dot(a_scr[...], wa_ref[...].astype(BF16), preferred_element_type=F32)
    pb = jnp.dot(ob_ref[...], wb_ref[...].astype(BF16), preferred_element_type=F32)
    o_ref[...] = (_sigmoid(za_ref[...]) * pa + _sigmoid(zb_ref[...]) * pb).astype(o_ref.dtype)


def gated_mix(oc, os_, ow, ob, gates, wa, wb, z, *, tm, tn):
    T, K = oc.shape
    N = wa.shape[1]
    row = lambda w: pl.BlockSpec((tm, w), lambda m, n: (m, 0))
    return pl.pallas_call(
        _mix_kernel,
        grid=(T // tm, N // tn),
        in_specs=[row(K), row(K), row(K), row(K), row(LANES),
                  pl.BlockSpec((K, tn), lambda m, n: (0, n)),
                  pl.BlockSpec((K, tn), lambda m, n: (0, n)),
                  pl.BlockSpec((tm, tn), lambda m, n: (m, OFF_M // tn + n)),
                  pl.BlockSpec((tm, tn), lambda m, n: (m, (OFF_M + D_MODEL) // tn + n))],
        out_specs=pl.BlockSpec((tm, tn), lambda m, n: (m, n)),
        out_shape=jax.ShapeDtypeStruct((T, N), BF16),
        scratch_shapes=[pltpu.VMEM((tm, K), BF16)],
        compiler_params=_cparams(("parallel", "arbitrary")),
        name="gated_mix",
    )(oc, os_, ow, ob, gates, wa, wb, z, z)


def _ffn_up_kernel(x_ref, g_ref, w1_ref, w3_ref, o_ref, h_scr):
    @pl.when(pl.program_id(1) == 0)
    def _():
        h_scr[...] = _rms(x_ref[...], g_ref[...]).astype(BF16)

    h = h_scr[...]
    a = jnp.dot(h, w1_ref[...].astype(BF16), preferred_element_type=F32)
    b = jnp.dot(h, w3_ref[...].astype(BF16), preferred_element_type=F32)
    o_ref[...] = (a * _sigmoid(a) * b).astype(o_ref.dtype)


def ffn_up(x, g, w1, w3, *, tm, tn):
    T, K = x.shape
    N = w1.shape[1]
    return pl.pallas_call(
        _ffn_up_kernel,
        grid=(T // tm, N // tn),
        in_specs=[pl.BlockSpec((tm, K), lambda m, n: (m, 0)),
                  pl.BlockSpec((1, K), lambda m, n: (0, 0)),
                  pl.BlockSpec((K, tn), lambda m, n: (0, n)),
                  pl.BlockSpec((K, tn), lambda m, n: (0, n))],
        out_specs=pl.BlockSpec((tm, tn), lambda m, n: (m, n)),
        out_shape=jax.ShapeDtypeStruct((T, N), BF16),
        scratch_shapes=[pltpu.VMEM((tm, K), BF16)],
        compiler_params=_cparams(("parallel", "arbitrary")),
        name="ffn_up",
    )(x, g.reshape(1, K), w1, w3)


def _pack_bf16_pairs(h):
    k = h.shape[1] // 2
    hi = lax.bitcast_convert_type(h[:, :k].astype(jnp.bfloat16).astype(F32), jnp.uint32)
    lo = lax.bitcast_convert_type(h[:, k:].astype(jnp.bfloat16).astype(F32), jnp.uint32)
    return hi | (lo >> 16)


def _unpack_bf16_pairs(xp):
    hi = lax.bitcast_convert_type(xp & jnp.uint32(0xFFFF0000), F32)
    lo = lax.bitcast_convert_type(xp << 16, F32)
    return hi.astype(BF16), lo.astype(BF16)


def _router_kernel(x_ref, g_ref, wr_ref, h_ref, idx_ref, gate_ref):
    h = _rms(x_ref[...], g_ref[...])
    h_ref[...] = _pack_bf16_pairs(h)
    logits = jnp.dot(h, wr_ref[...], preferred_element_type=F32, precision=lax.Precision.HIGHEST)
    lane = lax.broadcasted_iota(jnp.int32, logits.shape, 1).astype(F32)
    logits = jnp.where(lane < float(N_EXPERTS), logits, NEG_INF)
    m1 = jnp.max(logits, axis=-1, keepdims=True)
    i1 = jnp.min(jnp.where(logits == m1, lane, float(LANES)), axis=-1, keepdims=True)
    rest = jnp.where(lane == i1, NEG_INF, logits)
    m2 = jnp.max(rest, axis=-1, keepdims=True)
    i2 = jnp.min(jnp.where(rest == m2, lane, float(LANES)), axis=-1, keepdims=True)
    e = jnp.exp(m2 - m1)
    den = 1.0 + e
    idx_ref[...] = jnp.where(lane == 0.0, i1, jnp.where(lane == 1.0, i2, 0.0)).astype(jnp.int32)
    gate_ref[...] = jnp.where(lane == 0.0, 1.0 / den, jnp.where(lane == 1.0, e / den, 0.0))


def router(x, g, wr_pad, *, tm):
    T, K = x.shape
    return pl.pallas_call(
        _router_kernel,
        grid=(T // tm,),
        in_specs=[pl.BlockSpec((tm, K), lambda i: (i, 0)),
                  pl.BlockSpec((1, K), lambda i: (0, 0)),
                  pl.BlockSpec((K, LANES), lambda i: (0, 0))],
        out_specs=[pl.BlockSpec((tm, K // 2), lambda i: (i, 0)),
                   pl.BlockSpec((tm, LANES), lambda i: (i, 0)),
                   pl.BlockSpec((tm, LANES), lambda i: (i, 0))],
        out_shape=[jax.ShapeDtypeStruct((T, K // 2), jnp.uint32),
                   jax.ShapeDtypeStruct((T, LANES), jnp.int32),
                   jax.ShapeDtypeStruct((T, LANES), F32)],
        compiler_params=_cparams(("parallel",)),
        name="router",
    )(x, g.reshape(1, K), wr_pad)


def _moe_fresh(be_ref, r):
    return (r == 0) | (be_ref[r] != be_ref[jnp.maximum(r - 1, 0)])


def _moe_up_kernel(be_ref, nu_ref, x_ref, w1_ref, w3_ref, o_ref, w1_s, w3_s):
    r = pl.program_id(1)
    used = r < nu_ref[0]

    @pl.when(used & _moe_fresh(be_ref, r))
    def _():
        w1_s[...] = w1_ref[...].astype(BF16)
        w3_s[...] = w3_ref[...].astype(BF16)

    @pl.when(used)
    def _():
        xa, xb = _unpack_bf16_pairs(x_ref[...])
        k2 = xa.shape[1]
        a = (jnp.dot(xa, w1_s[:k2, :], preferred_element_type=F32)
             + jnp.dot(xb, w1_s[k2:, :], preferred_element_type=F32))
        b = (jnp.dot(xa, w3_s[:k2, :], preferred_element_type=F32)
             + jnp.dot(xb, w3_s[k2:, :], preferred_element_type=F32))
        o_ref[...] = (a * _sigmoid(a) * b).astype(o_ref.dtype)

    @pl.when(jnp.logical_not(used))
    def _():
        o_ref[...] = jnp.zeros(o_ref.shape, o_ref.dtype)


def moe_up(blk_exp, n_used, xb, w1, w3, *, tmb, tn):
    n_slot = xb.shape[0]
    K, N = w1.shape[1], w1.shape[2]
    row = lambda r, nu: jnp.minimum(r, nu[0] - 1)
    wspec = pl.BlockSpec((None, K, tn), lambda n, r, be, nu: (be[row(r, nu)], 0, n))
    return pl.pallas_call(
        _moe_up_kernel,
        grid_spec=pltpu.PrefetchScalarGridSpec(
            num_scalar_prefetch=2,
            grid=(N // tn, n_slot // tmb),
            in_specs=[pl.BlockSpec((tmb, K // 2), lambda n, r, be, nu: (row(r, nu), 0)), wspec, wspec],
            out_specs=pl.BlockSpec((tmb, tn), lambda n, r, be, nu: (r, n)),
            scratch_shapes=[pltpu.VMEM((K, tn), BF16), pltpu.VMEM((K, tn), BF16)]),
        out_shape=jax.ShapeDtypeStruct((n_slot, N), BF16),
        compiler_params=_cparams(("arbitrary", "arbitrary")),
        name="moe_up",
    )(blk_exp, n_used, xb, w1, w3)


def _moe_down_kernel(be_ref, nu_ref, a_ref, w2_ref, g_ref, o_ref, w2_s):
    r = pl.program_id(1)
    used = r < nu_ref[0]

    @pl.when(used & _moe_fresh(be_ref, r))
    def _():
        w2_s[...] = w2_ref[...].astype(BF16)

    @pl.when(used)
    def _():
        o_ref[...] = jnp.dot(a_ref[...], w2_s[...], preferred_element_type=F32) * g_ref[...]

    @pl.when(jnp.logical_not(used))
    def _():
        o_ref[...] = jnp.zeros(o_ref.shape, o_ref.dtype)


def moe_down(blk_exp, n_used, act, w2, slot_gate, *, tmb, tn):
    n_slot, K = act.shape
    N = w2.shape[2]
    row = lambda r, nu: jnp.minimum(r, nu[0] - 1)
    return pl.pallas_call(
        _moe_down_kernel,
        grid_spec=pltpu.PrefetchScalarGridSpec(
            num_scalar_prefetch=2,
            grid=(N // tn, n_slot // tmb),
            in_specs=[pl.BlockSpec((tmb, K), lambda n, r, be, nu: (row(r, nu), 0)),
                      pl.BlockSpec((None, K, tn), lambda n, r, be, nu: (be[row(r, nu)], 0, n)),
                      pl.BlockSpec((tmb, 1), lambda n, r, be, nu: (row(r, nu), 0))],
            out_specs=pl.BlockSpec((tmb, tn), lambda n, r, be, nu: (r, n)),
            scratch_shapes=[pltpu.VMEM((K, tn), BF16)]),
        out_shape=jax.ShapeDtypeStruct((n_slot, N), F32),
        compiler_params=_cparams(("arbitrary", "arbitrary")),
        name="moe_down",
    )(blk_exp, n_used, act, w2, slot_gate)


def _rope_tabs(pos, rot_dim, n_rows):
    half = rot_dim // 2
    inv = 1.0 / (ROPE_THETA ** (jnp.arange(0, rot_dim, 2, dtype=F32) / rot_dim))
    ang = jnp.asarray(pos).astype(F32)[:, None] * inv[None, :]
    c, s = jnp.cos(ang), jnp.sin(ang)
    z = jnp.zeros_like(c)
    pad = lambda a, fill: jnp.pad(a, ((0, n_rows - a.shape[0]), (0, LANES - a.shape[1])), constant_values=fill)
    return pad(jnp.concatenate([c, c], 1), 1.0), pad(jnp.concatenate([-s, z], 1), 0.0), \
        pad(jnp.concatenate([z, s], 1), 0.0)


def _mla_tabs(S):
    ct, s1, s2 = _rope_tabs(jnp.arange(S), QK_ROPE, S)
    lane = jnp.arange(LANES)[None, :]
    return jnp.where(lane < QK_ROPE, ct, 0.0), s1, s2


def _pad_cols(w, n):
    return jnp.pad(w, ((0, 0), (0, n - w.shape[1])))


def _layout_w_in(w):
    sp = np.cumsum([0, NSA_HEADS * HEAD_DIM, NSA_KV_COLS, 3 * NSA_HEADS, Q_LORA, KV_LORA, QK_ROPE, 2 * D_MODEL])
    q, kv, g, qa, kva, kr, m = [w[:, sp[i]:sp[i + 1]] for i in range(7)]
    return jnp.concatenate([qa, _pad_cols(g, LANES), _pad_cols(kr, LANES), q, kva, kv, m], axis=1).astype(BF16)


def _layout_w_q_b(w):
    w = w.reshape(Q_LORA, MLA_HEADS, QK_NOPE + QK_ROPE)
    w = jnp.pad(w, ((0, 0), (0, 0), (0, MLA_QK_PAD - QK_NOPE - QK_ROPE)))
    return w.reshape(Q_LORA, MLA_HEADS * MLA_QK_PAD).astype(BF16)


def _layout_w_kv_b(w):
    w = w.reshape(KV_LORA, MLA_HEADS, QK_NOPE + V_HEAD)
    return jnp.concatenate([w[:, :, :QK_NOPE].reshape(KV_LORA, -1), w[:, :, QK_NOPE:].reshape(KV_LORA, -1)],
                           axis=1).astype(BF16)


def _layout_w_cmp(w):
    h = CMP_LEN // 2
    return jnp.concatenate([w[:h].reshape(h * HEAD_DIM, HEAD_DIM), w[h:].reshape(h * HEAD_DIM, HEAD_DIM)],
                           axis=1).astype(BF16)


def _layout_pe(pe):
    return jnp.pad(pe.reshape(2, (CMP_LEN // 2) * HEAD_DIM), ((0, 6), (0, 0))).astype(BF16)


def _overlap(nc, ns):
    n = np.arange(nc)[None, :] * CMP_STRIDE
    j = np.arange(ns)[:, None] * SLC_LEN
    ov = (n <= j + SLC_LEN - 1) & (j <= n + CMP_LEN - 1) & (np.arange(nc)[None, :] < nc - 1)
    return jnp.asarray(ov.astype(np.float32), BF16)


def _expand(ns, S, tk):
    e = ((np.arange(S // tk)[:, None, None] * tk + np.arange(tk)[None, :, None]) // SLC_LEN
         == np.arange(ns)[None, None, :])
    return jnp.asarray(e.astype(np.float32), BF16)


def _chunk_transpose(v, B, S, n_heads, dv, tk):
    return v.reshape(B, S // tk, tk, n_heads, dv).transpose(0, 3, 1, 4, 2)


def _attention_block(x2d, B, S, p, l, tabs):
    T = x2d.shape[0]
    z = norm_matmul(x2d, p['attn_norm_g'][l], _layout_w_in(p['w_in'][l]), tm=min(1024, T), tn=512)
    gn = jnp.concatenate([p['nsa_q_norm_g'][l][None], p['nsa_k_norm_g'][l], jnp.zeros((4, HEAD_DIM), F32)], 0)
    qn, kvp, gates = nsa_prep(z, tabs['tok'], gn, S, tm=256)

    nc = S // CMP_STRIDE
    x2 = kvp[:, :2 * NSA_KV_GROUPS * HEAD_DIM].reshape(B, S, 2 * NSA_KV_GROUPS, HEAD_DIM)
    x2 = x2.transpose(0, 2, 1, 3).reshape(B, 2 * NSA_KV_GROUPS, nc, CMP_STRIDE * HEAD_DIM)
    kct, vc = nsa_compress(x2, _layout_w_cmp(p['w_cmp_k'][l]), _layout_w_cmp(p['w_cmp_v'][l]),
                           _layout_pe(p['cmp_pe_k'][l]), _layout_pe(p['cmp_pe_v'][l]),
                           p['nsa_k_norm_g'][l][0:1], tabs['cmp'])
    ns = S // SLC_LEN
    oc, sel_t = nsa_cmp_attention(qn, kct, vc, _overlap(nc, ns), B, S, tq=min(256, S))
    nsa = dict(n_groups=NSA_KV_GROUPS, hg=NSA_HPG, dqk=HEAD_DIM, dv=HEAD_DIM)
    tks, tkw = min(512, S), WINDOW // 2
    gv = NSA_KV_GROUPS * HEAD_DIM
    vts = _chunk_transpose(kvp[:, 3 * gv:4 * gv], B, S, NSA_KV_GROUPS, HEAD_DIM, tks)
    vtw = _chunk_transpose(kvp[:, 5 * gv:6 * gv], B, S, NSA_KV_GROUPS, HEAD_DIM, tkw)
    os_ = flash_attention(qn, kvp, vts, B, S, mode="sel", kcol0=4, tq=128, tk=tks,
                          sel_t=sel_t, expand=_expand(ns, S, tks), **nsa)
    ow = flash_attention(qn, kvp, vtw, B, S, mode="win", kcol0=8, tq=tkw, tk=tkw, **nsa)

    gh_q = jnp.concatenate([p['mla_q_norm_g'][l][None], _pad_cols(p['mla_qr_norm_g'][l][None], LANES),
                            jnp.zeros((6, LANES), F32)], 0)
    gh_k = jnp.concatenate([p['mla_k_norm_g'][l][None], _pad_cols(p['mla_kr_norm_g'][l][None], LANES),
                            jnp.zeros((6, LANES), F32)], 0)
    qm = mla_q_proj(z, p['mla_qa_norm_g'][l][None], _layout_w_q_b(p['w_q_b'][l]), gh_q, tabs['mla'], S, tm=256)
    km, vm = mla_kv_proj(z, p['mla_kva_norm_g'][l][None], _layout_w_kv_b(p['w_kv_b'][l]), gh_k, tabs['mla'],
                         S, tm=256)
    tkm = min(512, S)
    ob = flash_attention(qm, km, _chunk_transpose(vm, B, S, MLA_HEADS, V_HEAD, tkm), B, S, mode="causal",
                         n_groups=MLA_HEADS, hg=1, dqk=MLA_QK_PAD, dv=V_HEAD, kcol0=0, tq=tkm, tk=tkm)

    tm = min(1024, T)
    mix = gated_mix(oc, os_, ow, ob, gates, p['w_proj_nsa'][l], p['w_proj_mla'][l], z, tm=tm, tn=512)
    return matmul_residual(mix, p['w_out'][l], x2d, tm=tm, tn=512)


def _dense_ffn(x2d, g, w1, w3, w2):
    T = x2d.shape[0]
    act = ffn_up(x2d, g, w1.astype(BF16), w3.astype(BF16), tm=min(1024, T), tn=512)
    return matmul_residual(act, w2.astype(BF16), x2d, tm=512, tn=512)


def _moe_ffn(x2d, g, w_router, w1, w3, w2, *, tmb=512):
    T = x2d.shape[0]
    hp, idx, gate = router(x2d, g, _pad_cols(w_router, LANES), tm=256)
    A = T * TOP_K
    e_flat = idx[:, :TOP_K].reshape(A)
    g_flat = gate[:, :TOP_K].reshape(A)
    tok_flat = jnp.repeat(jnp.arange(T, dtype=jnp.int32), TOP_K)
    oh = (e_flat[:, None] == jnp.arange(N_EXPERTS)[None, :]).astype(jnp.int32)
    csum = jnp.cumsum(oh, axis=0)
    rank = jnp.sum(oh * csum, axis=1) - 1
    counts = csum[-1]
    padded = (counts + tmb - 1) // tmb * tmb
    pad_end = jnp.cumsum(padded)
    dest = (pad_end - padded)[e_flat] + rank
    n_blk = -(-A // tmb) + N_EXPERTS
    n_slot = n_blk * tmb
    slot_tok = jnp.full((n_slot,), T, jnp.int32).at[dest].set(tok_flat)
    slot_gate = jnp.zeros((n_slot,), F32).at[dest].set(g_flat)
    blk_start = jnp.arange(n_blk, dtype=jnp.int32) * tmb
    blk_exp = jnp.minimum(jnp.sum((pad_end[None, :] <= blk_start[:, None]).astype(jnp.int32), axis=1),
                          N_EXPERTS - 1).astype(jnp.int32)
    n_used = (pad_end[-1:] // tmb).astype(jnp.int32)
    xb = jnp.concatenate([hp, jnp.zeros((1, D_MODEL // 2), jnp.uint32)], 0)[slot_tok]
    act = moe_up(blk_exp, n_used, xb, w1, w3, tmb=tmb, tn=512)
    yb = moe_down(jnp.repeat(blk_exp, 2), n_used * 2, act, w2, slot_gate[:, None], tmb=tmb // 2, tn=512)
    d2 = dest.reshape(T, TOP_K)
    return x2d + yb[d2[:, 0]] + yb[d2[:, 1]]


def kernel(x, attn_norm_g, w_in, nsa_q_norm_g, nsa_k_norm_g, cmp_pe_k, cmp_pe_v, w_cmp_k, w_cmp_v, mla_qa_norm_g, w_q_b, mla_kva_norm_g, w_kv_b, mla_q_norm_g, mla_qr_norm_g, mla_k_norm_g, mla_kr_norm_g, w_proj_nsa, w_proj_mla, w_out, ffn_norm_g, w_ff1, w_ff3, w_ff2, w_router, w_e1, w_e3, w_e2):
    p = dict(attn_norm_g=attn_norm_g, w_in=w_in, nsa_q_norm_g=nsa_q_norm_g, nsa_k_norm_g=nsa_k_norm_g,
             cmp_pe_k=cmp_pe_k, cmp_pe_v=cmp_pe_v, w_cmp_k=w_cmp_k, w_cmp_v=w_cmp_v,
             mla_qa_norm_g=mla_qa_norm_g, w_q_b=w_q_b, mla_kva_norm_g=mla_kva_norm_g, w_kv_b=w_kv_b,
             mla_q_norm_g=mla_q_norm_g, mla_qr_norm_g=mla_qr_norm_g, mla_k_norm_g=mla_k_norm_g,
             mla_kr_norm_g=mla_kr_norm_g, w_proj_nsa=w_proj_nsa, w_proj_mla=w_proj_mla, w_out=w_out)
    B, S, D = x.shape
    depth = w_in.shape[0]
    nc = S // CMP_STRIDE
    tabs = dict(tok=_rope_tabs(jnp.arange(S), NSA_ROT, S),
                cmp=_rope_tabs(jnp.arange(nc - 1) * CMP_STRIDE + CMP_LEN - 1, NSA_ROT, nc),
                mla=_mla_tabs(S))
    x2d = x.reshape(B * S, D)
    for l in range(depth):
        x2d = _attention_block(x2d, B, S, p, l, tabs)
        if l % 2 == 0:
            x2d = _dense_ffn(x2d, ffn_norm_g[l], w_ff1[l // 2], w_ff3[l // 2], w_ff2[l // 2])
        else:
            x2d = _moe_ffn(x2d, ffn_norm_g[l], w_router[l // 2], w_e1[l // 2], w_e3[l // 2], w_e2[l // 2])
    return x2d.reshape(B, S, D)
```

```python
import functools

import numpy as np
import jax
import jax.numpy as jnp
from jax import lax
from jax.experimental import pallas as pl
from jax.experimental.pallas import tpu as pltpu

D_MODEL = 2048
HEAD_DIM = 128
NSA_HEADS = 8
NSA_KV_GROUPS = 2
NSA_HPG = NSA_HEADS // NSA_KV_GROUPS
NSA_ROT = HEAD_DIM // 4
CMP_LEN = 32
CMP_STRIDE = 16
SLC_LEN = 64
SLC_TOPK = 16
N_LOCAL_SLC = 2
WINDOW = 512
FORCE_SCORE = 1.0e4
MLA_HEADS = 8
Q_LORA = 768
KV_LORA = 512
QK_NOPE = 128
QK_ROPE = 64
V_HEAD = 128
ROPE_THETA = 500000.0
EPS = 1e-6
D_FF = 7168
N_EXPERTS = 8
TOP_K = 2

LANES = 128
MLA_QK_PAD = 256
NEG_INF = float("-inf")
LOG2E = 1.4426950408889634
BF16 = jnp.bfloat16
F32 = jnp.float32

OFF_QA = 0
OFF_G = 768
OFF_KR = 896
OFF_Q = 1024
OFF_KVA = 2048
OFF_KV = 2560
OFF_M = 4096
D_INP = 8192
NSA_KV_COLS = 3 * 2 * NSA_KV_GROUPS * HEAD_DIM

VMEM_LIMIT = 56 * 1024 * 1024


def _cparams(sem):
    return pltpu.CompilerParams(dimension_semantics=sem, vmem_limit_bytes=VMEM_LIMIT)


def _rms(x, g):
    ms = jnp.mean(x * x, axis=-1, keepdims=True)
    return x * lax.rsqrt(ms + EPS) * g


def _rope_lanes(y, ct, s1, s2, half):
    return y * ct + pltpu.roll(y, LANES - half, 1) * s1 + pltpu.roll(y, half, 1) * s2


def _sigmoid(x):
    return 1.0 / (1.0 + jnp.exp(-x))


def _nmm_kernel(x_ref, g_ref, w_ref, o_ref, h_scr):
    @pl.when(pl.program_id(1) == 0)
    def _():
        h_scr[...] = _rms(x_ref[...], g_ref[...]).astype(BF16)

    o_ref[...] = jnp.dot(h_scr[...], w_ref[...].astype(BF16),
                         preferred_element_type=F32).astype(o_ref.dtype)


def norm_matmul(x, g, w, *, tm, tn, out_dtype=F32):
    T, K = x.shape
    N = w.shape[1]
    return pl.pallas_call(
        _nmm_kernel,
        grid=(T // tm, N // tn),
        in_specs=[pl.BlockSpec((tm, K), lambda m, n: (m, 0)),
                  pl.BlockSpec((1, K), lambda m, n: (0, 0)),
                  pl.BlockSpec((K, tn), lambda m, n: (0, n))],
        out_specs=pl.BlockSpec((tm, tn), lambda m, n: (m, n)),
        out_shape=jax.ShapeDtypeStruct((T, N), out_dtype),
        scratch_shapes=[pltpu.VMEM((tm, K), BF16)],
        compiler_params=_cparams(("parallel", "arbitrary")),
        name="norm_matmul",
    )(x, g.reshape(1, K), w)


def _mmres_kernel(a_ref, w_ref, r_ref, o_ref):
    o_ref[...] = r_ref[...] + jnp.dot(a_ref[...], w_ref[...].astype(BF16), preferred_element_type=F32)


def matmul_residual(a, w, res, *, tm, tn):
    T, K = a.shape
    N = w.shape[1]
    return pl.pallas_call(
        _mmres_kernel,
        grid=(T // tm, N // tn),
        in_specs=[pl.BlockSpec((tm, K), lambda m, n: (m, 0)),
                  pl.BlockSpec((K, tn), lambda m, n: (0, n)),
                  pl.BlockSpec((tm, tn), lambda m, n: (m, n))],
        out_specs=pl.BlockSpec((tm, tn), lambda m, n: (m, n)),
        out_shape=jax.ShapeDtypeStruct((T, N), F32),
        compiler_params=_cparams(("parallel", "arbitrary")),
        name="matmul_residual",
    )(a, w, res)


def _prep_kernel(z_ref, ct_ref, s1_ref, s2_ref, gn_ref, qn_ref, kvp_ref, gate_ref):
    ct, s1, s2 = ct_ref[...], s1_ref[...], s2_ref[...]
    half = NSA_ROT // 2
    gq = gn_ref[0:1, :]
    for h in range(NSA_HEADS):
        c = OFF_Q + h * HEAD_DIM
        y = _rope_lanes(_rms(z_ref[:, c:c + HEAD_DIM], gq), ct, s1, s2, half)
        qn_ref[:, h * HEAD_DIM:(h + 1) * HEAD_DIM] = (y * (LOG2E * HEAD_DIM ** -0.5)).astype(BF16)
    for br in range(3):
        for kv in range(2):
            for g in range(NSA_KV_GROUPS):
                c = ((br * 2 + kv) * NSA_KV_GROUPS + g) * HEAD_DIM
                y = z_ref[:, OFF_KV + c:OFF_KV + c + HEAD_DIM]
                if kv == 0 and br > 0:
                    y = _rope_lanes(_rms(y, gn_ref[1 + br:2 + br, :]), ct, s1, s2, half)
                kvp_ref[:, c:c + HEAD_DIM] = y.astype(BF16)
    gate_ref[...] = _sigmoid(z_ref[:, OFF_G:OFF_G + LANES])


def nsa_prep(z, tabs, gn, S, *, tm):
    T = z.shape[0]
    ns = S // tm
    tab_spec = pl.BlockSpec((tm, LANES), lambda i: (i % ns, 0))
    return pl.pallas_call(
        _prep_kernel,
        grid=(T // tm,),
        in_specs=[pl.BlockSpec((tm, OFF_M), lambda i: (i, 0)), tab_spec, tab_spec, tab_spec,
                  pl.BlockSpec((8, LANES), lambda i: (0, 0))],
        out_specs=[pl.BlockSpec((tm, NSA_HEADS * HEAD_DIM), lambda i: (i, 0)),
                   pl.BlockSpec((tm, NSA_KV_COLS), lambda i: (i, 0)),
                   pl.BlockSpec((tm, LANES), lambda i: (i, 0))],
        out_shape=[jax.ShapeDtypeStruct((T, NSA_HEADS * HEAD_DIM), BF16),
                   jax.ShapeDtypeStruct((T, NSA_KV_COLS), BF16),
                   jax.ShapeDtypeStruct((T, LANES), F32)],
        compiler_params=_cparams(("parallel",)),
        name="nsa_prep",
    )(z, *tabs, gn)


def _cmp_kernel(xk_ref, xv_ref, wk_ref, wv_ref, pek_ref, pev_ref, gk_ref, ct_ref, s1_ref, s2_ref,
                kct_ref, vc_ref):
    nc = xk_ref.shape[0]
    row = lax.broadcasted_iota(jnp.int32, (nc, HEAD_DIM), 0)

    def comp(x_ref, w_ref, pe_ref):
        w = w_ref[...]
        y = jnp.dot(x_ref[...], w, preferred_element_type=F32)
        ype = jnp.dot(pe_ref[...], w, preferred_element_type=F32)
        bias = ype[0:1, :HEAD_DIM] + ype[1:2, HEAD_DIM:]
        out = y[:, :HEAD_DIM] + pltpu.roll(y[:, HEAD_DIM:], nc - 1, 0) + bias
        return jnp.where(row < nc - 1, out, 0.0)

    k = _rms(comp(xk_ref, wk_ref, pek_ref), gk_ref[...])
    k = _rope_lanes(k, ct_ref[...], s1_ref[...], s2_ref[...], NSA_ROT // 2)
    kct_ref[...] = k.T.astype(BF16)
    vc_ref[...] = comp(xv_ref, wv_ref, pev_ref).astype(BF16)


def nsa_compress(x2, wk2, wv2, pek2, pev2, gk, tabs_cmp):
    B, _, nc, kk = x2.shape
    G = NSA_KV_GROUPS
    full = lambda shape: pl.BlockSpec(shape, lambda b, g: (0,) * len(shape))
    return pl.pallas_call(
        _cmp_kernel,
        grid=(B, G),
        in_specs=[pl.BlockSpec((None, None, nc, kk), lambda b, g: (b, g, 0, 0)),
                  pl.BlockSpec((None, None, nc, kk), lambda b, g: (b, G + g, 0, 0)),
                  full((kk, 2 * HEAD_DIM)), full((kk, 2 * HEAD_DIM)),
                  full((8, kk)), full((8, kk)), full((1, HEAD_DIM)),
                  full((nc, LANES)), full((nc, LANES)), full((nc, LANES))],
        out_specs=[pl.BlockSpec((None, None, HEAD_DIM, nc), lambda b, g: (b, g, 0, 0)),
                   pl.BlockSpec((None, None, nc, HEAD_DIM), lambda b, g: (b, g, 0, 0))],
        out_shape=[jax.ShapeDtypeStruct((B, G, HEAD_DIM, nc), BF16),
                   jax.ShapeDtypeStruct((B, G, nc, HEAD_DIM), BF16)],
        compiler_params=_cparams(("parallel", "parallel")),
        name="nsa_compress",
    )(x2, x2, wk2, wv2, pek2, pev2, gk, *tabs_cmp)


def _cattn_kernel(q_ref, kct_ref, vc_ref, ov_ref, oc_ref, sel_ref, *, tq):
    nc = vc_ref.shape[0]
    ns = sel_ref.shape[0]
    t0 = pl.program_id(2) * tq
    t_pos = t0 + lax.broadcasted_iota(jnp.int32, (tq, nc), 0)
    n_idx = lax.broadcasted_iota(jnp.int32, (tq, nc), 1)
    vis = (t_pos >= n_idx * CMP_STRIDE + (CMP_LEN - 1)) & (n_idx < nc - 1)
    bias = jnp.where(vis, 0.0, NEG_INF)
    kct = kct_ref[...]
    vc = vc_ref[...]
    psum = jnp.zeros((tq, nc), F32)
    for hh in range(NSA_HPG):
        s = jnp.dot(q_ref[:, hh * HEAD_DIM:(hh + 1) * HEAD_DIM], kct, preferred_element_type=F32) + bias
        m = jnp.max(s, axis=-1, keepdims=True)
        m = jnp.where(m == NEG_INF, 0.0, m)
        p = jnp.exp2(s - m)
        den = jnp.sum(p, axis=-1, keepdims=True)
        p = p * (1.0 / jnp.where(den > 0.0, den, 1.0))
        oc_ref[:, hh * HEAD_DIM:(hh + 1) * HEAD_DIM] = jnp.dot(
            p.astype(BF16), vc, preferred_element_type=F32).astype(oc_ref.dtype)
        psum = psum + p
    pt = psum.T
    hi = pt.astype(BF16)
    lo = (pt - hi.astype(F32)).astype(BF16)
    ov = ov_ref[...]
    imp = jnp.dot(ov, hi, preferred_element_type=F32) + jnp.dot(ov, lo, preferred_element_type=F32)
    blk = lax.broadcasted_iota(jnp.int32, (ns, tq), 0)
    cur = (t0 + lax.broadcasted_iota(jnp.int32, (ns, tq), 1)) // SLC_LEN
    forced = (blk == 0) | ((blk <= cur) & (blk > cur - N_LOCAL_SLC))
    val = jnp.where(blk > cur, NEG_INF, jnp.where(forced, FORCE_SCORE, imp))
    rank = jnp.zeros((ns, tq), F32)
    for i in range(ns):
        other = val[i:i + 1, :]
        ahead = (other > val) | ((other == val) & (blk > i))
        rank = rank + jnp.where(ahead, 1.0, 0.0)
    sel_ref[...] = jnp.where(rank < float(min(SLC_TOPK, ns)), 1.0, 0.0).astype(sel_ref.dtype)


def nsa_cmp_attention(qn, kct, vc, ov, B, S, *, tq):
    G = NSA_KV_GROUPS
    nq = S // tq
    nc = vc.shape[2]
    ns = S // SLC_LEN
    gw = NSA_HPG * HEAD_DIM
    return pl.pallas_call(
        functools.partial(_cattn_kernel, tq=tq),
        grid=(B, G, nq),
        in_specs=[pl.BlockSpec((tq, gw), lambda b, g, i: (b * nq + i, g)),
                  pl.BlockSpec((None, None, HEAD_DIM, nc), lambda b, g, i: (b, g, 0, 0)),
                  pl.BlockSpec((None, None, nc, HEAD_DIM), lambda b, g, i: (b, g, 0, 0)),
                  pl.BlockSpec((ns, nc), lambda b, g, i: (0, 0))],
        out_specs=[pl.BlockSpec((tq, gw), lambda b, g, i: (b * nq + i, g)),
                   pl.BlockSpec((None, None, ns, tq), lambda b, g, i: (b, g, 0, i))],
        out_shape=[jax.ShapeDtypeStruct((B * S, NSA_HEADS * HEAD_DIM), BF16),
                   jax.ShapeDtypeStruct((B, G, ns, S), BF16)],
        compiler_params=_cparams(("parallel", "parallel", "parallel")),
        name="nsa_cmp_attention",
    )(qn, kct, vc, ov)


def _flash_kernel(*refs, mode, hg, tq, tk, dqk, dv):
    if mode == "sel":
        q_ref, k_ref, vt_ref, sel_ref, ex_ref, o_ref, qs, m_s, l_s, acc_s = refs
    else:
        q_ref, k_ref, vt_ref, o_ref, qs, m_s, l_s, acc_s = refs
    rows = hg * tq
    q0 = pl.program_id(2) * tq
    cd = q0 // tk
    for hh in range(hg):
        qs[hh * tq:(hh + 1) * tq, :] = q_ref[:, hh * dqk:(hh + 1) * dqk]

    def scores(c, kind):
        start = pl.multiple_of(c * tk, tk)
        s = lax.dot_general(k_ref[pl.ds(start, tk), :], qs[...], (((1,), (1,)), ((), ())),
                            preferred_element_type=F32)
        ok = None
        if kind != "full":
            k_pos = start + lax.broadcasted_iota(jnp.int32, (tk, tq), 0)
            q_pos = q0 + lax.broadcasted_iota(jnp.int32, (tk, tq), 1)
            ok = (k_pos <= q_pos) if kind == "diag" else (q_pos - k_pos < WINDOW)
        if mode == "sel":
            chosen = jnp.dot(ex_ref[c], sel_ref[...], preferred_element_type=F32) > 0.5
            ok = chosen if ok is None else (ok & chosen)
        if ok is not None:
            bias = jnp.where(ok, 0.0, NEG_INF)
            s = s + (jnp.concatenate([bias] * hg, axis=1) if hg > 1 else bias)
        return s

    def update(c, s, carry):
        m, l, acc = carry
        m_new = jnp.maximum(m, jnp.max(s, axis=0, keepdims=True))
        p = jnp.exp2(s - m_new)
        alpha = jnp.exp2(m - m_new)
        l = alpha * l + jnp.sum(p, axis=0, keepdims=True)
        acc = alpha * acc + jnp.dot(vt_ref[c], p.astype(BF16), preferred_element_type=F32)
        return m_new, l, acc

    def step(c, carry, kind):
        return update(c, scores(c, kind), carry)

    def load():
        return m_s[...], l_s[...], acc_s[...]

    def store(carry):
        m_s[...], l_s[...], acc_s[...] = carry

    store(step(cd, (jnp.full((1, rows), NEG_INF, F32), jnp.zeros((1, rows), F32),
                    jnp.zeros((dv, rows), F32)), "diag"))

    if mode == "win":
        @pl.when(cd >= 1)
        def _():
            store(step(cd - 1, load(), "full"))

        @pl.when(cd >= 2)
        def _():
            store(step(cd - 2, load(), "far"))
    else:
        def group(first, count):
            carry = load()
            s_next = scores(first, "full")
            for j in range(count):
                s_cur = s_next
                if j + 1 < count:
                    s_next = scores(first + j + 1, "full")
                carry = update(first + j, s_cur, carry)
            store(carry)

        def quad(i, _):
            group(4 * i, 4)
            return 0

        lax.fori_loop(0, cd // 4, quad, 0)

        @pl.when(cd % 4 >= 2)
        def _():
            group((cd // 4) * 4, 2)

        @pl.when(cd % 2 == 1)
        def _():
            group(cd - 1, 1)

    o = acc_s[...] * (1.0 / l_s[...])
    for hh in range(hg):
        o_ref[:, hh * dv:(hh + 1) * dv] = o[:, hh * tq:(hh + 1) * tq].T.astype(o_ref.dtype)


def flash_attention(q, k, vt, B, S, *, mode, n_groups, hg, dqk, dv, kcol0, tq, tk, sel_t=None, expand=None):
    nq = S // tq
    nch = S // tk
    assert tk % tq == 0 and S % tk == 0 and tq % LANES == 0
    if mode == "win":
        assert tq == tk and WINDOW == 2 * tk
    in_specs = [pl.BlockSpec((tq, hg * dqk), lambda b, g, i: (b * nq + i, g)),
                pl.BlockSpec((S, dqk), lambda b, g, i: (b, kcol0 + g)),
                pl.BlockSpec((None, None, nch, dv, tk), lambda b, g, i: (b, g, 0, 0, 0))]
    args = [q, k, vt]
    if mode == "sel":
        ns = S // SLC_LEN
        in_specs += [pl.BlockSpec((None, None, ns, tq), lambda b, g, i: (b, g, 0, i)),
                     pl.BlockSpec((nch, tk, ns), lambda b, g, i: (0, 0, 0))]
        args += [sel_t, expand]
    return pl.pallas_call(
        functools.partial(_flash_kernel, mode=mode, hg=hg, tq=tq, tk=tk, dqk=dqk, dv=dv),
        grid=(B, n_groups, nq),
        in_specs=in_specs,
        out_specs=pl.BlockSpec((tq, hg * dv), lambda b, g, i: (b * nq + i, g)),
        out_shape=jax.ShapeDtypeStruct((B * S, n_groups * hg * dv), BF16),
        scratch_shapes=[pltpu.VMEM((hg * tq, dqk), BF16),
                        pltpu.VMEM((1, hg * tq), F32),
                        pltpu.VMEM((1, hg * tq), F32),
                        pltpu.VMEM((dv, hg * tq), F32)],
        compiler_params=_cparams(("parallel", "parallel", "arbitrary")),
        name="flash_" + mode,
    )(*args)


def _mla_q_kernel(z_ref, ga_ref, w_ref, gh_ref, ct_ref, s1_ref, s2_ref, o_ref):
    h = _rms(z_ref[...], ga_ref[...]).astype(BF16)
    y = jnp.dot(h, w_ref[...], preferred_element_type=F32)
    ct, s1, s2 = ct_ref[...], s1_ref[...], s2_ref[...]
    scale = LOG2E * (QK_NOPE + QK_ROPE) ** -0.5
    for hd in range(MLA_HEADS):
        c = hd * MLA_QK_PAD
        nope = _rms(y[:, c:c + QK_NOPE], gh_ref[0:1, :])
        r = y[:, c + QK_NOPE:c + MLA_QK_PAD]
        ms = jnp.sum(r * r, axis=-1, keepdims=True) * (1.0 / QK_ROPE)
        r = _rope_lanes(r * lax.rsqrt(ms + EPS) * gh_ref[1:2, :], ct, s1, s2, QK_ROPE // 2)
        o_ref[:, c:c + QK_NOPE] = (nope * scale).astype(BF16)
        o_ref[:, c + QK_NOPE:c + MLA_QK_PAD] = (r * scale).astype(BF16)


def mla_q_proj(z, ga, wq, gh, tabs, S, *, tm):
    T = z.shape[0]
    ns = S // tm
    tab_spec = pl.BlockSpec((tm, LANES), lambda i: (i % ns, 0))
    nout = MLA_HEADS * MLA_QK_PAD
    return pl.pallas_call(
        _mla_q_kernel,
        grid=(T // tm,),
        in_specs=[pl.BlockSpec((tm, Q_LORA), lambda i: (i, OFF_QA // Q_LORA)),
                  pl.BlockSpec((1, Q_LORA), lambda i: (0, 0)),
                  pl.BlockSpec((Q_LORA, nout), lambda i: (0, 0)),
                  pl.BlockSpec((8, LANES), lambda i: (0, 0)),
                  tab_spec, tab_spec, tab_spec],
        out_specs=pl.BlockSpec((tm, nout), lambda i: (i, 0)),
        out_shape=jax.ShapeDtypeStruct((T, nout), BF16),
        compiler_params=_cparams(("parallel",)),
        name="mla_q_proj",
    )(z, ga, wq, gh, *tabs)


def _mla_kv_kernel(z_ref, zr_ref, ga_ref, w_ref, gh_ref, ct_ref, s1_ref, s2_ref, k_ref, v_ref):
    h = _rms(z_ref[...], ga_ref[...]).astype(BF16)
    y = jnp.dot(h, w_ref[...], preferred_element_type=F32)
    r = zr_ref[...]
    ms = jnp.sum(r * r, axis=-1, keepdims=True) * (1.0 / QK_ROPE)
    r = _rope_lanes(r * lax.rsqrt(ms + EPS) * gh_ref[1:2, :], ct_ref[...], s1_ref[...], s2_ref[...],
                    QK_ROPE // 2).astype(BF16)
    nv = MLA_HEADS * QK_NOPE
    for hd in range(MLA_HEADS):
        c = hd * MLA_QK_PAD
        k_ref[:, c:c + QK_NOPE] = _rms(y[:, hd * QK_NOPE:(hd + 1) * QK_NOPE], gh_ref[0:1, :]).astype(BF16)
        k_ref[:, c + QK_NOPE:c + MLA_QK_PAD] = r
    v_ref[...] = y[:, nv:].astype(BF16)


def mla_kv_proj(z, ga, wkv, gh, tabs, S, *, tm):
    T = z.shape[0]
    ns = S // tm
    tab_spec = pl.BlockSpec((tm, LANES), lambda i: (i % ns, 0))
    nk = MLA_HEADS * MLA_QK_PAD
    nv = MLA_HEADS * V_HEAD
    return pl.pallas_call(
        _mla_kv_kernel,
        grid=(T // tm,),
        in_specs=[pl.BlockSpec((tm, KV_LORA), lambda i: (i, OFF_KVA // KV_LORA)),
                  pl.BlockSpec((tm, LANES), lambda i: (i, OFF_KR // LANES)),
                  pl.BlockSpec((1, KV_LORA), lambda i: (0, 0)),
                  pl.BlockSpec((KV_LORA, MLA_HEADS * (QK_NOPE + V_HEAD)), lambda i: (0, 0)),
                  pl.BlockSpec((8, LANES), lambda i: (0, 0)),
                  tab_spec, tab_spec, tab_spec],
        out_specs=[pl.BlockSpec((tm, nk), lambda i: (i, 0)),
                   pl.BlockSpec((tm, nv), lambda i: (i, 0))],
        out_shape=[jax.ShapeDtypeStruct((T, nk), BF16),
                   jax.ShapeDtypeStruct((T, nv), BF16)],
        compiler_params=_cparams(("parallel",)),
        name="mla_kv_proj",
    )(z, z, ga, wkv, gh, *tabs)


def _mix_kernel(oc_ref, os_ref, ow_ref, ob_ref, gate_ref, wa_ref, wb_ref, za_ref, zb_ref, o_ref, a_scr):
    @pl.when(pl.program_id(1) == 0)
    def _():
        for h in range(NSA_HEADS):
            sl = slice(h * HEAD_DIM, (h + 1) * HEAD_DIM)
            a = (gate_ref[:, h:h + 1] * oc_ref[:, sl].astype(F32)
                 + gate_ref[:, NSA_HEADS + h:NSA_HEADS + h + 1] * os_ref[:, sl].astype(F32)
                 + gate_ref[:, 2 * NSA_HEADS + h:2 * NSA_HEADS + h + 1] * ow_ref[:, sl].astype(F32))
            a_scr[:, sl] = a.astype(BF16)

    pa = jnp.dot(a_scr[...], wa_ref[...].astype(BF16), preferred_element_type=F32)
    pb = jnp.dot(ob_ref[...], wb_ref[...].astype(BF16), preferred_element_type=F32)
    o_ref[...] = (_sigmoid(za_ref[...]) * pa + _sigmoid(zb_ref[...]) * pb).astype(o_ref.dtype)


def gated_mix(oc, os_, ow, ob, gates, wa, wb, z, *, tm, tn):
    T, K = oc.shape
    N = wa.shape[1]
    row = lambda w: pl.BlockSpec((tm, w), lambda m, n: (m, 0))
    return pl.pallas_call(
        _mix_kernel,
        grid=(T // tm, N // tn),
        in_specs=[row(K), row(K), row(K), row(K), row(LANES),
                  pl.BlockSpec((K, tn), lambda m, n: (0, n)),
                  pl.BlockSpec((K, tn), lambda m, n: (0, n)),
                  pl.BlockSpec((tm, tn), lambda m, n: (m, OFF_M // tn + n)),
                  pl.BlockSpec((tm, tn), lambda m, n: (m, (OFF_M + D_MODEL) // tn + n))],
        out_specs=pl.BlockSpec((tm, tn), lambda m, n: (m, n)),
        out_shape=jax.ShapeDtypeStruct((T, N), BF16),
        scratch_shapes=[pltpu.VMEM((tm, K), BF16)],
        compiler_params=_cparams(("parallel", "arbitrary")),
        name="gated_mix",
    )(oc, os_, ow, ob, gates, wa, wb, z, z)


def _ffn_up_kernel(x_ref, g_ref, w1_ref, w3_ref, o_ref, h_scr):
    @pl.when(pl.program_id(1) == 0)
    def _():
        h_scr[...] = _rms(x_ref[...], g_ref[...]).astype(BF16)

    h = h_scr[...]
    a = jnp.dot(h, w1_ref[...].astype(BF16), preferred_element_type=F32)
    b = jnp.dot(h, w3_ref[...].astype(BF16), preferred_element_type=F32)
    o_ref[...] = (a * _sigmoid(a) * b).astype(o_ref.dtype)


def ffn_up(x, g, w1, w3, *, tm, tn):
    T, K = x.shape
    N = w1.shape[1]
    return pl.pallas_call(
        _ffn_up_kernel,
        grid=(T // tm, N // tn),
        in_specs=[pl.BlockSpec((tm, K), lambda m, n: (m, 0)),
                  pl.BlockSpec((1, K), lambda m, n: (0, 0)),
                  pl.BlockSpec((K, tn), lambda m, n: (0, n)),
                  pl.BlockSpec((K, tn), lambda m, n: (0, n))],
        out_specs=pl.BlockSpec((tm, tn), lambda m, n: (m, n)),
        out_shape=jax.ShapeDtypeStruct((T, N), BF16),
        scratch_shapes=[pltpu.VMEM((tm, K), BF16)],
        compiler_params=_cparams(("parallel", "arbitrary")),
        name="ffn_up",
    )(x, g.reshape(1, K), w1, w3)


def _pack_bf16_pairs(h):
    k = h.shape[1] // 2
    hi = lax.bitcast_convert_type(h[:, :k].astype(jnp.bfloat16).astype(F32), jnp.uint32)
    lo = lax.bitcast_convert_type(h[:, k:].astype(jnp.bfloat16).astype(F32), jnp.uint32)
    return hi | (lo >> 16)


def _unpack_bf16_pairs(xp):
    hi = lax.bitcast_convert_type(xp & jnp.uint32(0xFFFF0000), F32)
    lo = lax.bitcast_convert_type(xp << 16, F32)
    return hi.astype(BF16), lo.astype(BF16)


def _router_kernel(x_ref, g_ref, wr_ref, h_ref, idx_ref, gate_ref):
    h = _rms(x_ref[...], g_ref[...])
    h_ref[...] = _pack_bf16_pairs(h)
    logits = jnp.dot(h, wr_ref[...], preferred_element_type=F32, precision=lax.Precision.HIGHEST)
    lane = lax.broadcasted_iota(jnp.int32, logits.shape, 1).astype(F32)
    logits = jnp.where(lane < float(N_EXPERTS), logits, NEG_INF)
    m1 = jnp.max(logits, axis=-1, keepdims=True)
    i1 = jnp.min(jnp.where(logits == m1, lane, float(LANES)), axis=-1, keepdims=True)
    rest = jnp.where(lane == i1, NEG_INF, logits)
    m2 = jnp.max(rest, axis=-1, keepdims=True)
    i2 = jnp.min(jnp.where(rest == m2, lane, float(LANES)), axis=-1, keepdims=True)
    e = jnp.exp(m2 - m1)
    den = 1.0 + e
    idx_ref[...] = jnp.where(lane == 0.0, i1, jnp.where(lane == 1.0, i2, 0.0)).astype(jnp.int32)
    gate_ref[...] = jnp.where(lane == 0.0, 1.0 / den, jnp.where(lane == 1.0, e / den, 0.0))


def router(x, g, wr_pad, *, tm):
    T, K = x.shape
    return pl.pallas_call(
        _router_kernel,
        grid=(T // tm,),
        in_specs=[pl.BlockSpec((tm, K), lambda i: (i, 0)),
                  pl.BlockSpec((1, K), lambda i: (0, 0)),
                  pl.BlockSpec((K, LANES), lambda i: (0, 0))],
        out_specs=[pl.BlockSpec((tm, K // 2), lambda i: (i, 0)),
                   pl.BlockSpec((tm, LANES), lambda i: (i, 0)),
                   pl.BlockSpec((tm, LANES), lambda i: (i, 0))],
        out_shape=[jax.ShapeDtypeStruct((T, K // 2), jnp.uint32),
                   jax.ShapeDtypeStruct((T, LANES), jnp.int32),
                   jax.ShapeDtypeStruct((T, LANES), F32)],
        compiler_params=_cparams(("parallel",)),
        name="router",
    )(x, g.reshape(1, K), wr_pad)


def _moe_weight_stream(w_hbms, stages, casts, sems, be_ref, nu_ref, gs_ref, ne_ref, lg_ref, tn):
    n = pl.program_id(0)
    r = pl.program_id(1)
    used = r < nu_ref[0]

    def copies(e, nt):
        c0 = pl.multiple_of(nt * tn, tn)
        return [pltpu.make_async_copy(w.at[e, :, pl.ds(c0, tn)], st, sems.at[i])
                for i, (w, st) in enumerate(zip(w_hbms, stages))]

    @pl.when((n == 0) & (r == 0))
    def _():
        for c in copies(be_ref[0], 0):
            c.start()

    @pl.when(used & (gs_ref[r] == 1))
    def _():
        for c in copies(be_ref[r], n):
            c.wait()
        for st, wb in zip(stages, casts):
            wb[...] = st[...].astype(BF16)
        last = lg_ref[r] == 1

        @pl.when(jnp.logical_not(last & (n == pl.num_programs(0) - 1)))
        def _():
            for c in copies(ne_ref[r], n + last.astype(jnp.int32)):
                c.start()

    return used


def _moe_up_kernel(be_ref, nu_ref, gs_ref, ne_ref, lg_ref, x_ref, w1_hbm, w3_hbm, o_ref,
                   st1, st3, w1_s, w3_s, sems, *, tn):
    used = _moe_weight_stream((w1_hbm, w3_hbm), (st1, st3), (w1_s, w3_s), sems,
                              be_ref, nu_ref, gs_ref, ne_ref, lg_ref, tn)

    @pl.when(used)
    def _():
        xa, xb = _unpack_bf16_pairs(x_ref[...])
        k2 = xa.shape[1]
        a = (jnp.dot(xa, w1_s[:k2, :], preferred_element_type=F32)
             + jnp.dot(xb, w1_s[k2:, :], preferred_element_type=F32))
        b = (jnp.dot(xa, w3_s[:k2, :], preferred_element_type=F32)
             + jnp.dot(xb, w3_s[k2:, :], preferred_element_type=F32))
        o_ref[...] = (a * _sigmoid(a) * b).astype(o_ref.dtype)

    @pl.when(jnp.logical_not(used))
    def _():
        o_ref[...] = jnp.zeros(o_ref.shape, o_ref.dtype)


def moe_up(tables, xb, w1, w3, *, tmb, tn):
    n_slot = xb.shape[0]
    K, N = w1.shape[1], w1.shape[2]
    row = lambda r, nu: jnp.minimum(r, nu[0] - 1)
    return pl.pallas_call(
        functools.partial(_moe_up_kernel, tn=tn),
        grid_spec=pltpu.PrefetchScalarGridSpec(
            num_scalar_prefetch=5,
            grid=(N // tn, n_slot // tmb),
            in_specs=[pl.BlockSpec((tmb, K // 2), lambda n, r, be, nu, gs, ne, lg: (row(r, nu), 0)),
                      pl.BlockSpec(memory_space=pl.ANY), pl.BlockSpec(memory_space=pl.ANY)],
            out_specs=pl.BlockSpec((tmb, tn), lambda n, r, be, nu, gs, ne, lg: (r, n)),
            scratch_shapes=[pltpu.VMEM((K, tn), F32), pltpu.VMEM((K, tn), F32),
                            pltpu.VMEM((K, tn), BF16), pltpu.VMEM((K, tn), BF16),
                            pltpu.SemaphoreType.DMA((2,))]),
        out_shape=jax.ShapeDtypeStruct((n_slot, N), BF16),
        compiler_params=_cparams(("arbitrary", "arbitrary")),
        name="moe_up",
    )(*tables, xb, w1, w3)


def _moe_down_kernel(be_ref, nu_ref, gs_ref, ne_ref, lg_ref, a_ref, w2_hbm, o_ref, st2, w2_s, sems, *, tn):
    used = _moe_weight_stream((w2_hbm,), (st2,), (w2_s,), sems, be_ref, nu_ref, gs_ref, ne_ref, lg_ref, tn)

    @pl.when(used)
    def _():
        o_ref[...] = jnp.dot(a_ref[...], w2_s[...], preferred_element_type=F32)

    @pl.when(jnp.logical_not(used))
    def _():
        o_ref[...] = jnp.zeros(o_ref.shape, o_ref.dtype)


def moe_down(tables, act, w2, *, tmb, tn):
    n_slot, K = act.shape
    N = w2.shape[2]
    row = lambda r, nu: jnp.minimum(r, nu[0] - 1)
    return pl.pallas_call(
        functools.partial(_moe_down_kernel, tn=tn),
        grid_spec=pltpu.PrefetchScalarGridSpec(
            num_scalar_prefetch=5,
            grid=(N // tn, n_slot // tmb),
            in_specs=[pl.BlockSpec((tmb, K), lambda n, r, be, nu, gs, ne, lg: (row(r, nu), 0)),
                      pl.BlockSpec(memory_space=pl.ANY)],
            out_specs=pl.BlockSpec((tmb, tn), lambda n, r, be, nu, gs, ne, lg: (r, n)),
            scratch_shapes=[pltpu.VMEM((K, tn), F32), pltpu.VMEM((K, tn), BF16),
                            pltpu.SemaphoreType.DMA((1,))]),
        out_shape=jax.ShapeDtypeStruct((n_slot, N), F32),
        compiler_params=_cparams(("arbitrary", "arbitrary")),
        name="moe_down",
    )(*tables, act, w2)


def _rope_tabs(pos, rot_dim, n_rows):
    half = rot_dim // 2
    inv = 1.0 / (ROPE_THETA ** (jnp.arange(0, rot_dim, 2, dtype=F32) / rot_dim))
    ang = jnp.asarray(pos).astype(F32)[:, None] * inv[None, :]
    c, s = jnp.cos(ang), jnp.sin(ang)
    z = jnp.zeros_like(c)
    pad = lambda a, fill: jnp.pad(a, ((0, n_rows - a.shape[0]), (0, LANES - a.shape[1])), constant_values=fill)
    return pad(jnp.concatenate([c, c], 1), 1.0), pad(jnp.concatenate([-s, z], 1), 0.0), \
        pad(jnp.concatenate([z, s], 1), 0.0)


def _mla_tabs(S):
    ct, s1, s2 = _rope_tabs(jnp.arange(S), QK_ROPE, S)
    lane = jnp.arange(LANES)[None, :]
    return jnp.where(lane < QK_ROPE, ct, 0.0), s1, s2


def _pad_cols(w, n):
    return jnp.pad(w, ((0, 0), (0, n - w.shape[1])))


def _layout_w_in(w):
    sp = np.cumsum([0, NSA_HEADS * HEAD_DIM, NSA_KV_COLS, 3 * NSA_HEADS, Q_LORA, KV_LORA, QK_ROPE, 2 * D_MODEL])
    q, kv, g, qa, kva, kr, m = [w[:, sp[i]:sp[i + 1]] for i in range(7)]
    return jnp.concatenate([qa, _pad_cols(g, LANES), _pad_cols(kr, LANES), q, kva, kv, m], axis=1).astype(BF16)


def _layout_w_q_b(w):
    w = w.reshape(Q_LORA, MLA_HEADS, QK_NOPE + QK_ROPE)
    w = jnp.pad(w, ((0, 0), (0, 0), (0, MLA_QK_PAD - QK_NOPE - QK_ROPE)))
    return w.reshape(Q_LORA, MLA_HEADS * MLA_QK_PAD).astype(BF16)


def _layout_w_kv_b(w):
    w = w.reshape(KV_LORA, MLA_HEADS, QK_NOPE + V_HEAD)
    return jnp.concatenate([w[:, :, :QK_NOPE].reshape(KV_LORA, -1), w[:, :, QK_NOPE:].reshape(KV_LORA, -1)],
                           axis=1).astype(BF16)


def _layout_w_cmp(w):
    h = CMP_LEN // 2
    return jnp.concatenate([w[:h].reshape(h * HEAD_DIM, HEAD_DIM), w[h:].reshape(h * HEAD_DIM, HEAD_DIM)],
                           axis=1).astype(BF16)


def _layout_pe(pe):
    return jnp.pad(pe.reshape(2, (CMP_LEN // 2) * HEAD_DIM), ((0, 6), (0, 0))).astype(BF16)


def _overlap(nc, ns):
    n = np.arange(nc)[None, :] * CMP_STRIDE
    j = np.arange(ns)[:, None] * SLC_LEN
    ov = (n <= j + SLC_LEN - 1) & (j <= n + CMP_LEN - 1) & (np.arange(nc)[None, :] < nc - 1)
    return jnp.asarray(ov.astype(np.float32), BF16)


def _expand(ns, S, tk):
    e = ((np.arange(S // tk)[:, None, None] * tk + np.arange(tk)[None, :, None]) // SLC_LEN
         == np.arange(ns)[None, None, :])
    return jnp.asarray(e.astype(np.float32), BF16)


def _chunk_transpose(v, B, S, n_heads, dv, tk):
    return v.reshape(B, S // tk, tk, n_heads, dv).transpose(0, 3, 1, 4, 2)


def _attention_block(x2d, B, S, p, l, tabs):
    T = x2d.shape[0]
    z = norm_matmul(x2d, p['attn_norm_g'][l], _layout_w_in(p['w_in'][l]), tm=min(1024, T), tn=512)
    gn = jnp.concatenate([p['nsa_q_norm_g'][l][None], p['nsa_k_norm_g'][l], jnp.zeros((4, HEAD_DIM), F32)], 0)
    qn, kvp, gates = nsa_prep(z, tabs['tok'], gn, S, tm=256)

    nc = S // CMP_STRIDE
    x2 = kvp[:, :2 * NSA_KV_GROUPS * HEAD_DIM].reshape(B, S, 2 * NSA_KV_GROUPS, HEAD_DIM)
    x2 = x2.transpose(0, 2, 1, 3).reshape(B, 2 * NSA_KV_GROUPS, nc, CMP_STRIDE * HEAD_DIM)
    kct, vc = nsa_compress(x2, _layout_w_cmp(p['w_cmp_k'][l]), _layout_w_cmp(p['w_cmp_v'][l]),
                           _layout_pe(p['cmp_pe_k'][l]), _layout_pe(p['cmp_pe_v'][l]),
                           p['nsa_k_norm_g'][l][0:1], tabs['cmp'])
    ns = S // SLC_LEN
    oc, sel_t = nsa_cmp_attention(qn, kct, vc, _overlap(nc, ns), B, S, tq=min(256, S))
    nsa = dict(n_groups=NSA_KV_GROUPS, hg=NSA_HPG, dqk=HEAD_DIM, dv=HEAD_DIM)
    tks, tkw = min(512, S), WINDOW // 2
    gv = NSA_KV_GROUPS * HEAD_DIM
    vts = _chunk_transpose(kvp[:, 3 * gv:4 * gv], B, S, NSA_KV_GROUPS, HEAD_DIM, tks)
    vtw = _chunk_transpose(kvp[:, 5 * gv:6 * gv], B, S, NSA_KV_GROUPS, HEAD_DIM, tkw)
    os_ = flash_attention(qn, kvp, vts, B, S, mode="sel", kcol0=4, tq=128, tk=tks,
                          sel_t=sel_t, expand=_expand(ns, S, tks), **nsa)
    ow = flash_attention(qn, kvp, vtw, B, S, mode="win", kcol0=8, tq=tkw, tk=tkw, **nsa)

    gh_q = jnp.concatenate([p['mla_q_norm_g'][l][None], _pad_cols(p['mla_qr_norm_g'][l][None], LANES),
                            jnp.zeros((6, LANES), F32)], 0)
    gh_k = jnp.concatenate([p['mla_k_norm_g'][l][None], _pad_cols(p['mla_kr_norm_g'][l][None], LANES),
                            jnp.zeros((6, LANES), F32)], 0)
    qm = mla_q_proj(z, p['mla_qa_norm_g'][l][None], _layout_w_q_b(p['w_q_b'][l]), gh_q, tabs['mla'], S, tm=256)
    km, vm = mla_kv_proj(z, p['mla_kva_norm_g'][l][None], _layout_w_kv_b(p['w_kv_b'][l]), gh_k, tabs['mla'],
                         S, tm=256)
    tkm = min(512, S)
    ob = flash_attention(qm, km, _chunk_transpose(vm, B, S, MLA_HEADS, V_HEAD, tkm), B, S, mode="causal",
                         n_groups=MLA_HEADS, hg=1, dqk=MLA_QK_PAD, dv=V_HEAD, kcol0=0, tq=tkm, tk=tkm)

    tm = min(1024, T)
    mix = gated_mix(oc, os_, ow, ob, gates, p['w_proj_nsa'][l], p['w_proj_mla'][l], z, tm=tm, tn=512)
    return matmul_residual(mix, p['w_out'][l], x2d, tm=tm, tn=512)


def _dense_ffn(x2d, g, w1, w3, w2):
    T = x2d.shape[0]
    act = ffn_up(x2d, g, w1.astype(BF16), w3.astype(BF16), tm=min(1024, T), tn=512)
    return matmul_residual(act, w2.astype(BF16), x2d, tm=512, tn=512)


def _moe_ffn(x2d, g, w_router, w1, w3, w2, *, tmb=512):
    T = x2d.shape[0]
    hp, idx, gate = router(x2d, g, _pad_cols(w_router, LANES), tm=256)
    A = T * TOP_K
    e_flat = idx[:, :TOP_K].reshape(A)
    tok_flat = jnp.repeat(jnp.arange(T, dtype=jnp.int32), TOP_K)
    oh = (e_flat[:, None] == jnp.arange(N_EXPERTS)[None, :]).astype(jnp.int32)
    csum = jnp.cumsum(oh, axis=0)
    rank = jnp.sum(oh * csum, axis=1) - 1
    counts = csum[-1]
    padded = (counts + tmb - 1) // tmb * tmb
    pad_end = jnp.cumsum(padded)
    dest = (pad_end - padded)[e_flat] + rank
    n_blk = -(-A // tmb) + N_EXPERTS
    n_slot = n_blk * tmb
    slot_tok = jnp.full((n_slot,), T, jnp.int32).at[dest].set(tok_flat)
    eidx = jnp.arange(N_EXPERTS, dtype=jnp.int32)
    blk = jnp.arange(n_blk, dtype=jnp.int32)
    blk_exp = jnp.minimum(jnp.sum((pad_end[None, :] <= (blk * tmb)[:, None]).astype(jnp.int32), axis=1),
                          N_EXPERTS - 1).astype(jnp.int32)
    n_used = (pad_end[-1:] // tmb).astype(jnp.int32)
    present = counts > 0
    first_e = jnp.min(jnp.where(present, eidx, N_EXPERTS))
    last_e = jnp.max(jnp.where(present, eidx, -1))
    later = jnp.where(present[None, :] & (eidx[None, :] > eidx[:, None]), eidx[None, :], N_EXPERTS)
    next_e = jnp.min(later, axis=1)
    next_e = jnp.where(next_e == N_EXPERTS, first_e, next_e).astype(jnp.int32)
    starts = ((blk == 0) | (blk_exp != jnp.roll(blk_exp, 1))) & (blk < n_used[0])
    tables = (blk_exp, n_used, starts.astype(jnp.int32), next_e[blk_exp],
              (blk_exp == last_e).astype(jnp.int32))
    xb = jnp.concatenate([hp, jnp.zeros((1, D_MODEL // 2), jnp.uint32)], 0)[slot_tok]
    act = moe_up(tables, xb, w1, w3, tmb=tmb, tn=1024)
    yb = moe_down(tables, act, w2, tmb=tmb, tn=512)
    d2 = dest.reshape(T, TOP_K)
    g2 = gate[:, :TOP_K]
    return x2d + g2[:, 0:1] * yb[d2[:, 0]] + g2[:, 1:2] * yb[d2[:, 1]]


def kernel(x, attn_norm_g, w_in, nsa_q_norm_g, nsa_k_norm_g, cmp_pe_k, cmp_pe_v, w_cmp_k, w_cmp_v, mla_qa_norm_g, w_q_b, mla_kva_norm_g, w_kv_b, mla_q_norm_g, mla_qr_norm_g, mla_k_norm_g, mla_kr_norm_g, w_proj_nsa, w_proj_mla, w_out, ffn_norm_g, w_ff1, w_ff3, w_ff2, w_router, w_e1, w_e3, w_e2):
    p = dict(attn_norm_g=attn_norm_g, w_in=w_in, nsa_q_norm_g=nsa_q_norm_g, nsa_k_norm_g=nsa_k_norm_g,
             cmp_pe_k=cmp_pe_k, cmp_pe_v=cmp_pe_v, w_cmp_k=w_cmp_k, w_cmp_v=w_cmp_v,
             mla_qa_norm_g=mla_qa_norm_g, w_q_b=w_q_b, mla_kva_norm_g=mla_kva_norm_g, w_kv_b=w_kv_b,
             mla_q_norm_g=mla_q_norm_g, mla_qr_norm_g=mla_qr_norm_g, mla_k_norm_g=mla_k_norm_g,
             mla_kr_norm_g=mla_kr_norm_g, w_proj_nsa=w_proj_nsa, w_proj_mla=w_proj_mla, w_out=w_out)
    B, S, D = x.shape
    depth = w_in.shape[0]
    nc = S // CMP_STRIDE
    tabs = dict(tok=_rope_tabs(jnp.arange(S), NSA_ROT, S),
                cmp=_rope_tabs(jnp.arange(nc - 1) * CMP_STRIDE + CMP_LEN - 1, NSA_ROT, nc),
                mla=_mla_tabs(S))
    x2d = x.reshape(B * S, D)
    for l in range(depth):
        x2d = _attention_block(x2d, B, S, p, l, tabs)
        if l % 2 == 0:
            x2d = _dense_ffn(x2d, ffn_norm_g[l], w_ff1[l // 2], w_ff3[l // 2], w_ff2[l // 2])
        else:
            x2d = _moe_ffn(x2d, ffn_norm_g[l], w_router[l // 2], w_e1[l // 2], w_e3[l // 2], w_e2[l // 2])
    return x2d.reshape(B, S, D)
```

```python
import functools

import numpy as np
import jax
import jax.numpy as jnp
from jax import lax
from jax.experimental import pallas as pl
from jax.experimental.pallas import tpu as pltpu

D_MODEL = 2048
HEAD_DIM = 128
NSA_HEADS = 8
NSA_KV_GROUPS = 2
NSA_HPG = NSA_HEADS // NSA_KV_GROUPS
NSA_ROT = HEAD_DIM // 4
CMP_LEN = 32
CMP_STRIDE = 16
SLC_LEN = 64
SLC_TOPK = 16
N_LOCAL_SLC = 2
WINDOW = 512
FORCE_SCORE = 1.0e4
MLA_HEADS = 8
Q_LORA = 768
KV_LORA = 512
QK_NOPE = 128
QK_ROPE = 64
V_HEAD = 128
ROPE_THETA = 500000.0
EPS = 1e-6
D_FF = 7168
N_EXPERTS = 8
TOP_K = 2

LANES = 128
MLA_QK_PAD = 256
NEG_INF = float("-inf")
LOG2E = 1.4426950408889634
BF16 = jnp.bfloat16
F32 = jnp.float32

OFF_QA = 0
OFF_G = 768
OFF_KR = 896
OFF_Q = 1024
OFF_KVA = 2048
OFF_KV = 2560
OFF_M = 4096
D_INP = 8192
NSA_KV_COLS = 3 * 2 * NSA_KV_GROUPS * HEAD_DIM

VMEM_LIMIT = 56 * 1024 * 1024


def _cparams(sem):
    return pltpu.CompilerParams(dimension_semantics=sem, vmem_limit_bytes=VMEM_LIMIT)


def _rms(x, g):
    ms = jnp.mean(x * x, axis=-1, keepdims=True)
    return x * lax.rsqrt(ms + EPS) * g


def _rope_lanes(y, ct, s1, s2, half):
    return y * ct + pltpu.roll(y, LANES - half, 1) * s1 + pltpu.roll(y, half, 1) * s2


def _sigmoid(x):
    return 1.0 / (1.0 + jnp.exp(-x))


def _nmm_kernel(x_ref, g_ref, w_ref, o_ref, h_scr):
    @pl.when(pl.program_id(1) == 0)
    def _():
        h_scr[...] = _rms(x_ref[...], g_ref[...]).astype(BF16)

    o_ref[...] = jnp.dot(h_scr[...], w_ref[...].astype(BF16),
                         preferred_element_type=F32).astype(o_ref.dtype)


def norm_matmul(x, g, w, *, tm, tn, out_dtype=F32):
    T, K = x.shape
    N = w.shape[1]
    return pl.pallas_call(
        _nmm_kernel,
        grid=(T // tm, N // tn),
        in_specs=[pl.BlockSpec((tm, K), lambda m, n: (m, 0)),
                  pl.BlockSpec((1, K), lambda m, n: (0, 0)),
                  pl.BlockSpec((K, tn), lambda m, n: (0, n))],
        out_specs=pl.BlockSpec((tm, tn), lambda m, n: (m, n)),
        out_shape=jax.ShapeDtypeStruct((T, N), out_dtype),
        scratch_shapes=[pltpu.VMEM((tm, K), BF16)],
        compiler_params=_cparams(("parallel", "arbitrary")),
        name="norm_matmul",
    )(x, g.reshape(1, K), w)


def _mmres_kernel(a_ref, w_ref, r_ref, o_ref):
    o_ref[...] = r_ref[...] + jnp.dot(a_ref[...], w_ref[...].astype(BF16), preferred_element_type=F32)


def matmul_residual(a, w, res, *, tm, tn):
    T, K = a.shape
    N = w.shape[1]
    return pl.pallas_call(
        _mmres_kernel,
        grid=(T // tm, N // tn),
        in_specs=[pl.BlockSpec((tm, K), lambda m, n: (m, 0)),
                  pl.BlockSpec((K, tn), lambda m, n: (0, n)),
                  pl.BlockSpec((tm, tn), lambda m, n: (m, n))],
        out_specs=pl.BlockSpec((tm, tn), lambda m, n: (m, n)),
        out_shape=jax.ShapeDtypeStruct((T, N), F32),
        compiler_params=_cparams(("parallel", "arbitrary")),
        name="matmul_residual",
    )(a, w, res)


def _prep_kernel(z_ref, ct_ref, s1_ref, s2_ref, gn_ref, qn_ref, kvp_ref, gate_ref):
    ct, s1, s2 = ct_ref[...], s1_ref[...], s2_ref[...]
    half = NSA_ROT // 2
    gq = gn_ref[0:1, :]
    for h in range(NSA_HEADS):
        c = OFF_Q + h * HEAD_DIM
        y = _rope_lanes(_rms(z_ref[:, c:c + HEAD_DIM], gq), ct, s1, s2, half)
        qn_ref[:, h * HEAD_DIM:(h + 1) * HEAD_DIM] = (y * (LOG2E * HEAD_DIM ** -0.5)).astype(BF16)
    for br in range(3):
        for kv in range(2):
            for g in range(NSA_KV_GROUPS):
                c = ((br * 2 + kv) * NSA_KV_GROUPS + g) * HEAD_DIM
                y = z_ref[:, OFF_KV + c:OFF_KV + c + HEAD_DIM]
                if kv == 0 and br > 0:
                    y = _rope_lanes(_rms(y, gn_ref[1 + br:2 + br, :]), ct, s1, s2, half)
                kvp_ref[:, c:c + HEAD_DIM] = y.astype(BF16)
    gate_ref[...] = _sigmoid(z_ref[:, OFF_G:OFF_G + LANES])


def nsa_prep(z, tabs, gn, S, *, tm):
    T = z.shape[0]
    ns = S // tm
    tab_spec = pl.BlockSpec((tm, LANES), lambda i: (i % ns, 0))
    return pl.pallas_call(
        _prep_kernel,
        grid=(T // tm,),
        in_specs=[pl.BlockSpec((tm, OFF_M), lambda i: (i, 0)), tab_spec, tab_spec, tab_spec,
                  pl.BlockSpec((8, LANES), lambda i: (0, 0))],
        out_specs=[pl.BlockSpec((tm, NSA_HEADS * HEAD_DIM), lambda i: (i, 0)),
                   pl.BlockSpec((tm, NSA_KV_COLS), lambda i: (i, 0)),
                   pl.BlockSpec((tm, LANES), lambda i: (i, 0))],
        out_shape=[jax.ShapeDtypeStruct((T, NSA_HEADS * HEAD_DIM), BF16),
                   jax.ShapeDtypeStruct((T, NSA_KV_COLS), BF16),
                   jax.ShapeDtypeStruct((T, LANES), F32)],
        compiler_params=_cparams(("parallel",)),
        name="nsa_prep",
    )(z, *tabs, gn)


def _cmp_kernel(xk_ref, xv_ref, wk_ref, wv_ref, pek_ref, pev_ref, gk_ref, ct_ref, s1_ref, s2_ref,
                kct_ref, vc_ref):
    nc = xk_ref.shape[0]
    row = lax.broadcasted_iota(jnp.int32, (nc, HEAD_DIM), 0)

    def comp(x_ref, w_ref, pe_ref):
        w = w_ref[...]
        y = jnp.dot(x_ref[...], w, preferred_element_type=F32)
        ype = jnp.dot(pe_ref[...], w, preferred_element_type=F32)
        bias = ype[0:1, :HEAD_DIM] + ype[1:2, HEAD_DIM:]
        out = y[:, :HEAD_DIM] + pltpu.roll(y[:, HEAD_DIM:], nc - 1, 0) + bias
        return jnp.where(row < nc - 1, out, 0.0)

    k = _rms(comp(xk_ref, wk_ref, pek_ref), gk_ref[...])
    k = _rope_lanes(k, ct_ref[...], s1_ref[...], s2_ref[...], NSA_ROT // 2)
    kct_ref[...] = k.T.astype(BF16)
    vc_ref[...] = comp(xv_ref, wv_ref, pev_ref).astype(BF16)


def nsa_compress(x2, wk2, wv2, pek2, pev2, gk, tabs_cmp):
    B, _, nc, kk = x2.shape
    G = NSA_KV_GROUPS
    full = lambda shape: pl.BlockSpec(shape, lambda b, g: (0,) * len(shape))
    return pl.pallas_call(
        _cmp_kernel,
        grid=(B, G),
        in_specs=[pl.BlockSpec((None, None, nc, kk), lambda b, g: (b, g, 0, 0)),
                  pl.BlockSpec((None, None, nc, kk), lambda b, g: (b, G + g, 0, 0)),
                  full((kk, 2 * HEAD_DIM)), full((kk, 2 * HEAD_DIM)),
                  full((8, kk)), full((8, kk)), full((1, HEAD_DIM)),
                  full((nc, LANES)), full((nc, LANES)), full((nc, LANES))],
        out_specs=[pl.BlockSpec((None, None, HEAD_DIM, nc), lambda b, g: (b, g, 0, 0)),
                   pl.BlockSpec((None, None, nc, HEAD_DIM), lambda b, g: (b, g, 0, 0))],
        out_shape=[jax.ShapeDtypeStruct((B, G, HEAD_DIM, nc), BF16),
                   jax.ShapeDtypeStruct((B, G, nc, HEAD_DIM), BF16)],
        compiler_params=_cparams(("parallel", "parallel")),
        name="nsa_compress",
    )(x2, x2, wk2, wv2, pek2, pev2, gk, *tabs_cmp)


def _cattn_kernel(q_ref, kct_ref, vc_ref, ov_ref, oc_ref, sel_ref, *, tq):
    nc = vc_ref.shape[0]
    ns = sel_ref.shape[0]
    t0 = pl.program_id(2) * tq
    t_pos = t0 + lax.broadcasted_iota(jnp.int32, (tq, nc), 0)
    n_idx = lax.broadcasted_iota(jnp.int32, (tq, nc), 1)
    vis = (t_pos >= n_idx * CMP_STRIDE + (CMP_LEN - 1)) & (n_idx < nc - 1)
    bias = jnp.where(vis, 0.0, NEG_INF)
    kct = kct_ref[...]
    vc = vc_ref[...]
    psum = jnp.zeros((tq, nc), F32)
    for hh in range(NSA_HPG):
        s = jnp.dot(q_ref[:, hh * HEAD_DIM:(hh + 1) * HEAD_DIM], kct, preferred_element_type=F32) + bias
        m = jnp.max(s, axis=-1, keepdims=True)
        m = jnp.where(m == NEG_INF, 0.0, m)
        p = jnp.exp2(s - m)
        den = jnp.sum(p, axis=-1, keepdims=True)
        p = p * (1.0 / jnp.where(den > 0.0, den, 1.0))
        oc_ref[:, hh * HEAD_DIM:(hh + 1) * HEAD_DIM] = jnp.dot(
            p.astype(BF16), vc, preferred_element_type=F32).astype(oc_ref.dtype)
        psum = psum + p
    pt = psum.T
    hi = pt.astype(BF16)
    lo = (pt - hi.astype(F32)).astype(BF16)
    ov = ov_ref[...]
    imp = jnp.dot(ov, hi, preferred_element_type=F32) + jnp.dot(ov, lo, preferred_element_type=F32)
    blk = lax.broadcasted_iota(jnp.int32, (ns, tq), 0)
    cur = (t0 + lax.broadcasted_iota(jnp.int32, (ns, tq), 1)) // SLC_LEN
    forced = (blk == 0) | ((blk <= cur) & (blk > cur - N_LOCAL_SLC))
    val = jnp.where(blk > cur, NEG_INF, jnp.where(forced, FORCE_SCORE, imp))
    rank = jnp.zeros((ns, tq), F32)
    for i in range(ns):
        other = val[i:i + 1, :]
        ahead = (other > val) | ((other == val) & (blk > i))
        rank = rank + jnp.where(ahead, 1.0, 0.0)
    sel_ref[...] = jnp.where(rank < float(min(SLC_TOPK, ns)), 1.0, 0.0).astype(sel_ref.dtype)


def nsa_cmp_attention(qn, kct, vc, ov, B, S, *, tq):
    G = NSA_KV_GROUPS
    nq = S // tq
    nc = vc.shape[2]
    ns = S // SLC_LEN
    gw = NSA_HPG * HEAD_DIM
    return pl.pallas_call(
        functools.partial(_cattn_kernel, tq=tq),
        grid=(B, G, nq),
        in_specs=[pl.BlockSpec((tq, gw), lambda b, g, i: (b * nq + i, g)),
                  pl.BlockSpec((None, None, HEAD_DIM, nc), lambda b, g, i: (b, g, 0, 0)),
                  pl.BlockSpec((None, None, nc, HEAD_DIM), lambda b, g, i: (b, g, 0, 0)),
                  pl.BlockSpec((ns, nc), lambda b, g, i: (0, 0))],
        out_specs=[pl.BlockSpec((tq, gw), lambda b, g, i: (b * nq + i, g)),
                   pl.BlockSpec((None, None, ns, tq), lambda b, g, i: (b, g, 0, i))],
        out_shape=[jax.ShapeDtypeStruct((B * S, NSA_HEADS * HEAD_DIM), BF16),
                   jax.ShapeDtypeStruct((B, G, ns, S), BF16)],
        compiler_params=_cparams(("parallel", "parallel", "parallel")),
        name="nsa_cmp_attention",
    )(qn, kct, vc, ov)


def _flash_kernel(*refs, mode, hg, tq, tk, dqk, dv):
    if mode == "sel":
        q_ref, k_ref, vt_ref, sel_ref, ex_ref, o_ref, qs, m_s, l_s, acc_s = refs
    else:
        q_ref, k_ref, vt_ref, o_ref, qs, m_s, l_s, acc_s = refs
    rows = hg * tq
    q0 = pl.program_id(2) * tq
    cd = q0 // tk
    for hh in range(hg):
        qs[hh * tq:(hh + 1) * tq, :] = q_ref[:, hh * dqk:(hh + 1) * dqk]

    def scores(c, kind):
        start = pl.multiple_of(c * tk, tk)
        s = lax.dot_general(k_ref[pl.ds(start, tk), :], qs[...], (((1,), (1,)), ((), ())),
                            preferred_element_type=F32)
        ok = None
        if kind != "full":
            k_pos = start + lax.broadcasted_iota(jnp.int32, (tk, tq), 0)
            q_pos = q0 + lax.broadcasted_iota(jnp.int32, (tk, tq), 1)
            ok = (k_pos <= q_pos) if kind == "diag" else (q_pos - k_pos < WINDOW)
        if mode == "sel":
            chosen = jnp.dot(ex_ref[c], sel_ref[...], preferred_element_type=F32) > 0.5
            ok = chosen if ok is None else (ok & chosen)
        if ok is not None:
            bias = jnp.where(ok, 0.0, NEG_INF)
            s = s + (jnp.concatenate([bias] * hg, axis=1) if hg > 1 else bias)
        return s

    def update(c, s, carry):
        m, l, acc = carry
        m_new = jnp.maximum(m, jnp.max(s, axis=0, keepdims=True))
        p = jnp.exp2(s - m_new)
        alpha = jnp.exp2(m - m_new)
        l = alpha * l + jnp.sum(p, axis=0, keepdims=True)
        acc = alpha * acc + jnp.dot(vt_ref[c], p.astype(BF16), preferred_element_type=F32)
        return m_new, l, acc

    def step(c, carry, kind):
        return update(c, scores(c, kind), carry)

    def load():
        return m_s[...], l_s[...], acc_s[...]

    def store(carry):
        m_s[...], l_s[...], acc_s[...] = carry

    def run(first, count, kind, carry):
        s_next = scores(first, kind)
        for j in range(count):
            s_cur = s_next
            if j + 1 < count:
                s_next = scores(first + j + 1, kind)
            carry = update(first + j, s_cur, carry)
        return carry

    store(run(cd, max(tq // tk, 1), "diag",
              (jnp.full((1, rows), NEG_INF, F32), jnp.zeros((1, rows), F32), jnp.zeros((dv, rows), F32))))

    if mode == "win":
        @pl.when(cd >= 1)
        def _():
            store(step(cd - 1, load(), "full"))

        @pl.when(cd >= 2)
        def _():
            store(step(cd - 2, load(), "far"))
    else:
        def group(first, count):
            store(run(first, count, "full", load()))

        def quad(i, _):
            group(4 * i, 4)
            return 0

        lax.fori_loop(0, cd // 4, quad, 0)

        @pl.when(cd % 4 >= 2)
        def _():
            group((cd // 4) * 4, 2)

        @pl.when(cd % 2 == 1)
        def _():
            group(cd - 1, 1)

    o = acc_s[...] * (1.0 / l_s[...])
    for hh in range(hg):
        o_ref[:, hh * dv:(hh + 1) * dv] = o[:, hh * tq:(hh + 1) * tq].T.astype(o_ref.dtype)


def flash_attention(q, k, vt, B, S, *, mode, n_groups, hg, dqk, dv, kcol0, tq, tk, sel_t=None, expand=None):
    nq = S // tq
    nch = S // tk
    assert (tk % tq == 0 or tq % tk == 0) and S % tk == 0 and S % tq == 0 and tq % LANES == 0
    if mode == "win":
        assert tq == tk and WINDOW == 2 * tk
    in_specs = [pl.BlockSpec((tq, hg * dqk), lambda b, g, i: (b * nq + i, g)),
                pl.BlockSpec((S, dqk), lambda b, g, i: (b, kcol0 + g)),
                pl.BlockSpec((None, None, nch, dv, tk), lambda b, g, i: (b, g, 0, 0, 0))]
    args = [q, k, vt]
    if mode == "sel":
        ns = S // SLC_LEN
        in_specs += [pl.BlockSpec((None, None, ns, tq), lambda b, g, i: (b, g, 0, i)),
                     pl.BlockSpec((nch, tk, ns), lambda b, g, i: (0, 0, 0))]
        args += [sel_t, expand]
    return pl.pallas_call(
        functools.partial(_flash_kernel, mode=mode, hg=hg, tq=tq, tk=tk, dqk=dqk, dv=dv),
        grid=(B, n_groups, nq),
        in_specs=in_specs,
        out_specs=pl.BlockSpec((tq, hg * dv), lambda b, g, i: (b * nq + i, g)),
        out_shape=jax.ShapeDtypeStruct((B * S, n_groups * hg * dv), BF16),
        scratch_shapes=[pltpu.VMEM((hg * tq, dqk), BF16),
                        pltpu.VMEM((1, hg * tq), F32),
                        pltpu.VMEM((1, hg * tq), F32),
                        pltpu.VMEM((dv, hg * tq), F32)],
        compiler_params=_cparams(("parallel", "parallel", "arbitrary")),
        name="flash_" + mode,
    )(*args)


def _mla_q_kernel(z_ref, ga_ref, w_ref, gh_ref, ct_ref, s1_ref, s2_ref, o_ref):
    h = _rms(z_ref[...], ga_ref[...]).astype(BF16)
    y = jnp.dot(h, w_ref[...], preferred_element_type=F32)
    ct, s1, s2 = ct_ref[...], s1_ref[...], s2_ref[...]
    scale = LOG2E * (QK_NOPE + QK_ROPE) ** -0.5
    for hd in range(MLA_HEADS):
        c = hd * MLA_QK_PAD
        nope = _rms(y[:, c:c + QK_NOPE], gh_ref[0:1, :])
        r = y[:, c + QK_NOPE:c + MLA_QK_PAD]
        ms = jnp.sum(r * r, axis=-1, keepdims=True) * (1.0 / QK_ROPE)
        r = _rope_lanes(r * lax.rsqrt(ms + EPS) * gh_ref[1:2, :], ct, s1, s2, QK_ROPE // 2)
        o_ref[:, c:c + QK_NOPE] = (nope * scale).astype(BF16)
        o_ref[:, c + QK_NOPE:c + MLA_QK_PAD] = (r * scale).astype(BF16)


def mla_q_proj(z, ga, wq, gh, tabs, S, *, tm):
    T = z.shape[0]
    ns = S // tm
    tab_spec = pl.BlockSpec((tm, LANES), lambda i: (i % ns, 0))
    nout = MLA_HEADS * MLA_QK_PAD
    return pl.pallas_call(
        _mla_q_kernel,
        grid=(T // tm,),
        in_specs=[pl.BlockSpec((tm, Q_LORA), lambda i: (i, OFF_QA // Q_LORA)),
                  pl.BlockSpec((1, Q_LORA), lambda i: (0, 0)),
                  pl.BlockSpec((Q_LORA, nout), lambda i: (0, 0)),
                  pl.BlockSpec((8, LANES), lambda i: (0, 0)),
                  tab_spec, tab_spec, tab_spec],
        out_specs=pl.BlockSpec((tm, nout), lambda i: (i, 0)),
        out_shape=jax.ShapeDtypeStruct((T, nout), BF16),
        compiler_params=_cparams(("parallel",)),
        name="mla_q_proj",
    )(z, ga, wq, gh, *tabs)


def _mla_kv_kernel(z_ref, zr_ref, ga_ref, w_ref, gh_ref, ct_ref, s1_ref, s2_ref, k_ref, v_ref):
    h = _rms(z_ref[...], ga_ref[...]).astype(BF16)
    y = jnp.dot(h, w_ref[...], preferred_element_type=F32)
    r = zr_ref[...]
    ms = jnp.sum(r * r, axis=-1, keepdims=True) * (1.0 / QK_ROPE)
    r = _rope_lanes(r * lax.rsqrt(ms + EPS) * gh_ref[1:2, :], ct_ref[...], s1_ref[...], s2_ref[...],
                    QK_ROPE // 2).astype(BF16)
    nv = MLA_HEADS * QK_NOPE
    for hd in range(MLA_HEADS):
        c = hd * MLA_QK_PAD
        k_ref[:, c:c + QK_NOPE] = _rms(y[:, hd * QK_NOPE:(hd + 1) * QK_NOPE], gh_ref[0:1, :]).astype(BF16)
        k_ref[:, c + QK_NOPE:c + MLA_QK_PAD] = r
    v_ref[...] = y[:, nv:].astype(BF16)


def mla_kv_proj(z, ga, wkv, gh, tabs, S, *, tm):
    T = z.shape[0]
    ns = S // tm
    tab_spec = pl.BlockSpec((tm, LANES), lambda i: (i % ns, 0))
    nk = MLA_HEADS * MLA_QK_PAD
    nv = MLA_HEADS * V_HEAD
    return pl.pallas_call(
        _mla_kv_kernel,
        grid=(T // tm,),
        in_specs=[pl.BlockSpec((tm, KV_LORA), lambda i: (i, OFF_KVA // KV_LORA)),
                  pl.BlockSpec((tm, LANES), lambda i: (i, OFF_KR // LANES)),
                  pl.BlockSpec((1, KV_LORA), lambda i: (0, 0)),
                  pl.BlockSpec((KV_LORA, MLA_HEADS * (QK_NOPE + V_HEAD)), lambda i: (0, 0)),
                  pl.BlockSpec((8, LANES), lambda i: (0, 0)),
                  tab_spec, tab_spec, tab_spec],
        out_specs=[pl.BlockSpec((tm, nk), lambda i: (i, 0)),
                   pl.BlockSpec((tm, nv), lambda i: (i, 0))],
        out_shape=[jax.ShapeDtypeStruct((T, nk), BF16),
                   jax.ShapeDtypeStruct((T, nv), BF16)],
        compiler_params=_cparams(("parallel",)),
        name="mla_kv_proj",
    )(z, z, ga, wkv, gh, *tabs)


def _mix_kernel(oc_ref, os_ref, ow_ref, ob_ref, gate_ref, wa_ref, wb_ref, za_ref, zb_ref, o_ref, a_scr):
    @pl.when(pl.program_id(1) == 0)
    def _():
        for h in range(NSA_HEADS):
            sl = slice(h * HEAD_DIM, (h + 1) * HEAD_DIM)
            a = (gate_ref[:, h:h + 1] * oc_ref[:, sl].astype(F32)
                 + gate_ref[:, NSA_HEADS + h:NSA_HEADS + h + 1] * os_ref[:, sl].astype(F32)
                 + gate_ref[:, 2 * NSA_HEADS + h:2 * NSA_HEADS + h + 1] * ow_ref[:, sl].astype(F32))
            a_scr[:, sl] = a.astype(BF16)

    pa = jnp.dot(a_scr[...], wa_ref[...].astype(BF16), preferred_element_type=F32)
    pb = jnp.dot(ob_ref[...], wb_ref[...].astype(BF16), preferred_element_type=F32)
    o_ref[...] = (_sigmoid(za_ref[...]) * pa + _sigmoid(zb_ref[...]) * pb).astype(o_ref.dtype)


def gated_mix(oc, os_, ow, ob, gates, wa, wb, z, *, tm, tn):
    T, K = oc.shape
    N = wa.shape[1]
    row = lambda w: pl.BlockSpec((tm, w), lambda m, n: (m, 0))
    return pl.pallas_call(
        _mix_kernel,
        grid=(T // tm, N // tn),
        in_specs=[row(K), row(K), row(K), row(K), row(LANES),
                  pl.BlockSpec((K, tn), lambda m, n: (0, n)),
                  pl.BlockSpec((K, tn), lambda m, n: (0, n)),
                  pl.BlockSpec((tm, tn), lambda m, n: (m, OFF_M // tn + n)),
                  pl.BlockSpec((tm, tn), lambda m, n: (m, (OFF_M + D_MODEL) // tn + n))],
        out_specs=pl.BlockSpec((tm, tn), lambda m, n: (m, n)),
        out_shape=jax.ShapeDtypeStruct((T, N), BF16),
        scratch_shapes=[pltpu.VMEM((tm, K), BF16)],
        compiler_params=_cparams(("parallel", "arbitrary")),
        name="gated_mix",
    )(oc, os_, ow, ob, gates, wa, wb, z, z)


def _ffn_up_kernel(x_ref, g_ref, w1_ref, w3_ref, o_ref, h_scr):
    @pl.when(pl.program_id(1) == 0)
    def _():
        h_scr[...] = _rms(x_ref[...], g_ref[...]).astype(BF16)

    h = h_scr[...]
    a = jnp.dot(h, w1_ref[...].astype(BF16), preferred_element_type=F32)
    b = jnp.dot(h, w3_ref[...].astype(BF16), preferred_element_type=F32)
    o_ref[...] = (a * _sigmoid(a) * b).astype(o_ref.dtype)


def ffn_up(x, g, w1, w3, *, tm, tn):
    T, K = x.shape
    N = w1.shape[1]
    return pl.pallas_call(
        _ffn_up_kernel,
        grid=(T // tm, N // tn),
        in_specs=[pl.BlockSpec((tm, K), lambda m, n: (m, 0)),
                  pl.BlockSpec((1, K), lambda m, n: (0, 0)),
                  pl.BlockSpec((K, tn), lambda m, n: (0, n)),
                  pl.BlockSpec((K, tn), lambda m, n: (0, n))],
        out_specs=pl.BlockSpec((tm, tn), lambda m, n: (m, n)),
        out_shape=jax.ShapeDtypeStruct((T, N), BF16),
        scratch_shapes=[pltpu.VMEM((tm, K), BF16)],
        compiler_params=_cparams(("parallel", "arbitrary")),
        name="ffn_up",
    )(x, g.reshape(1, K), w1, w3)


def _pack_bf16_pairs(h):
    k = h.shape[1] // 2
    hi = lax.bitcast_convert_type(h[:, :k].astype(jnp.bfloat16).astype(F32), jnp.uint32)
    lo = lax.bitcast_convert_type(h[:, k:].astype(jnp.bfloat16).astype(F32), jnp.uint32)
    return hi | (lo >> 16)


def _unpack_bf16_pairs(xp):
    hi = lax.bitcast_convert_type(xp & jnp.uint32(0xFFFF0000), F32)
    lo = lax.bitcast_convert_type(xp << 16, F32)
    return hi.astype(BF16), lo.astype(BF16)


def _router_kernel(x_ref, g_ref, wr_ref, h_ref, idx_ref, gate_ref):
    h = _rms(x_ref[...], g_ref[...])
    h_ref[...] = _pack_bf16_pairs(h)
    logits = jnp.dot(h, wr_ref[...], preferred_element_type=F32, precision=lax.Precision.HIGHEST)
    lane = lax.broadcasted_iota(jnp.int32, logits.shape, 1).astype(F32)
    logits = jnp.where(lane < float(N_EXPERTS), logits, NEG_INF)
    m1 = jnp.max(logits, axis=-1, keepdims=True)
    i1 = jnp.min(jnp.where(logits == m1, lane, float(LANES)), axis=-1, keepdims=True)
    rest = jnp.where(lane == i1, NEG_INF, logits)
    m2 = jnp.max(rest, axis=-1, keepdims=True)
    i2 = jnp.min(jnp.where(rest == m2, lane, float(LANES)), axis=-1, keepdims=True)
    e = jnp.exp(m2 - m1)
    den = 1.0 + e
    idx_ref[...] = jnp.where(lane == 0.0, i1, jnp.where(lane == 1.0, i2, 0.0)).astype(jnp.int32)
    gate_ref[...] = jnp.where(lane == 0.0, 1.0 / den, jnp.where(lane == 1.0, e / den, 0.0))


def router(x, g, wr_pad, *, tm):
    T, K = x.shape
    return pl.pallas_call(
        _router_kernel,
        grid=(T // tm,),
        in_specs=[pl.BlockSpec((tm, K), lambda i: (i, 0)),
                  pl.BlockSpec((1, K), lambda i: (0, 0)),
                  pl.BlockSpec((K, LANES), lambda i: (0, 0))],
        out_specs=[pl.BlockSpec((tm, K // 2), lambda i: (i, 0)),
                   pl.BlockSpec((tm, LANES), lambda i: (i, 0)),
                   pl.BlockSpec((tm, LANES), lambda i: (i, 0))],
        out_shape=[jax.ShapeDtypeStruct((T, K // 2), jnp.uint32),
                   jax.ShapeDtypeStruct((T, LANES), jnp.int32),
                   jax.ShapeDtypeStruct((T, LANES), F32)],
        compiler_params=_cparams(("parallel",)),
        name="router",
    )(x, g.reshape(1, K), wr_pad)


def _moe_weight_stream(w_hbms, stages, casts, sems, be_ref, nu_ref, gs_ref, ne_ref, lg_ref, tn):
    n = pl.program_id(0)
    r = pl.program_id(1)
    used = r < nu_ref[0]

    def copies(e, nt):
        c0 = pl.multiple_of(nt * tn, tn)
        return [pltpu.make_async_copy(w.at[e, :, pl.ds(c0, tn)], st, sems.at[i])
                for i, (w, st) in enumerate(zip(w_hbms, stages))]

    @pl.when((n == 0) & (r == 0))
    def _():
        for c in copies(be_ref[0], 0):
            c.start()

    @pl.when(used & (gs_ref[r] == 1))
    def _():
        for c in copies(be_ref[r], n):
            c.wait()
        for st, wb in zip(stages, casts):
            wb[...] = st[...].astype(BF16)
        last = lg_ref[r] == 1

        @pl.when(jnp.logical_not(last & (n == pl.num_programs(0) - 1)))
        def _():
            for c in copies(ne_ref[r], n + last.astype(jnp.int32)):
                c.start()

    return used


def _moe_up_kernel(be_ref, nu_ref, gs_ref, ne_ref, lg_ref, x_ref, w1_hbm, w3_hbm, o_ref,
                   st1, st3, w1_s, w3_s, sems, *, tn):
    used = _moe_weight_stream((w1_hbm, w3_hbm), (st1, st3), (w1_s, w3_s), sems,
                              be_ref, nu_ref, gs_ref, ne_ref, lg_ref, tn)

    @pl.when(used)
    def _():
        xa, xb = _unpack_bf16_pairs(x_ref[...])
        k2 = xa.shape[1]
        a = (jnp.dot(xa, w1_s[:k2, :], preferred_element_type=F32)
             + jnp.dot(xb, w1_s[k2:, :], preferred_element_type=F32))
        b = (jnp.dot(xa, w3_s[:k2, :], preferred_element_type=F32)
             + jnp.dot(xb, w3_s[k2:, :], preferred_element_type=F32))
        o_ref[...] = (a * _sigmoid(a) * b).astype(o_ref.dtype)

    @pl.when(jnp.logical_not(used))
    def _():
        o_ref[...] = jnp.zeros(o_ref.shape, o_ref.dtype)


def moe_up(tables, xb, w1, w3, *, tmb, tn):
    n_slot = xb.shape[0]
    K, N = w1.shape[1], w1.shape[2]
    row = lambda r, nu: jnp.minimum(r, nu[0] - 1)
    return pl.pallas_call(
        functools.partial(_moe_up_kernel, tn=tn),
        grid_spec=pltpu.PrefetchScalarGridSpec(
            num_scalar_prefetch=5,
            grid=(N // tn, n_slot // tmb),
            in_specs=[pl.BlockSpec((tmb, K // 2), lambda n, r, be, nu, gs, ne, lg: (row(r, nu), 0)),
                      pl.BlockSpec(memory_space=pl.ANY), pl.BlockSpec(memory_space=pl.ANY)],
            out_specs=pl.BlockSpec((tmb, tn), lambda n, r, be, nu, gs, ne, lg: (r, n)),
            scratch_shapes=[pltpu.VMEM((K, tn), F32), pltpu.VMEM((K, tn), F32),
                            pltpu.VMEM((K, tn), BF16), pltpu.VMEM((K, tn), BF16),
                            pltpu.SemaphoreType.DMA((2,))]),
        out_shape=jax.ShapeDtypeStruct((n_slot, N), BF16),
        compiler_params=_cparams(("arbitrary", "arbitrary")),
        name="moe_up",
    )(*tables, xb, w1, w3)


def _moe_down_kernel(be_ref, nu_ref, gs_ref, ne_ref, lg_ref, a_ref, w2_hbm, o_ref, st2, w2_s, sems, *, tn):
    used = _moe_weight_stream((w2_hbm,), (st2,), (w2_s,), sems, be_ref, nu_ref, gs_ref, ne_ref, lg_ref, tn)

    @pl.when(used)
    def _():
        o_ref[...] = jnp.dot(a_ref[...], w2_s[...], preferred_element_type=F32)

    @pl.when(jnp.logical_not(used))
    def _():
        o_ref[...] = jnp.zeros(o_ref.shape, o_ref.dtype)


def moe_down(tables, act, w2, *, tmb, tn):
    n_slot, K = act.shape
    N = w2.shape[2]
    row = lambda r, nu: jnp.minimum(r, nu[0] - 1)
    return pl.pallas_call(
        functools.partial(_moe_down_kernel, tn=tn),
        grid_spec=pltpu.PrefetchScalarGridSpec(
            num_scalar_prefetch=5,
            grid=(N // tn, n_slot // tmb),
            in_specs=[pl.BlockSpec((tmb, K), lambda n, r, be, nu, gs, ne, lg: (row(r, nu), 0)),
                      pl.BlockSpec(memory_space=pl.ANY)],
            out_specs=pl.BlockSpec((tmb, tn), lambda n, r, be, nu, gs, ne, lg: (r, n)),
            scratch_shapes=[pltpu.VMEM((K, tn), F32), pltpu.VMEM((K, tn), BF16),
                            pltpu.SemaphoreType.DMA((1,))]),
        out_shape=jax.ShapeDtypeStruct((n_slot, N), F32),
        compiler_params=_cparams(("arbitrary", "arbitrary")),
        name="moe_down",
    )(*tables, act, w2)


def _rope_tabs(pos, rot_dim, n_rows):
    half = rot_dim // 2
    inv = 1.0 / (ROPE_THETA ** (jnp.arange(0, rot_dim, 2, dtype=F32) / rot_dim))
    ang = jnp.asarray(pos).astype(F32)[:, None] * inv[None, :]
    c, s = jnp.cos(ang), jnp.sin(ang)
    z = jnp.zeros_like(c)
    pad = lambda a, fill: jnp.pad(a, ((0, n_rows - a.shape[0]), (0, LANES - a.shape[1])), constant_values=fill)
    return pad(jnp.concatenate([c, c], 1), 1.0), pad(jnp.concatenate([-s, z], 1), 0.0), \
        pad(jnp.concatenate([z, s], 1), 0.0)


def _mla_tabs(S):
    ct, s1, s2 = _rope_tabs(jnp.arange(S), QK_ROPE, S)
    lane = jnp.arange(LANES)[None, :]
    return jnp.where(lane < QK_ROPE, ct, 0.0), s1, s2


def _pad_cols(w, n):
    return jnp.pad(w, ((0, 0), (0, n - w.shape[1])))


def _layout_w_in(w):
    sp = np.cumsum([0, NSA_HEADS * HEAD_DIM, NSA_KV_COLS, 3 * NSA_HEADS, Q_LORA, KV_LORA, QK_ROPE, 2 * D_MODEL])
    q, kv, g, qa, kva, kr, m = [w[:, sp[i]:sp[i + 1]] for i in range(7)]
    return jnp.concatenate([qa, _pad_cols(g, LANES), _pad_cols(kr, LANES), q, kva, kv, m], axis=1).astype(BF16)


def _layout_w_q_b(w):
    w = w.reshape(Q_LORA, MLA_HEADS, QK_NOPE + QK_ROPE)
    w = jnp.pad(w, ((0, 0), (0, 0), (0, MLA_QK_PAD - QK_NOPE - QK_ROPE)))
    return w.reshape(Q_LORA, MLA_HEADS * MLA_QK_PAD).astype(BF16)


def _layout_w_kv_b(w):
    w = w.reshape(KV_LORA, MLA_HEADS, QK_NOPE + V_HEAD)
    return jnp.concatenate([w[:, :, :QK_NOPE].reshape(KV_LORA, -1), w[:, :, QK_NOPE:].reshape(KV_LORA, -1)],
                           axis=1).astype(BF16)


def _layout_w_cmp(w):
    h = CMP_LEN // 2
    return jnp.concatenate([w[:h].reshape(h * HEAD_DIM, HEAD_DIM), w[h:].reshape(h * HEAD_DIM, HEAD_DIM)],
                           axis=1).astype(BF16)


def _layout_pe(pe):
    return jnp.pad(pe.reshape(2, (CMP_LEN // 2) * HEAD_DIM), ((0, 6), (0, 0))).astype(BF16)


def _overlap(nc, ns):
    n = np.arange(nc)[None, :] * CMP_STRIDE
    j = np.arange(ns)[:, None] * SLC_LEN
    ov = (n <= j + SLC_LEN - 1) & (j <= n + CMP_LEN - 1) & (np.arange(nc)[None, :] < nc - 1)
    return jnp.asarray(ov.astype(np.float32), BF16)


def _expand(ns, S, tk):
    e = ((np.arange(S // tk)[:, None, None] * tk + np.arange(tk)[None, :, None]) // SLC_LEN
         == np.arange(ns)[None, None, :])
    return jnp.asarray(e.astype(np.float32), BF16)


def _chunk_transpose(v, B, S, n_heads, dv, tk):
    return v.reshape(B, S // tk, tk, n_heads, dv).transpose(0, 3, 1, 4, 2)


def _attention_block(x2d, B, S, p, l, tabs):
    T = x2d.shape[0]
    z = norm_matmul(x2d, p['attn_norm_g'][l], _layout_w_in(p['w_in'][l]), tm=min(1024, T), tn=512)
    gn = jnp.concatenate([p['nsa_q_norm_g'][l][None], p['nsa_k_norm_g'][l], jnp.zeros((4, HEAD_DIM), F32)], 0)
    qn, kvp, gates = nsa_prep(z, tabs['tok'], gn, S, tm=256)

    nc = S // CMP_STRIDE
    x2 = kvp[:, :2 * NSA_KV_GROUPS * HEAD_DIM].reshape(B, S, 2 * NSA_KV_GROUPS, HEAD_DIM)
    x2 = x2.transpose(0, 2, 1, 3).reshape(B, 2 * NSA_KV_GROUPS, nc, CMP_STRIDE * HEAD_DIM)
    kct, vc = nsa_compress(x2, _layout_w_cmp(p['w_cmp_k'][l]), _layout_w_cmp(p['w_cmp_v'][l]),
                           _layout_pe(p['cmp_pe_k'][l]), _layout_pe(p['cmp_pe_v'][l]),
                           p['nsa_k_norm_g'][l][0:1], tabs['cmp'])
    ns = S // SLC_LEN
    oc, sel_t = nsa_cmp_attention(qn, kct, vc, _overlap(nc, ns), B, S, tq=min(256, S))
    nsa = dict(n_groups=NSA_KV_GROUPS, hg=NSA_HPG, dqk=HEAD_DIM, dv=HEAD_DIM)
    tks, tkw = min(512, S), WINDOW // 2
    gv = NSA_KV_GROUPS * HEAD_DIM
    vts = _chunk_transpose(kvp[:, 3 * gv:4 * gv], B, S, NSA_KV_GROUPS, HEAD_DIM, tks)
    vtw = _chunk_transpose(kvp[:, 5 * gv:6 * gv], B, S, NSA_KV_GROUPS, HEAD_DIM, tkw)
    os_ = flash_attention(qn, kvp, vts, B, S, mode="sel", kcol0=4, tq=256, tk=tks,
                          sel_t=sel_t, expand=_expand(ns, S, tks), **nsa)
    ow = flash_attention(qn, kvp, vtw, B, S, mode="win", kcol0=8, tq=tkw, tk=tkw, **nsa)

    gh_q = jnp.concatenate([p['mla_q_norm_g'][l][None], _pad_cols(p['mla_qr_norm_g'][l][None], LANES),
                            jnp.zeros((6, LANES), F32)], 0)
    gh_k = jnp.concatenate([p['mla_k_norm_g'][l][None], _pad_cols(p['mla_kr_norm_g'][l][None], LANES),
                            jnp.zeros((6, LANES), F32)], 0)
    qm = mla_q_proj(z, p['mla_qa_norm_g'][l][None], _layout_w_q_b(p['w_q_b'][l]), gh_q, tabs['mla'], S, tm=256)
    km, vm = mla_kv_proj(z, p['mla_kva_norm_g'][l][None], _layout_w_kv_b(p['w_kv_b'][l]), gh_k, tabs['mla'],
                         S, tm=256)
    tkm = min(512, S)
    ob = flash_attention(qm, km, _chunk_transpose(vm, B, S, MLA_HEADS, V_HEAD, tkm), B, S, mode="causal",
                         n_groups=MLA_HEADS, hg=1, dqk=MLA_QK_PAD, dv=V_HEAD, kcol0=0, tq=min(1024, S), tk=tkm)

    tm = min(1024, T)
    mix = gated_mix(oc, os_, ow, ob, gates, p['w_proj_nsa'][l], p['w_proj_mla'][l], z, tm=tm, tn=512)
    return matmul_residual(mix, p['w_out'][l], x2d, tm=min(2048, T), tn=512)


def _dense_ffn(x2d, g, w1, w3, w2):
    T = x2d.shape[0]
    act = ffn_up(x2d, g, w1.astype(BF16), w3.astype(BF16), tm=min(1024, T), tn=512)
    return matmul_residual(act, w2.astype(BF16), x2d, tm=min(1024, T), tn=256)


def _moe_ffn(x2d, g, w_router, w1, w3, w2, *, tmb=512):
    T = x2d.shape[0]
    hp, idx, gate = router(x2d, g, _pad_cols(w_router, LANES), tm=256)
    A = T * TOP_K
    e_flat = idx[:, :TOP_K].reshape(A)
    tok_flat = jnp.repeat(jnp.arange(T, dtype=jnp.int32), TOP_K)
    oh = (e_flat[:, None] == jnp.arange(N_EXPERTS)[None, :]).astype(jnp.int32)
    csum = jnp.cumsum(oh, axis=0)
    rank = jnp.sum(oh * csum, axis=1) - 1
    counts = csum[-1]
    padded = (counts + tmb - 1) // tmb * tmb
    pad_end = jnp.cumsum(padded)
    dest = (pad_end - padded)[e_flat] + rank
    n_blk = -(-A // tmb) + N_EXPERTS
    n_slot = n_blk * tmb
    slot_tok = jnp.full((n_slot,), T, jnp.int32).at[dest].set(tok_flat)
    eidx = jnp.arange(N_EXPERTS, dtype=jnp.int32)
    blk = jnp.arange(n_blk, dtype=jnp.int32)
    blk_exp = jnp.minimum(jnp.sum((pad_end[None, :] <= (blk * tmb)[:, None]).astype(jnp.int32), axis=1),
                          N_EXPERTS - 1).astype(jnp.int32)
    n_used = (pad_end[-1:] // tmb).astype(jnp.int32)
    present = counts > 0
    first_e = jnp.min(jnp.where(present, eidx, N_EXPERTS))
    last_e = jnp.max(jnp.where(present, eidx, -1))
    later = jnp.where(present[None, :] & (eidx[None, :] > eidx[:, None]), eidx[None, :], N_EXPERTS)
    next_e = jnp.min(later, axis=1)
    next_e = jnp.where(next_e == N_EXPERTS, first_e, next_e).astype(jnp.int32)
    starts = ((blk == 0) | (blk_exp != jnp.roll(blk_exp, 1))) & (blk < n_used[0])
    tables = (blk_exp, n_used, starts.astype(jnp.int32), next_e[blk_exp],
              (blk_exp == last_e).astype(jnp.int32))
    xb = jnp.concatenate([hp, jnp.zeros((1, D_MODEL // 2), jnp.uint32)], 0)[slot_tok]
    act = moe_up(tables, xb, w1, w3, tmb=tmb, tn=1024)
    yb = moe_down(tables, act, w2, tmb=tmb, tn=512)
    d2 = dest.reshape(T, TOP_K)
    g2 = gate[:, :TOP_K]
    return x2d + g2[:, 0:1] * yb[d2[:, 0]] + g2[:, 1:2] * yb[d2[:, 1]]


def kernel(x, attn_norm_g, w_in, nsa_q_norm_g, nsa_k_norm_g, cmp_pe_k, cmp_pe_v, w_cmp_k, w_cmp_v, mla_qa_norm_g, w_q_b, mla_kva_norm_g, w_kv_b, mla_q_norm_g, mla_qr_norm_g, mla_k_norm_g, mla_kr_norm_g, w_proj_nsa, w_proj_mla, w_out, ffn_norm_g, w_ff1, w_ff3, w_ff2, w_router, w_e1, w_e3, w_e2):
    p = dict(attn_norm_g=attn_norm_g, w_in=w_in, nsa_q_norm_g=nsa_q_norm_g, nsa_k_norm_g=nsa_k_norm_g,
             cmp_pe_k=cmp_pe_k, cmp_pe_v=cmp_pe_v, w_cmp_k=w_cmp_k, w_cmp_v=w_cmp_v,
             mla_qa_norm_g=mla_qa_norm_g, w_q_b=w_q_b, mla_kva_norm_g=mla_kva_norm_g, w_kv_b=w_kv_b,
             mla_q_norm_g=mla_q_norm_g, mla_qr_norm_g=mla_qr_norm_g, mla_k_norm_g=mla_k_norm_g,
             mla_kr_norm_g=mla_kr_norm_g, w_proj_nsa=w_proj_nsa, w_proj_mla=w_proj_mla, w_out=w_out)
    B, S, D = x.shape
    depth = w_in.shape[0]
    nc = S // CMP_STRIDE
    tabs = dict(tok=_rope_tabs(jnp.arange(S), NSA_ROT, S),
                cmp=_rope_tabs(jnp.arange(nc - 1) * CMP_STRIDE + CMP_LEN - 1, NSA_ROT, nc),
                mla=_mla_tabs(S))
    x2d = x.reshape(B * S, D)
    for l in range(depth):
        x2d = _attention_block(x2d, B, S, p, l, tabs)
        if l % 2 == 0:
            x2d = _dense_ffn(x2d, ffn_norm_g[l], w_ff1[l // 2], w_ff3[l // 2], w_ff2[l // 2])
        else:
            x2d = _moe_ffn(x2d, ffn_norm_g[l], w_router[l // 2], w_e1[l // 2], w_e3[l // 2], w_e2[l // 2])
    return x2d.reshape(B, S, D)
```

```python
import functools

import numpy as np
import jax
import jax.numpy as jnp
from jax import lax
from jax.experimental import pallas as pl
from jax.experimental.pallas import tpu as pltpu

D_MODEL = 2048
HEAD_DIM = 128
NSA_HEADS = 8
NSA_KV_GROUPS = 2
NSA_HPG = NSA_HEADS // NSA_KV_GROUPS
NSA_ROT = HEAD_DIM // 4
CMP_LEN = 32
CMP_STRIDE = 16
SLC_LEN = 64
SLC_TOPK = 16
N_LOCAL_SLC = 2
WINDOW = 512
FORCE_SCORE = 1.0e4
MLA_HEADS = 8
Q_LORA = 768
KV_LORA = 512
QK_NOPE = 128
QK_ROPE = 64
V_HEAD = 128
ROPE_THETA = 500000.0
EPS = 1e-6
D_FF = 7168
N_EXPERTS = 8
TOP_K = 2

LANES = 128
MLA_QK_PAD = 256
NEG_INF = float("-inf")
LOG2E = 1.4426950408889634
BF16 = jnp.bfloat16
F32 = jnp.float32

OFF_QA = 0
OFF_G = 768
OFF_KR = 896
OFF_Q = 1024
OFF_KVA = 2048
OFF_KV = 2560
OFF_M = 4096
D_INP = 8192
NSA_KV_COLS = 3 * 2 * NSA_KV_GROUPS * HEAD_DIM

VMEM_LIMIT = 56 * 1024 * 1024


def _cparams(sem):
    return pltpu.CompilerParams(dimension_semantics=sem, vmem_limit_bytes=VMEM_LIMIT)


def _rms(x, g):
    ms = jnp.mean(x * x, axis=-1, keepdims=True)
    return x * lax.rsqrt(ms + EPS) * g


def _rope_lanes(y, ct, s1, s2, half):
    return y * ct + pltpu.roll(y, LANES - half, 1) * s1 + pltpu.roll(y, half, 1) * s2


def _sigmoid(x):
    return 1.0 / (1.0 + jnp.exp(-x))


def _nmm_kernel(x_ref, g_ref, w_ref, o_ref, h_scr):
    @pl.when(pl.program_id(1) == 0)
    def _():
        h_scr[...] = _rms(x_ref[...], g_ref[...]).astype(BF16)

    o_ref[...] = jnp.dot(h_scr[...], w_ref[...].astype(BF16),
                         preferred_element_type=F32).astype(o_ref.dtype)


def norm_matmul(x, g, w, *, tm, tn, out_dtype=F32):
    T, K = x.shape
    N = w.shape[1]
    return pl.pallas_call(
        _nmm_kernel,
        grid=(T // tm, N // tn),
        in_specs=[pl.BlockSpec((tm, K), lambda m, n: (m, 0)),
                  pl.BlockSpec((1, K), lambda m, n: (0, 0)),
                  pl.BlockSpec((K, tn), lambda m, n: (0, n))],
        out_specs=pl.BlockSpec((tm, tn), lambda m, n: (m, n)),
        out_shape=jax.ShapeDtypeStruct((T, N), out_dtype),
        scratch_shapes=[pltpu.VMEM((tm, K), BF16)],
        compiler_params=_cparams(("parallel", "arbitrary")),
        name="norm_matmul",
    )(x, g.reshape(1, K), w)


def _mmres_kernel(a_ref, w_ref, r_ref, o_ref):
    o_ref[...] = r_ref[...] + jnp.dot(a_ref[...], w_ref[...].astype(BF16), preferred_element_type=F32)


def matmul_residual(a, w, res, *, tm, tn):
    T, K = a.shape
    N = w.shape[1]
    return pl.pallas_call(
        _mmres_kernel,
        grid=(T // tm, N // tn),
        in_specs=[pl.BlockSpec((tm, K), lambda m, n: (m, 0)),
                  pl.BlockSpec((K, tn), lambda m, n: (0, n)),
                  pl.BlockSpec((tm, tn), lambda m, n: (m, n))],
        out_specs=pl.BlockSpec((tm, tn), lambda m, n: (m, n)),
        out_shape=jax.ShapeDtypeStruct((T, N), F32),
        compiler_params=_cparams(("parallel", "arbitrary")),
        name="matmul_residual",
    )(a, w, res)


def _prep_kernel(z_ref, ct_ref, s1_ref, s2_ref, gn_ref, qn_ref, kvp_ref, gate_ref):
    ct, s1, s2 = ct_ref[...], s1_ref[...], s2_ref[...]
    half = NSA_ROT // 2
    gq = gn_ref[0:1, :]
    for h in range(NSA_HEADS):
        c = OFF_Q + h * HEAD_DIM
        y = _rope_lanes(_rms(z_ref[:, c:c + HEAD_DIM], gq), ct, s1, s2, half)
        qn_ref[:, h * HEAD_DIM:(h + 1) * HEAD_DIM] = (y * (LOG2E * HEAD_DIM ** -0.5)).astype(BF16)
    for br in range(3):
        for kv in range(2):
            for g in range(NSA_KV_GROUPS):
                c = ((br * 2 + kv) * NSA_KV_GROUPS + g) * HEAD_DIM
                y = z_ref[:, OFF_KV + c:OFF_KV + c + HEAD_DIM]
                if kv == 0 and br > 0:
                    y = _rope_lanes(_rms(y, gn_ref[1 + br:2 + br, :]), ct, s1, s2, half)
                kvp_ref[:, c:c + HEAD_DIM] = y.astype(BF16)
    gate_ref[...] = _sigmoid(z_ref[:, OFF_G:OFF_G + LANES])


def nsa_prep(z, tabs, gn, S, *, tm):
    T = z.shape[0]
    ns = S // tm
    tab_spec = pl.BlockSpec((tm, LANES), lambda i: (i % ns, 0))
    return pl.pallas_call(
        _prep_kernel,
        grid=(T // tm,),
        in_specs=[pl.BlockSpec((tm, OFF_M), lambda i: (i, 0)), tab_spec, tab_spec, tab_spec,
                  pl.BlockSpec((8, LANES), lambda i: (0, 0))],
        out_specs=[pl.BlockSpec((tm, NSA_HEADS * HEAD_DIM), lambda i: (i, 0)),
                   pl.BlockSpec((tm, NSA_KV_COLS), lambda i: (i, 0)),
                   pl.BlockSpec((tm, LANES), lambda i: (i, 0))],
        out_shape=[jax.ShapeDtypeStruct((T, NSA_HEADS * HEAD_DIM), BF16),
                   jax.ShapeDtypeStruct((T, NSA_KV_COLS), BF16),
                   jax.ShapeDtypeStruct((T, LANES), F32)],
        compiler_params=_cparams(("parallel",)),
        name="nsa_prep",
    )(z, *tabs, gn)


def _cmp_kernel(xk_ref, xv_ref, wk_ref, wv_ref, pek_ref, pev_ref, gk_ref, ct_ref, s1_ref, s2_ref,
                kct_ref, vc_ref):
    nc = xk_ref.shape[0]
    row = lax.broadcasted_iota(jnp.int32, (nc, HEAD_DIM), 0)

    def comp(x_ref, w_ref, pe_ref):
        w = w_ref[...]
        y = jnp.dot(x_ref[...], w, preferred_element_type=F32)
        ype = jnp.dot(pe_ref[...], w, preferred_element_type=F32)
        bias = ype[0:1, :HEAD_DIM] + ype[1:2, HEAD_DIM:]
        out = y[:, :HEAD_DIM] + pltpu.roll(y[:, HEAD_DIM:], nc - 1, 0) + bias
        return jnp.where(row < nc - 1, out, 0.0)

    k = _rms(comp(xk_ref, wk_ref, pek_ref), gk_ref[...])
    k = _rope_lanes(k, ct_ref[...], s1_ref[...], s2_ref[...], NSA_ROT // 2)
    kct_ref[...] = k.T.astype(BF16)
    vc_ref[...] = comp(xv_ref, wv_ref, pev_ref).astype(BF16)


def nsa_compress(x2, wk2, wv2, pek2, pev2, gk, tabs_cmp):
    B, _, nc, kk = x2.shape
    G = NSA_KV_GROUPS
    full = lambda shape: pl.BlockSpec(shape, lambda b, g: (0,) * len(shape))
    return pl.pallas_call(
        _cmp_kernel,
        grid=(B, G),
        in_specs=[pl.BlockSpec((None, None, nc, kk), lambda b, g: (b, g, 0, 0)),
                  pl.BlockSpec((None, None, nc, kk), lambda b, g: (b, G + g, 0, 0)),
                  full((kk, 2 * HEAD_DIM)), full((kk, 2 * HEAD_DIM)),
                  full((8, kk)), full((8, kk)), full((1, HEAD_DIM)),
                  full((nc, LANES)), full((nc, LANES)), full((nc, LANES))],
        out_specs=[pl.BlockSpec((None, None, HEAD_DIM, nc), lambda b, g: (b, g, 0, 0)),
                   pl.BlockSpec((None, None, nc, HEAD_DIM), lambda b, g: (b, g, 0, 0))],
        out_shape=[jax.ShapeDtypeStruct((B, G, HEAD_DIM, nc), BF16),
                   jax.ShapeDtypeStruct((B, G, nc, HEAD_DIM), BF16)],
        compiler_params=_cparams(("parallel", "parallel")),
        name="nsa_compress",
    )(x2, x2, wk2, wv2, pek2, pev2, gk, *tabs_cmp)


def _cattn_kernel(q_ref, kct_ref, vc_ref, ov_ref, oc_ref, sel_ref, *, tq):
    nc = vc_ref.shape[0]
    ns = sel_ref.shape[0]
    t0 = pl.program_id(2) * tq
    t_pos = t0 + lax.broadcasted_iota(jnp.int32, (tq, nc), 0)
    n_idx = lax.broadcasted_iota(jnp.int32, (tq, nc), 1)
    vis = (t_pos >= n_idx * CMP_STRIDE + (CMP_LEN - 1)) & (n_idx < nc - 1)
    bias = jnp.where(vis, 0.0, NEG_INF)
    kct = kct_ref[...]
    vc = vc_ref[...]
    psum = jnp.zeros((tq, nc), F32)
    for hh in range(NSA_HPG):
        s = jnp.dot(q_ref[:, hh * HEAD_DIM:(hh + 1) * HEAD_DIM], kct, preferred_element_type=F32) + bias
        m = jnp.max(s, axis=-1, keepdims=True)
        m = jnp.where(m == NEG_INF, 0.0, m)
        p = jnp.exp2(s - m)
        den = jnp.sum(p, axis=-1, keepdims=True)
        p = p * (1.0 / jnp.where(den > 0.0, den, 1.0))
        oc_ref[:, hh * HEAD_DIM:(hh + 1) * HEAD_DIM] = jnp.dot(
            p.astype(BF16), vc, preferred_element_type=F32).astype(oc_ref.dtype)
        psum = psum + p
    pt = psum.T
    hi = pt.astype(BF16)
    lo = (pt - hi.astype(F32)).astype(BF16)
    ov = ov_ref[...]
    imp = jnp.dot(ov, hi, preferred_element_type=F32) + jnp.dot(ov, lo, preferred_element_type=F32)
    blk = lax.broadcasted_iota(jnp.int32, (ns, tq), 0)
    cur = (t0 + lax.broadcasted_iota(jnp.int32, (ns, tq), 1)) // SLC_LEN
    forced = (blk == 0) | ((blk <= cur) & (blk > cur - N_LOCAL_SLC))
    val = jnp.where(blk > cur, NEG_INF, jnp.where(forced, FORCE_SCORE, imp))
    rank = jnp.zeros((ns, tq), F32)
    for i in range(ns):
        other = val[i:i + 1, :]
        ahead = (other > val) | ((other == val) & (blk > i))
        rank = rank + jnp.where(ahead, 1.0, 0.0)
    sel_ref[...] = jnp.where(rank < float(min(SLC_TOPK, ns)), 1.0, 0.0).astype(sel_ref.dtype)


def nsa_cmp_attention(qn, kct, vc, ov, B, S, *, tq):
    G = NSA_KV_GROUPS
    nq = S // tq
    nc = vc.shape[2]
    ns = S // SLC_LEN
    gw = NSA_HPG * HEAD_DIM
    return pl.pallas_call(
        functools.partial(_cattn_kernel, tq=tq),
        grid=(B, G, nq),
        in_specs=[pl.BlockSpec((tq, gw), lambda b, g, i: (b * nq + i, g)),
                  pl.BlockSpec((None, None, HEAD_DIM, nc), lambda b, g, i: (b, g, 0, 0)),
                  pl.BlockSpec((None, None, nc, HEAD_DIM), lambda b, g, i: (b, g, 0, 0)),
                  pl.BlockSpec((ns, nc), lambda b, g, i: (0, 0))],
        out_specs=[pl.BlockSpec((tq, gw), lambda b, g, i: (b * nq + i, g)),
                   pl.BlockSpec((None, None, ns, tq), lambda b, g, i: (b, g, 0, i))],
        out_shape=[jax.ShapeDtypeStruct((B * S, NSA_HEADS * HEAD_DIM), BF16),
                   jax.ShapeDtypeStruct((B, G, ns, S), BF16)],
        compiler_params=_cparams(("parallel", "parallel", "parallel")),
        name="nsa_cmp_attention",
    )(qn, kct, vc, ov)


def _flash_kernel(*refs, mode, hg, tq, tk, dqk, dv):
    if mode == "sel":
        q_ref, k_ref, vt_ref, sel_ref, ex_ref, o_ref, qs, m_s, l_s, acc_s = refs
    else:
        q_ref, k_ref, vt_ref, o_ref, qs, m_s, l_s, acc_s = refs
    rows = hg * tq
    q0 = pl.program_id(2) * tq
    cd = q0 // tk
    for hh in range(hg):
        qs[hh * tq:(hh + 1) * tq, :] = q_ref[:, hh * dqk:(hh + 1) * dqk]

    def scores(c, kind):
        start = pl.multiple_of(c * tk, tk)
        s = lax.dot_general(k_ref[pl.ds(start, tk), :], qs[...], (((1,), (1,)), ((), ())),
                            preferred_element_type=F32)
        ok = None
        if kind == "diag":
            k_pos = start + lax.broadcasted_iota(jnp.int32, (tk, tq), 0)
            q_pos = q0 + lax.broadcasted_iota(jnp.int32, (tk, tq), 1)
            ok = k_pos <= q_pos
        if mode == "sel":
            chosen = jnp.dot(ex_ref[c], sel_ref[...], preferred_element_type=F32) > 0.5
            ok = chosen if ok is None else (ok & chosen)
        if ok is not None:
            bias = jnp.where(ok, 0.0, NEG_INF)
            s = s + (jnp.concatenate([bias] * hg, axis=1) if hg > 1 else bias)
        return s

    def update(c, s, carry):
        m, l, acc = carry
        m_new = jnp.maximum(m, jnp.max(s, axis=0, keepdims=True))
        p = jnp.exp2(s - m_new)
        alpha = jnp.exp2(m - m_new)
        l = alpha * l + jnp.sum(p, axis=0, keepdims=True)
        acc = alpha * acc + jnp.dot(vt_ref[c], p.astype(BF16), preferred_element_type=F32)
        return m_new, l, acc

    def load():
        return m_s[...], l_s[...], acc_s[...]

    def store(carry):
        m_s[...], l_s[...], acc_s[...] = carry

    def run(first, count, kind, carry):
        s_next = scores(first, kind)
        for j in range(count):
            s_cur = s_next
            if j + 1 < count:
                s_next = scores(first + j + 1, kind)
            carry = update(first + j, s_cur, carry)
        return carry

    store(run(cd, max(tq // tk, 1), "diag",
              (jnp.full((1, rows), NEG_INF, F32), jnp.zeros((1, rows), F32), jnp.zeros((dv, rows), F32))))

    def group(first, count):
        store(run(first, count, "full", load()))

    def quad(i, _):
        group(4 * i, 4)
        return 0

    lax.fori_loop(0, cd // 4, quad, 0)

    @pl.when(cd % 4 >= 2)
    def _():
        group((cd // 4) * 4, 2)

    @pl.when(cd % 2 == 1)
    def _():
        group(cd - 1, 1)

    o = acc_s[...] * (1.0 / l_s[...])
    for hh in range(hg):
        o_ref[:, hh * dv:(hh + 1) * dv] = o[:, hh * tq:(hh + 1) * tq].T.astype(o_ref.dtype)


def flash_attention(q, k, vt, B, S, *, mode, n_groups, hg, dqk, dv, kcol0, tq, tk, sel_t=None, expand=None):
    nq = S // tq
    nch = S // tk
    assert (tk % tq == 0 or tq % tk == 0) and S % tk == 0 and S % tq == 0 and tq % LANES == 0
    in_specs = [pl.BlockSpec((tq, hg * dqk), lambda b, g, i: (b * nq + i, g)),
                pl.BlockSpec((S, dqk), lambda b, g, i: (b, kcol0 + g)),
                pl.BlockSpec((None, None, nch, dv, tk), lambda b, g, i: (b, g, 0, 0, 0))]
    args = [q, k, vt]
    if mode == "sel":
        ns = S // SLC_LEN
        in_specs += [pl.BlockSpec((None, None, ns, tq), lambda b, g, i: (b, g, 0, i)),
                     pl.BlockSpec((nch, tk, ns), lambda b, g, i: (0, 0, 0))]
        args += [sel_t, expand]
    return pl.pallas_call(
        functools.partial(_flash_kernel, mode=mode, hg=hg, tq=tq, tk=tk, dqk=dqk, dv=dv),
        grid=(B, n_groups, nq),
        in_specs=in_specs,
        out_specs=pl.BlockSpec((tq, hg * dv), lambda b, g, i: (b * nq + i, g)),
        out_shape=jax.ShapeDtypeStruct((B * S, n_groups * hg * dv), BF16),
        scratch_shapes=[pltpu.VMEM((hg * tq, dqk), BF16),
                        pltpu.VMEM((1, hg * tq), F32),
                        pltpu.VMEM((1, hg * tq), F32),
                        pltpu.VMEM((dv, hg * tq), F32)],
        compiler_params=_cparams(("parallel", "parallel", "arbitrary")),
        name="flash_" + mode,
    )(*args)


def _window_kernel(q_ref, k_ref, vt_ref, o_ref, qs, *, hg, tq, dqk, dv):
    nk = WINDOW + tq
    q0 = pl.program_id(2) * tq
    start = pl.multiple_of(jnp.maximum(q0 - WINDOW, 0), tq)
    for hh in range(hg):
        qs[hh * tq:(hh + 1) * tq, :] = q_ref[:, hh * dqk:(hh + 1) * dqk]
    s = lax.dot_general(k_ref[pl.ds(start, nk), :], qs[...], (((1,), (1,)), ((), ())),
                        preferred_element_type=F32)
    k_pos = start + lax.broadcasted_iota(jnp.int32, (nk, tq), 0)
    q_pos = q0 + lax.broadcasted_iota(jnp.int32, (nk, tq), 1)
    bias = jnp.where((k_pos <= q_pos) & (q_pos - k_pos < WINDOW), 0.0, NEG_INF)
    s = s + jnp.concatenate([bias] * hg, axis=1)
    p = jnp.exp2(s - jnp.max(s, axis=0, keepdims=True))
    l = jnp.sum(p, axis=0, keepdims=True)
    pb = p.astype(BF16)
    c0 = start // tq
    o = jnp.zeros((dv, hg * tq), F32)
    for j in range(nk // tq):
        o = o + jnp.dot(vt_ref[c0 + j], pb[j * tq:(j + 1) * tq, :], preferred_element_type=F32)
    o = o * (1.0 / l)
    for hh in range(hg):
        o_ref[:, hh * dv:(hh + 1) * dv] = o[:, hh * tq:(hh + 1) * tq].T.astype(o_ref.dtype)


def window_attention(q, k, vt, B, S, *, n_groups, hg, dqk, dv, kcol0, tq):
    nq = S // tq
    assert WINDOW % tq == 0 and S >= WINDOW + tq and tq % LANES == 0
    return pl.pallas_call(
        functools.partial(_window_kernel, hg=hg, tq=tq, dqk=dqk, dv=dv),
        grid=(B, n_groups, nq),
        in_specs=[pl.BlockSpec((tq, hg * dqk), lambda b, g, i: (b * nq + i, g)),
                  pl.BlockSpec((S, dqk), lambda b, g, i: (b, kcol0 + g)),
                  pl.BlockSpec((None, None, nq, dv, tq), lambda b, g, i: (b, g, 0, 0, 0))],
        out_specs=pl.BlockSpec((tq, hg * dv), lambda b, g, i: (b * nq + i, g)),
        out_shape=jax.ShapeDtypeStruct((B * S, n_groups * hg * dv), BF16),
        scratch_shapes=[pltpu.VMEM((hg * tq, dqk), BF16)],
        compiler_params=_cparams(("parallel", "parallel", "arbitrary")),
        name="window_attention",
    )(q, k, vt)


def _mla_q_kernel(z_ref, ga_ref, w_ref, gh_ref, ct_ref, s1_ref, s2_ref, o_ref):
    h = _rms(z_ref[...], ga_ref[...]).astype(BF16)
    y = jnp.dot(h, w_ref[...], preferred_element_type=F32)
    ct, s1, s2 = ct_ref[...], s1_ref[...], s2_ref[...]
    scale = LOG2E * (QK_NOPE + QK_ROPE) ** -0.5
    for hd in range(MLA_HEADS):
        c = hd * MLA_QK_PAD
        nope = _rms(y[:, c:c + QK_NOPE], gh_ref[0:1, :])
        r = y[:, c + QK_NOPE:c + MLA_QK_PAD]
        ms = jnp.sum(r * r, axis=-1, keepdims=True) * (1.0 / QK_ROPE)
        r = _rope_lanes(r * lax.rsqrt(ms + EPS) * gh_ref[1:2, :], ct, s1, s2, QK_ROPE // 2)
        o_ref[:, c:c + QK_NOPE] = (nope * scale).astype(BF16)
        o_ref[:, c + QK_NOPE:c + MLA_QK_PAD] = (r * scale).astype(BF16)


def mla_q_proj(z, ga, wq, gh, tabs, S, *, tm):
    T = z.shape[0]
    ns = S // tm
    tab_spec = pl.BlockSpec((tm, LANES), lambda i: (i % ns, 0))
    nout = MLA_HEADS * MLA_QK_PAD
    return pl.pallas_call(
        _mla_q_kernel,
        grid=(T // tm,),
        in_specs=[pl.BlockSpec((tm, Q_LORA), lambda i: (i, OFF_QA // Q_LORA)),
                  pl.BlockSpec((1, Q_LORA), lambda i: (0, 0)),
                  pl.BlockSpec((Q_LORA, nout), lambda i: (0, 0)),
                  pl.BlockSpec((8, LANES), lambda i: (0, 0)),
                  tab_spec, tab_spec, tab_spec],
        out_specs=pl.BlockSpec((tm, nout), lambda i: (i, 0)),
        out_shape=jax.ShapeDtypeStruct((T, nout), BF16),
        compiler_params=_cparams(("parallel",)),
        name="mla_q_proj",
    )(z, ga, wq, gh, *tabs)


def _mla_kv_kernel(z_ref, zr_ref, ga_ref, w_ref, gh_ref, ct_ref, s1_ref, s2_ref, k_ref, v_ref):
    h = _rms(z_ref[...], ga_ref[...]).astype(BF16)
    y = jnp.dot(h, w_ref[...], preferred_element_type=F32)
    r = zr_ref[...]
    ms = jnp.sum(r * r, axis=-1, keepdims=True) * (1.0 / QK_ROPE)
    r = _rope_lanes(r * lax.rsqrt(ms + EPS) * gh_ref[1:2, :], ct_ref[...], s1_ref[...], s2_ref[...],
                    QK_ROPE // 2).astype(BF16)
    nv = MLA_HEADS * QK_NOPE
    for hd in range(MLA_HEADS):
        c = hd * MLA_QK_PAD
        k_ref[:, c:c + QK_NOPE] = _rms(y[:, hd * QK_NOPE:(hd + 1) * QK_NOPE], gh_ref[0:1, :]).astype(BF16)
        k_ref[:, c + QK_NOPE:c + MLA_QK_PAD] = r
    v_ref[...] = y[:, nv:].astype(BF16)


def mla_kv_proj(z, ga, wkv, gh, tabs, S, *, tm):
    T = z.shape[0]
    ns = S // tm
    tab_spec = pl.BlockSpec((tm, LANES), lambda i: (i % ns, 0))
    nk = MLA_HEADS * MLA_QK_PAD
    nv = MLA_HEADS * V_HEAD
    return pl.pallas_call(
        _mla_kv_kernel,
        grid=(T // tm,),
        in_specs=[pl.BlockSpec((tm, KV_LORA), lambda i: (i, OFF_KVA // KV_LORA)),
                  pl.BlockSpec((tm, LANES), lambda i: (i, OFF_KR // LANES)),
                  pl.BlockSpec((1, KV_LORA), lambda i: (0, 0)),
                  pl.BlockSpec((KV_LORA, MLA_HEADS * (QK_NOPE + V_HEAD)), lambda i: (0, 0)),
                  pl.BlockSpec((8, LANES), lambda i: (0, 0)),
                  tab_spec, tab_spec, tab_spec],
        out_specs=[pl.BlockSpec((tm, nk), lambda i: (i, 0)),
                   pl.BlockSpec((tm, nv), lambda i: (i, 0))],
        out_shape=[jax.ShapeDtypeStruct((T, nk), BF16),
                   jax.ShapeDtypeStruct((T, nv), BF16)],
        compiler_params=_cparams(("parallel",)),
        name="mla_kv_proj",
    )(z, z, ga, wkv, gh, *tabs)


def _mix_kernel(oc_ref, os_ref, ow_ref, ob_ref, gate_ref, wa_ref, wb_ref, za_ref, zb_ref, o_ref, a_scr):
    @pl.when(pl.program_id(1) == 0)
    def _():
        for h in range(NSA_HEADS):
            sl = slice(h * HEAD_DIM, (h + 1) * HEAD_DIM)
            a = (gate_ref[:, h:h + 1] * oc_ref[:, sl].astype(F32)
                 + gate_ref[:, NSA_HEADS + h:NSA_HEADS + h + 1] * os_ref[:, sl].astype(F32)
                 + gate_ref[:, 2 * NSA_HEADS + h:2 * NSA_HEADS + h + 1] * ow_ref[:, sl].astype(F32))
            a_scr[:, sl] = a.astype(BF16)

    pa = jnp.dot(a_scr[...], wa_ref[...].astype(BF16), preferred_element_type=F32)
    pb = jnp.dot(ob_ref[...], wb_ref[...].astype(BF16), preferred_element_type=F32)
    o_ref[...] = (_sigmoid(za_ref[...]) * pa + _sigmoid(zb_ref[...]) * pb).astype(o_ref.dtype)


def gated_mix(oc, os_, ow, ob, gates, wa, wb, z, *, tm, tn):
    T, K = oc.shape
    N = wa.shape[1]
    row = lambda w: pl.BlockSpec((tm, w), lambda m, n: (m, 0))
    return pl.pallas_call(
        _mix_kernel,
        grid=(T // tm, N // tn),
        in_specs=[row(K), row(K), row(K), row(K), row(LANES),
                  pl.BlockSpec((K, tn), lambda m, n: (0, n)),
                  pl.BlockSpec((K, tn), lambda m, n: (0, n)),
                  pl.BlockSpec((tm, tn), lambda m, n: (m, OFF_M // tn + n)),
                  pl.BlockSpec((tm, tn), lambda m, n: (m, (OFF_M + D_MODEL) // tn + n))],
        out_specs=pl.BlockSpec((tm, tn), lambda m, n: (m, n)),
        out_shape=jax.ShapeDtypeStruct((T, N), BF16),
        scratch_shapes=[pltpu.VMEM((tm, K), BF16)],
        compiler_params=_cparams(("parallel", "arbitrary")),
        name="gated_mix",
    )(oc, os_, ow, ob, gates, wa, wb, z, z)


def _ffn_up_kernel(x_ref, g_ref, w1_ref, w3_ref, o_ref, h_scr):
    @pl.when(pl.program_id(1) == 0)
    def _():
        h_scr[...] = _rms(x_ref[...], g_ref[...]).astype(BF16)

    h = h_scr[...]
    a = jnp.dot(h, w1_ref[...].astype(BF16), preferred_element_type=F32)
    b = jnp.dot(h, w3_ref[...].astype(BF16), preferred_element_type=F32)
    o_ref[...] = (a * _sigmoid(a) * b).astype(o_ref.dtype)


def ffn_up(x, g, w1, w3, *, tm, tn):
    T, K = x.shape
    N = w1.shape[1]
    return pl.pallas_call(
        _ffn_up_kernel,
        grid=(T // tm, N // tn),
        in_specs=[pl.BlockSpec((tm, K), lambda m, n: (m, 0)),
                  pl.BlockSpec((1, K), lambda m, n: (0, 0)),
                  pl.BlockSpec((K, tn), lambda m, n: (0, n)),
                  pl.BlockSpec((K, tn), lambda m, n: (0, n))],
        out_specs=pl.BlockSpec((tm, tn), lambda m, n: (m, n)),
        out_shape=jax.ShapeDtypeStruct((T, N), BF16),
        scratch_shapes=[pltpu.VMEM((tm, K), BF16)],
        compiler_params=_cparams(("parallel", "arbitrary")),
        name="ffn_up",
    )(x, g.reshape(1, K), w1, w3)


def _pack_bf16_pairs(h):
    k = h.shape[1] // 2
    hi = lax.bitcast_convert_type(h[:, :k].astype(jnp.bfloat16).astype(F32), jnp.uint32)
    lo = lax.bitcast_convert_type(h[:, k:].astype(jnp.bfloat16).astype(F32), jnp.uint32)
    return hi | (lo >> 16)


def _unpack_bf16_pairs(xp):
    hi = lax.bitcast_convert_type(xp & jnp.uint32(0xFFFF0000), F32)
    lo = lax.bitcast_convert_type(xp << 16, F32)
    return hi.astype(BF16), lo.astype(BF16)


def _router_kernel(x_ref, g_ref, wr_ref, h_ref, idx_ref, gate_ref):
    h = _rms(x_ref[...], g_ref[...])
    is_pad = pl.program_id(0) == pl.num_programs(0) - 1
    h_ref[...] = jnp.where(is_pad, jnp.uint32(0), _pack_bf16_pairs(h))
    logits = jnp.dot(h, wr_ref[...], preferred_element_type=F32, precision=lax.Precision.HIGHEST)
    lane = lax.broadcasted_iota(jnp.int32, logits.shape, 1).astype(F32)
    logits = jnp.where(lane < float(N_EXPERTS), logits, NEG_INF)
    m1 = jnp.max(logits, axis=-1, keepdims=True)
    i1 = jnp.min(jnp.where(logits == m1, lane, float(LANES)), axis=-1, keepdims=True)
    rest = jnp.where(lane == i1, NEG_INF, logits)
    m2 = jnp.max(rest, axis=-1, keepdims=True)
    i2 = jnp.min(jnp.where(rest == m2, lane, float(LANES)), axis=-1, keepdims=True)
    e = jnp.exp(m2 - m1)
    den = 1.0 + e
    idx_ref[...] = jnp.where(lane == 0.0, i1, jnp.where(lane == 1.0, i2, 0.0)).astype(jnp.int32)
    gate_ref[...] = jnp.where(lane == 0.0, 1.0 / den, jnp.where(lane == 1.0, e / den, 0.0))


def router(x, g, wr_pad, *, tm):
    T, K = x.shape
    nt = T // tm
    return pl.pallas_call(
        _router_kernel,
        grid=(nt + 1,),
        in_specs=[pl.BlockSpec((tm, K), lambda i: (jnp.minimum(i, nt - 1), 0)),
                  pl.BlockSpec((1, K), lambda i: (0, 0)),
                  pl.BlockSpec((K, LANES), lambda i: (0, 0))],
        out_specs=[pl.BlockSpec((tm, K // 2), lambda i: (i, 0)),
                   pl.BlockSpec((tm, LANES), lambda i: (jnp.minimum(i, nt - 1), 0)),
                   pl.BlockSpec((tm, LANES), lambda i: (jnp.minimum(i, nt - 1), 0))],
        out_shape=[jax.ShapeDtypeStruct((T + tm, K // 2), jnp.uint32),
                   jax.ShapeDtypeStruct((T, LANES), jnp.int32),
                   jax.ShapeDtypeStruct((T, LANES), F32)],
        compiler_params=_cparams(("arbitrary",)),
        name="router",
    )(x, g.reshape(1, K), wr_pad)


def _moe_weight_stream(w_hbms, stages, casts, sems, be_ref, nu_ref, gs_ref, ne_ref, lg_ref, tn):
    n = pl.program_id(0)
    r = pl.program_id(1)
    used = r < nu_ref[0]

    def copies(e, nt):
        c0 = pl.multiple_of(nt * tn, tn)
        return [pltpu.make_async_copy(w.at[e, :, pl.ds(c0, tn)], st, sems.at[i])
                for i, (w, st) in enumerate(zip(w_hbms, stages))]

    @pl.when((n == 0) & (r == 0))
    def _():
        for c in copies(be_ref[0], 0):
            c.start()

    @pl.when(used & (gs_ref[r] == 1))
    def _():
        for c in copies(be_ref[r], n):
            c.wait()
        for st, wb in zip(stages, casts):
            wb[...] = st[...].astype(BF16)
        last = lg_ref[r] == 1

        @pl.when(jnp.logical_not(last & (n == pl.num_programs(0) - 1)))
        def _():
            for c in copies(ne_ref[r], n + last.astype(jnp.int32)):
                c.start()

    return used


def _moe_up_kernel(be_ref, nu_ref, gs_ref, ne_ref, lg_ref, x_ref, w1_hbm, w3_hbm, o_ref,
                   st1, st3, w1_s, w3_s, sems, *, tn):
    used = _moe_weight_stream((w1_hbm, w3_hbm), (st1, st3), (w1_s, w3_s), sems,
                              be_ref, nu_ref, gs_ref, ne_ref, lg_ref, tn)

    @pl.when(used)
    def _():
        xa, xb = _unpack_bf16_pairs(x_ref[...])
        k2 = xa.shape[1]
        a = (jnp.dot(xa, w1_s[:k2, :], preferred_element_type=F32)
             + jnp.dot(xb, w1_s[k2:, :], preferred_element_type=F32))
        b = (jnp.dot(xa, w3_s[:k2, :], preferred_element_type=F32)
             + jnp.dot(xb, w3_s[k2:, :], preferred_element_type=F32))
        o_ref[...] = (a * _sigmoid(a) * b).astype(o_ref.dtype)

    @pl.when(jnp.logical_not(used))
    def _():
        o_ref[...] = jnp.zeros(o_ref.shape, o_ref.dtype)


def moe_up(tables, xb, w1, w3, *, tmb, tn):
    n_slot = xb.shape[0]
    K, N = w1.shape[1], w1.shape[2]
    row = lambda r, nu: jnp.minimum(r, nu[0] - 1)
    return pl.pallas_call(
        functools.partial(_moe_up_kernel, tn=tn),
        grid_spec=pltpu.PrefetchScalarGridSpec(
            num_scalar_prefetch=5,
            grid=(N // tn, n_slot // tmb),
            in_specs=[pl.BlockSpec((tmb, K // 2), lambda n, r, be, nu, gs, ne, lg: (row(r, nu), 0)),
                      pl.BlockSpec(memory_space=pl.ANY), pl.BlockSpec(memory_space=pl.ANY)],
            out_specs=pl.BlockSpec((tmb, tn), lambda n, r, be, nu, gs, ne, lg: (r, n)),
            scratch_shapes=[pltpu.VMEM((K, tn), F32), pltpu.VMEM((K, tn), F32),
                            pltpu.VMEM((K, tn), BF16), pltpu.VMEM((K, tn), BF16),
                            pltpu.SemaphoreType.DMA((2,))]),
        out_shape=jax.ShapeDtypeStruct((n_slot, N), BF16),
        compiler_params=_cparams(("arbitrary", "arbitrary")),
        name="moe_up",
    )(*tables, xb, w1, w3)


def _moe_down_kernel(be_ref, nu_ref, gs_ref, ne_ref, lg_ref, a_ref, w2_hbm, o_ref, st2, w2_s, sems, *, tn):
    used = _moe_weight_stream((w2_hbm,), (st2,), (w2_s,), sems, be_ref, nu_ref, gs_ref, ne_ref, lg_ref, tn)

    @pl.when(used)
    def _():
        o_ref[...] = jnp.dot(a_ref[...], w2_s[...], preferred_element_type=F32)

    @pl.when(jnp.logical_not(used))
    def _():
        o_ref[...] = jnp.zeros(o_ref.shape, o_ref.dtype)


def moe_down(tables, act, w2, *, tmb, tn):
    n_slot, K = act.shape
    N = w2.shape[2]
    row = lambda r, nu: jnp.minimum(r, nu[0] - 1)
    return pl.pallas_call(
        functools.partial(_moe_down_kernel, tn=tn),
        grid_spec=pltpu.PrefetchScalarGridSpec(
            num_scalar_prefetch=5,
            grid=(N // tn, n_slot // tmb),
            in_specs=[pl.BlockSpec((tmb, K), lambda n, r, be, nu, gs, ne, lg: (row(r, nu), 0)),
                      pl.BlockSpec(memory_space=pl.ANY)],
            out_specs=pl.BlockSpec((tmb, tn), lambda n, r, be, nu, gs, ne, lg: (r, n)),
            scratch_shapes=[pltpu.VMEM((K, tn), F32), pltpu.VMEM((K, tn), BF16),
                            pltpu.SemaphoreType.DMA((1,))]),
        out_shape=jax.ShapeDtypeStruct((n_slot, N), F32),
        compiler_params=_cparams(("arbitrary", "arbitrary")),
        name="moe_down",
    )(*tables, act, w2)


def _rope_tabs(pos, rot_dim, n_rows):
    half = rot_dim // 2
    inv = 1.0 / (ROPE_THETA ** (jnp.arange(0, rot_dim, 2, dtype=F32) / rot_dim))
    ang = jnp.asarray(pos).astype(F32)[:, None] * inv[None, :]
    c, s = jnp.cos(ang), jnp.sin(ang)
    z = jnp.zeros_like(c)
    pad = lambda a, fill: jnp.pad(a, ((0, n_rows - a.shape[0]), (0, LANES - a.shape[1])), constant_values=fill)
    return pad(jnp.concatenate([c, c], 1), 1.0), pad(jnp.concatenate([-s, z], 1), 0.0), \
        pad(jnp.concatenate([z, s], 1), 0.0)


def _mla_tabs(S):
    ct, s1, s2 = _rope_tabs(jnp.arange(S), QK_ROPE, S)
    lane = jnp.arange(LANES)[None, :]
    return jnp.where(lane < QK_ROPE, ct, 0.0), s1, s2


def _pad_cols(w, n):
    return jnp.pad(w, ((0, 0), (0, n - w.shape[1])))


def _layout_w_in(w):
    sp = np.cumsum([0, NSA_HEADS * HEAD_DIM, NSA_KV_COLS, 3 * NSA_HEADS, Q_LORA, KV_LORA, QK_ROPE, 2 * D_MODEL])
    q, kv, g, qa, kva, kr, m = [w[:, sp[i]:sp[i + 1]] for i in range(7)]
    return jnp.concatenate([qa, _pad_cols(g, LANES), _pad_cols(kr, LANES), q, kva, kv, m], axis=1).astype(BF16)


def _layout_w_q_b(w):
    w = w.reshape(Q_LORA, MLA_HEADS, QK_NOPE + QK_ROPE)
    w = jnp.pad(w, ((0, 0), (0, 0), (0, MLA_QK_PAD - QK_NOPE - QK_ROPE)))
    return w.reshape(Q_LORA, MLA_HEADS * MLA_QK_PAD).astype(BF16)


def _layout_w_kv_b(w):
    w = w.reshape(KV_LORA, MLA_HEADS, QK_NOPE + V_HEAD)
    return jnp.concatenate([w[:, :, :QK_NOPE].reshape(KV_LORA, -1), w[:, :, QK_NOPE:].reshape(KV_LORA, -1)],
                           axis=1).astype(BF16)


def _layout_w_cmp(w):
    h = CMP_LEN // 2
    return jnp.concatenate([w[:h].reshape(h * HEAD_DIM, HEAD_DIM), w[h:].reshape(h * HEAD_DIM, HEAD_DIM)],
                           axis=1).astype(BF16)


def _layout_pe(pe):
    return jnp.pad(pe.reshape(2, (CMP_LEN // 2) * HEAD_DIM), ((0, 6), (0, 0))).astype(BF16)


def _overlap(nc, ns):
    n = np.arange(nc)[None, :] * CMP_STRIDE
    j = np.arange(ns)[:, None] * SLC_LEN
    ov = (n <= j + SLC_LEN - 1) & (j <= n + CMP_LEN - 1) & (np.arange(nc)[None, :] < nc - 1)
    return jnp.asarray(ov.astype(np.float32), BF16)


def _expand(ns, S, tk):
    e = ((np.arange(S // tk)[:, None, None] * tk + np.arange(tk)[None, :, None]) // SLC_LEN
         == np.arange(ns)[None, None, :])
    return jnp.asarray(e.astype(np.float32), BF16)


def _chunk_transpose(v, B, S, n_heads, dv, tk):
    return v.reshape(B, S // tk, tk, n_heads, dv).transpose(0, 3, 1, 4, 2)


def _attention_block(x2d, B, S, p, l, tabs):
    T = x2d.shape[0]
    z = norm_matmul(x2d, p['attn_norm_g'][l], _layout_w_in(p['w_in'][l]), tm=min(1024, T), tn=512)
    gn = jnp.concatenate([p['nsa_q_norm_g'][l][None], p['nsa_k_norm_g'][l], jnp.zeros((4, HEAD_DIM), F32)], 0)
    qn, kvp, gates = nsa_prep(z, tabs['tok'], gn, S, tm=256)

    nc = S // CMP_STRIDE
    x2 = kvp[:, :2 * NSA_KV_GROUPS * HEAD_DIM].reshape(B, S, 2 * NSA_KV_GROUPS, HEAD_DIM)
    x2 = x2.transpose(0, 2, 1, 3).reshape(B, 2 * NSA_KV_GROUPS, nc, CMP_STRIDE * HEAD_DIM)
    kct, vc = nsa_compress(x2, _layout_w_cmp(p['w_cmp_k'][l]), _layout_w_cmp(p['w_cmp_v'][l]),
                           _layout_pe(p['cmp_pe_k'][l]), _layout_pe(p['cmp_pe_v'][l]),
                           p['nsa_k_norm_g'][l][0:1], tabs['cmp'])
    ns = S // SLC_LEN
    oc, sel_t = nsa_cmp_attention(qn, kct, vc, _overlap(nc, ns), B, S, tq=min(256, S))
    nsa = dict(n_groups=NSA_KV_GROUPS, hg=NSA_HPG, dqk=HEAD_DIM, dv=HEAD_DIM)
    tks, tkw = min(512, S), WINDOW // 2
    gv = NSA_KV_GROUPS * HEAD_DIM
    vts = _chunk_transpose(kvp[:, 3 * gv:4 * gv], B, S, NSA_KV_GROUPS, HEAD_DIM, tks)
    vtw = _chunk_transpose(kvp[:, 5 * gv:6 * gv], B, S, NSA_KV_GROUPS, HEAD_DIM, tkw)
    os_ = flash_attention(qn, kvp, vts, B, S, mode="sel", kcol0=4, tq=256, tk=tks,
                          sel_t=sel_t, expand=_expand(ns, S, tks), **nsa)
    ow = window_attention(qn, kvp, vtw, B, S, kcol0=8, tq=tkw, **nsa)

    gh_q = jnp.concatenate([p['mla_q_norm_g'][l][None], _pad_cols(p['mla_qr_norm_g'][l][None], LANES),
                            jnp.zeros((6, LANES), F32)], 0)
    gh_k = jnp.concatenate([p['mla_k_norm_g'][l][None], _pad_cols(p['mla_kr_norm_g'][l][None], LANES),
                            jnp.zeros((6, LANES), F32)], 0)
    qm = mla_q_proj(z, p['mla_qa_norm_g'][l][None], _layout_w_q_b(p['w_q_b'][l]), gh_q, tabs['mla'], S, tm=256)
    km, vm = mla_kv_proj(z, p['mla_kva_norm_g'][l][None], _layout_w_kv_b(p['w_kv_b'][l]), gh_k, tabs['mla'],
                         S, tm=256)
    tkm = min(512, S)
    ob = flash_attention(qm, km, _chunk_transpose(vm, B, S, MLA_HEADS, V_HEAD, tkm), B, S, mode="causal",
                         n_groups=MLA_HEADS, hg=1, dqk=MLA_QK_PAD, dv=V_HEAD, kcol0=0, tq=min(1024, S), tk=tkm)

    tm = min(1024, T)
    mix = gated_mix(oc, os_, ow, ob, gates, p['w_proj_nsa'][l], p['w_proj_mla'][l], z, tm=tm, tn=512)
    return matmul_residual(mix, p['w_out'][l], x2d, tm=min(2048, T), tn=512)


def _dense_ffn(x2d, g, w1, w3, w2):
    T = x2d.shape[0]
    act = ffn_up(x2d, g, w1.astype(BF16), w3.astype(BF16), tm=min(1024, T), tn=512)
    return matmul_residual(act, w2.astype(BF16), x2d, tm=min(1024, T), tn=256)


def _cumsum_rows(oh, blk=128):
    A, E = oh.shape
    nb = A // blk
    x = oh.astype(F32).reshape(nb, blk, E)
    within = jnp.einsum('ij,bje->bie', jnp.tril(jnp.ones((blk, blk), F32)), x)
    before = jnp.tril(jnp.ones((nb, nb), F32), -1) @ within[:, -1, :]
    return (within + before[:, None, :]).astype(jnp.int32).reshape(A, E)


def _moe_ffn(x2d, g, w_router, w1, w3, w2, *, tmb=512):
    T = x2d.shape[0]
    hp, idx, gate = router(x2d, g, _pad_cols(w_router, LANES), tm=256)
    A = T * TOP_K
    e_flat = idx[:, :TOP_K].reshape(A)
    tok_flat = jnp.repeat(jnp.arange(T, dtype=jnp.int32), TOP_K)
    oh = (e_flat[:, None] == jnp.arange(N_EXPERTS)[None, :]).astype(jnp.int32)
    csum = _cumsum_rows(oh)
    rank = jnp.sum(oh * csum, axis=1) - 1
    counts = csum[-1]
    padded = (counts + tmb - 1) // tmb * tmb
    pad_end = jnp.cumsum(padded)
    dest = (pad_end - padded)[e_flat] + rank
    n_blk = -(-A // tmb) + N_EXPERTS
    n_slot = n_blk * tmb
    slot_tok = jnp.full((n_slot,), T, jnp.int32).at[dest].set(tok_flat)
    eidx = jnp.arange(N_EXPERTS, dtype=jnp.int32)
    blk = jnp.arange(n_blk, dtype=jnp.int32)
    blk_exp = jnp.minimum(jnp.sum((pad_end[None, :] <= (blk * tmb)[:, None]).astype(jnp.int32), axis=1),
                          N_EXPERTS - 1).astype(jnp.int32)
    n_used = (pad_end[-1:] // tmb).astype(jnp.int32)
    present = counts > 0
    first_e = jnp.min(jnp.where(present, eidx, N_EXPERTS))
    last_e = jnp.max(jnp.where(present, eidx, -1))
    later = jnp.where(present[None, :] & (eidx[None, :] > eidx[:, None]), eidx[None, :], N_EXPERTS)
    next_e = jnp.min(later, axis=1)
    next_e = jnp.where(next_e == N_EXPERTS, first_e, next_e).astype(jnp.int32)
    starts = ((blk == 0) | (blk_exp != jnp.roll(blk_exp, 1))) & (blk < n_used[0])
    tables = (blk_exp, n_used, starts.astype(jnp.int32), next_e[blk_exp],
              (blk_exp == last_e).astype(jnp.int32))
    xb = hp[slot_tok]
    act = moe_up(tables, xb, w1, w3, tmb=tmb, tn=1024)
    yb = moe_down(tables, act, w2, tmb=tmb, tn=512)
    d2 = dest.reshape(T, TOP_K)
    g2 = gate[:, :TOP_K]
    return x2d + g2[:, 0:1] * yb[d2[:, 0]] + g2[:, 1:2] * yb[d2[:, 1]]


def kernel(x, attn_norm_g, w_in, nsa_q_norm_g, nsa_k_norm_g, cmp_pe_k, cmp_pe_v, w_cmp_k, w_cmp_v, mla_qa_norm_g, w_q_b, mla_kva_norm_g, w_kv_b, mla_q_norm_g, mla_qr_norm_g, mla_k_norm_g, mla_kr_norm_g, w_proj_nsa, w_proj_mla, w_out, ffn_norm_g, w_ff1, w_ff3, w_ff2, w_router, w_e1, w_e3, w_e2):
    p = dict(attn_norm_g=attn_norm_g, w_in=w_in, nsa_q_norm_g=nsa_q_norm_g, nsa_k_norm_g=nsa_k_norm_g,
             cmp_pe_k=cmp_pe_k, cmp_pe_v=cmp_pe_v, w_cmp_k=w_cmp_k, w_cmp_v=w_cmp_v,
             mla_qa_norm_g=mla_qa_norm_g, w_q_b=w_q_b, mla_kva_norm_g=mla_kva_norm_g, w_kv_b=w_kv_b,
             mla_q_norm_g=mla_q_norm_g, mla_qr_norm_g=mla_qr_norm_g, mla_k_norm_g=mla_k_norm_g,
             mla_kr_norm_g=mla_kr_norm_g, w_proj_nsa=w_proj_nsa, w_proj_mla=w_proj_mla, w_out=w_out)
    B, S, D = x.shape
    depth = w_in.shape[0]
    nc = S // CMP_STRIDE
    tabs = dict(tok=_rope_tabs(jnp.arange(S), NSA_ROT, S),
                cmp=_rope_tabs(jnp.arange(nc - 1) * CMP_STRIDE + CMP_LEN - 1, NSA_ROT, nc),
                mla=_mla_tabs(S))
    x2d = x.reshape(B * S, D)
    for l in range(depth):
        x2d = _attention_block(x2d, B, S, p, l, tabs)
        if l % 2 == 0:
            x2d = _dense_ffn(x2d, ffn_norm_g[l], w_ff1[l // 2], w_ff3[l // 2], w_ff2[l // 2])
        else:
            x2d = _moe_ffn(x2d, ffn_norm_g[l], w_router[l // 2], w_e1[l // 2], w_e3[l // 2], w_e2[l // 2])
    return x2d.reshape(B, S, D)
```

```python
import functools

import numpy as np
import jax
import jax.numpy as jnp
from jax import lax
from jax.experimental import pallas as pl
from jax.experimental.pallas import tpu as pltpu

D_MODEL = 2048
HEAD_DIM = 128
NSA_HEADS = 8
NSA_KV_GROUPS = 2
NSA_HPG = NSA_HEADS // NSA_KV_GROUPS
NSA_ROT = HEAD_DIM // 4
CMP_LEN = 32
CMP_STRIDE = 16
SLC_LEN = 64
SLC_TOPK = 16
N_LOCAL_SLC = 2
WINDOW = 512
FORCE_SCORE = 1.0e4
MLA_HEADS = 8
Q_LORA = 768
KV_LORA = 512
QK_NOPE = 128
QK_ROPE = 64
V_HEAD = 128
ROPE_THETA = 500000.0
EPS = 1e-6
D_FF = 7168
N_EXPERTS = 8
TOP_K = 2

LANES = 128
MLA_QK_PAD = 256
NEG_INF = float("-inf")
LOG2E = 1.4426950408889634
BF16 = jnp.bfloat16
F32 = jnp.float32

OFF_QA = 0
OFF_G = 768
OFF_KR = 896
OFF_Q = 1024
OFF_KVA = 2048
OFF_KV = 2560
OFF_M = 4096
D_INP = 8192
NSA_KV_COLS = 3 * 2 * NSA_KV_GROUPS * HEAD_DIM
NSA_KVP_COLS = 4 * NSA_KV_GROUPS * HEAD_DIM
VT_CHUNK = 256

VMEM_LIMIT = 56 * 1024 * 1024


def _cparams(sem):
    return pltpu.CompilerParams(dimension_semantics=sem, vmem_limit_bytes=VMEM_LIMIT)


def _rms(x, g):
    ms = jnp.mean(x * x, axis=-1, keepdims=True)
    return x * lax.rsqrt(ms + EPS) * g


def _rope_lanes(y, ct, s1, s2, half):
    return y * ct + pltpu.roll(y, LANES - half, 1) * s1 + pltpu.roll(y, half, 1) * s2


def _sigmoid(x):
    return 1.0 / (1.0 + jnp.exp(-x))


def _nmm_kernel(x_ref, g_ref, w_ref, o_ref, h_scr):
    @pl.when(pl.program_id(1) == 0)
    def _():
        h_scr[...] = _rms(x_ref[...], g_ref[...]).astype(BF16)

    o_ref[...] = jnp.dot(h_scr[...], w_ref[...].astype(BF16),
                         preferred_element_type=F32).astype(o_ref.dtype)


def norm_matmul(x, g, w, *, tm, tn, out_dtype=F32):
    T, K = x.shape
    N = w.shape[1]
    return pl.pallas_call(
        _nmm_kernel,
        grid=(T // tm, N // tn),
        in_specs=[pl.BlockSpec((tm, K), lambda m, n: (m, 0)),
                  pl.BlockSpec((1, K), lambda m, n: (0, 0)),
                  pl.BlockSpec((K, tn), lambda m, n: (0, n))],
        out_specs=pl.BlockSpec((tm, tn), lambda m, n: (m, n)),
        out_shape=jax.ShapeDtypeStruct((T, N), out_dtype),
        scratch_shapes=[pltpu.VMEM((tm, K), BF16)],
        compiler_params=_cparams(("parallel", "arbitrary")),
        name="norm_matmul",
    )(x, g.reshape(1, K), w)


def _mmres_kernel(a_ref, w_ref, r_ref, o_ref):
    o_ref[...] = r_ref[...] + jnp.dot(a_ref[...], w_ref[...].astype(BF16), preferred_element_type=F32)


def matmul_residual(a, w, res, *, tm, tn):
    T, K = a.shape
    N = w.shape[1]
    return pl.pallas_call(
        _mmres_kernel,
        grid=(T // tm, N // tn),
        in_specs=[pl.BlockSpec((tm, K), lambda m, n: (m, 0)),
                  pl.BlockSpec((K, tn), lambda m, n: (0, n)),
                  pl.BlockSpec((tm, tn), lambda m, n: (m, n))],
        out_specs=pl.BlockSpec((tm, tn), lambda m, n: (m, n)),
        out_shape=jax.ShapeDtypeStruct((T, N), F32),
        compiler_params=_cparams(("parallel", "arbitrary")),
        name="matmul_residual",
    )(a, w, res)


def _prep_kernel(z_ref, ct_ref, s1_ref, s2_ref, gn_ref, qn_ref, kvp_ref, vts_ref, vtw_ref, gate_ref):
    ct, s1, s2 = ct_ref[...], s1_ref[...], s2_ref[...]
    half = NSA_ROT // 2
    gq = gn_ref[0:1, :]
    for h in range(NSA_HEADS):
        c = OFF_Q + h * HEAD_DIM
        y = _rope_lanes(_rms(z_ref[:, c:c + HEAD_DIM], gq), ct, s1, s2, half)
        qn_ref[:, h * HEAD_DIM:(h + 1) * HEAD_DIM] = (y * (LOG2E * HEAD_DIM ** -0.5)).astype(BF16)
    for br in range(3):
        for kv in range(2):
            for g in range(NSA_KV_GROUPS):
                c = ((br * 2 + kv) * NSA_KV_GROUPS + g) * HEAD_DIM
                y = z_ref[:, OFF_KV + c:OFF_KV + c + HEAD_DIM]
                if kv == 0 and br > 0:
                    y = _rope_lanes(_rms(y, gn_ref[1 + br:2 + br, :]), ct, s1, s2, half)
                if br == 0:
                    kvp_ref[:, c:c + HEAD_DIM] = y.astype(BF16)
                elif kv == 0:
                    d = (1 + br) * NSA_KV_GROUPS * HEAD_DIM + g * HEAD_DIM
                    kvp_ref[:, d:d + HEAD_DIM] = y.astype(BF16)
                else:
                    (vts_ref if br == 1 else vtw_ref)[g] = y.T.astype(BF16)
    gate_ref[...] = _sigmoid(z_ref[:, OFF_G:OFF_G + LANES])


def nsa_prep(z, tabs, gn, S, *, tm=VT_CHUNK):
    T = z.shape[0]
    ns = S // tm
    tab_spec = pl.BlockSpec((tm, LANES), lambda i: (i % ns, 0))
    vt_spec = pl.BlockSpec((None, NSA_KV_GROUPS, None, HEAD_DIM, tm), lambda i: (i // ns, 0, i % ns, 0, 0))
    vt_shape = jax.ShapeDtypeStruct((T // S, NSA_KV_GROUPS, ns, HEAD_DIM, tm), BF16)
    return pl.pallas_call(
        _prep_kernel,
        grid=(T // tm,),
        in_specs=[pl.BlockSpec((tm, OFF_M), lambda i: (i, 0)), tab_spec, tab_spec, tab_spec,
                  pl.BlockSpec((8, LANES), lambda i: (0, 0))],
        out_specs=[pl.BlockSpec((tm, NSA_HEADS * HEAD_DIM), lambda i: (i, 0)),
                   pl.BlockSpec((tm, NSA_KVP_COLS), lambda i: (i, 0)), vt_spec, vt_spec,
                   pl.BlockSpec((tm, LANES), lambda i: (i, 0))],
        out_shape=[jax.ShapeDtypeStruct((T, NSA_HEADS * HEAD_DIM), BF16),
                   jax.ShapeDtypeStruct((T, NSA_KVP_COLS), BF16), vt_shape, vt_shape,
                   jax.ShapeDtypeStruct((T, LANES), F32)],
        compiler_params=_cparams(("parallel",)),
        name="nsa_prep",
    )(z, *tabs, gn)


def _cmp_kernel(xk_ref, xv_ref, wk_ref, wv_ref, pek_ref, pev_ref, gk_ref, ct_ref, s1_ref, s2_ref,
                kct_ref, vc_ref):
    nc = xk_ref.shape[0]
    row = lax.broadcasted_iota(jnp.int32, (nc, HEAD_DIM), 0)

    def comp(x_ref, w_ref, pe_ref):
        w = w_ref[...]
        y = jnp.dot(x_ref[...], w, preferred_element_type=F32)
        ype = jnp.dot(pe_ref[...], w, preferred_element_type=F32)
        bias = ype[0:1, :HEAD_DIM] + ype[1:2, HEAD_DIM:]
        out = y[:, :HEAD_DIM] + pltpu.roll(y[:, HEAD_DIM:], nc - 1, 0) + bias
        return jnp.where(row < nc - 1, out, 0.0)

    k = _rms(comp(xk_ref, wk_ref, pek_ref), gk_ref[...])
    k = _rope_lanes(k, ct_ref[...], s1_ref[...], s2_ref[...], NSA_ROT // 2)
    kct_ref[...] = k.T.astype(BF16)
    vc_ref[...] = comp(xv_ref, wv_ref, pev_ref).astype(BF16)


def nsa_compress(x2, wk2, wv2, pek2, pev2, gk, tabs_cmp):
    B, _, nc, kk = x2.shape
    G = NSA_KV_GROUPS
    full = lambda shape: pl.BlockSpec(shape, lambda b, g: (0,) * len(shape))
    return pl.pallas_call(
        _cmp_kernel,
        grid=(B, G),
        in_specs=[pl.BlockSpec((None, None, nc, kk), lambda b, g: (b, g, 0, 0)),
                  pl.BlockSpec((None, None, nc, kk), lambda b, g: (b, G + g, 0, 0)),
                  full((kk, 2 * HEAD_DIM)), full((kk, 2 * HEAD_DIM)),
                  full((8, kk)), full((8, kk)), full((1, HEAD_DIM)),
                  full((nc, LANES)), full((nc, LANES)), full((nc, LANES))],
        out_specs=[pl.BlockSpec((None, None, HEAD_DIM, nc), lambda b, g: (b, g, 0, 0)),
                   pl.BlockSpec((None, None, nc, HEAD_DIM), lambda b, g: (b, g, 0, 0))],
        out_shape=[jax.ShapeDtypeStruct((B, G, HEAD_DIM, nc), BF16),
                   jax.ShapeDtypeStruct((B, G, nc, HEAD_DIM), BF16)],
        compiler_params=_cparams(("parallel", "parallel")),
        name="nsa_compress",
    )(x2, x2, wk2, wv2, pek2, pev2, gk, *tabs_cmp)


def _cattn_kernel(q_ref, kct_ref, vc_ref, ov_ref, oc_ref, sel_ref, *, tq):
    nc = vc_ref.shape[0]
    ns = sel_ref.shape[0]
    t0 = pl.program_id(2) * tq
    t_pos = t0 + lax.broadcasted_iota(jnp.int32, (tq, nc), 0)
    n_idx = lax.broadcasted_iota(jnp.int32, (tq, nc), 1)
    vis = (t_pos >= n_idx * CMP_STRIDE + (CMP_LEN - 1)) & (n_idx < nc - 1)
    bias = jnp.where(vis, 0.0, NEG_INF)
    kct = kct_ref[...]
    vc = vc_ref[...]
    psum = jnp.zeros((tq, nc), F32)
    for hh in range(NSA_HPG):
        s = jnp.dot(q_ref[:, hh * HEAD_DIM:(hh + 1) * HEAD_DIM], kct, preferred_element_type=F32) + bias
        m = jnp.max(s, axis=-1, keepdims=True)
        m = jnp.where(m == NEG_INF, 0.0, m)
        p = jnp.exp2(s - m)
        den = jnp.sum(p, axis=-1, keepdims=True)
        p = p * (1.0 / jnp.where(den > 0.0, den, 1.0))
        oc_ref[:, hh * HEAD_DIM:(hh + 1) * HEAD_DIM] = jnp.dot(
            p.astype(BF16), vc, preferred_element_type=F32).astype(oc_ref.dtype)
        psum = psum + p
    pt = psum.T
    hi = pt.astype(BF16)
    lo = (pt - hi.astype(F32)).astype(BF16)
    ov = ov_ref[...]
    imp = jnp.dot(ov, hi, preferred_element_type=F32) + jnp.dot(ov, lo, preferred_element_type=F32)
    blk = lax.broadcasted_iota(jnp.int32, (ns, tq), 0)
    cur = (t0 + lax.broadcasted_iota(jnp.int32, (ns, tq), 1)) // SLC_LEN
    forced = (blk == 0) | ((blk <= cur) & (blk > cur - N_LOCAL_SLC))
    val = jnp.where(blk > cur, NEG_INF, jnp.where(forced, FORCE_SCORE, imp))
    rank = jnp.zeros((ns, tq), F32)
    for i in range(ns):
        other = val[i:i + 1, :]
        ahead = (other > val) | ((other == val) & (blk > i))
        rank = rank + jnp.where(ahead, 1.0, 0.0)
    sel_ref[...] = jnp.where(rank < float(min(SLC_TOPK, ns)), 1.0, 0.0).astype(sel_ref.dtype)


def nsa_cmp_attention(qn, kct, vc, ov, B, S, *, tq):
    G = NSA_KV_GROUPS
    nq = S // tq
    nc = vc.shape[2]
    ns = S // SLC_LEN
    gw = NSA_HPG * HEAD_DIM
    return pl.pallas_call(
        functools.partial(_cattn_kernel, tq=tq),
        grid=(B, G, nq),
        in_specs=[pl.BlockSpec((tq, gw), lambda b, g, i: (b * nq + i, g)),
                  pl.BlockSpec((None, None, HEAD_DIM, nc), lambda b, g, i: (b, g, 0, 0)),
                  pl.BlockSpec((None, None, nc, HEAD_DIM), lambda b, g, i: (b, g, 0, 0)),
                  pl.BlockSpec((ns, nc), lambda b, g, i: (0, 0))],
        out_specs=[pl.BlockSpec((tq, gw), lambda b, g, i: (b * nq + i, g)),
                   pl.BlockSpec((None, None, ns, tq), lambda b, g, i: (b, g, 0, i))],
        out_shape=[jax.ShapeDtypeStruct((B * S, NSA_HEADS * HEAD_DIM), BF16),
                   jax.ShapeDtypeStruct((B, G, ns, S), BF16)],
        compiler_params=_cparams(("parallel", "parallel", "parallel")),
        name="nsa_cmp_attention",
    )(qn, kct, vc, ov)


def _flash_kernel(*refs, mode, hg, tq, tk, tv, dqk, dv):
    if mode == "sel":
        q_ref, k_ref, vt_ref, sel_ref, ex_ref, o_ref, qs, m_s, l_s, acc_s = refs
    else:
        q_ref, k_ref, vt_ref, o_ref, qs, m_s, l_s, acc_s = refs
    rows = hg * tq
    q0 = pl.program_id(2) * tq
    cd = q0 // tk
    for hh in range(hg):
        qs[hh * tq:(hh + 1) * tq, :] = q_ref[:, hh * dqk:(hh + 1) * dqk]

    def scores(c, kind):
        start = pl.multiple_of(c * tk, tk)
        s = lax.dot_general(k_ref[pl.ds(start, tk), :], qs[...], (((1,), (1,)), ((), ())),
                            preferred_element_type=F32)
        ok = None
        if kind == "diag":
            k_pos = start + lax.broadcasted_iota(jnp.int32, (tk, tq), 0)
            q_pos = q0 + lax.broadcasted_iota(jnp.int32, (tk, tq), 1)
            ok = k_pos <= q_pos
        if mode == "sel":
            chosen = jnp.dot(ex_ref[c], sel_ref[...], preferred_element_type=F32) > 0.5
            ok = chosen if ok is None else (ok & chosen)
        if ok is not None:
            bias = jnp.where(ok, 0.0, NEG_INF)
            s = s + (jnp.concatenate([bias] * hg, axis=1) if hg > 1 else bias)
        return s

    def update(c, s, carry):
        m, l, acc = carry
        m_new = jnp.maximum(m, jnp.max(s, axis=0, keepdims=True))
        p = jnp.exp2(s - m_new)
        alpha = jnp.exp2(m - m_new)
        l = alpha * l + jnp.sum(p, axis=0, keepdims=True)
        pb = p.astype(BF16)
        acc = alpha * acc
        for j in range(tk // tv):
            acc = acc + jnp.dot(vt_ref[c * (tk // tv) + j], pb[j * tv:(j + 1) * tv, :],
                                preferred_element_type=F32)
        return m_new, l, acc

    def load():
        return m_s[...], l_s[...], acc_s[...]

    def store(carry):
        m_s[...], l_s[...], acc_s[...] = carry

    def run(first, count, kind, carry):
        s_next = scores(first, kind)
        for j in range(count):
            s_cur = s_next
            if j + 1 < count:
                s_next = scores(first + j + 1, kind)
            carry = update(first + j, s_cur, carry)
        return carry

    store(run(cd, max(tq // tk, 1), "diag",
              (jnp.full((1, rows), NEG_INF, F32), jnp.zeros((1, rows), F32), jnp.zeros((dv, rows), F32))))

    def group(first, count):
        store(run(first, count, "full", load()))

    def quad(i, _):
        group(4 * i, 4)
        return 0

    lax.fori_loop(0, cd // 4, quad, 0)

    @pl.when(cd % 4 >= 2)
    def _():
        group((cd // 4) * 4, 2)

    @pl.when(cd % 2 == 1)
    def _():
        group(cd - 1, 1)

    o = acc_s[...] * (1.0 / l_s[...])
    for hh in range(hg):
        o_ref[:, hh * dv:(hh + 1) * dv] = o[:, hh * tq:(hh + 1) * tq].T.astype(o_ref.dtype)


def flash_attention(q, k, vt, B, S, *, mode, n_groups, hg, dqk, dv, kcol0, tq, tk, sel_t=None, expand=None):
    nq = S // tq
    nch = S // tk
    tv = vt.shape[-1]
    assert (tk % tq == 0 or tq % tk == 0) and S % tk == 0 and S % tq == 0 and tq % LANES == 0 and tk % tv == 0
    in_specs = [pl.BlockSpec((tq, hg * dqk), lambda b, g, i: (b * nq + i, g)),
                pl.BlockSpec((S, dqk), lambda b, g, i: (b, kcol0 + g)),
                pl.BlockSpec((None, None, S // tv, dv, tv), lambda b, g, i: (b, g, 0, 0, 0))]
    args = [q, k, vt]
    if mode == "sel":
        ns = S // SLC_LEN
        in_specs += [pl.BlockSpec((None, None, ns, tq), lambda b, g, i: (b, g, 0, i)),
                     pl.BlockSpec((nch, tk, ns), lambda b, g, i: (0, 0, 0))]
        args += [sel_t, expand]
    return pl.pallas_call(
        functools.partial(_flash_kernel, mode=mode, hg=hg, tq=tq, tk=tk, tv=tv, dqk=dqk, dv=dv),
        grid=(B, n_groups, nq),
        in_specs=in_specs,
        out_specs=pl.BlockSpec((tq, hg * dv), lambda b, g, i: (b * nq + i, g)),
        out_shape=jax.ShapeDtypeStruct((B * S, n_groups * hg * dv), BF16),
        scratch_shapes=[pltpu.VMEM((hg * tq, dqk), BF16),
                        pltpu.VMEM((1, hg * tq), F32),
                        pltpu.VMEM((1, hg * tq), F32),
                        pltpu.VMEM((dv, hg * tq), F32)],
        compiler_params=_cparams(("parallel", "parallel", "arbitrary")),
        name="flash_" + mode,
    )(*args)


def _window_kernel(q_ref, k_ref, vt_ref, o_ref, qs, *, hg, tq, dqk, dv):
    nk = WINDOW + tq
    q0 = pl.program_id(2) * tq
    start = pl.multiple_of(jnp.maximum(q0 - WINDOW, 0), tq)
    for hh in range(hg):
        qs[hh * tq:(hh + 1) * tq, :] = q_ref[:, hh * dqk:(hh + 1) * dqk]
    s = lax.dot_general(k_ref[pl.ds(start, nk), :], qs[...], (((1,), (1,)), ((), ())),
                        preferred_element_type=F32)
    k_pos = start + lax.broadcasted_iota(jnp.int32, (nk, tq), 0)
    q_pos = q0 + lax.broadcasted_iota(jnp.int32, (nk, tq), 1)
    bias = jnp.where((k_pos <= q_pos) & (q_pos - k_pos < WINDOW), 0.0, NEG_INF)
    s = s + jnp.concatenate([bias] * hg, axis=1)
    p = jnp.exp2(s - jnp.max(s, axis=0, keepdims=True))
    l = jnp.sum(p, axis=0, keepdims=True)
    pb = p.astype(BF16)
    c0 = start // tq
    o = jnp.zeros((dv, hg * tq), F32)
    for j in range(nk // tq):
        o = o + jnp.dot(vt_ref[c0 + j], pb[j * tq:(j + 1) * tq, :], preferred_element_type=F32)
    o = o * (1.0 / l)
    for hh in range(hg):
        o_ref[:, hh * dv:(hh + 1) * dv] = o[:, hh * tq:(hh + 1) * tq].T.astype(o_ref.dtype)


def window_attention(q, k, vt, B, S, *, n_groups, hg, dqk, dv, kcol0, tq):
    nq = S // tq
    assert WINDOW % tq == 0 and S >= WINDOW + tq and tq % LANES == 0
    return pl.pallas_call(
        functools.partial(_window_kernel, hg=hg, tq=tq, dqk=dqk, dv=dv),
        grid=(B, n_groups, nq),
        in_specs=[pl.BlockSpec((tq, hg * dqk), lambda b, g, i: (b * nq + i, g)),
                  pl.BlockSpec((S, dqk), lambda b, g, i: (b, kcol0 + g)),
                  pl.BlockSpec((None, None, nq, dv, tq), lambda b, g, i: (b, g, 0, 0, 0))],
        out_specs=pl.BlockSpec((tq, hg * dv), lambda b, g, i: (b * nq + i, g)),
        out_shape=jax.ShapeDtypeStruct((B * S, n_groups * hg * dv), BF16),
        scratch_shapes=[pltpu.VMEM((hg * tq, dqk), BF16)],
        compiler_params=_cparams(("parallel", "parallel", "arbitrary")),
        name="window_attention",
    )(q, k, vt)


def _mla_q_kernel(z_ref, ga_ref, w_ref, gh_ref, ct_ref, s1_ref, s2_ref, o_ref):
    h = _rms(z_ref[...], ga_ref[...]).astype(BF16)
    y = jnp.dot(h, w_ref[...], preferred_element_type=F32)
    ct, s1, s2 = ct_ref[...], s1_ref[...], s2_ref[...]
    scale = LOG2E * (QK_NOPE + QK_ROPE) ** -0.5
    for hd in range(MLA_HEADS):
        c = hd * MLA_QK_PAD
        nope = _rms(y[:, c:c + QK_NOPE], gh_ref[0:1, :])
        r = y[:, c + QK_NOPE:c + MLA_QK_PAD]
        ms = jnp.sum(r * r, axis=-1, keepdims=True) * (1.0 / QK_ROPE)
        r = _rope_lanes(r * lax.rsqrt(ms + EPS) * gh_ref[1:2, :], ct, s1, s2, QK_ROPE // 2)
        o_ref[:, c:c + QK_NOPE] = (nope * scale).astype(BF16)
        o_ref[:, c + QK_NOPE:c + MLA_QK_PAD] = (r * scale).astype(BF16)


def mla_q_proj(z, ga, wq, gh, tabs, S, *, tm):
    T = z.shape[0]
    ns = S // tm
    tab_spec = pl.BlockSpec((tm, LANES), lambda i: (i % ns, 0))
    nout = MLA_HEADS * MLA_QK_PAD
    return pl.pallas_call(
        _mla_q_kernel,
        grid=(T // tm,),
        in_specs=[pl.BlockSpec((tm, Q_LORA), lambda i: (i, OFF_QA // Q_LORA)),
                  pl.BlockSpec((1, Q_LORA), lambda i: (0, 0)),
                  pl.BlockSpec((Q_LORA, nout), lambda i: (0, 0)),
                  pl.BlockSpec((8, LANES), lambda i: (0, 0)),
                  tab_spec, tab_spec, tab_spec],
        out_specs=pl.BlockSpec((tm, nout), lambda i: (i, 0)),
        out_shape=jax.ShapeDtypeStruct((T, nout), BF16),
        compiler_params=_cparams(("parallel",)),
        name="mla_q_proj",
    )(z, ga, wq, gh, *tabs)


def _mla_kv_kernel(z_ref, zr_ref, ga_ref, w_ref, gh_ref, ct_ref, s1_ref, s2_ref, k_ref, v_ref):
    h = _rms(z_ref[...], ga_ref[...]).astype(BF16)
    y = jnp.dot(h, w_ref[...], preferred_element_type=F32)
    r = zr_ref[...]
    ms = jnp.sum(r * r, axis=-1, keepdims=True) * (1.0 / QK_ROPE)
    r = _rope_lanes(r * lax.rsqrt(ms + EPS) * gh_ref[1:2, :], ct_ref[...], s1_ref[...], s2_ref[...],
                    QK_ROPE // 2).astype(BF16)
    nv = MLA_HEADS * QK_NOPE
    for hd in range(MLA_HEADS):
        c = hd * MLA_QK_PAD
        k_ref[:, c:c + QK_NOPE] = _rms(y[:, hd * QK_NOPE:(hd + 1) * QK_NOPE], gh_ref[0:1, :]).astype(BF16)
        k_ref[:, c + QK_NOPE:c + MLA_QK_PAD] = r
    for hd in range(MLA_HEADS):
        v_ref[hd] = y[:, nv + hd * V_HEAD:nv + (hd + 1) * V_HEAD].T.astype(BF16)


def mla_kv_proj(z, ga, wkv, gh, tabs, S, *, tm=VT_CHUNK):
    T = z.shape[0]
    ns = S // tm
    tab_spec = pl.BlockSpec((tm, LANES), lambda i: (i % ns, 0))
    nk = MLA_HEADS * MLA_QK_PAD
    nv = MLA_HEADS * V_HEAD
    return pl.pallas_call(
        _mla_kv_kernel,
        grid=(T // tm,),
        in_specs=[pl.BlockSpec((tm, KV_LORA), lambda i: (i, OFF_KVA // KV_LORA)),
                  pl.BlockSpec((tm, LANES), lambda i: (i, OFF_KR // LANES)),
                  pl.BlockSpec((1, KV_LORA), lambda i: (0, 0)),
                  pl.BlockSpec((KV_LORA, MLA_HEADS * (QK_NOPE + V_HEAD)), lambda i: (0, 0)),
                  pl.BlockSpec((8, LANES), lambda i: (0, 0)),
                  tab_spec, tab_spec, tab_spec],
        out_specs=[pl.BlockSpec((tm, nk), lambda i: (i, 0)),
                   pl.BlockSpec((None, MLA_HEADS, None, V_HEAD, tm), lambda i: (i // ns, 0, i % ns, 0, 0))],
        out_shape=[jax.ShapeDtypeStruct((T, nk), BF16),
                   jax.ShapeDtypeStruct((T // S, MLA_HEADS, ns, V_HEAD, tm), BF16)],
        compiler_params=_cparams(("parallel",)),
        name="mla_kv_proj",
    )(z, z, ga, wkv, gh, *tabs)


def _mix_kernel(oc_ref, os_ref, ow_ref, ob_ref, gate_ref, wa_ref, wb_ref, za_ref, zb_ref, o_ref, a_scr):
    @pl.when(pl.program_id(1) == 0)
    def _():
        for h in range(NSA_HEADS):
            sl = slice(h * HEAD_DIM, (h + 1) * HEAD_DIM)
            a = (gate_ref[:, h:h + 1] * oc_ref[:, sl].astype(F32)
                 + gate_ref[:, NSA_HEADS + h:NSA_HEADS + h + 1] * os_ref[:, sl].astype(F32)
                 + gate_ref[:, 2 * NSA_HEADS + h:2 * NSA_HEADS + h + 1] * ow_ref[:, sl].astype(F32))
            a_scr[:, sl] = a.astype(BF16)

    pa = jnp.dot(a_scr[...], wa_ref[...].astype(BF16), preferred_element_type=F32)
    pb = jnp.dot(ob_ref[...], wb_ref[...].astype(BF16), preferred_element_type=F32)
    o_ref[...] = (_sigmoid(za_ref[...]) * pa + _sigmoid(zb_ref[...]) * pb).astype(o_ref.dtype)


def gated_mix(oc, os_, ow, ob, gates, wa, wb, z, *, tm, tn):
    T, K = oc.shape
    N = wa.shape[1]
    row = lambda w: pl.BlockSpec((tm, w), lambda m, n: (m, 0))
    return pl.pallas_call(
        _mix_kernel,
        grid=(T // tm, N // tn),
        in_specs=[row(K), row(K), row(K), row(K), row(LANES),
                  pl.BlockSpec((K, tn), lambda m, n: (0, n)),
                  pl.BlockSpec((K, tn), lambda m, n: (0, n)),
                  pl.BlockSpec((tm, tn), lambda m, n: (m, OFF_M // tn + n)),
                  pl.BlockSpec((tm, tn), lambda m, n: (m, (OFF_M + D_MODEL) // tn + n))],
        out_specs=pl.BlockSpec((tm, tn), lambda m, n: (m, n)),
        out_shape=jax.ShapeDtypeStruct((T, N), BF16),
        scratch_shapes=[pltpu.VMEM((tm, K), BF16)],
        compiler_params=_cparams(("parallel", "arbitrary")),
        name="gated_mix",
    )(oc, os_, ow, ob, gates, wa, wb, z, z)


def _ffn_up_kernel(x_ref, g_ref, w1_ref, w3_ref, o_ref, h_scr):
    @pl.when(pl.program_id(1) == 0)
    def _():
        h_scr[...] = _rms(x_ref[...], g_ref[...]).astype(BF16)

    h = h_scr[...]
    a = jnp.dot(h, w1_ref[...].astype(BF16), preferred_element_type=F32)
    b = jnp.dot(h, w3_ref[...].astype(BF16), preferred_element_type=F32)
    o_ref[...] = (a * _sigmoid(a) * b).astype(o_ref.dtype)


def ffn_up(x, g, w1, w3, *, tm, tn):
    T, K = x.shape
    N = w1.shape[1]
    return pl.pallas_call(
        _ffn_up_kernel,
        grid=(T // tm, N // tn),
        in_specs=[pl.BlockSpec((tm, K), lambda m, n: (m, 0)),
                  pl.BlockSpec((1, K), lambda m, n: (0, 0)),
                  pl.BlockSpec((K, tn), lambda m, n: (0, n)),
                  pl.BlockSpec((K, tn), lambda m, n: (0, n))],
        out_specs=pl.BlockSpec((tm, tn), lambda m, n: (m, n)),
        out_shape=jax.ShapeDtypeStruct((T, N), BF16),
        scratch_shapes=[pltpu.VMEM((tm, K), BF16)],
        compiler_params=_cparams(("parallel", "arbitrary")),
        name="ffn_up",
    )(x, g.reshape(1, K), w1, w3)


def _pack_bf16_pairs(h):
    k = h.shape[1] // 2
    hi = lax.bitcast_convert_type(h[:, :k].astype(jnp.bfloat16).astype(F32), jnp.uint32)
    lo = lax.bitcast_convert_type(h[:, k:].astype(jnp.bfloat16).astype(F32), jnp.uint32)
    return lax.bitcast_convert_type(hi | (lo >> 16), jnp.int32)


def _unpack_bf16_pairs(xp):
    xp = lax.bitcast_convert_type(xp, jnp.uint32)
    hi = lax.bitcast_convert_type(xp & jnp.uint32(0xFFFF0000), F32)
    lo = lax.bitcast_convert_type(xp << 16, F32)
    return hi.astype(BF16), lo.astype(BF16)


def _router_kernel(x_ref, g_ref, wr_ref, h_ref, idx_ref, gate_ref):
    h = _rms(x_ref[...], g_ref[...])
    is_pad = pl.program_id(0) == pl.num_programs(0) - 1
    h_ref[...] = jnp.where(is_pad, 0, _pack_bf16_pairs(h))
    logits = jnp.dot(h, wr_ref[...], preferred_element_type=F32, precision=lax.Precision.HIGHEST)
    lane = lax.broadcasted_iota(jnp.int32, logits.shape, 1).astype(F32)
    logits = jnp.where(lane < float(N_EXPERTS), logits, NEG_INF)
    m1 = jnp.max(logits, axis=-1, keepdims=True)
    i1 = jnp.min(jnp.where(logits == m1, lane, float(LANES)), axis=-1, keepdims=True)
    rest = jnp.where(lane == i1, NEG_INF, logits)
    m2 = jnp.max(rest, axis=-1, keepdims=True)
    i2 = jnp.min(jnp.where(rest == m2, lane, float(LANES)), axis=-1, keepdims=True)
    e = jnp.exp(m2 - m1)
    den = 1.0 + e
    idx_ref[...] = jnp.where(lane == 0.0, i1, jnp.where(lane == 1.0, i2, 0.0)).astype(jnp.int32)
    gate_ref[...] = jnp.where(lane == 0.0, 1.0 / den, jnp.where(lane == 1.0, e / den, 0.0))


def router(x, g, wr_pad, *, tm):
    T, K = x.shape
    nt = T // tm
    return pl.pallas_call(
        _router_kernel,
        grid=(nt + 1,),
        in_specs=[pl.BlockSpec((tm, K), lambda i: (jnp.minimum(i, nt - 1), 0)),
                  pl.BlockSpec((1, K), lambda i: (0, 0)),
                  pl.BlockSpec((K, LANES), lambda i: (0, 0))],
        out_specs=[pl.BlockSpec((tm, K // 2), lambda i: (i, 0)),
                   pl.BlockSpec((tm, LANES), lambda i: (jnp.minimum(i, nt - 1), 0)),
                   pl.BlockSpec((tm, LANES), lambda i: (jnp.minimum(i, nt - 1), 0))],
        out_shape=[jax.ShapeDtypeStruct((T + tm, K // 2), jnp.int32),
                   jax.ShapeDtypeStruct((T, LANES), jnp.int32),
                   jax.ShapeDtypeStruct((T, LANES), F32)],
        compiler_params=_cparams(("arbitrary",)),
        name="router",
    )(x, g.reshape(1, K), wr_pad)


def _moe_weight_stream(w_hbms, stages, casts, sems, be_ref, nu_ref, gs_ref, ne_ref, lg_ref, tn):
    n = pl.program_id(0)
    r = pl.program_id(1)
    used = r < nu_ref[0]

    def copies(e, nt):
        c0 = pl.multiple_of(nt * tn, tn)
        return [pltpu.make_async_copy(w.at[e, :, pl.ds(c0, tn)], st, sems.at[i])
                for i, (w, st) in enumerate(zip(w_hbms, stages))]

    @pl.when((n == 0) & (r == 0))
    def _():
        for c in copies(be_ref[0], 0):
            c.start()

    @pl.when(used & (gs_ref[r] == 1))
    def _():
        for c in copies(be_ref[r], n):
            c.wait()
        for st, wb in zip(stages, casts):
            wb[...] = st[...].astype(BF16)
        last = lg_ref[r] == 1

        @pl.when(jnp.logical_not(last & (n == pl.num_programs(0) - 1)))
        def _():
            for c in copies(ne_ref[r], n + last.astype(jnp.int32)):
                c.start()

    return used


def _moe_up_kernel(be_ref, nu_ref, gs_ref, ne_ref, lg_ref, x_ref, w1_hbm, w3_hbm, o_ref,
                   st1, st3, w1_s, w3_s, sems, *, tn):
    used = _moe_weight_stream((w1_hbm, w3_hbm), (st1, st3), (w1_s, w3_s), sems,
                              be_ref, nu_ref, gs_ref, ne_ref, lg_ref, tn)

    @pl.when(used)
    def _():
        xa, xb = _unpack_bf16_pairs(x_ref[...])
        k2 = xa.shape[1]
        a = (jnp.dot(xa, w1_s[:k2, :], preferred_element_type=F32)
             + jnp.dot(xb, w1_s[k2:, :], preferred_element_type=F32))
        b = (jnp.dot(xa, w3_s[:k2, :], preferred_element_type=F32)
             + jnp.dot(xb, w3_s[k2:, :], preferred_element_type=F32))
        o_ref[...] = (a * _sigmoid(a) * b).astype(o_ref.dtype)

    @pl.when(jnp.logical_not(used))
    def _():
        o_ref[...] = jnp.zeros(o_ref.shape, o_ref.dtype)


def moe_up(tables, xb, w1, w3, *, tmb, tn):
    n_slot = xb.shape[0]
    K, N = w1.shape[1], w1.shape[2]
    row = lambda r, nu: jnp.minimum(r, nu[0] - 1)
    return pl.pallas_call(
        functools.partial(_moe_up_kernel, tn=tn),
        grid_spec=pltpu.PrefetchScalarGridSpec(
            num_scalar_prefetch=5,
            grid=(N // tn, n_slot // tmb),
            in_specs=[pl.BlockSpec((tmb, K // 2), lambda n, r, be, nu, gs, ne, lg: (row(r, nu), 0)),
                      pl.BlockSpec(memory_space=pl.ANY), pl.BlockSpec(memory_space=pl.ANY)],
            out_specs=pl.BlockSpec((tmb, tn), lambda n, r, be, nu, gs, ne, lg: (r, n)),
            scratch_shapes=[pltpu.VMEM((K, tn), F32), pltpu.VMEM((K, tn), F32),
                            pltpu.VMEM((K, tn), BF16), pltpu.VMEM((K, tn), BF16),
                            pltpu.SemaphoreType.DMA((2,))]),
        out_shape=jax.ShapeDtypeStruct((n_slot, N), BF16),
        compiler_params=_cparams(("arbitrary", "arbitrary")),
        name="moe_up",
    )(*tables, xb, w1, w3)


def _moe_down_kernel(be_ref, nu_ref, gs_ref, ne_ref, lg_ref, a_ref, w2_hbm, o_ref, st2, w2_s, sems, *, tn):
    used = _moe_weight_stream((w2_hbm,), (st2,), (w2_s,), sems, be_ref, nu_ref, gs_ref, ne_ref, lg_ref, tn)

    @pl.when(used)
    def _():
        o_ref[...] = jnp.dot(a_ref[...], w2_s[...], preferred_element_type=F32)

    @pl.when(jnp.logical_not(used))
    def _():
        o_ref[...] = jnp.zeros(o_ref.shape, o_ref.dtype)


def moe_down(tables, act, w2, *, tmb, tn):
    n_slot, K = act.shape
    N = w2.shape[2]
    row = lambda r, nu: jnp.minimum(r, nu[0] - 1)
    return pl.pallas_call(
        functools.partial(_moe_down_kernel, tn=tn),
        grid_spec=pltpu.PrefetchScalarGridSpec(
            num_scalar_prefetch=5,
            grid=(N // tn, n_slot // tmb),
            in_specs=[pl.BlockSpec((tmb, K), lambda n, r, be, nu, gs, ne, lg: (row(r, nu), 0)),
                      pl.BlockSpec(memory_space=pl.ANY)],
            out_specs=pl.BlockSpec((tmb, tn), lambda n, r, be, nu, gs, ne, lg: (r, n)),
            scratch_shapes=[pltpu.VMEM((K, tn), F32), pltpu.VMEM((K, tn), BF16),
                            pltpu.SemaphoreType.DMA((1,))]),
        out_shape=jax.ShapeDtypeStruct((n_slot, N), F32),
        compiler_params=_cparams(("arbitrary", "arbitrary")),
        name="moe_down",
    )(*tables, act, w2)


def _rope_tabs(pos, rot_dim, n_rows):
    half = rot_dim // 2
    inv = 1.0 / (ROPE_THETA ** (jnp.arange(0, rot_dim, 2, dtype=F32) / rot_dim))
    ang = jnp.asarray(pos).astype(F32)[:, None] * inv[None, :]
    c, s = jnp.cos(ang), jnp.sin(ang)
    z = jnp.zeros_like(c)
    pad = lambda a, fill: jnp.pad(a, ((0, n_rows - a.shape[0]), (0, LANES - a.shape[1])), constant_values=fill)
    return pad(jnp.concatenate([c, c], 1), 1.0), pad(jnp.concatenate([-s, z], 1), 0.0), \
        pad(jnp.concatenate([z, s], 1), 0.0)


def _mla_tabs(S):
    ct, s1, s2 = _rope_tabs(jnp.arange(S), QK_ROPE, S)
    lane = jnp.arange(LANES)[None, :]
    return jnp.where(lane < QK_ROPE, ct, 0.0), s1, s2


def _pad_cols(w, n):
    return jnp.pad(w, ((0, 0), (0, n - w.shape[1])))


def _layout_w_in(w):
    sp = np.cumsum([0, NSA_HEADS * HEAD_DIM, NSA_KV_COLS, 3 * NSA_HEADS, Q_LORA, KV_LORA, QK_ROPE, 2 * D_MODEL])
    q, kv, g, qa, kva, kr, m = [w[:, sp[i]:sp[i + 1]] for i in range(7)]
    return jnp.concatenate([qa, _pad_cols(g, LANES), _pad_cols(kr, LANES), q, kva, kv, m], axis=1).astype(BF16)


def _layout_w_q_b(w):
    w = w.reshape(Q_LORA, MLA_HEADS, QK_NOPE + QK_ROPE)
    w = jnp.pad(w, ((0, 0), (0, 0), (0, MLA_QK_PAD - QK_NOPE - QK_ROPE)))
    return w.reshape(Q_LORA, MLA_HEADS * MLA_QK_PAD).astype(BF16)


def _layout_w_kv_b(w):
    w = w.reshape(KV_LORA, MLA_HEADS, QK_NOPE + V_HEAD)
    return jnp.concatenate([w[:, :, :QK_NOPE].reshape(KV_LORA, -1), w[:, :, QK_NOPE:].reshape(KV_LORA, -1)],
                           axis=1).astype(BF16)


def _layout_w_cmp(w):
    h = CMP_LEN // 2
    return jnp.concatenate([w[:h].reshape(h * HEAD_DIM, HEAD_DIM), w[h:].reshape(h * HEAD_DIM, HEAD_DIM)],
                           axis=1).astype(BF16)


def _layout_pe(pe):
    return jnp.pad(pe.reshape(2, (CMP_LEN // 2) * HEAD_DIM), ((0, 6), (0, 0))).astype(BF16)


def _overlap(nc, ns):
    n = np.arange(nc)[None, :] * CMP_STRIDE
    j = np.arange(ns)[:, None] * SLC_LEN
    ov = (n <= j + SLC_LEN - 1) & (j <= n + CMP_LEN - 1) & (np.arange(nc)[None, :] < nc - 1)
    return jnp.asarray(ov.astype(np.float32), BF16)


def _expand(ns, S, tk):
    e = ((np.arange(S // tk)[:, None, None] * tk + np.arange(tk)[None, :, None]) // SLC_LEN
         == np.arange(ns)[None, None, :])
    return jnp.asarray(e.astype(np.float32), BF16)


def _layout_layers(p):
    depth = p['w_in'].shape[0]
    lane_row = lambda g: jnp.pad(g, ((0, 0), (0, LANES - g.shape[-1])))[:, None, :]
    z4 = jnp.zeros((depth, 4, HEAD_DIM), F32)
    z6 = jnp.zeros((depth, 6, LANES), F32)
    return dict(
        w_in=jax.vmap(_layout_w_in)(p['w_in']),
        gn=jnp.concatenate([p['nsa_q_norm_g'][:, None, :], p['nsa_k_norm_g'], z4], axis=1),
        w_ck=jax.vmap(_layout_w_cmp)(p['w_cmp_k']), w_cv=jax.vmap(_layout_w_cmp)(p['w_cmp_v']),
        pe_k=jax.vmap(_layout_pe)(p['cmp_pe_k']), pe_v=jax.vmap(_layout_pe)(p['cmp_pe_v']),
        gh_q=jnp.concatenate([p['mla_q_norm_g'][:, None, :], lane_row(p['mla_qr_norm_g']), z6], axis=1),
        gh_k=jnp.concatenate([p['mla_k_norm_g'][:, None, :], lane_row(p['mla_kr_norm_g']), z6], axis=1),
        w_qb=jax.vmap(_layout_w_q_b)(p['w_q_b']), w_kvb=jax.vmap(_layout_w_kv_b)(p['w_kv_b']))


def _attention_block(x2d, B, S, p, pw, l, tabs):
    T = x2d.shape[0]
    z = norm_matmul(x2d, p['attn_norm_g'][l], pw['w_in'][l], tm=min(2048, T), tn=256)
    qn, kvp, vts, vtw, gates = nsa_prep(z, tabs['tok'], pw['gn'][l], S)

    nc = S // CMP_STRIDE
    x2 = kvp[:, :2 * NSA_KV_GROUPS * HEAD_DIM].reshape(B, S, 2 * NSA_KV_GROUPS, HEAD_DIM)
    x2 = x2.transpose(0, 2, 1, 3).reshape(B, 2 * NSA_KV_GROUPS, nc, CMP_STRIDE * HEAD_DIM)
    kct, vc = nsa_compress(x2, pw['w_ck'][l], pw['w_cv'][l], pw['pe_k'][l], pw['pe_v'][l],
                           p['nsa_k_norm_g'][l][0:1], tabs['cmp'])
    ns = S // SLC_LEN
    oc, sel_t = nsa_cmp_attention(qn, kct, vc, _overlap(nc, ns), B, S, tq=min(256, S))
    nsa = dict(n_groups=NSA_KV_GROUPS, hg=NSA_HPG, dqk=HEAD_DIM, dv=HEAD_DIM)
    tks = min(512, S)
    os_ = flash_attention(qn, kvp, vts, B, S, mode="sel", kcol0=4, tq=256, tk=tks,
                          sel_t=sel_t, expand=_expand(ns, S, tks), **nsa)
    ow = window_attention(qn, kvp, vtw, B, S, kcol0=6, tq=VT_CHUNK, **nsa)

    qm = mla_q_proj(z, p['mla_qa_norm_g'][l][None], pw['w_qb'][l], pw['gh_q'][l], tabs['mla'], S, tm=256)
    km, vmt = mla_kv_proj(z, p['mla_kva_norm_g'][l][None], pw['w_kvb'][l], pw['gh_k'][l], tabs['mla'], S)
    tkm = min(512, S)
    ob = flash_attention(qm, km, vmt, B, S, mode="causal",
                         n_groups=MLA_HEADS, hg=1, dqk=MLA_QK_PAD, dv=V_HEAD, kcol0=0, tq=min(1024, S), tk=tkm)

    tm = min(1024, T)
    mix = gated_mix(oc, os_, ow, ob, gates, p['w_proj_nsa'][l], p['w_proj_mla'][l], z, tm=tm, tn=512)
    return matmul_residual(mix, p['w_out'][l], x2d, tm=min(2048, T), tn=512)


def _dense_ffn(x2d, g, w1, w3, w2):
    T = x2d.shape[0]
    act = ffn_up(x2d, g, w1.astype(BF16), w3.astype(BF16), tm=min(1024, T), tn=512)
    return matmul_residual(act, w2.astype(BF16), x2d, tm=min(1024, T), tn=256)


def _cumsum_rows(oh, blk=128):
    A, E = oh.shape
    nb = A // blk
    x = oh.astype(F32).reshape(nb, blk, E)
    within = jnp.einsum('ij,bje->bie', jnp.tril(jnp.ones((blk, blk), F32)), x)
    before = jnp.tril(jnp.ones((nb, nb), F32), -1) @ within[:, -1, :]
    return (within + before[:, None, :]).astype(jnp.int32).reshape(A, E)


def _moe_ffn(x2d, g, w_router, w1, w3, w2, *, tmb=512):
    T = x2d.shape[0]
    hp, idx, gate = router(x2d, g, _pad_cols(w_router, LANES), tm=256)
    A = T * TOP_K
    e_flat = idx[:, :TOP_K].reshape(A)
    tok_flat = jnp.repeat(jnp.arange(T, dtype=jnp.int32), TOP_K)
    oh = (e_flat[:, None] == jnp.arange(N_EXPERTS)[None, :]).astype(jnp.int32)
    csum = _cumsum_rows(oh)
    rank = jnp.sum(oh * csum, axis=1) - 1
    counts = csum[-1]
    padded = (counts + tmb - 1) // tmb * tmb
    pad_end = jnp.cumsum(padded)
    dest = (pad_end - padded)[e_flat] + rank
    n_blk = -(-A // tmb) + N_EXPERTS
    n_slot = n_blk * tmb
    slot_tok = jnp.full((n_slot,), T, jnp.int32).at[dest].set(tok_flat)
    eidx = jnp.arange(N_EXPERTS, dtype=jnp.int32)
    blk = jnp.arange(n_blk, dtype=jnp.int32)
    blk_exp = jnp.minimum(jnp.sum((pad_end[None, :] <= (blk * tmb)[:, None]).astype(jnp.int32), axis=1),
                          N_EXPERTS - 1).astype(jnp.int32)
    n_used = (pad_end[-1:] // tmb).astype(jnp.int32)
    present = counts > 0
    first_e = jnp.min(jnp.where(present, eidx, N_EXPERTS))
    last_e = jnp.max(jnp.where(present, eidx, -1))
    later = jnp.where(present[None, :] & (eidx[None, :] > eidx[:, None]), eidx[None, :], N_EXPERTS)
    next_e = jnp.min(later, axis=1)
    next_e = jnp.where(next_e == N_EXPERTS, first_e, next_e).astype(jnp.int32)
    starts = ((blk == 0) | (blk_exp != jnp.roll(blk_exp, 1))) & (blk < n_used[0])
    tables = (blk_exp, n_used, starts.astype(jnp.int32), next_e[blk_exp],
              (blk_exp == last_e).astype(jnp.int32))
    xb = hp[slot_tok]
    act = moe_up(tables, xb, w1, w3, tmb=tmb, tn=1024)
    yb = moe_down(tables, act, w2, tmb=tmb, tn=512)
    d2 = dest.reshape(T, TOP_K)
    g2 = gate[:, :TOP_K]
    return x2d + g2[:, 0:1] * yb[d2[:, 0]] + g2[:, 1:2] * yb[d2[:, 1]]


def kernel(x, attn_norm_g, w_in, nsa_q_norm_g, nsa_k_norm_g, cmp_pe_k, cmp_pe_v, w_cmp_k, w_cmp_v, mla_qa_norm_g, w_q_b, mla_kva_norm_g, w_kv_b, mla_q_norm_g, mla_qr_norm_g, mla_k_norm_g, mla_kr_norm_g, w_proj_nsa, w_proj_mla, w_out, ffn_norm_g, w_ff1, w_ff3, w_ff2, w_router, w_e1, w_e3, w_e2):
    p = dict(attn_norm_g=attn_norm_g, w_in=w_in, nsa_q_norm_g=nsa_q_norm_g, nsa_k_norm_g=nsa_k_norm_g,
             cmp_pe_k=cmp_pe_k, cmp_pe_v=cmp_pe_v, w_cmp_k=w_cmp_k, w_cmp_v=w_cmp_v,
             mla_qa_norm_g=mla_qa_norm_g, w_q_b=w_q_b, mla_kva_norm_g=mla_kva_norm_g, w_kv_b=w_kv_b,
             mla_q_norm_g=mla_q_norm_g, mla_qr_norm_g=mla_qr_norm_g, mla_k_norm_g=mla_k_norm_g,
             mla_kr_norm_g=mla_kr_norm_g, w_proj_nsa=w_proj_nsa, w_proj_mla=w_proj_mla, w_out=w_out)
    B, S, D = x.shape
    depth = w_in.shape[0]
    nc = S // CMP_STRIDE
    tabs = dict(tok=_rope_tabs(jnp.arange(S), NSA_ROT, S),
                cmp=_rope_tabs(jnp.arange(nc - 1) * CMP_STRIDE + CMP_LEN - 1, NSA_ROT, nc),
                mla=_mla_tabs(S))
    pw = _layout_layers(p)
    x2d = x.reshape(B * S, D)
    for l in range(depth):
        x2d = _attention_block(x2d, B, S, p, pw, l, tabs)
        if l % 2 == 0:
            x2d = _dense_ffn(x2d, ffn_norm_g[l], w_ff1[l // 2], w_ff3[l // 2], w_ff2[l // 2])
        else:
            x2d = _moe_ffn(x2d, ffn_norm_g[l], w_router[l // 2], w_e1[l // 2], w_e3[l // 2], w_e2[l // 2])
    return x2d.reshape(B, S, D)
```

```python
import functools

import numpy as np
import jax
import jax.numpy as jnp
from jax import lax
from jax.experimental import pallas as pl
from jax.experimental.pallas import tpu as pltpu

D_MODEL = 2048
HEAD_DIM = 128
NSA_HEADS = 8
NSA_KV_GROUPS = 2
NSA_HPG = NSA_HEADS // NSA_KV_GROUPS
NSA_ROT = HEAD_DIM // 4
CMP_LEN = 32
CMP_STRIDE = 16
SLC_LEN = 64
SLC_TOPK = 16
N_LOCAL_SLC = 2
WINDOW = 512
FORCE_SCORE = 1.0e4
MLA_HEADS = 8
Q_LORA = 768
KV_LORA = 512
QK_NOPE = 128
QK_ROPE = 64
V_HEAD = 128
ROPE_THETA = 500000.0
EPS = 1e-6
D_FF = 7168
N_EXPERTS = 8
TOP_K = 2

LANES = 128
MLA_QK_PAD = 256
NEG_INF = float("-inf")
LOG2E = 1.4426950408889634
BF16 = jnp.bfloat16
F32 = jnp.float32

OFF_QA = 0
OFF_G = 768
OFF_KR = 896
OFF_Q = 1024
OFF_KVA = 2048
OFF_KV = 2560
OFF_M = 4096
D_INP = 8192
NSA_KV_COLS = 3 * 2 * NSA_KV_GROUPS * HEAD_DIM
NSA_KVP_COLS = 4 * NSA_KV_GROUPS * HEAD_DIM
VT_CHUNK = 256

VMEM_LIMIT = 56 * 1024 * 1024


def _cparams(sem):
    return pltpu.CompilerParams(dimension_semantics=sem, vmem_limit_bytes=VMEM_LIMIT)


def _rms(x, g):
    ms = jnp.mean(x * x, axis=-1, keepdims=True)
    return x * lax.rsqrt(ms + EPS) * g


def _rope_lanes(y, ct, s1, s2, half):
    return y * ct + pltpu.roll(y, LANES - half, 1) * s1 + pltpu.roll(y, half, 1) * s2


def _sigmoid(x):
    return 1.0 / (1.0 + jnp.exp(-x))


def _nmm_kernel(x_ref, g_ref, w_ref, o_ref, h_scr):
    @pl.when(pl.program_id(1) == 0)
    def _():
        h_scr[...] = _rms(x_ref[...], g_ref[...]).astype(BF16)

    o_ref[...] = jnp.dot(h_scr[...], w_ref[...].astype(BF16),
                         preferred_element_type=F32).astype(o_ref.dtype)


def norm_matmul(x, g, w, *, tm, tn, out_dtype=F32):
    T, K = x.shape
    N = w.shape[1]
    return pl.pallas_call(
        _nmm_kernel,
        grid=(T // tm, N // tn),
        in_specs=[pl.BlockSpec((tm, K), lambda m, n: (m, 0)),
                  pl.BlockSpec((1, K), lambda m, n: (0, 0)),
                  pl.BlockSpec((K, tn), lambda m, n: (0, n))],
        out_specs=pl.BlockSpec((tm, tn), lambda m, n: (m, n)),
        out_shape=jax.ShapeDtypeStruct((T, N), out_dtype),
        scratch_shapes=[pltpu.VMEM((tm, K), BF16)],
        compiler_params=_cparams(("parallel", "arbitrary")),
        name="norm_matmul",
    )(x, g.reshape(1, K), w)


def _mmres_kernel(a_ref, w_ref, r_ref, o_ref):
    o_ref[...] = r_ref[...] + jnp.dot(a_ref[...], w_ref[...].astype(BF16), preferred_element_type=F32)


def matmul_residual(a, w, res, *, tm, tn):
    T, K = a.shape
    N = w.shape[1]
    return pl.pallas_call(
        _mmres_kernel,
        grid=(T // tm, N // tn),
        in_specs=[pl.BlockSpec((tm, K), lambda m, n: (m, 0)),
                  pl.BlockSpec((K, tn), lambda m, n: (0, n)),
                  pl.BlockSpec((tm, tn), lambda m, n: (m, n))],
        out_specs=pl.BlockSpec((tm, tn), lambda m, n: (m, n)),
        out_shape=jax.ShapeDtypeStruct((T, N), F32),
        compiler_params=_cparams(("parallel", "arbitrary")),
        name="matmul_residual",
    )(a, w, res)


def _prep_kernel(zg_ref, zq_ref, zc_ref, zs_ref, zw_ref, ct_ref, s1_ref, s2_ref, gn_ref,
                 qn_ref, kvp_ref, vts_ref, vtw_ref, gate_ref):
    ct, s1, s2 = ct_ref[...], s1_ref[...], s2_ref[...]
    half = NSA_ROT // 2
    gq = gn_ref[0:1, :]
    for h in range(NSA_HEADS):
        y = _rope_lanes(_rms(zq_ref[:, h * HEAD_DIM:(h + 1) * HEAD_DIM], gq), ct, s1, s2, half)
        qn_ref[:, h * HEAD_DIM:(h + 1) * HEAD_DIM] = (y * (LOG2E * HEAD_DIM ** -0.5)).astype(BF16)
    for br, zb_ref in enumerate((zc_ref, zs_ref, zw_ref)):
        for kv in range(2):
            for g in range(NSA_KV_GROUPS):
                c = (kv * NSA_KV_GROUPS + g) * HEAD_DIM
                y = zb_ref[:, c:c + HEAD_DIM]
                if kv == 0 and br > 0:
                    y = _rope_lanes(_rms(y, gn_ref[1 + br:2 + br, :]), ct, s1, s2, half)
                if br == 0:
                    kvp_ref[:, c:c + HEAD_DIM] = y.astype(BF16)
                elif kv == 0:
                    d = (1 + br) * NSA_KV_GROUPS * HEAD_DIM + g * HEAD_DIM
                    kvp_ref[:, d:d + HEAD_DIM] = y.astype(BF16)
                else:
                    (vts_ref if br == 1 else vtw_ref)[g] = y.T.astype(BF16)
    gate_ref[...] = _sigmoid(zg_ref[...])


def nsa_prep(z, tabs, gn, S, *, tm=VT_CHUNK):
    T = z.shape[0]
    ns = S // tm
    tab_spec = pl.BlockSpec((tm, LANES), lambda i: (i % ns, 0))
    vt_spec = pl.BlockSpec((None, NSA_KV_GROUPS, None, HEAD_DIM, tm), lambda i: (i // ns, 0, i % ns, 0, 0))
    vt_shape = jax.ShapeDtypeStruct((T // S, NSA_KV_GROUPS, ns, HEAD_DIM, tm), BF16)
    wq, wb = NSA_HEADS * HEAD_DIM, 2 * NSA_KV_GROUPS * HEAD_DIM
    return pl.pallas_call(
        _prep_kernel,
        grid=(T // tm,),
        in_specs=[pl.BlockSpec((tm, LANES), lambda i: (i, OFF_G // LANES)),
                  pl.BlockSpec((tm, wq), lambda i: (i, OFF_Q // wq)),
                  pl.BlockSpec((tm, wb), lambda i: (i, OFF_KV // wb)),
                  pl.BlockSpec((tm, wb), lambda i: (i, OFF_KV // wb + 1)),
                  pl.BlockSpec((tm, wb), lambda i: (i, OFF_KV // wb + 2)),
                  tab_spec, tab_spec, tab_spec, pl.BlockSpec((8, LANES), lambda i: (0, 0))],
        out_specs=[pl.BlockSpec((tm, NSA_HEADS * HEAD_DIM), lambda i: (i, 0)),
                   pl.BlockSpec((tm, NSA_KVP_COLS), lambda i: (i, 0)), vt_spec, vt_spec,
                   pl.BlockSpec((tm, LANES), lambda i: (i, 0))],
        out_shape=[jax.ShapeDtypeStruct((T, NSA_HEADS * HEAD_DIM), BF16),
                   jax.ShapeDtypeStruct((T, NSA_KVP_COLS), BF16), vt_shape, vt_shape,
                   jax.ShapeDtypeStruct((T, LANES), F32)],
        compiler_params=_cparams(("parallel",)),
        name="nsa_prep",
    )(z, z, z, z, z, *tabs, gn)


def _cmp_kernel(xk_ref, xv_ref, wk_ref, wv_ref, pek_ref, pev_ref, gk_ref, ct_ref, s1_ref, s2_ref,
                kct_ref, vc_ref):
    nc = xk_ref.shape[0]
    row = lax.broadcasted_iota(jnp.int32, (nc, HEAD_DIM), 0)

    def comp(x_ref, w_ref, pe_ref):
        w = w_ref[...]
        y = jnp.dot(x_ref[...], w, preferred_element_type=F32)
        ype = jnp.dot(pe_ref[...], w, preferred_element_type=F32)
        bias = ype[0:1, :HEAD_DIM] + ype[1:2, HEAD_DIM:]
        out = y[:, :HEAD_DIM] + pltpu.roll(y[:, HEAD_DIM:], nc - 1, 0) + bias
        return jnp.where(row < nc - 1, out, 0.0)

    k = _rms(comp(xk_ref, wk_ref, pek_ref), gk_ref[...])
    k = _rope_lanes(k, ct_ref[...], s1_ref[...], s2_ref[...], NSA_ROT // 2)
    kct_ref[...] = k.T.astype(BF16)
    vc_ref[...] = comp(xv_ref, wv_ref, pev_ref).astype(BF16)


def nsa_compress(x2, wk2, wv2, pek2, pev2, gk, tabs_cmp):
    B, _, nc, kk = x2.shape
    G = NSA_KV_GROUPS
    full = lambda shape: pl.BlockSpec(shape, lambda b, g: (0,) * len(shape))
    return pl.pallas_call(
        _cmp_kernel,
        grid=(B, G),
        in_specs=[pl.BlockSpec((None, None, nc, kk), lambda b, g: (b, g, 0, 0)),
                  pl.BlockSpec((None, None, nc, kk), lambda b, g: (b, G + g, 0, 0)),
                  full((kk, 2 * HEAD_DIM)), full((kk, 2 * HEAD_DIM)),
                  full((8, kk)), full((8, kk)), full((1, HEAD_DIM)),
                  full((nc, LANES)), full((nc, LANES)), full((nc, LANES))],
        out_specs=[pl.BlockSpec((None, None, HEAD_DIM, nc), lambda b, g: (b, g, 0, 0)),
                   pl.BlockSpec((None, None, nc, HEAD_DIM), lambda b, g: (b, g, 0, 0))],
        out_shape=[jax.ShapeDtypeStruct((B, G, HEAD_DIM, nc), BF16),
                   jax.ShapeDtypeStruct((B, G, nc, HEAD_DIM), BF16)],
        compiler_params=_cparams(("parallel", "parallel")),
        name="nsa_compress",
    )(x2, x2, wk2, wv2, pek2, pev2, gk, *tabs_cmp)


def _cattn_kernel(q_ref, kct_ref, vc_ref, ov_ref, oc_ref, sel_ref, *, tq):
    nc = vc_ref.shape[0]
    ns = sel_ref.shape[0]
    t0 = pl.program_id(2) * tq
    t_pos = t0 + lax.broadcasted_iota(jnp.int32, (tq, nc), 0)
    n_idx = lax.broadcasted_iota(jnp.int32, (tq, nc), 1)
    vis = (t_pos >= n_idx * CMP_STRIDE + (CMP_LEN - 1)) & (n_idx < nc - 1)
    bias = jnp.where(vis, 0.0, NEG_INF)
    kct = kct_ref[...]
    vc = vc_ref[...]
    psum = jnp.zeros((tq, nc), F32)
    for hh in range(NSA_HPG):
        s = jnp.dot(q_ref[:, hh * HEAD_DIM:(hh + 1) * HEAD_DIM], kct, preferred_element_type=F32) + bias
        m = jnp.max(s, axis=-1, keepdims=True)
        m = jnp.where(m == NEG_INF, 0.0, m)
        p = jnp.exp2(s - m)
        den = jnp.sum(p, axis=-1, keepdims=True)
        p = p * (1.0 / jnp.where(den > 0.0, den, 1.0))
        oc_ref[:, hh * HEAD_DIM:(hh + 1) * HEAD_DIM] = jnp.dot(
            p.astype(BF16), vc, preferred_element_type=F32).astype(oc_ref.dtype)
        psum = psum + p
    pt = psum.T
    hi = pt.astype(BF16)
    lo = (pt - hi.astype(F32)).astype(BF16)
    ov = ov_ref[...]
    imp = jnp.dot(ov, hi, preferred_element_type=F32) + jnp.dot(ov, lo, preferred_element_type=F32)
    blk = lax.broadcasted_iota(jnp.int32, (ns, tq), 0)
    cur = (t0 + lax.broadcasted_iota(jnp.int32, (ns, tq), 1)) // SLC_LEN
    forced = (blk == 0) | ((blk <= cur) & (blk > cur - N_LOCAL_SLC))
    val = jnp.where(blk > cur, NEG_INF, jnp.where(forced, FORCE_SCORE, imp))
    rank = jnp.zeros((ns, tq), F32)
    for i in range(ns):
        other = val[i:i + 1, :]
        ahead = (other > val) | ((other == val) & (blk > i))
        rank = rank + jnp.where(ahead, 1.0, 0.0)
    sel_ref[...] = jnp.where(rank < float(min(SLC_TOPK, ns)), 1.0, 0.0).astype(sel_ref.dtype)


def nsa_cmp_attention(qn, kct, vc, ov, B, S, *, tq):
    G = NSA_KV_GROUPS
    nq = S // tq
    nc = vc.shape[2]
    ns = S // SLC_LEN
    gw = NSA_HPG * HEAD_DIM
    return pl.pallas_call(
        functools.partial(_cattn_kernel, tq=tq),
        grid=(B, G, nq),
        in_specs=[pl.BlockSpec((tq, gw), lambda b, g, i: (b * nq + i, g)),
                  pl.BlockSpec((None, None, HEAD_DIM, nc), lambda b, g, i: (b, g, 0, 0)),
                  pl.BlockSpec((None, None, nc, HEAD_DIM), lambda b, g, i: (b, g, 0, 0)),
                  pl.BlockSpec((ns, nc), lambda b, g, i: (0, 0))],
        out_specs=[pl.BlockSpec((tq, gw), lambda b, g, i: (b * nq + i, g)),
                   pl.BlockSpec((None, None, ns, tq), lambda b, g, i: (b, g, 0, i))],
        out_shape=[jax.ShapeDtypeStruct((B * S, NSA_HEADS * HEAD_DIM), BF16),
                   jax.ShapeDtypeStruct((B, G, ns, S), BF16)],
        compiler_params=_cparams(("parallel", "parallel", "parallel")),
        name="nsa_cmp_attention",
    )(qn, kct, vc, ov)


def _flash_kernel(*refs, mode, hg, tq, tk, tv, dqk, dv):
    if mode == "sel":
        q_ref, k_ref, vt_ref, sel_ref, ex_ref, o_ref, qs, m_s, l_s, acc_s = refs
    else:
        q_ref, k_ref, vt_ref, o_ref, qs, m_s, l_s, acc_s = refs
    rows = hg * tq
    q0 = pl.program_id(2) * tq
    cd = q0 // tk
    for hh in range(hg):
        qs[hh * tq:(hh + 1) * tq, :] = q_ref[:, hh * dqk:(hh + 1) * dqk]

    def scores(c, kind):
        start = pl.multiple_of(c * tk, tk)
        s = lax.dot_general(k_ref[pl.ds(start, tk), :], qs[...], (((1,), (1,)), ((), ())),
                            preferred_element_type=F32)
        ok = None
        if kind == "diag":
            k_pos = start + lax.broadcasted_iota(jnp.int32, (tk, tq), 0)
            q_pos = q0 + lax.broadcasted_iota(jnp.int32, (tk, tq), 1)
            ok = k_pos <= q_pos
        if mode == "sel":
            chosen = jnp.dot(ex_ref[c], sel_ref[...], preferred_element_type=F32) > 0.5
            ok = chosen if ok is None else (ok & chosen)
        if ok is not None:
            bias = jnp.where(ok, 0.0, NEG_INF)
            s = s + (jnp.concatenate([bias] * hg, axis=1) if hg > 1 else bias)
        return s

    def update(c, s, carry):
        m, l, acc = carry
        m_new = jnp.maximum(m, jnp.max(s, axis=0, keepdims=True))
        p = jnp.exp2(s - m_new)
        alpha = jnp.exp2(m - m_new)
        l = alpha * l + jnp.sum(p, axis=0, keepdims=True)
        r = tk // tv
        vt = vt_ref[c] if r == 1 else jnp.concatenate([vt_ref[c * r + j] for j in range(r)], axis=1)
        acc = alpha * acc + jnp.dot(vt, p.astype(BF16), preferred_element_type=F32)
        return m_new, l, acc

    def load():
        return m_s[...], l_s[...], acc_s[...]

    def store(carry):
        m_s[...], l_s[...], acc_s[...] = carry

    def run(first, count, kind, carry):
        s_next = scores(first, kind)
        for j in range(count):
            s_cur = s_next
            if j + 1 < count:
                s_next = scores(first + j + 1, kind)
            carry = update(first + j, s_cur, carry)
        return carry

    store(run(cd, max(tq // tk, 1), "diag",
              (jnp.full((1, rows), NEG_INF, F32), jnp.zeros((1, rows), F32), jnp.zeros((dv, rows), F32))))

    def group(first, count):
        store(run(first, count, "full", load()))

    def quad(i, _):
        group(4 * i, 4)
        return 0

    lax.fori_loop(0, cd // 4, quad, 0)

    @pl.when(cd % 4 >= 2)
    def _():
        group((cd // 4) * 4, 2)

    @pl.when(cd % 2 == 1)
    def _():
        group(cd - 1, 1)

    o = acc_s[...] * (1.0 / l_s[...])
    for hh in range(hg):
        o_ref[:, hh * dv:(hh + 1) * dv] = o[:, hh * tq:(hh + 1) * tq].T.astype(o_ref.dtype)


def flash_attention(q, k, vt, B, S, *, mode, n_groups, hg, dqk, dv, kcol0, tq, tk, sel_t=None, expand=None):
    nq = S // tq
    nch = S // tk
    tv = vt.shape[-1]
    assert (tk % tq == 0 or tq % tk == 0) and S % tk == 0 and S % tq == 0 and tq % LANES == 0 and tk % tv == 0
    in_specs = [pl.BlockSpec((tq, hg * dqk), lambda b, g, i: (b * nq + i, g)),
                pl.BlockSpec((S, dqk), lambda b, g, i: (b, kcol0 + g)),
                pl.BlockSpec((None, None, S // tv, dv, tv), lambda b, g, i: (b, g, 0, 0, 0))]
    args = [q, k, vt]
    if mode == "sel":
        ns = S // SLC_LEN
        in_specs += [pl.BlockSpec((None, None, ns, tq), lambda b, g, i: (b, g, 0, i)),
                     pl.BlockSpec((nch, tk, ns), lambda b, g, i: (0, 0, 0))]
        args += [sel_t, expand]
    return pl.pallas_call(
        functools.partial(_flash_kernel, mode=mode, hg=hg, tq=tq, tk=tk, tv=tv, dqk=dqk, dv=dv),
        grid=(B, n_groups, nq),
        in_specs=in_specs,
        out_specs=pl.BlockSpec((tq, hg * dv), lambda b, g, i: (b * nq + i, g)),
        out_shape=jax.ShapeDtypeStruct((B * S, n_groups * hg * dv), BF16),
        scratch_shapes=[pltpu.VMEM((hg * tq, dqk), BF16),
                        pltpu.VMEM((1, hg * tq), F32),
                        pltpu.VMEM((1, hg * tq), F32),
                        pltpu.VMEM((dv, hg * tq), F32)],
        compiler_params=_cparams(("parallel", "parallel", "arbitrary")),
        name="flash_" + mode,
    )(*args)


def _window_kernel(q_ref, k_ref, vt_ref, o_ref, qs, *, hg, tq, dqk, dv):
    nk = WINDOW + tq
    q0 = pl.program_id(2) * tq
    start = pl.multiple_of(jnp.maximum(q0 - WINDOW, 0), tq)
    for hh in range(hg):
        qs[hh * tq:(hh + 1) * tq, :] = q_ref[:, hh * dqk:(hh + 1) * dqk]
    s = lax.dot_general(k_ref[pl.ds(start, nk), :], qs[...], (((1,), (1,)), ((), ())),
                        preferred_element_type=F32)
    k_pos = start + lax.broadcasted_iota(jnp.int32, (nk, tq), 0)
    q_pos = q0 + lax.broadcasted_iota(jnp.int32, (nk, tq), 1)
    bias = jnp.where((k_pos <= q_pos) & (q_pos - k_pos < WINDOW), 0.0, NEG_INF)
    s = s + jnp.concatenate([bias] * hg, axis=1)
    p = jnp.exp2(s - jnp.max(s, axis=0, keepdims=True))
    l = jnp.sum(p, axis=0, keepdims=True)
    pb = p.astype(BF16)
    c0 = start // tq
    o = jnp.zeros((dv, hg * tq), F32)
    for j in range(nk // tq):
        o = o + jnp.dot(vt_ref[c0 + j], pb[j * tq:(j + 1) * tq, :], preferred_element_type=F32)
    o = o * (1.0 / l)
    for hh in range(hg):
        o_ref[:, hh * dv:(hh + 1) * dv] = o[:, hh * tq:(hh + 1) * tq].T.astype(o_ref.dtype)


def window_attention(q, k, vt, B, S, *, n_groups, hg, dqk, dv, kcol0, tq):
    nq = S // tq
    assert WINDOW % tq == 0 and S >= WINDOW + tq and tq % LANES == 0
    return pl.pallas_call(
        functools.partial(_window_kernel, hg=hg, tq=tq, dqk=dqk, dv=dv),
        grid=(B, n_groups, nq),
        in_specs=[pl.BlockSpec((tq, hg * dqk), lambda b, g, i: (b * nq + i, g)),
                  pl.BlockSpec((S, dqk), lambda b, g, i: (b, kcol0 + g)),
                  pl.BlockSpec((None, None, nq, dv, tq), lambda b, g, i: (b, g, 0, 0, 0))],
        out_specs=pl.BlockSpec((tq, hg * dv), lambda b, g, i: (b * nq + i, g)),
        out_shape=jax.ShapeDtypeStruct((B * S, n_groups * hg * dv), BF16),
        scratch_shapes=[pltpu.VMEM((hg * tq, dqk), BF16)],
        compiler_params=_cparams(("parallel", "parallel", "arbitrary")),
        name="window_attention",
    )(q, k, vt)


def _mla_q_kernel(z_ref, ga_ref, w_ref, gh_ref, ct_ref, s1_ref, s2_ref, o_ref):
    h = _rms(z_ref[...], ga_ref[...]).astype(BF16)
    y = jnp.dot(h, w_ref[...], preferred_element_type=F32)
    ct, s1, s2 = ct_ref[...], s1_ref[...], s2_ref[...]
    scale = LOG2E * (QK_NOPE + QK_ROPE) ** -0.5
    for hd in range(MLA_HEADS):
        c = hd * MLA_QK_PAD
        nope = _rms(y[:, c:c + QK_NOPE], gh_ref[0:1, :])
        r = y[:, c + QK_NOPE:c + MLA_QK_PAD]
        ms = jnp.sum(r * r, axis=-1, keepdims=True) * (1.0 / QK_ROPE)
        r = _rope_lanes(r * lax.rsqrt(ms + EPS) * gh_ref[1:2, :], ct, s1, s2, QK_ROPE // 2)
        o_ref[:, c:c + QK_NOPE] = (nope * scale).astype(BF16)
        o_ref[:, c + QK_NOPE:c + MLA_QK_PAD] = (r * scale).astype(BF16)


def mla_q_proj(z, ga, wq, gh, tabs, S, *, tm):
    T = z.shape[0]
    ns = S // tm
    tab_spec = pl.BlockSpec((tm, LANES), lambda i: (i % ns, 0))
    nout = MLA_HEADS * MLA_QK_PAD
    return pl.pallas_call(
        _mla_q_kernel,
        grid=(T // tm,),
        in_specs=[pl.BlockSpec((tm, Q_LORA), lambda i: (i, OFF_QA // Q_LORA)),
                  pl.BlockSpec((1, Q_LORA), lambda i: (0, 0)),
                  pl.BlockSpec((Q_LORA, nout), lambda i: (0, 0)),
                  pl.BlockSpec((8, LANES), lambda i: (0, 0)),
                  tab_spec, tab_spec, tab_spec],
        out_specs=pl.BlockSpec((tm, nout), lambda i: (i, 0)),
        out_shape=jax.ShapeDtypeStruct((T, nout), BF16),
        compiler_params=_cparams(("parallel",)),
        name="mla_q_proj",
    )(z, ga, wq, gh, *tabs)


def _mla_kv_kernel(z_ref, zr_ref, ga_ref, w_ref, gh_ref, ct_ref, s1_ref, s2_ref, k_ref, v_ref):
    h = _rms(z_ref[...], ga_ref[...]).astype(BF16)
    y = jnp.dot(h, w_ref[...], preferred_element_type=F32)
    r = zr_ref[...]
    ms = jnp.sum(r * r, axis=-1, keepdims=True) * (1.0 / QK_ROPE)
    r = _rope_lanes(r * lax.rsqrt(ms + EPS) * gh_ref[1:2, :], ct_ref[...], s1_ref[...], s2_ref[...],
                    QK_ROPE // 2).astype(BF16)
    nv = MLA_HEADS * QK_NOPE
    for hd in range(MLA_HEADS):
        c = hd * MLA_QK_PAD
        k_ref[:, c:c + QK_NOPE] = _rms(y[:, hd * QK_NOPE:(hd + 1) * QK_NOPE], gh_ref[0:1, :]).astype(BF16)
        k_ref[:, c + QK_NOPE:c + MLA_QK_PAD] = r
    for hd in range(MLA_HEADS):
        v_ref[hd] = y[:, nv + hd * V_HEAD:nv + (hd + 1) * V_HEAD].T.astype(BF16)


def mla_kv_proj(z, ga, wkv, gh, tabs, S, *, tm=VT_CHUNK):
    T = z.shape[0]
    ns = S // tm
    tab_spec = pl.BlockSpec((tm, LANES), lambda i: (i % ns, 0))
    nk = MLA_HEADS * MLA_QK_PAD
    nv = MLA_HEADS * V_HEAD
    return pl.pallas_call(
        _mla_kv_kernel,
        grid=(T // tm,),
        in_specs=[pl.BlockSpec((tm, KV_LORA), lambda i: (i, OFF_KVA // KV_LORA)),
                  pl.BlockSpec((tm, LANES), lambda i: (i, OFF_KR // LANES)),
                  pl.BlockSpec((1, KV_LORA), lambda i: (0, 0)),
                  pl.BlockSpec((KV_LORA, MLA_HEADS * (QK_NOPE + V_HEAD)), lambda i: (0, 0)),
                  pl.BlockSpec((8, LANES), lambda i: (0, 0)),
                  tab_spec, tab_spec, tab_spec],
        out_specs=[pl.BlockSpec((tm, nk), lambda i: (i, 0)),
                   pl.BlockSpec((None, MLA_HEADS, None, V_HEAD, tm), lambda i: (i // ns, 0, i % ns, 0, 0))],
        out_shape=[jax.ShapeDtypeStruct((T, nk), BF16),
                   jax.ShapeDtypeStruct((T // S, MLA_HEADS, ns, V_HEAD, tm), BF16)],
        compiler_params=_cparams(("parallel",)),
        name="mla_kv_proj",
    )(z, z, ga, wkv, gh, *tabs)


def _mix_kernel(oc_ref, os_ref, ow_ref, ob_ref, gate_ref, wa_ref, wb_ref, za_ref, zb_ref, o_ref, a_scr):
    @pl.when(pl.program_id(1) == 0)
    def _():
        for h in range(NSA_HEADS):
            sl = slice(h * HEAD_DIM, (h + 1) * HEAD_DIM)
            a = (gate_ref[:, h:h + 1] * oc_ref[:, sl].astype(F32)
                 + gate_ref[:, NSA_HEADS + h:NSA_HEADS + h + 1] * os_ref[:, sl].astype(F32)
                 + gate_ref[:, 2 * NSA_HEADS + h:2 * NSA_HEADS + h + 1] * ow_ref[:, sl].astype(F32))
            a_scr[:, sl] = a.astype(BF16)

    pa = jnp.dot(a_scr[...], wa_ref[...].astype(BF16), preferred_element_type=F32)
    pb = jnp.dot(ob_ref[...], wb_ref[...].astype(BF16), preferred_element_type=F32)
    o_ref[...] = (_sigmoid(za_ref[...]) * pa + _sigmoid(zb_ref[...]) * pb).astype(o_ref.dtype)


def gated_mix(oc, os_, ow, ob, gates, wa, wb, z, *, tm, tn):
    T, K = oc.shape
    N = wa.shape[1]
    row = lambda w: pl.BlockSpec((tm, w), lambda m, n: (m, 0))
    return pl.pallas_call(
        _mix_kernel,
        grid=(T // tm, N // tn),
        in_specs=[row(K), row(K), row(K), row(K), row(LANES),
                  pl.BlockSpec((K, tn), lambda m, n: (0, n)),
                  pl.BlockSpec((K, tn), lambda m, n: (0, n)),
                  pl.BlockSpec((tm, tn), lambda m, n: (m, OFF_M // tn + n)),
                  pl.BlockSpec((tm, tn), lambda m, n: (m, (OFF_M + D_MODEL) // tn + n))],
        out_specs=pl.BlockSpec((tm, tn), lambda m, n: (m, n)),
        out_shape=jax.ShapeDtypeStruct((T, N), BF16),
        scratch_shapes=[pltpu.VMEM((tm, K), BF16)],
        compiler_params=_cparams(("parallel", "arbitrary")),
        name="gated_mix",
    )(oc, os_, ow, ob, gates, wa, wb, z, z)


def _ffn_up_kernel(x_ref, g_ref, w1_ref, w3_ref, o_ref, h_scr):
    @pl.when(pl.program_id(1) == 0)
    def _():
        h_scr[...] = _rms(x_ref[...], g_ref[...]).astype(BF16)

    h = h_scr[...]
    a = jnp.dot(h, w1_ref[...].astype(BF16), preferred_element_type=F32)
    b = jnp.dot(h, w3_ref[...].astype(BF16), preferred_element_type=F32)
    o_ref[...] = (a * _sigmoid(a) * b).astype(o_ref.dtype)


def ffn_up(x, g, w1, w3, *, tm, tn):
    T, K = x.shape
    N = w1.shape[1]
    return pl.pallas_call(
        _ffn_up_kernel,
        grid=(T // tm, N // tn),
        in_specs=[pl.BlockSpec((tm, K), lambda m, n: (m, 0)),
                  pl.BlockSpec((1, K), lambda m, n: (0, 0)),
                  pl.BlockSpec((K, tn), lambda m, n: (0, n)),
                  pl.BlockSpec((K, tn), lambda m, n: (0, n))],
        out_specs=pl.BlockSpec((tm, tn), lambda m, n: (m, n)),
        out_shape=jax.ShapeDtypeStruct((T, N), BF16),
        scratch_shapes=[pltpu.VMEM((tm, K), BF16)],
        compiler_params=_cparams(("parallel", "arbitrary")),
        name="ffn_up",
    )(x, g.reshape(1, K), w1, w3)


def _pack_bf16_pairs(h):
    k = h.shape[1] // 2
    hi = lax.bitcast_convert_type(h[:, :k].astype(jnp.bfloat16).astype(F32), jnp.uint32)
    lo = lax.bitcast_convert_type(h[:, k:].astype(jnp.bfloat16).astype(F32), jnp.uint32)
    return lax.bitcast_convert_type(hi | (lo >> 16), jnp.int32)


def _unpack_bf16_pairs(xp):
    xp = lax.bitcast_convert_type(xp, jnp.uint32)
    hi = lax.bitcast_convert_type(xp & jnp.uint32(0xFFFF0000), F32)
    lo = lax.bitcast_convert_type(xp << 16, F32)
    return hi.astype(BF16), lo.astype(BF16)


def _router_kernel(x_ref, g_ref, wr_ref, h_ref, idx_ref, gate_ref):
    h = _rms(x_ref[...], g_ref[...])
    is_pad = pl.program_id(0) == pl.num_programs(0) - 1
    h_ref[...] = jnp.where(is_pad, 0, _pack_bf16_pairs(h))
    logits = jnp.dot(h, wr_ref[...], preferred_element_type=F32, precision=lax.Precision.HIGHEST)
    lane = lax.broadcasted_iota(jnp.int32, logits.shape, 1).astype(F32)
    logits = jnp.where(lane < float(N_EXPERTS), logits, NEG_INF)
    m1 = jnp.max(logits, axis=-1, keepdims=True)
    i1 = jnp.min(jnp.where(logits == m1, lane, float(LANES)), axis=-1, keepdims=True)
    rest = jnp.where(lane == i1, NEG_INF, logits)
    m2 = jnp.max(rest, axis=-1, keepdims=True)
    i2 = jnp.min(jnp.where(rest == m2, lane, float(LANES)), axis=-1, keepdims=True)
    e = jnp.exp(m2 - m1)
    den = 1.0 + e
    idx_ref[...] = jnp.where(lane == 0.0, i1, jnp.where(lane == 1.0, i2, 0.0)).astype(jnp.int32)
    gate_ref[...] = jnp.where(lane == 0.0, 1.0 / den, jnp.where(lane == 1.0, e / den, 0.0))


def router(x, g, wr_pad, *, tm):
    T, K = x.shape
    nt = T // tm
    return pl.pallas_call(
        _router_kernel,
        grid=(nt + 1,),
        in_specs=[pl.BlockSpec((tm, K), lambda i: (jnp.minimum(i, nt - 1), 0)),
                  pl.BlockSpec((1, K), lambda i: (0, 0)),
                  pl.BlockSpec((K, LANES), lambda i: (0, 0))],
        out_specs=[pl.BlockSpec((tm, K // 2), lambda i: (i, 0)),
                   pl.BlockSpec((tm, LANES), lambda i: (jnp.minimum(i, nt - 1), 0)),
                   pl.BlockSpec((tm, LANES), lambda i: (jnp.minimum(i, nt - 1), 0))],
        out_shape=[jax.ShapeDtypeStruct((T + tm, K // 2), jnp.int32),
                   jax.ShapeDtypeStruct((T, LANES), jnp.int32),
                   jax.ShapeDtypeStruct((T, LANES), F32)],
        compiler_params=_cparams(("arbitrary",)),
        name="router",
    )(x, g.reshape(1, K), wr_pad)


def _moe_weight_stream(w_hbms, stages, casts, sems, be_ref, nu_ref, gs_ref, ne_ref, lg_ref, tn):
    n = pl.program_id(0)
    r = pl.program_id(1)
    used = r < nu_ref[0]

    def copies(e, nt):
        c0 = pl.multiple_of(nt * tn, tn)
        return [pltpu.make_async_copy(w.at[e, :, pl.ds(c0, tn)], st, sems.at[i])
                for i, (w, st) in enumerate(zip(w_hbms, stages))]

    @pl.when((n == 0) & (r == 0))
    def _():
        for c in copies(be_ref[0], 0):
            c.start()

    @pl.when(used & (gs_ref[r] == 1))
    def _():
        for c in copies(be_ref[r], n):
            c.wait()
        for st, wb in zip(stages, casts):
            wb[...] = st[...].astype(BF16)
        last = lg_ref[r] == 1

        @pl.when(jnp.logical_not(last & (n == pl.num_programs(0) - 1)))
        def _():
            for c in copies(ne_ref[r], n + last.astype(jnp.int32)):
                c.start()

    return used


def _moe_up_kernel(be_ref, nu_ref, gs_ref, ne_ref, lg_ref, x_ref, w1_hbm, w3_hbm, o_ref,
                   st1, st3, w1_s, w3_s, sems, *, tn):
    used = _moe_weight_stream((w1_hbm, w3_hbm), (st1, st3), (w1_s, w3_s), sems,
                              be_ref, nu_ref, gs_ref, ne_ref, lg_ref, tn)

    @pl.when(used)
    def _():
        xa, xb = _unpack_bf16_pairs(x_ref[...])
        k2 = xa.shape[1]
        a = (jnp.dot(xa, w1_s[:k2, :], preferred_element_type=F32)
             + jnp.dot(xb, w1_s[k2:, :], preferred_element_type=F32))
        b = (jnp.dot(xa, w3_s[:k2, :], preferred_element_type=F32)
             + jnp.dot(xb, w3_s[k2:, :], preferred_element_type=F32))
        o_ref[...] = (a * _sigmoid(a) * b).astype(o_ref.dtype)

    @pl.when(jnp.logical_not(used))
    def _():
        o_ref[...] = jnp.zeros(o_ref.shape, o_ref.dtype)


def moe_up(tables, xb, w1, w3, *, tmb, tn):
    n_slot = xb.shape[0]
    K, N = w1.shape[1], w1.shape[2]
    row = lambda r, nu: jnp.minimum(r, nu[0] - 1)
    return pl.pallas_call(
        functools.partial(_moe_up_kernel, tn=tn),
        grid_spec=pltpu.PrefetchScalarGridSpec(
            num_scalar_prefetch=5,
            grid=(N // tn, n_slot // tmb),
            in_specs=[pl.BlockSpec((tmb, K // 2), lambda n, r, be, nu, gs, ne, lg: (row(r, nu), 0)),
                      pl.BlockSpec(memory_space=pl.ANY), pl.BlockSpec(memory_space=pl.ANY)],
            out_specs=pl.BlockSpec((tmb, tn), lambda n, r, be, nu, gs, ne, lg: (r, n)),
            scratch_shapes=[pltpu.VMEM((K, tn), F32), pltpu.VMEM((K, tn), F32),
                            pltpu.VMEM((K, tn), BF16), pltpu.VMEM((K, tn), BF16),
                            pltpu.SemaphoreType.DMA((2,))]),
        out_shape=jax.ShapeDtypeStruct((n_slot, N), BF16),
        compiler_params=_cparams(("arbitrary", "arbitrary")),
        name="moe_up",
    )(*tables, xb, w1, w3)


def _moe_down_kernel(be_ref, nu_ref, gs_ref, ne_ref, lg_ref, a_ref, w2_hbm, o_ref, st2, w2_s, sems, *, tn):
    used = _moe_weight_stream((w2_hbm,), (st2,), (w2_s,), sems, be_ref, nu_ref, gs_ref, ne_ref, lg_ref, tn)

    @pl.when(used)
    def _():
        o_ref[...] = jnp.dot(a_ref[...], w2_s[...], preferred_element_type=F32)

    @pl.when(jnp.logical_not(used))
    def _():
        o_ref[...] = jnp.zeros(o_ref.shape, o_ref.dtype)


def moe_down(tables, act, w2, *, tmb, tn):
    n_slot, K = act.shape
    N = w2.shape[2]
    row = lambda r, nu: jnp.minimum(r, nu[0] - 1)
    return pl.pallas_call(
        functools.partial(_moe_down_kernel, tn=tn),
        grid_spec=pltpu.PrefetchScalarGridSpec(
            num_scalar_prefetch=5,
            grid=(N // tn, n_slot // tmb),
            in_specs=[pl.BlockSpec((tmb, K), lambda n, r, be, nu, gs, ne, lg: (row(r, nu), 0)),
                      pl.BlockSpec(memory_space=pl.ANY)],
            out_specs=pl.BlockSpec((tmb, tn), lambda n, r, be, nu, gs, ne, lg: (r, n)),
            scratch_shapes=[pltpu.VMEM((K, tn), F32), pltpu.VMEM((K, tn), BF16),
                            pltpu.SemaphoreType.DMA((1,))]),
        out_shape=jax.ShapeDtypeStruct((n_slot, N), F32),
        compiler_params=_cparams(("arbitrary", "arbitrary")),
        name="moe_down",
    )(*tables, act, w2)


def _rope_tabs(pos, rot_dim, n_rows):
    half = rot_dim // 2
    inv = 1.0 / (ROPE_THETA ** (jnp.arange(0, rot_dim, 2, dtype=F32) / rot_dim))
    ang = jnp.asarray(pos).astype(F32)[:, None] * inv[None, :]
    c, s = jnp.cos(ang), jnp.sin(ang)
    z = jnp.zeros_like(c)
    pad = lambda a, fill: jnp.pad(a, ((0, n_rows - a.shape[0]), (0, LANES - a.shape[1])), constant_values=fill)
    return pad(jnp.concatenate([c, c], 1), 1.0), pad(jnp.concatenate([-s, z], 1), 0.0), \
        pad(jnp.concatenate([z, s], 1), 0.0)


def _mla_tabs(S):
    ct, s1, s2 = _rope_tabs(jnp.arange(S), QK_ROPE, S)
    lane = jnp.arange(LANES)[None, :]
    return jnp.where(lane < QK_ROPE, ct, 0.0), s1, s2


def _pad_cols(w, n):
    return jnp.pad(w, ((0, 0), (0, n - w.shape[1])))


def _layout_w_in(w):
    sp = np.cumsum([0, NSA_HEADS * HEAD_DIM, NSA_KV_COLS, 3 * NSA_HEADS, Q_LORA, KV_LORA, QK_ROPE, 2 * D_MODEL])
    q, kv, g, qa, kva, kr, m = [w[:, sp[i]:sp[i + 1]] for i in range(7)]
    return jnp.concatenate([qa, _pad_cols(g, LANES), _pad_cols(kr, LANES), q, kva, kv, m], axis=1).astype(BF16)


def _layout_w_q_b(w):
    w = w.reshape(Q_LORA, MLA_HEADS, QK_NOPE + QK_ROPE)
    w = jnp.pad(w, ((0, 0), (0, 0), (0, MLA_QK_PAD - QK_NOPE - QK_ROPE)))
    return w.reshape(Q_LORA, MLA_HEADS * MLA_QK_PAD).astype(BF16)


def _layout_w_kv_b(w):
    w = w.reshape(KV_LORA, MLA_HEADS, QK_NOPE + V_HEAD)
    return jnp.concatenate([w[:, :, :QK_NOPE].reshape(KV_LORA, -1), w[:, :, QK_NOPE:].reshape(KV_LORA, -1)],
                           axis=1).astype(BF16)


def _layout_w_cmp(w):
    h = CMP_LEN // 2
    return jnp.concatenate([w[:h].reshape(h * HEAD_DIM, HEAD_DIM), w[h:].reshape(h * HEAD_DIM, HEAD_DIM)],
                           axis=1).astype(BF16)


def _layout_pe(pe):
    return jnp.pad(pe.reshape(2, (CMP_LEN // 2) * HEAD_DIM), ((0, 6), (0, 0))).astype(BF16)


def _overlap(nc, ns):
    n = np.arange(nc)[None, :] * CMP_STRIDE
    j = np.arange(ns)[:, None] * SLC_LEN
    ov = (n <= j + SLC_LEN - 1) & (j <= n + CMP_LEN - 1) & (np.arange(nc)[None, :] < nc - 1)
    return jnp.asarray(ov.astype(np.float32), BF16)


def _expand(ns, S, tk):
    e = ((np.arange(S // tk)[:, None, None] * tk + np.arange(tk)[None, :, None]) // SLC_LEN
         == np.arange(ns)[None, None, :])
    return jnp.asarray(e.astype(np.float32), BF16)


def _layout_layers(p):
    depth = p['w_in'].shape[0]
    lane_row = lambda g: jnp.pad(g, ((0, 0), (0, LANES - g.shape[-1])))[:, None, :]
    z4 = jnp.zeros((depth, 4, HEAD_DIM), F32)
    z6 = jnp.zeros((depth, 6, LANES), F32)
    return dict(
        gn=jnp.concatenate([p['nsa_q_norm_g'][:, None, :], p['nsa_k_norm_g'], z4], axis=1),
        w_ck=jax.vmap(_layout_w_cmp)(p['w_cmp_k']), w_cv=jax.vmap(_layout_w_cmp)(p['w_cmp_v']),
        pe_k=jax.vmap(_layout_pe)(p['cmp_pe_k']), pe_v=jax.vmap(_layout_pe)(p['cmp_pe_v']),
        gh_q=jnp.concatenate([p['mla_q_norm_g'][:, None, :], lane_row(p['mla_qr_norm_g']), z6], axis=1),
        gh_k=jnp.concatenate([p['mla_k_norm_g'][:, None, :], lane_row(p['mla_kr_norm_g']), z6], axis=1),
        w_qb=jax.vmap(_layout_w_q_b)(p['w_q_b']), w_kvb=jax.vmap(_layout_w_kv_b)(p['w_kv_b']))


def _attention_block(x2d, B, S, p, pw, l, tabs):
    T = x2d.shape[0]
    z = norm_matmul(x2d, p['attn_norm_g'][l], _layout_w_in(p['w_in'][l]), tm=min(1024, T), tn=512)
    qn, kvp, vts, vtw, gates = nsa_prep(z, tabs['tok'], pw['gn'][l], S)

    nc = S // CMP_STRIDE
    x2 = kvp[:, :2 * NSA_KV_GROUPS * HEAD_DIM].reshape(B, S, 2 * NSA_KV_GROUPS, HEAD_DIM)
    x2 = x2.transpose(0, 2, 1, 3).reshape(B, 2 * NSA_KV_GROUPS, nc, CMP_STRIDE * HEAD_DIM)
    kct, vc = nsa_compress(x2, pw['w_ck'][l], pw['w_cv'][l], pw['pe_k'][l], pw['pe_v'][l],
                           p['nsa_k_norm_g'][l][0:1], tabs['cmp'])
    ns = S // SLC_LEN
    oc, sel_t = nsa_cmp_attention(qn, kct, vc, _overlap(nc, ns), B, S, tq=min(256, S))
    nsa = dict(n_groups=NSA_KV_GROUPS, hg=NSA_HPG, dqk=HEAD_DIM, dv=HEAD_DIM)
    tks = min(512, S)
    os_ = flash_attention(qn, kvp, vts, B, S, mode="sel", kcol0=4, tq=256, tk=tks,
                          sel_t=sel_t, expand=_expand(ns, S, tks), **nsa)
    ow = window_attention(qn, kvp, vtw, B, S, kcol0=6, tq=VT_CHUNK, **nsa)

    qm = mla_q_proj(z, p['mla_qa_norm_g'][l][None], pw['w_qb'][l], pw['gh_q'][l], tabs['mla'], S, tm=256)
    km, vmt = mla_kv_proj(z, p['mla_kva_norm_g'][l][None], pw['w_kvb'][l], pw['gh_k'][l], tabs['mla'], S)
    tkm = min(512, S)
    ob = flash_attention(qm, km, vmt, B, S, mode="causal",
                         n_groups=MLA_HEADS, hg=1, dqk=MLA_QK_PAD, dv=V_HEAD, kcol0=0, tq=min(1024, S), tk=tkm)

    tm = min(1024, T)
    mix = gated_mix(oc, os_, ow, ob, gates, p['w_proj_nsa'][l], p['w_proj_mla'][l], z, tm=tm, tn=512)
    return matmul_residual(mix, p['w_out'][l], x2d, tm=min(2048, T), tn=512)


def _dense_ffn(x2d, g, w1, w3, w2):
    T = x2d.shape[0]
    act = ffn_up(x2d, g, w1.astype(BF16), w3.astype(BF16), tm=min(1024, T), tn=512)
    return matmul_residual(act, w2.astype(BF16), x2d, tm=min(1024, T), tn=256)


def _cumsum_rows(oh, blk=128):
    A, E = oh.shape
    nb = A // blk
    x = oh.astype(F32).reshape(nb, blk, E)
    within = jnp.einsum('ij,bje->bie', jnp.tril(jnp.ones((blk, blk), F32)), x)
    before = jnp.tril(jnp.ones((nb, nb), F32), -1) @ within[:, -1, :]
    return (within + before[:, None, :]).astype(jnp.int32).reshape(A, E)


def _moe_ffn(x2d, g, w_router, w1, w3, w2, *, tmb=512):
    T = x2d.shape[0]
    hp, idx, gate = router(x2d, g, _pad_cols(w_router, LANES), tm=256)
    A = T * TOP_K
    e_flat = idx[:, :TOP_K].reshape(A)
    tok_flat = jnp.repeat(jnp.arange(T, dtype=jnp.int32), TOP_K)
    oh = (e_flat[:, None] == jnp.arange(N_EXPERTS)[None, :]).astype(jnp.int32)
    csum = _cumsum_rows(oh)
    rank = jnp.sum(oh * csum, axis=1) - 1
    counts = csum[-1]
    padded = (counts + tmb - 1) // tmb * tmb
    pad_end = jnp.cumsum(padded)
    dest = (pad_end - padded)[e_flat] + rank
    n_blk = -(-A // tmb) + N_EXPERTS
    n_slot = n_blk * tmb
    slot_tok = jnp.full((n_slot,), T, jnp.int32).at[dest].set(tok_flat)
    eidx = jnp.arange(N_EXPERTS, dtype=jnp.int32)
    blk = jnp.arange(n_blk, dtype=jnp.int32)
    blk_exp = jnp.minimum(jnp.sum((pad_end[None, :] <= (blk * tmb)[:, None]).astype(jnp.int32), axis=1),
                          N_EXPERTS - 1).astype(jnp.int32)
    n_used = (pad_end[-1:] // tmb).astype(jnp.int32)
    present = counts > 0
    first_e = jnp.min(jnp.where(present, eidx, N_EXPERTS))
    last_e = jnp.max(jnp.where(present, eidx, -1))
    later = jnp.where(present[None, :] & (eidx[None, :] > eidx[:, None]), eidx[None, :], N_EXPERTS)
    next_e = jnp.min(later, axis=1)
    next_e = jnp.where(next_e == N_EXPERTS, first_e, next_e).astype(jnp.int32)
    starts = ((blk == 0) | (blk_exp != jnp.roll(blk_exp, 1))) & (blk < n_used[0])
    tables = (blk_exp, n_used, starts.astype(jnp.int32), next_e[blk_exp],
              (blk_exp == last_e).astype(jnp.int32))
    xb = hp[slot_tok]
    act = moe_up(tables, xb, w1, w3, tmb=tmb, tn=1024)
    yb = moe_down(tables, act, w2, tmb=tmb, tn=512)
    d2 = dest.reshape(T, TOP_K)
    g2 = gate[:, :TOP_K]
    return x2d + g2[:, 0:1] * yb[d2[:, 0]] + g2[:, 1:2] * yb[d2[:, 1]]


def kernel(x, attn_norm_g, w_in, nsa_q_norm_g, nsa_k_norm_g, cmp_pe_k, cmp_pe_v, w_cmp_k, w_cmp_v, mla_qa_norm_g, w_q_b, mla_kva_norm_g, w_kv_b, mla_q_norm_g, mla_qr_norm_g, mla_k_norm_g, mla_kr_norm_g, w_proj_nsa, w_proj_mla, w_out, ffn_norm_g, w_ff1, w_ff3, w_ff2, w_router, w_e1, w_e3, w_e2):
    p = dict(attn_norm_g=attn_norm_g, w_in=w_in, nsa_q_norm_g=nsa_q_norm_g, nsa_k_norm_g=nsa_k_norm_g,
             cmp_pe_k=cmp_pe_k, cmp_pe_v=cmp_pe_v, w_cmp_k=w_cmp_k, w_cmp_v=w_cmp_v,
             mla_qa_norm_g=mla_qa_norm_g, w_q_b=w_q_b, mla_kva_norm_g=mla_kva_norm_g, w_kv_b=w_kv_b,
             mla_q_norm_g=mla_q_norm_g, mla_qr_norm_g=mla_qr_norm_g, mla_k_norm_g=mla_k_norm_g,
             mla_kr_norm_g=mla_kr_norm_g, w_proj_nsa=w_proj_nsa, w_proj_mla=w_proj_mla, w_out=w_out)
    B, S, D = x.shape
    depth = w_in.shape[0]
    nc = S // CMP_STRIDE
    tabs = dict(tok=_rope_tabs(jnp.arange(S), NSA_ROT, S),
                cmp=_rope_tabs(jnp.arange(nc - 1) * CMP_STRIDE + CMP_LEN - 1, NSA_ROT, nc),
                mla=_mla_tabs(S))
    pw = _layout_layers(p)
    x2d = x.reshape(B * S, D)
    for l in range(depth):
        x2d = _attention_block(x2d, B, S, p, pw, l, tabs)
        if l % 2 == 0:
            x2d = _dense_ffn(x2d, ffn_norm_g[l], w_ff1[l // 2], w_ff3[l // 2], w_ff2[l // 2])
        else:
            x2d = _moe_ffn(x2d, ffn_norm_g[l], w_router[l // 2], w_e1[l // 2], w_e3[l // 2], w_e2[l // 2])
    return x2d.reshape(B, S, D)
```

```python
import functools

import numpy as np
import jax
import jax.numpy as jnp
from jax import lax
from jax.experimental import pallas as pl
from jax.experimental.pallas import tpu as pltpu

D_MODEL = 2048
HEAD_DIM = 128
NSA_HEADS = 8
NSA_KV_GROUPS = 2
NSA_HPG = NSA_HEADS // NSA_KV_GROUPS
NSA_ROT = HEAD_DIM // 4
CMP_LEN = 32
CMP_STRIDE = 16
SLC_LEN = 64
SLC_TOPK = 16
N_LOCAL_SLC = 2
WINDOW = 512
FORCE_SCORE = 1.0e4
MLA_HEADS = 8
Q_LORA = 768
KV_LORA = 512
QK_NOPE = 128
QK_ROPE = 64
V_HEAD = 128
ROPE_THETA = 500000.0
EPS = 1e-6
D_FF = 7168
N_EXPERTS = 8
TOP_K = 2

LANES = 128
MLA_QK_PAD = 256
NEG_INF = float("-inf")
LOG2E = 1.4426950408889634
BF16 = jnp.bfloat16
F32 = jnp.float32

OFF_QA = 0
OFF_G = 768
OFF_KR = 896
OFF_Q = 1024
OFF_KVA = 2048
OFF_KV = 2560
OFF_M = 4096
D_INP = 8192
NSA_KV_COLS = 3 * 2 * NSA_KV_GROUPS * HEAD_DIM
NSA_KVP_COLS = 2 * NSA_KV_GROUPS * HEAD_DIM
VT_CHUNK = 256

VMEM_LIMIT = 56 * 1024 * 1024


def _cparams(sem):
    return pltpu.CompilerParams(dimension_semantics=sem, vmem_limit_bytes=VMEM_LIMIT)


def _rms(x, g):
    ms = jnp.mean(x * x, axis=-1, keepdims=True)
    return x * lax.rsqrt(ms + EPS) * g


def _rope_lanes(y, ct, s1, s2, half):
    return y * ct + pltpu.roll(y, LANES - half, 1) * s1 + pltpu.roll(y, half, 1) * s2


def _sigmoid(x):
    return 1.0 / (1.0 + jnp.exp(-x))


def _nmm_kernel(x_ref, g_ref, w_ref, o_ref, h_scr):
    @pl.when(pl.program_id(1) == 0)
    def _():
        h_scr[...] = _rms(x_ref[...], g_ref[...]).astype(BF16)

    o_ref[...] = jnp.dot(h_scr[...], w_ref[...].astype(BF16),
                         preferred_element_type=F32).astype(o_ref.dtype)


def norm_matmul(x, g, w, *, tm, tn, out_dtype=F32):
    T, K = x.shape
    N = w.shape[1]
    return pl.pallas_call(
        _nmm_kernel,
        grid=(T // tm, N // tn),
        in_specs=[pl.BlockSpec((tm, K), lambda m, n: (m, 0)),
                  pl.BlockSpec((1, K), lambda m, n: (0, 0)),
                  pl.BlockSpec((K, tn), lambda m, n: (0, n))],
        out_specs=pl.BlockSpec((tm, tn), lambda m, n: (m, n)),
        out_shape=jax.ShapeDtypeStruct((T, N), out_dtype),
        scratch_shapes=[pltpu.VMEM((tm, K), BF16)],
        compiler_params=_cparams(("parallel", "arbitrary")),
        name="norm_matmul",
    )(x, g.reshape(1, K), w)


def _mmres_kernel(a_ref, w_ref, r_ref, o_ref):
    o_ref[...] = r_ref[...] + jnp.dot(a_ref[...], w_ref[...].astype(BF16), preferred_element_type=F32)


def matmul_residual(a, w, res, *, tm, tn):
    T, K = a.shape
    N = w.shape[1]
    return pl.pallas_call(
        _mmres_kernel,
        grid=(T // tm, N // tn),
        in_specs=[pl.BlockSpec((tm, K), lambda m, n: (m, 0)),
                  pl.BlockSpec((K, tn), lambda m, n: (0, n)),
                  pl.BlockSpec((tm, tn), lambda m, n: (m, n))],
        out_specs=pl.BlockSpec((tm, tn), lambda m, n: (m, n)),
        out_shape=jax.ShapeDtypeStruct((T, N), F32),
        compiler_params=_cparams(("parallel", "arbitrary")),
        name="matmul_residual",
    )(a, w, res)


def _prep_kernel(zg_ref, zq_ref, zc_ref, zs_ref, zw_ref, ct_ref, s1_ref, s2_ref, gn_ref,
                 qn_ref, kvp_ref, xc_ref, vts_ref, vtw_ref, gate_ref):
    ct, s1, s2 = ct_ref[...], s1_ref[...], s2_ref[...]
    half = NSA_ROT // 2
    gq = gn_ref[0:1, :]
    for h in range(NSA_HEADS):
        y = _rope_lanes(_rms(zq_ref[:, h * HEAD_DIM:(h + 1) * HEAD_DIM], gq), ct, s1, s2, half)
        qn_ref[:, h * HEAD_DIM:(h + 1) * HEAD_DIM] = (y * (LOG2E * HEAD_DIM ** -0.5)).astype(BF16)
    for br, zb_ref in enumerate((zc_ref, zs_ref, zw_ref)):
        for kv in range(2):
            for g in range(NSA_KV_GROUPS):
                c = (kv * NSA_KV_GROUPS + g) * HEAD_DIM
                y = zb_ref[:, c:c + HEAD_DIM]
                if kv == 0 and br > 0:
                    y = _rope_lanes(_rms(y, gn_ref[1 + br:2 + br, :]), ct, s1, s2, half)
                if br == 0:
                    xc_ref[kv * NSA_KV_GROUPS + g] = y.astype(BF16)
                elif kv == 0:
                    d = (br - 1) * NSA_KV_GROUPS * HEAD_DIM + g * HEAD_DIM
                    kvp_ref[:, d:d + HEAD_DIM] = y.astype(BF16)
                else:
                    (vts_ref if br == 1 else vtw_ref)[g] = y.T.astype(BF16)
    gate_ref[...] = _sigmoid(zg_ref[...])


def nsa_prep(z, tabs, gn, S, *, tm=VT_CHUNK):
    T = z.shape[0]
    ns = S // tm
    tab_spec = pl.BlockSpec((tm, LANES), lambda i: (i % ns, 0))
    vt_spec = pl.BlockSpec((None, NSA_KV_GROUPS, None, HEAD_DIM, tm), lambda i: (i // ns, 0, i % ns, 0, 0))
    vt_shape = jax.ShapeDtypeStruct((T // S, NSA_KV_GROUPS, ns, HEAD_DIM, tm), BF16)
    wq, wb = NSA_HEADS * HEAD_DIM, 2 * NSA_KV_GROUPS * HEAD_DIM
    npl = 2 * NSA_KV_GROUPS
    return pl.pallas_call(
        _prep_kernel,
        grid=(T // tm,),
        in_specs=[pl.BlockSpec((tm, LANES), lambda i: (i, OFF_G // LANES)),
                  pl.BlockSpec((tm, wq), lambda i: (i, OFF_Q // wq)),
                  pl.BlockSpec((tm, wb), lambda i: (i, OFF_KV // wb)),
                  pl.BlockSpec((tm, wb), lambda i: (i, OFF_KV // wb + 1)),
                  pl.BlockSpec((tm, wb), lambda i: (i, OFF_KV // wb + 2)),
                  tab_spec, tab_spec, tab_spec, pl.BlockSpec((8, LANES), lambda i: (0, 0))],
        out_specs=[pl.BlockSpec((tm, NSA_HEADS * HEAD_DIM), lambda i: (i, 0)),
                   pl.BlockSpec((tm, NSA_KVP_COLS), lambda i: (i, 0)),
                   pl.BlockSpec((None, npl, tm, HEAD_DIM), lambda i: (i // ns, 0, i % ns, 0)), vt_spec, vt_spec,
                   pl.BlockSpec((tm, LANES), lambda i: (i, 0))],
        out_shape=[jax.ShapeDtypeStruct((T, NSA_HEADS * HEAD_DIM), BF16),
                   jax.ShapeDtypeStruct((T, NSA_KVP_COLS), BF16),
                   jax.ShapeDtypeStruct((T // S, npl, S, HEAD_DIM), BF16), vt_shape, vt_shape,
                   jax.ShapeDtypeStruct((T, LANES), F32)],
        compiler_params=_cparams(("parallel",)),
        name="nsa_prep",
    )(z, z, z, z, z, *tabs, gn)


def _cmp_kernel(xk_ref, xv_ref, wk_ref, wv_ref, pek_ref, pev_ref, gk_ref, ct_ref, s1_ref, s2_ref,
                kct_ref, vc_ref):
    nc = xk_ref.shape[0]
    row = lax.broadcasted_iota(jnp.int32, (nc, HEAD_DIM), 0)

    def comp(x_ref, w_ref, pe_ref):
        w = w_ref[...]
        y = jnp.dot(x_ref[...], w, preferred_element_type=F32)
        ype = jnp.dot(pe_ref[...], w, preferred_element_type=F32)
        bias = ype[0:1, :HEAD_DIM] + ype[1:2, HEAD_DIM:]
        out = y[:, :HEAD_DIM] + pltpu.roll(y[:, HEAD_DIM:], nc - 1, 0) + bias
        return jnp.where(row < nc - 1, out, 0.0)

    k = _rms(comp(xk_ref, wk_ref, pek_ref), gk_ref[...])
    k = _rope_lanes(k, ct_ref[...], s1_ref[...], s2_ref[...], NSA_ROT // 2)
    kct_ref[...] = k.T.astype(BF16)
    vc_ref[...] = comp(xv_ref, wv_ref, pev_ref).astype(BF16)


def nsa_compress(x2, wk2, wv2, pek2, pev2, gk, tabs_cmp):
    B, _, nc, kk = x2.shape
    G = NSA_KV_GROUPS
    full = lambda shape: pl.BlockSpec(shape, lambda b, g: (0,) * len(shape))
    return pl.pallas_call(
        _cmp_kernel,
        grid=(B, G),
        in_specs=[pl.BlockSpec((None, None, nc, kk), lambda b, g: (b, g, 0, 0)),
                  pl.BlockSpec((None, None, nc, kk), lambda b, g: (b, G + g, 0, 0)),
                  full((kk, 2 * HEAD_DIM)), full((kk, 2 * HEAD_DIM)),
                  full((8, kk)), full((8, kk)), full((1, HEAD_DIM)),
                  full((nc, LANES)), full((nc, LANES)), full((nc, LANES))],
        out_specs=[pl.BlockSpec((None, None, HEAD_DIM, nc), lambda b, g: (b, g, 0, 0)),
                   pl.BlockSpec((None, None, nc, HEAD_DIM), lambda b, g: (b, g, 0, 0))],
        out_shape=[jax.ShapeDtypeStruct((B, G, HEAD_DIM, nc), BF16),
                   jax.ShapeDtypeStruct((B, G, nc, HEAD_DIM), BF16)],
        compiler_params=_cparams(("parallel", "parallel")),
        name="nsa_compress",
    )(x2, x2, wk2, wv2, pek2, pev2, gk, *tabs_cmp)


def _cattn_kernel(q_ref, kct_ref, vc_ref, ov_ref, oc_ref, sel_ref, *, tq):
    nc = vc_ref.shape[0]
    ns = sel_ref.shape[0]
    t0 = pl.program_id(2) * tq
    t_pos = t0 + lax.broadcasted_iota(jnp.int32, (tq, nc), 0)
    n_idx = lax.broadcasted_iota(jnp.int32, (tq, nc), 1)
    vis = (t_pos >= n_idx * CMP_STRIDE + (CMP_LEN - 1)) & (n_idx < nc - 1)
    bias = jnp.where(vis, 0.0, NEG_INF)
    kct = kct_ref[...]
    vc = vc_ref[...]
    psum = jnp.zeros((tq, nc), F32)
    for hh in range(NSA_HPG):
        s = jnp.dot(q_ref[:, hh * HEAD_DIM:(hh + 1) * HEAD_DIM], kct, preferred_element_type=F32) + bias
        m = jnp.max(s, axis=-1, keepdims=True)
        m = jnp.where(m == NEG_INF, 0.0, m)
        p = jnp.exp2(s - m)
        den = jnp.sum(p, axis=-1, keepdims=True)
        p = p * (1.0 / jnp.where(den > 0.0, den, 1.0))
        oc_ref[:, hh * HEAD_DIM:(hh + 1) * HEAD_DIM] = jnp.dot(
            p.astype(BF16), vc, preferred_element_type=F32).astype(oc_ref.dtype)
        psum = psum + p
    pt = psum.T
    hi = pt.astype(BF16)
    lo = (pt - hi.astype(F32)).astype(BF16)
    ov = ov_ref[...]
    imp = jnp.dot(ov, hi, preferred_element_type=F32) + jnp.dot(ov, lo, preferred_element_type=F32)
    blk = lax.broadcasted_iota(jnp.int32, (ns, tq), 0)
    cur = (t0 + lax.broadcasted_iota(jnp.int32, (ns, tq), 1)) // SLC_LEN
    forced = (blk == 0) | ((blk <= cur) & (blk > cur - N_LOCAL_SLC))
    val = jnp.where(blk > cur, NEG_INF, jnp.where(forced, FORCE_SCORE, imp))
    rank = jnp.zeros((ns, tq), F32)
    for i in range(ns):
        other = val[i:i + 1, :]
        ahead = (other > val) | ((other == val) & (blk > i))
        rank = rank + jnp.where(ahead, 1.0, 0.0)
    sel_ref[...] = jnp.where(rank < float(min(SLC_TOPK, ns)), 1.0, 0.0).astype(sel_ref.dtype)


def nsa_cmp_attention(qn, kct, vc, ov, B, S, *, tq):
    G = NSA_KV_GROUPS
    nq = S // tq
    nc = vc.shape[2]
    ns = S // SLC_LEN
    gw = NSA_HPG * HEAD_DIM
    return pl.pallas_call(
        functools.partial(_cattn_kernel, tq=tq),
        grid=(B, G, nq),
        in_specs=[pl.BlockSpec((tq, gw), lambda b, g, i: (b * nq + i, g)),
                  pl.BlockSpec((None, None, HEAD_DIM, nc), lambda b, g, i: (b, g, 0, 0)),
                  pl.BlockSpec((None, None, nc, HEAD_DIM), lambda b, g, i: (b, g, 0, 0)),
                  pl.BlockSpec((ns, nc), lambda b, g, i: (0, 0))],
        out_specs=[pl.BlockSpec((tq, gw), lambda b, g, i: (b * nq + i, g)),
                   pl.BlockSpec((None, None, ns, tq), lambda b, g, i: (b, g, 0, i))],
        out_shape=[jax.ShapeDtypeStruct((B * S, NSA_HEADS * HEAD_DIM), BF16),
                   jax.ShapeDtypeStruct((B, G, ns, S), BF16)],
        compiler_params=_cparams(("parallel", "parallel", "parallel")),
        name="nsa_cmp_attention",
    )(qn, kct, vc, ov)


def _flash_kernel(*refs, mode, hg, tq, tk, tv, dqk, dv):
    if mode == "sel":
        q_ref, k_ref, vt_ref, sel_ref, ex_ref, o_ref, qs, m_s, l_s, acc_s, sa, sb = refs
    else:
        q_ref, k_ref, vt_ref, o_ref, qs, m_s, l_s, acc_s, sa, sb = refs
    rows = hg * tq
    q0 = pl.program_id(2) * tq
    cd = q0 // tk
    for hh in range(hg):
        qs[hh * tq:(hh + 1) * tq, :] = q_ref[:, hh * dqk:(hh + 1) * dqk]

    def scores(c, kind):
        start = pl.multiple_of(c * tk, tk)
        s = lax.dot_general(k_ref[pl.ds(start, tk), :], qs[...], (((1,), (1,)), ((), ())),
                            preferred_element_type=F32)
        ok = None
        if kind == "diag":
            k_pos = start + lax.broadcasted_iota(jnp.int32, (tk, tq), 0)
            q_pos = q0 + lax.broadcasted_iota(jnp.int32, (tk, tq), 1)
            ok = k_pos <= q_pos
        if mode == "sel":
            chosen = jnp.dot(ex_ref[c], sel_ref[...], preferred_element_type=F32) > 0.5
            ok = chosen if ok is None else (ok & chosen)
        if ok is not None:
            bias = jnp.where(ok, 0.0, NEG_INF)
            s = s + (jnp.concatenate([bias] * hg, axis=1) if hg > 1 else bias)
        return s

    def update(c, s, carry):
        m, l, acc = carry
        m_new = jnp.maximum(m, jnp.max(s, axis=0, keepdims=True))
        p = jnp.exp2(s - m_new)
        alpha = jnp.exp2(m - m_new)
        l = alpha * l + jnp.sum(p, axis=0, keepdims=True)
        r = tk // tv
        vt = vt_ref[c] if r == 1 else jnp.concatenate([vt_ref[c * r + j] for j in range(r)], axis=1)
        acc = alpha * acc + jnp.dot(vt, p.astype(BF16), preferred_element_type=F32)
        return m_new, l, acc

    def load():
        return m_s[...], l_s[...], acc_s[...]

    def store(carry):
        m_s[...], l_s[...], acc_s[...] = carry

    nd = max(tq // tk, 1)
    last = k_ref.shape[0] // tk - 1
    carry = (jnp.full((1, rows), NEG_INF, F32), jnp.zeros((1, rows), F32), jnp.zeros((dv, rows), F32))
    s_next = scores(cd, "diag")
    for j in range(nd):
        s_cur = s_next
        if j + 1 < nd:
            s_next = scores(cd + j + 1, "diag")
        else:
            sa[...] = scores(0, "full")
        carry = update(cd + j, s_cur, carry)
    store(carry)

    def pairs(first, count):
        carry = load()
        for j in range(count):
            c = first + 2 * j
            sb[...] = scores(c + 1, "full")
            carry = update(c, sa[...], carry)
            sa[...] = scores(jnp.minimum(c + 2, last), "full")
            carry = update(c + 1, sb[...], carry)
        store(carry)

    def four(i, _):
        pairs(4 * i, 2)
        return 0

    lax.fori_loop(0, cd // 4, four, 0)

    @pl.when(cd % 4 >= 2)
    def _():
        pairs((cd // 4) * 4, 1)

    @pl.when(cd % 2 == 1)
    def _():
        store(update(cd - 1, sa[...], load()))

    o = acc_s[...] * (1.0 / l_s[...])
    for hh in range(hg):
        o_ref[:, hh * dv:(hh + 1) * dv] = o[:, hh * tq:(hh + 1) * tq].T.astype(o_ref.dtype)


def flash_attention(q, k, vt, B, S, *, mode, n_groups, hg, dqk, dv, kcol0, tq, tk, sel_t=None, expand=None):
    nq = S // tq
    nch = S // tk
    tv = vt.shape[-1]
    assert (tk % tq == 0 or tq % tk == 0) and S % tk == 0 and S % tq == 0 and tq % LANES == 0 and tk % tv == 0
    in_specs = [pl.BlockSpec((tq, hg * dqk), lambda b, g, i: (b * nq + i, g)),
                pl.BlockSpec((S, dqk), lambda b, g, i: (b, kcol0 + g)),
                pl.BlockSpec((None, None, S // tv, dv, tv), lambda b, g, i: (b, g, 0, 0, 0))]
    args = [q, k, vt]
    if mode == "sel":
        ns = S // SLC_LEN
        in_specs += [pl.BlockSpec((None, None, ns, tq), lambda b, g, i: (b, g, 0, i)),
                     pl.BlockSpec((nch, tk, ns), lambda b, g, i: (0, 0, 0))]
        args += [sel_t, expand]
    return pl.pallas_call(
        functools.partial(_flash_kernel, mode=mode, hg=hg, tq=tq, tk=tk, tv=tv, dqk=dqk, dv=dv),
        grid=(B, n_groups, nq),
        in_specs=in_specs,
        out_specs=pl.BlockSpec((tq, hg * dv), lambda b, g, i: (b * nq + i, g)),
        out_shape=jax.ShapeDtypeStruct((B * S, n_groups * hg * dv), BF16),
        scratch_shapes=[pltpu.VMEM((hg * tq, dqk), BF16),
                        pltpu.VMEM((1, hg * tq), F32),
                        pltpu.VMEM((1, hg * tq), F32),
                        pltpu.VMEM((dv, hg * tq), F32),
                        pltpu.VMEM((tk, hg * tq), F32),
                        pltpu.VMEM((tk, hg * tq), F32)],
        compiler_params=_cparams(("parallel", "parallel", "arbitrary")),
        name="flash_" + mode,
    )(*args)


def _window_kernel(q_ref, k_ref, vt_ref, o_ref, qs, *, hg, tq, dqk, dv):
    nk = WINDOW + tq
    q0 = pl.program_id(2) * tq
    start = pl.multiple_of(jnp.maximum(q0 - WINDOW, 0), tq)
    for hh in range(hg):
        qs[hh * tq:(hh + 1) * tq, :] = q_ref[:, hh * dqk:(hh + 1) * dqk]
    s = lax.dot_general(k_ref[pl.ds(start, nk), :], qs[...], (((1,), (1,)), ((), ())),
                        preferred_element_type=F32)
    k_pos = start + lax.broadcasted_iota(jnp.int32, (nk, tq), 0)
    q_pos = q0 + lax.broadcasted_iota(jnp.int32, (nk, tq), 1)
    bias = jnp.where((k_pos <= q_pos) & (q_pos - k_pos < WINDOW), 0.0, NEG_INF)
    s = s + jnp.concatenate([bias] * hg, axis=1)
    p = jnp.exp2(s - jnp.max(s, axis=0, keepdims=True))
    l = jnp.sum(p, axis=0, keepdims=True)
    pb = p.astype(BF16)
    c0 = start // tq
    o = jnp.zeros((dv, hg * tq), F32)
    for j in range(nk // tq):
        o = o + jnp.dot(vt_ref[c0 + j], pb[j * tq:(j + 1) * tq, :], preferred_element_type=F32)
    o = o * (1.0 / l)
    for hh in range(hg):
        o_ref[:, hh * dv:(hh + 1) * dv] = o[:, hh * tq:(hh + 1) * tq].T.astype(o_ref.dtype)


def window_attention(q, k, vt, B, S, *, n_groups, hg, dqk, dv, kcol0, tq):
    nq = S // tq
    assert WINDOW % tq == 0 and S >= WINDOW + tq and tq % LANES == 0
    return pl.pallas_call(
        functools.partial(_window_kernel, hg=hg, tq=tq, dqk=dqk, dv=dv),
        grid=(B, n_groups, nq),
        in_specs=[pl.BlockSpec((tq, hg * dqk), lambda b, g, i: (b * nq + i, g)),
                  pl.BlockSpec((S, dqk), lambda b, g, i: (b, kcol0 + g)),
                  pl.BlockSpec((None, None, nq, dv, tq), lambda b, g, i: (b, g, 0, 0, 0))],
        out_specs=pl.BlockSpec((tq, hg * dv), lambda b, g, i: (b * nq + i, g)),
        out_shape=jax.ShapeDtypeStruct((B * S, n_groups * hg * dv), BF16),
        scratch_shapes=[pltpu.VMEM((hg * tq, dqk), BF16)],
        compiler_params=_cparams(("parallel", "parallel", "arbitrary")),
        name="window_attention",
    )(q, k, vt)


def _mla_q_kernel(z_ref, ga_ref, w_ref, gh_ref, ct_ref, s1_ref, s2_ref, o_ref):
    h = _rms(z_ref[...], ga_ref[...]).astype(BF16)
    y = jnp.dot(h, w_ref[...], preferred_element_type=F32)
    ct, s1, s2 = ct_ref[...], s1_ref[...], s2_ref[...]
    scale = LOG2E * (QK_NOPE + QK_ROPE) ** -0.5
    for hd in range(MLA_HEADS):
        c = hd * MLA_QK_PAD
        nope = _rms(y[:, c:c + QK_NOPE], gh_ref[0:1, :])
        r = y[:, c + QK_NOPE:c + MLA_QK_PAD]
        ms = jnp.sum(r * r, axis=-1, keepdims=True) * (1.0 / QK_ROPE)
        r = _rope_lanes(r * lax.rsqrt(ms + EPS) * gh_ref[1:2, :], ct, s1, s2, QK_ROPE // 2)
        o_ref[:, c:c + QK_NOPE] = (nope * scale).astype(BF16)
        o_ref[:, c + QK_NOPE:c + MLA_QK_PAD] = (r * scale).astype(BF16)


def mla_q_proj(z, ga, wq, gh, tabs, S, *, tm):
    T = z.shape[0]
    ns = S // tm
    tab_spec = pl.BlockSpec((tm, LANES), lambda i: (i % ns, 0))
    nout = MLA_HEADS * MLA_QK_PAD
    return pl.pallas_call(
        _mla_q_kernel,
        grid=(T // tm,),
        in_specs=[pl.BlockSpec((tm, Q_LORA), lambda i: (i, OFF_QA // Q_LORA)),
                  pl.BlockSpec((1, Q_LORA), lambda i: (0, 0)),
                  pl.BlockSpec((Q_LORA, nout), lambda i: (0, 0)),
                  pl.BlockSpec((8, LANES), lambda i: (0, 0)),
                  tab_spec, tab_spec, tab_spec],
        out_specs=pl.BlockSpec((tm, nout), lambda i: (i, 0)),
        out_shape=jax.ShapeDtypeStruct((T, nout), BF16),
        compiler_params=_cparams(("parallel",)),
        name="mla_q_proj",
    )(z, ga, wq, gh, *tabs)


def _mla_kv_kernel(z_ref, zr_ref, ga_ref, w_ref, gh_ref, ct_ref, s1_ref, s2_ref, k_ref, v_ref):
    h = _rms(z_ref[...], ga_ref[...]).astype(BF16)
    y = jnp.dot(h, w_ref[...], preferred_element_type=F32)
    r = zr_ref[...]
    ms = jnp.sum(r * r, axis=-1, keepdims=True) * (1.0 / QK_ROPE)
    r = _rope_lanes(r * lax.rsqrt(ms + EPS) * gh_ref[1:2, :], ct_ref[...], s1_ref[...], s2_ref[...],
                    QK_ROPE // 2).astype(BF16)
    nv = MLA_HEADS * QK_NOPE
    for hd in range(MLA_HEADS):
        c = hd * MLA_QK_PAD
        k_ref[:, c:c + QK_NOPE] = _rms(y[:, hd * QK_NOPE:(hd + 1) * QK_NOPE], gh_ref[0:1, :]).astype(BF16)
        k_ref[:, c + QK_NOPE:c + MLA_QK_PAD] = r
    for hd in range(MLA_HEADS):
        v_ref[hd] = y[:, nv + hd * V_HEAD:nv + (hd + 1) * V_HEAD].T.astype(BF16)


def mla_kv_proj(z, ga, wkv, gh, tabs, S, *, tm=VT_CHUNK):
    T = z.shape[0]
    ns = S // tm
    tab_spec = pl.BlockSpec((tm, LANES), lambda i: (i % ns, 0))
    nk = MLA_HEADS * MLA_QK_PAD
    nv = MLA_HEADS * V_HEAD
    return pl.pallas_call(
        _mla_kv_kernel,
        grid=(T // tm,),
        in_specs=[pl.BlockSpec((tm, KV_LORA), lambda i: (i, OFF_KVA // KV_LORA)),
                  pl.BlockSpec((tm, LANES), lambda i: (i, OFF_KR // LANES)),
                  pl.BlockSpec((1, KV_LORA), lambda i: (0, 0)),
                  pl.BlockSpec((KV_LORA, MLA_HEADS * (QK_NOPE + V_HEAD)), lambda i: (0, 0)),
                  pl.BlockSpec((8, LANES), lambda i: (0, 0)),
                  tab_spec, tab_spec, tab_spec],
        out_specs=[pl.BlockSpec((tm, nk), lambda i: (i, 0)),
                   pl.BlockSpec((None, MLA_HEADS, None, V_HEAD, tm), lambda i: (i // ns, 0, i % ns, 0, 0))],
        out_shape=[jax.ShapeDtypeStruct((T, nk), BF16),
                   jax.ShapeDtypeStruct((T // S, MLA_HEADS, ns, V_HEAD, tm), BF16)],
        compiler_params=_cparams(("parallel",)),
        name="mla_kv_proj",
    )(z, z, ga, wkv, gh, *tabs)


def _mix_kernel(oc_ref, os_ref, ow_ref, ob_ref, gate_ref, wa_ref, wb_ref, za_ref, zb_ref, o_ref, a_scr):
    @pl.when(pl.program_id(1) == 0)
    def _():
        for h in range(NSA_HEADS):
            sl = slice(h * HEAD_DIM, (h + 1) * HEAD_DIM)
            a = (gate_ref[:, h:h + 1] * oc_ref[:, sl].astype(F32)
                 + gate_ref[:, NSA_HEADS + h:NSA_HEADS + h + 1] * os_ref[:, sl].astype(F32)
                 + gate_ref[:, 2 * NSA_HEADS + h:2 * NSA_HEADS + h + 1] * ow_ref[:, sl].astype(F32))
            a_scr[:, sl] = a.astype(BF16)

    pa = jnp.dot(a_scr[...], wa_ref[...].astype(BF16), preferred_element_type=F32)
    pb = jnp.dot(ob_ref[...], wb_ref[...].astype(BF16), preferred_element_type=F32)
    o_ref[...] = (_sigmoid(za_ref[...]) * pa + _sigmoid(zb_ref[...]) * pb).astype(o_ref.dtype)


def gated_mix(oc, os_, ow, ob, gates, wa, wb, z, *, tm, tn):
    T, K = oc.shape
    N = wa.shape[1]
    row = lambda w: pl.BlockSpec((tm, w), lambda m, n: (m, 0))
    return pl.pallas_call(
        _mix_kernel,
        grid=(T // tm, N // tn),
        in_specs=[row(K), row(K), row(K), row(K), row(LANES),
                  pl.BlockSpec((K, tn), lambda m, n: (0, n)),
                  pl.BlockSpec((K, tn), lambda m, n: (0, n)),
                  pl.BlockSpec((tm, tn), lambda m, n: (m, OFF_M // tn + n)),
                  pl.BlockSpec((tm, tn), lambda m, n: (m, (OFF_M + D_MODEL) // tn + n))],
        out_specs=pl.BlockSpec((tm, tn), lambda m, n: (m, n)),
        out_shape=jax.ShapeDtypeStruct((T, N), BF16),
        scratch_shapes=[pltpu.VMEM((tm, K), BF16)],
        compiler_params=_cparams(("parallel", "arbitrary")),
        name="gated_mix",
    )(oc, os_, ow, ob, gates, wa, wb, z, z)


def _ffn_up_kernel(x_ref, g_ref, w1_ref, w3_ref, o_ref, h_scr):
    @pl.when(pl.program_id(1) == 0)
    def _():
        h_scr[...] = _rms(x_ref[...], g_ref[...]).astype(BF16)

    h = h_scr[...]
    a = jnp.dot(h, w1_ref[...].astype(BF16), preferred_element_type=F32)
    b = jnp.dot(h, w3_ref[...].astype(BF16), preferred_element_type=F32)
    o_ref[...] = (a * _sigmoid(a) * b).astype(o_ref.dtype)


def ffn_up(x, g, w1, w3, *, tm, tn):
    T, K = x.shape
    N = w1.shape[1]
    return pl.pallas_call(
        _ffn_up_kernel,
        grid=(T // tm, N // tn),
        in_specs=[pl.BlockSpec((tm, K), lambda m, n: (m, 0)),
                  pl.BlockSpec((1, K), lambda m, n: (0, 0)),
                  pl.BlockSpec((K, tn), lambda m, n: (0, n)),
                  pl.BlockSpec((K, tn), lambda m, n: (0, n))],
        out_specs=pl.BlockSpec((tm, tn), lambda m, n: (m, n)),
        out_shape=jax.ShapeDtypeStruct((T, N), BF16),
        scratch_shapes=[pltpu.VMEM((tm, K), BF16)],
        compiler_params=_cparams(("parallel", "arbitrary")),
        name="ffn_up",
    )(x, g.reshape(1, K), w1, w3)


def _pack_bf16_pairs(h):
    k = h.shape[1] // 2
    hi = lax.bitcast_convert_type(h[:, :k].astype(jnp.bfloat16).astype(F32), jnp.uint32)
    lo = lax.bitcast_convert_type(h[:, k:].astype(jnp.bfloat16).astype(F32), jnp.uint32)
    return lax.bitcast_convert_type(hi | (lo >> 16), jnp.int32)


def _unpack_bf16_pairs(xp):
    xp = lax.bitcast_convert_type(xp, jnp.uint32)
    hi = lax.bitcast_convert_type(xp & jnp.uint32(0xFFFF0000), F32)
    lo = lax.bitcast_convert_type(xp << 16, F32)
    return hi.astype(BF16), lo.astype(BF16)


def _router_kernel(x_ref, g_ref, wr_ref, h_ref, idx_ref, gate_ref):
    h = _rms(x_ref[...], g_ref[...])
    is_pad = pl.program_id(0) == pl.num_programs(0) - 1
    h_ref[...] = jnp.where(is_pad, 0, _pack_bf16_pairs(h))
    logits = jnp.dot(h, wr_ref[...], preferred_element_type=F32, precision=lax.Precision.HIGHEST)
    lane = lax.broadcasted_iota(jnp.int32, logits.shape, 1).astype(F32)
    logits = jnp.where(lane < float(N_EXPERTS), logits, NEG_INF)
    m1 = jnp.max(logits, axis=-1, keepdims=True)
    i1 = jnp.min(jnp.where(logits == m1, lane, float(LANES)), axis=-1, keepdims=True)
    rest = jnp.where(lane == i1, NEG_INF, logits)
    m2 = jnp.max(rest, axis=-1, keepdims=True)
    i2 = jnp.min(jnp.where(rest == m2, lane, float(LANES)), axis=-1, keepdims=True)
    e = jnp.exp(m2 - m1)
    den = 1.0 + e
    idx_ref[...] = jnp.where(lane == 0.0, i1, jnp.where(lane == 1.0, i2, 0.0)).astype(jnp.int32)
    gate_ref[...] = jnp.where(lane == 0.0, 1.0 / den, jnp.where(lane == 1.0, e / den, 0.0))


def router(x, g, wr_pad, *, tm):
    T, K = x.shape
    nt = T // tm
    return pl.pallas_call(
        _router_kernel,
        grid=(nt + 1,),
        in_specs=[pl.BlockSpec((tm, K), lambda i: (jnp.minimum(i, nt - 1), 0)),
                  pl.BlockSpec((1, K), lambda i: (0, 0)),
                  pl.BlockSpec((K, LANES), lambda i: (0, 0))],
        out_specs=[pl.BlockSpec((tm, K // 2), lambda i: (i, 0)),
                   pl.BlockSpec((tm, LANES), lambda i: (jnp.minimum(i, nt - 1), 0)),
                   pl.BlockSpec((tm, LANES), lambda i: (jnp.minimum(i, nt - 1), 0))],
        out_shape=[jax.ShapeDtypeStruct((T + tm, K // 2), jnp.int32),
                   jax.ShapeDtypeStruct((T, LANES), jnp.int32),
                   jax.ShapeDtypeStruct((T, LANES), F32)],
        compiler_params=_cparams(("arbitrary",)),
        name="router",
    )(x, g.reshape(1, K), wr_pad)


def _moe_weight_stream(w_hbms, stages, casts, sems, be_ref, nu_ref, gs_ref, ne_ref, lg_ref, tn):
    n = pl.program_id(0)
    r = pl.program_id(1)
    used = r < nu_ref[0]

    def copies(e, nt):
        c0 = pl.multiple_of(nt * tn, tn)
        return [pltpu.make_async_copy(w.at[e, :, pl.ds(c0, tn)], st, sems.at[i])
                for i, (w, st) in enumerate(zip(w_hbms, stages))]

    @pl.when((n == 0) & (r == 0))
    def _():
        for c in copies(be_ref[0], 0):
            c.start()

    @pl.when(used & (gs_ref[r] == 1))
    def _():
        for c in copies(be_ref[r], n):
            c.wait()
        for st, wb in zip(stages, casts):
            wb[...] = st[...].astype(BF16)
        last = lg_ref[r] == 1

        @pl.when(jnp.logical_not(last & (n == pl.num_programs(0) - 1)))
        def _():
            for c in copies(ne_ref[r], n + last.astype(jnp.int32)):
                c.start()

    return used


def _moe_up_kernel(be_ref, nu_ref, gs_ref, ne_ref, lg_ref, x_ref, w1_hbm, w3_hbm, o_ref,
                   st1, st3, w1_s, w3_s, sems, *, tn):
    used = _moe_weight_stream((w1_hbm, w3_hbm), (st1, st3), (w1_s, w3_s), sems,
                              be_ref, nu_ref, gs_ref, ne_ref, lg_ref, tn)

    @pl.when(used)
    def _():
        xa, xb = _unpack_bf16_pairs(x_ref[...])
        k2 = xa.shape[1]
        a = (jnp.dot(xa, w1_s[:k2, :], preferred_element_type=F32)
             + jnp.dot(xb, w1_s[k2:, :], preferred_element_type=F32))
        b = (jnp.dot(xa, w3_s[:k2, :], preferred_element_type=F32)
             + jnp.dot(xb, w3_s[k2:, :], preferred_element_type=F32))
        o_ref[...] = (a * _sigmoid(a) * b).astype(o_ref.dtype)

    @pl.when(jnp.logical_not(used))
    def _():
        o_ref[...] = jnp.zeros(o_ref.shape, o_ref.dtype)


def moe_up(tables, xb, w1, w3, *, tmb, tn):
    n_slot = xb.shape[0]
    K, N = w1.shape[1], w1.shape[2]
    row = lambda r, nu: jnp.minimum(r, nu[0] - 1)
    return pl.pallas_call(
        functools.partial(_moe_up_kernel, tn=tn),
        grid_spec=pltpu.PrefetchScalarGridSpec(
            num_scalar_prefetch=5,
            grid=(N // tn, n_slot // tmb),
            in_specs=[pl.BlockSpec((tmb, K // 2), lambda n, r, be, nu, gs, ne, lg: (row(r, nu), 0)),
                      pl.BlockSpec(memory_space=pl.ANY), pl.BlockSpec(memory_space=pl.ANY)],
            out_specs=pl.BlockSpec((tmb, tn), lambda n, r, be, nu, gs, ne, lg: (r, n)),
            scratch_shapes=[pltpu.VMEM((K, tn), F32), pltpu.VMEM((K, tn), F32),
                            pltpu.VMEM((K, tn), BF16), pltpu.VMEM((K, tn), BF16),
                            pltpu.SemaphoreType.DMA((2,))]),
        out_shape=jax.ShapeDtypeStruct((n_slot, N), BF16),
        compiler_params=_cparams(("arbitrary", "arbitrary")),
        name="moe_up",
    )(*tables, xb, w1, w3)


def _moe_down_kernel(be_ref, nu_ref, gs_ref, ne_ref, lg_ref, a_ref, w2_hbm, o_ref, st2, w2_s, sems, *, tn):
    used = _moe_weight_stream((w2_hbm,), (st2,), (w2_s,), sems, be_ref, nu_ref, gs_ref, ne_ref, lg_ref, tn)

    @pl.when(used)
    def _():
        o_ref[...] = jnp.dot(a_ref[...], w2_s[...], preferred_element_type=F32)

    @pl.when(jnp.logical_not(used))
    def _():
        o_ref[...] = jnp.zeros(o_ref.shape, o_ref.dtype)


def moe_down(tables, act, w2, *, tmb, tn):
    n_slot, K = act.shape
    N = w2.shape[2]
    row = lambda r, nu: jnp.minimum(r, nu[0] - 1)
    return pl.pallas_call(
        functools.partial(_moe_down_kernel, tn=tn),
        grid_spec=pltpu.PrefetchScalarGridSpec(
            num_scalar_prefetch=5,
            grid=(N // tn, n_slot // tmb),
            in_specs=[pl.BlockSpec((tmb, K), lambda n, r, be, nu, gs, ne, lg: (row(r, nu), 0)),
                      pl.BlockSpec(memory_space=pl.ANY)],
            out_specs=pl.BlockSpec((tmb, tn), lambda n, r, be, nu, gs, ne, lg: (r, n)),
            scratch_shapes=[pltpu.VMEM((K, tn), F32), pltpu.VMEM((K, tn), BF16),
                            pltpu.SemaphoreType.DMA((1,))]),
        out_shape=jax.ShapeDtypeStruct((n_slot, N), F32),
        compiler_params=_cparams(("arbitrary", "arbitrary")),
        name="moe_down",
    )(*tables, act, w2)


def _rope_tabs(pos, rot_dim, n_rows):
    half = rot_dim // 2
    inv = 1.0 / (ROPE_THETA ** (jnp.arange(0, rot_dim, 2, dtype=F32) / rot_dim))
    ang = jnp.asarray(pos).astype(F32)[:, None] * inv[None, :]
    c, s = jnp.cos(ang), jnp.sin(ang)
    z = jnp.zeros_like(c)
    pad = lambda a, fill: jnp.pad(a, ((0, n_rows - a.shape[0]), (0, LANES - a.shape[1])), constant_values=fill)
    return pad(jnp.concatenate([c, c], 1), 1.0), pad(jnp.concatenate([-s, z], 1), 0.0), \
        pad(jnp.concatenate([z, s], 1), 0.0)


def _mla_tabs(S):
    ct, s1, s2 = _rope_tabs(jnp.arange(S), QK_ROPE, S)
    lane = jnp.arange(LANES)[None, :]
    return jnp.where(lane < QK_ROPE, ct, 0.0), s1, s2


def _pad_cols(w, n):
    return jnp.pad(w, ((0, 0), (0, n - w.shape[1])))


def _layout_w_in(w):
    sp = np.cumsum([0, NSA_HEADS * HEAD_DIM, NSA_KV_COLS, 3 * NSA_HEADS, Q_LORA, KV_LORA, QK_ROPE, 2 * D_MODEL])
    q, kv, g, qa, kva, kr, m = [w[:, sp[i]:sp[i + 1]] for i in range(7)]
    return jnp.concatenate([qa, _pad_cols(g, LANES), _pad_cols(kr, LANES), q, kva, kv, m], axis=1).astype(BF16)


def _layout_w_q_b(w):
    w = w.reshape(Q_LORA, MLA_HEADS, QK_NOPE + QK_ROPE)
    w = jnp.pad(w, ((0, 0), (0, 0), (0, MLA_QK_PAD - QK_NOPE - QK_ROPE)))
    return w.reshape(Q_LORA, MLA_HEADS * MLA_QK_PAD).astype(BF16)


def _layout_w_kv_b(w):
    w = w.reshape(KV_LORA, MLA_HEADS, QK_NOPE + V_HEAD)
    return jnp.concatenate([w[:, :, :QK_NOPE].reshape(KV_LORA, -1), w[:, :, QK_NOPE:].reshape(KV_LORA, -1)],
                           axis=1).astype(BF16)


def _layout_w_cmp(w):
    h = CMP_LEN // 2
    return jnp.concatenate([w[:h].reshape(h * HEAD_DIM, HEAD_DIM), w[h:].reshape(h * HEAD_DIM, HEAD_DIM)],
                           axis=1).astype(BF16)


def _layout_pe(pe):
    return jnp.pad(pe.reshape(2, (CMP_LEN // 2) * HEAD_DIM), ((0, 6), (0, 0))).astype(BF16)


def _overlap(nc, ns):
    n = np.arange(nc)[None, :] * CMP_STRIDE
    j = np.arange(ns)[:, None] * SLC_LEN
    ov = (n <= j + SLC_LEN - 1) & (j <= n + CMP_LEN - 1) & (np.arange(nc)[None, :] < nc - 1)
    return jnp.asarray(ov.astype(np.float32), BF16)


def _expand(ns, S, tk):
    e = ((np.arange(S // tk)[:, None, None] * tk + np.arange(tk)[None, :, None]) // SLC_LEN
         == np.arange(ns)[None, None, :])
    return jnp.asarray(e.astype(np.float32), BF16)


def _layout_layers(p):
    depth = p['w_in'].shape[0]
    lane_row = lambda g: jnp.pad(g, ((0, 0), (0, LANES - g.shape[-1])))[:, None, :]
    z4 = jnp.zeros((depth, 4, HEAD_DIM), F32)
    z6 = jnp.zeros((depth, 6, LANES), F32)
    return dict(
        gn=jnp.concatenate([p['nsa_q_norm_g'][:, None, :], p['nsa_k_norm_g'], z4], axis=1),
        w_ck=jax.vmap(_layout_w_cmp)(p['w_cmp_k']), w_cv=jax.vmap(_layout_w_cmp)(p['w_cmp_v']),
        pe_k=jax.vmap(_layout_pe)(p['cmp_pe_k']), pe_v=jax.vmap(_layout_pe)(p['cmp_pe_v']),
        gh_q=jnp.concatenate([p['mla_q_norm_g'][:, None, :], lane_row(p['mla_qr_norm_g']), z6], axis=1),
        gh_k=jnp.concatenate([p['mla_k_norm_g'][:, None, :], lane_row(p['mla_kr_norm_g']), z6], axis=1),
        w_qb=jax.vmap(_layout_w_q_b)(p['w_q_b']), w_kvb=jax.vmap(_layout_w_kv_b)(p['w_kv_b']))


def _attention_block(x2d, B, S, p, pw, l, tabs):
    T = x2d.shape[0]
    z = norm_matmul(x2d, p['attn_norm_g'][l], _layout_w_in(p['w_in'][l]), tm=min(1024, T), tn=512)
    qn, kvp, xc, vts, vtw, gates = nsa_prep(z, tabs['tok'], pw['gn'][l], S)

    nc = S // CMP_STRIDE
    x2 = xc.reshape(B, 2 * NSA_KV_GROUPS, nc, CMP_STRIDE * HEAD_DIM)
    kct, vc = nsa_compress(x2, pw['w_ck'][l], pw['w_cv'][l], pw['pe_k'][l], pw['pe_v'][l],
                           p['nsa_k_norm_g'][l][0:1], tabs['cmp'])
    ns = S // SLC_LEN
    oc, sel_t = nsa_cmp_attention(qn, kct, vc, _overlap(nc, ns), B, S, tq=min(256, S))
    nsa = dict(n_groups=NSA_KV_GROUPS, hg=NSA_HPG, dqk=HEAD_DIM, dv=HEAD_DIM)
    tks = min(512, S)
    os_ = flash_attention(qn, kvp, vts, B, S, mode="sel", kcol0=0, tq=256, tk=tks,
                          sel_t=sel_t, expand=_expand(ns, S, tks), **nsa)
    ow = window_attention(qn, kvp, vtw, B, S, kcol0=NSA_KV_GROUPS, tq=VT_CHUNK, **nsa)

    qm = mla_q_proj(z, p['mla_qa_norm_g'][l][None], pw['w_qb'][l], pw['gh_q'][l], tabs['mla'], S, tm=256)
    km, vmt = mla_kv_proj(z, p['mla_kva_norm_g'][l][None], pw['w_kvb'][l], pw['gh_k'][l], tabs['mla'], S)
    tkm = min(512, S)
    ob = flash_attention(qm, km, vmt, B, S, mode="causal",
                         n_groups=MLA_HEADS, hg=1, dqk=MLA_QK_PAD, dv=V_HEAD, kcol0=0, tq=min(1024, S), tk=tkm)

    tm = min(1024, T)
    mix = gated_mix(oc, os_, ow, ob, gates, p['w_proj_nsa'][l], p['w_proj_mla'][l], z, tm=tm, tn=512)
    return matmul_residual(mix, p['w_out'][l], x2d, tm=min(2048, T), tn=512)


def _dense_ffn(x2d, g, w1, w3, w2):
    T = x2d.shape[0]
    act = ffn_up(x2d, g, w1.astype(BF16), w3.astype(BF16), tm=min(1024, T), tn=512)
    return matmul_residual(act, w2.astype(BF16), x2d, tm=min(1024, T), tn=256)


def _cumsum_rows(oh, blk=128):
    A, E = oh.shape
    nb = A // blk
    x = oh.astype(F32).reshape(nb, blk, E)
    within = jnp.einsum('ij,bje->bie', jnp.tril(jnp.ones((blk, blk), F32)), x)
    before = jnp.tril(jnp.ones((nb, nb), F32), -1) @ within[:, -1, :]
    return (within + before[:, None, :]).astype(jnp.int32).reshape(A, E)


def _moe_ffn(x2d, g, w_router, w1, w3, w2, *, tmb=512):
    T = x2d.shape[0]
    hp, idx, gate = router(x2d, g, _pad_cols(w_router, LANES), tm=256)
    A = T * TOP_K
    e_flat = idx[:, :TOP_K].reshape(A)
    tok_flat = jnp.repeat(jnp.arange(T, dtype=jnp.int32), TOP_K)
    oh = (e_flat[:, None] == jnp.arange(N_EXPERTS)[None, :]).astype(jnp.int32)
    csum = _cumsum_rows(oh)
    rank = jnp.sum(oh * csum, axis=1) - 1
    counts = csum[-1]
    padded = (counts + tmb - 1) // tmb * tmb
    pad_end = jnp.cumsum(padded)
    dest = (pad_end - padded)[e_flat] + rank
    n_blk = -(-A // tmb) + N_EXPERTS
    n_slot = n_blk * tmb
    slot_tok = jnp.full((n_slot,), T, jnp.int32).at[dest].set(tok_flat)
    eidx = jnp.arange(N_EXPERTS, dtype=jnp.int32)
    blk = jnp.arange(n_blk, dtype=jnp.int32)
    blk_exp = jnp.minimum(jnp.sum((pad_end[None, :] <= (blk * tmb)[:, None]).astype(jnp.int32), axis=1),
                          N_EXPERTS - 1).astype(jnp.int32)
    n_used = (pad_end[-1:] // tmb).astype(jnp.int32)
    present = counts > 0
    first_e = jnp.min(jnp.where(present, eidx, N_EXPERTS))
    last_e = jnp.max(jnp.where(present, eidx, -1))
    later = jnp.where(present[None, :] & (eidx[None, :] > eidx[:, None]), eidx[None, :], N_EXPERTS)
    next_e = jnp.min(later, axis=1)
    next_e = jnp.where(next_e == N_EXPERTS, first_e, next_e).astype(jnp.int32)
    starts = ((blk == 0) | (blk_exp != jnp.roll(blk_exp, 1))) & (blk < n_used[0])
    tables = (blk_exp, n_used, starts.astype(jnp.int32), next_e[blk_exp],
              (blk_exp == last_e).astype(jnp.int32))
    xb = hp[slot_tok]
    act = moe_up(tables, xb, w1, w3, tmb=tmb, tn=1024)
    yb = moe_down(tables, act, w2, tmb=tmb, tn=512)
    d2 = dest.reshape(T, TOP_K)
    g2 = gate[:, :TOP_K]
    return x2d + g2[:, 0:1] * yb[d2[:, 0]] + g2[:, 1:2] * yb[d2[:, 1]]


def kernel(x, attn_norm_g, w_in, nsa_q_norm_g, nsa_k_norm_g, cmp_pe_k, cmp_pe_v, w_cmp_k, w_cmp_v, mla_qa_norm_g, w_q_b, mla_kva_norm_g, w_kv_b, mla_q_norm_g, mla_qr_norm_g, mla_k_norm_g, mla_kr_norm_g, w_proj_nsa, w_proj_mla, w_out, ffn_norm_g, w_ff1, w_ff3, w_ff2, w_router, w_e1, w_e3, w_e2):
    p = dict(attn_norm_g=attn_norm_g, w_in=w_in, nsa_q_norm_g=nsa_q_norm_g, nsa_k_norm_g=nsa_k_norm_g,
             cmp_pe_k=cmp_pe_k, cmp_pe_v=cmp_pe_v, w_cmp_k=w_cmp_k, w_cmp_v=w_cmp_v,
             mla_qa_norm_g=mla_qa_norm_g, w_q_b=w_q_b, mla_kva_norm_g=mla_kva_norm_g, w_kv_b=w_kv_b,
             mla_q_norm_g=mla_q_norm_g, mla_qr_norm_g=mla_qr_norm_g, mla_k_norm_g=mla_k_norm_g,
             mla_kr_norm_g=mla_kr_norm_g, w_proj_nsa=w_proj_nsa, w_proj_mla=w_proj_mla, w_out=w_out)
    B, S, D = x.shape
    depth = w_in.shape[0]
    nc = S // CMP_STRIDE
    tabs = dict(tok=_rope_tabs(jnp.arange(S), NSA_ROT, S),
                cmp=_rope_tabs(jnp.arange(nc - 1) * CMP_STRIDE + CMP_LEN - 1, NSA_ROT, nc),
                mla=_mla_tabs(S))
    pw = _layout_layers(p)
    x2d = x.reshape(B * S, D)
    for l in range(depth):
        x2d = _attention_block(x2d, B, S, p, pw, l, tabs)
        if l % 2 == 0:
            x2d = _dense_ffn(x2d, ffn_norm_g[l], w_ff1[l // 2], w_ff3[l // 2], w_ff2[l // 2])
        else:
            x2d = _moe_ffn(x2d, ffn_norm_g[l], w_router[l // 2], w_e1[l // 2], w_e3[l // 2], w_e2[l // 2])
    return x2d.reshape(B, S, D)
```

```python
import functools

import numpy as np
import jax
import jax.numpy as jnp
from jax import lax
from jax.experimental import pallas as pl
from jax.experimental.pallas import tpu as pltpu

D_MODEL = 2048
HEAD_DIM = 128
NSA_HEADS = 8
NSA_KV_GROUPS = 2
NSA_HPG = NSA_HEADS // NSA_KV_GROUPS
NSA_ROT = HEAD_DIM // 4
CMP_LEN = 32
CMP_STRIDE = 16
SLC_LEN = 64
SLC_TOPK = 16
N_LOCAL_SLC = 2
WINDOW = 512
FORCE_SCORE = 1.0e4
MLA_HEADS = 8
Q_LORA = 768
KV_LORA = 512
QK_NOPE = 128
QK_ROPE = 64
V_HEAD = 128
ROPE_THETA = 500000.0
EPS = 1e-6
D_FF = 7168
N_EXPERTS = 8
TOP_K = 2

LANES = 128
MLA_QK_PAD = 256
NEG_INF = float("-inf")
LOG2E = 1.4426950408889634
BF16 = jnp.bfloat16
F32 = jnp.float32

OFF_QA = 0
OFF_G = 768
OFF_KR = 896
OFF_Q = 1024
OFF_KVA = 2048
OFF_KV = 2560
OFF_M = 4096
D_INP = 8192
NSA_KV_COLS = 3 * 2 * NSA_KV_GROUPS * HEAD_DIM
NSA_KVP_COLS = 2 * NSA_KV_GROUPS * HEAD_DIM
VT_CHUNK = 256

VMEM_LIMIT = 56 * 1024 * 1024


def _cparams(sem):
    return pltpu.CompilerParams(dimension_semantics=sem, vmem_limit_bytes=VMEM_LIMIT)


def _rms(x, g):
    ms = jnp.mean(x * x, axis=-1, keepdims=True)
    return x * lax.rsqrt(ms + EPS) * g


def _rope_lanes(y, ct, s1, s2, half):
    return y * ct + pltpu.roll(y, LANES - half, 1) * s1 + pltpu.roll(y, half, 1) * s2


def _sigmoid(x):
    return 1.0 / (1.0 + jnp.exp(-x))


def _nmm_kernel(x_ref, g_ref, w_ref, o_ref, h_scr):
    @pl.when(pl.program_id(1) == 0)
    def _():
        h_scr[...] = _rms(x_ref[...], g_ref[...]).astype(BF16)

    o_ref[...] = jnp.dot(h_scr[...], w_ref[...].astype(BF16),
                         preferred_element_type=F32).astype(o_ref.dtype)


def norm_matmul(x, g, w, *, tm, tn, out_dtype=F32):
    T, K = x.shape
    N = w.shape[1]
    return pl.pallas_call(
        _nmm_kernel,
        grid=(T // tm, N // tn),
        in_specs=[pl.BlockSpec((tm, K), lambda m, n: (m, 0)),
                  pl.BlockSpec((1, K), lambda m, n: (0, 0)),
                  pl.BlockSpec((K, tn), lambda m, n: (0, n))],
        out_specs=pl.BlockSpec((tm, tn), lambda m, n: (m, n)),
        out_shape=jax.ShapeDtypeStruct((T, N), out_dtype),
        scratch_shapes=[pltpu.VMEM((tm, K), BF16)],
        compiler_params=_cparams(("parallel", "arbitrary")),
        name="norm_matmul",
    )(x, g.reshape(1, K), w)


def _mmres_kernel(a_ref, w_ref, r_ref, o_ref):
    o_ref[...] = r_ref[...] + jnp.dot(a_ref[...], w_ref[...].astype(BF16), preferred_element_type=F32)


def matmul_residual(a, w, res, *, tm, tn):
    T, K = a.shape
    N = w.shape[1]
    return pl.pallas_call(
        _mmres_kernel,
        grid=(T // tm, N // tn),
        in_specs=[pl.BlockSpec((tm, K), lambda m, n: (m, 0)),
                  pl.BlockSpec((K, tn), lambda m, n: (0, n)),
                  pl.BlockSpec((tm, tn), lambda m, n: (m, n))],
        out_specs=pl.BlockSpec((tm, tn), lambda m, n: (m, n)),
        out_shape=jax.ShapeDtypeStruct((T, N), F32),
        compiler_params=_cparams(("parallel", "arbitrary")),
        name="matmul_residual",
    )(a, w, res)


def _prep_kernel(zg_ref, zq_ref, zc_ref, zs_ref, zw_ref, ct_ref, s1_ref, s2_ref, gn_ref,
                 qn_ref, kvp_ref, xc_ref, vts_ref, vtw_ref, gate_ref):
    ct, s1, s2 = ct_ref[...], s1_ref[...], s2_ref[...]
    half = NSA_ROT // 2
    gq = gn_ref[0:1, :]
    for h in range(NSA_HEADS):
        y = _rope_lanes(_rms(zq_ref[:, h * HEAD_DIM:(h + 1) * HEAD_DIM], gq), ct, s1, s2, half)
        qn_ref[:, h * HEAD_DIM:(h + 1) * HEAD_DIM] = (y * (LOG2E * HEAD_DIM ** -0.5)).astype(BF16)
    for br, zb_ref in enumerate((zc_ref, zs_ref, zw_ref)):
        for kv in range(2):
            for g in range(NSA_KV_GROUPS):
                c = (kv * NSA_KV_GROUPS + g) * HEAD_DIM
                y = zb_ref[:, c:c + HEAD_DIM]
                if kv == 0 and br > 0:
                    y = _rope_lanes(_rms(y, gn_ref[1 + br:2 + br, :]), ct, s1, s2, half)
                if br == 0:
                    xc_ref[kv * NSA_KV_GROUPS + g] = y.astype(BF16)
                elif kv == 0:
                    d = (br - 1) * NSA_KV_GROUPS * HEAD_DIM + g * HEAD_DIM
                    kvp_ref[:, d:d + HEAD_DIM] = y.astype(BF16)
                else:
                    (vts_ref if br == 1 else vtw_ref)[g] = y.T.astype(BF16)
    gate_ref[...] = _sigmoid(zg_ref[...])


def nsa_prep(z, tabs, gn, S, *, tm=VT_CHUNK):
    T = z.shape[0]
    ns = S // tm
    tab_spec = pl.BlockSpec((tm, LANES), lambda i: (i % ns, 0))
    vt_spec = pl.BlockSpec((None, NSA_KV_GROUPS, None, HEAD_DIM, tm), lambda i: (i // ns, 0, i % ns, 0, 0))
    vt_shape = jax.ShapeDtypeStruct((T // S, NSA_KV_GROUPS, ns, HEAD_DIM, tm), BF16)
    wq, wb = NSA_HEADS * HEAD_DIM, 2 * NSA_KV_GROUPS * HEAD_DIM
    npl = 2 * NSA_KV_GROUPS
    return pl.pallas_call(
        _prep_kernel,
        grid=(T // tm,),
        in_specs=[pl.BlockSpec((tm, LANES), lambda i: (i, OFF_G // LANES)),
                  pl.BlockSpec((tm, wq), lambda i: (i, OFF_Q // wq)),
                  pl.BlockSpec((tm, wb), lambda i: (i, OFF_KV // wb)),
                  pl.BlockSpec((tm, wb), lambda i: (i, OFF_KV // wb + 1)),
                  pl.BlockSpec((tm, wb), lambda i: (i, OFF_KV // wb + 2)),
                  tab_spec, tab_spec, tab_spec, pl.BlockSpec((8, LANES), lambda i: (0, 0))],
        out_specs=[pl.BlockSpec((tm, NSA_HEADS * HEAD_DIM), lambda i: (i, 0)),
                   pl.BlockSpec((tm, NSA_KVP_COLS), lambda i: (i, 0)),
                   pl.BlockSpec((None, npl, tm, HEAD_DIM), lambda i: (i // ns, 0, i % ns, 0)), vt_spec, vt_spec,
                   pl.BlockSpec((tm, LANES), lambda i: (i, 0))],
        out_shape=[jax.ShapeDtypeStruct((T, NSA_HEADS * HEAD_DIM), BF16),
                   jax.ShapeDtypeStruct((T, NSA_KVP_COLS), BF16),
                   jax.ShapeDtypeStruct((T // S, npl, S, HEAD_DIM), BF16), vt_shape, vt_shape,
                   jax.ShapeDtypeStruct((T, LANES), F32)],
        compiler_params=_cparams(("parallel",)),
        name="nsa_prep",
    )(z, z, z, z, z, *tabs, gn)


def _cmp_kernel(xk_ref, xv_ref, wk_ref, wv_ref, pek_ref, pev_ref, gk_ref, ct_ref, s1_ref, s2_ref,
                kct_ref, vc_ref):
    nc = xk_ref.shape[0]
    row = lax.broadcasted_iota(jnp.int32, (nc, HEAD_DIM), 0)

    def comp(x_ref, w_ref, pe_ref):
        w = w_ref[...]
        y = jnp.dot(x_ref[...], w, preferred_element_type=F32)
        ype = jnp.dot(pe_ref[...], w, preferred_element_type=F32)
        bias = ype[0:1, :HEAD_DIM] + ype[1:2, HEAD_DIM:]
        out = y[:, :HEAD_DIM] + pltpu.roll(y[:, HEAD_DIM:], nc - 1, 0) + bias
        return jnp.where(row < nc - 1, out, 0.0)

    k = _rms(comp(xk_ref, wk_ref, pek_ref), gk_ref[...])
    k = _rope_lanes(k, ct_ref[...], s1_ref[...], s2_ref[...], NSA_ROT // 2)
    kct_ref[...] = k.T.astype(BF16)
    vc_ref[...] = comp(xv_ref, wv_ref, pev_ref).astype(BF16)


def nsa_compress(x2, wk2, wv2, pek2, pev2, gk, tabs_cmp):
    B, _, nc, kk = x2.shape
    G = NSA_KV_GROUPS
    full = lambda shape: pl.BlockSpec(shape, lambda b, g: (0,) * len(shape))
    return pl.pallas_call(
        _cmp_kernel,
        grid=(B, G),
        in_specs=[pl.BlockSpec((None, None, nc, kk), lambda b, g: (b, g, 0, 0)),
                  pl.BlockSpec((None, None, nc, kk), lambda b, g: (b, G + g, 0, 0)),
                  full((kk, 2 * HEAD_DIM)), full((kk, 2 * HEAD_DIM)),
                  full((8, kk)), full((8, kk)), full((1, HEAD_DIM)),
                  full((nc, LANES)), full((nc, LANES)), full((nc, LANES))],
        out_specs=[pl.BlockSpec((None, None, HEAD_DIM, nc), lambda b, g: (b, g, 0, 0)),
                   pl.BlockSpec((None, None, nc, HEAD_DIM), lambda b, g: (b, g, 0, 0))],
        out_shape=[jax.ShapeDtypeStruct((B, G, HEAD_DIM, nc), BF16),
                   jax.ShapeDtypeStruct((B, G, nc, HEAD_DIM), BF16)],
        compiler_params=_cparams(("parallel", "parallel")),
        name="nsa_compress",
    )(x2, x2, wk2, wv2, pek2, pev2, gk, *tabs_cmp)


def _cattn_kernel(q_ref, kct_ref, vc_ref, ov_ref, oc_ref, sel_ref, *, tq):
    nc = vc_ref.shape[0]
    ns = sel_ref.shape[0]
    t0 = pl.program_id(2) * tq
    t_pos = t0 + lax.broadcasted_iota(jnp.int32, (tq, nc), 0)
    n_idx = lax.broadcasted_iota(jnp.int32, (tq, nc), 1)
    vis = (t_pos >= n_idx * CMP_STRIDE + (CMP_LEN - 1)) & (n_idx < nc - 1)
    bias = jnp.where(vis, 0.0, NEG_INF)
    kct = kct_ref[...]
    vc = vc_ref[...]
    psum = jnp.zeros((tq, nc), F32)
    for hh in range(NSA_HPG):
        s = jnp.dot(q_ref[:, hh * HEAD_DIM:(hh + 1) * HEAD_DIM], kct, preferred_element_type=F32) + bias
        m = jnp.max(s, axis=-1, keepdims=True)
        m = jnp.where(m == NEG_INF, 0.0, m)
        p = jnp.exp2(s - m)
        den = jnp.sum(p, axis=-1, keepdims=True)
        p = p * (1.0 / jnp.where(den > 0.0, den, 1.0))
        oc_ref[:, hh * HEAD_DIM:(hh + 1) * HEAD_DIM] = jnp.dot(
            p.astype(BF16), vc, preferred_element_type=F32).astype(oc_ref.dtype)
        psum = psum + p
    pt = psum.T
    hi = pt.astype(BF16)
    lo = (pt - hi.astype(F32)).astype(BF16)
    ov = ov_ref[...]
    imp = jnp.dot(ov, hi, preferred_element_type=F32) + jnp.dot(ov, lo, preferred_element_type=F32)
    blk = lax.broadcasted_iota(jnp.int32, (ns, tq), 0)
    cur = (t0 + lax.broadcasted_iota(jnp.int32, (ns, tq), 1)) // SLC_LEN
    forced = (blk == 0) | ((blk <= cur) & (blk > cur - N_LOCAL_SLC))
    val = jnp.where(blk > cur, NEG_INF, jnp.where(forced, FORCE_SCORE, imp))
    rank = jnp.zeros((ns, tq), F32)
    for i in range(ns):
        other = val[i:i + 1, :]
        ahead = (other > val) | ((other == val) & (blk > i))
        rank = rank + jnp.where(ahead, 1.0, 0.0)
    sel_ref[...] = jnp.where(rank < float(min(SLC_TOPK, ns)), 1.0, 0.0).astype(sel_ref.dtype)


def nsa_cmp_attention(qn, kct, vc, ov, B, S, *, tq):
    G = NSA_KV_GROUPS
    nq = S // tq
    nc = vc.shape[2]
    ns = S // SLC_LEN
    gw = NSA_HPG * HEAD_DIM
    return pl.pallas_call(
        functools.partial(_cattn_kernel, tq=tq),
        grid=(B, G, nq),
        in_specs=[pl.BlockSpec((tq, gw), lambda b, g, i: (b * nq + i, g)),
                  pl.BlockSpec((None, None, HEAD_DIM, nc), lambda b, g, i: (b, g, 0, 0)),
                  pl.BlockSpec((None, None, nc, HEAD_DIM), lambda b, g, i: (b, g, 0, 0)),
                  pl.BlockSpec((ns, nc), lambda b, g, i: (0, 0))],
        out_specs=[pl.BlockSpec((tq, gw), lambda b, g, i: (b * nq + i, g)),
                   pl.BlockSpec((None, None, ns, tq), lambda b, g, i: (b, g, 0, i))],
        out_shape=[jax.ShapeDtypeStruct((B * S, NSA_HEADS * HEAD_DIM), BF16),
                   jax.ShapeDtypeStruct((B, G, ns, S), BF16)],
        compiler_params=_cparams(("parallel", "parallel", "parallel")),
        name="nsa_cmp_attention",
    )(qn, kct, vc, ov)


def _flash_kernel(*refs, mode, hg, tq, tk, tv, dqk, dv):
    if mode == "sel":
        q_ref, k_ref, vt_ref, sel_ref, ex_ref, o_ref, qs, m_s, l_s, acc_s, sa, sb = refs
    else:
        q_ref, k_ref, vt_ref, o_ref, qs, m_s, l_s, acc_s, sa, sb = refs
    rows = hg * tq
    q0 = pl.program_id(2) * tq
    cd = q0 // tk
    for hh in range(hg):
        qs[hh * tq:(hh + 1) * tq, :] = q_ref[:, hh * dqk:(hh + 1) * dqk]

    def scores(c, kind):
        start = pl.multiple_of(c * tk, tk)
        s = lax.dot_general(k_ref[pl.ds(start, tk), :], qs[...], (((1,), (1,)), ((), ())),
                            preferred_element_type=F32)
        ok = None
        if kind == "diag":
            k_pos = start + lax.broadcasted_iota(jnp.int32, (tk, tq), 0)
            q_pos = q0 + lax.broadcasted_iota(jnp.int32, (tk, tq), 1)
            ok = k_pos <= q_pos
        if mode == "sel":
            chosen = jnp.dot(ex_ref[c], sel_ref[...], preferred_element_type=F32) > 0.5
            ok = chosen if ok is None else (ok & chosen)
        if ok is not None:
            bias = jnp.where(ok, 0.0, NEG_INF)
            s = s + (jnp.concatenate([bias] * hg, axis=1) if hg > 1 else bias)
        return s

    def update(c, s, carry):
        m, l, acc = carry
        m_new = jnp.maximum(m, jnp.max(s, axis=0, keepdims=True))
        p = jnp.exp2(s - m_new)
        alpha = jnp.exp2(m - m_new)
        l = alpha * l + jnp.sum(p, axis=0, keepdims=True)
        r = tk // tv
        vt = vt_ref[c] if r == 1 else jnp.concatenate([vt_ref[c * r + j] for j in range(r)], axis=1)
        acc = alpha * acc + jnp.dot(vt, p.astype(BF16), preferred_element_type=F32)
        return m_new, l, acc

    def load():
        return m_s[...], l_s[...], acc_s[...]

    def store(carry):
        m_s[...], l_s[...], acc_s[...] = carry

    nd = max(tq // tk, 1)
    last = k_ref.shape[0] // tk - 1
    carry = (jnp.full((1, rows), NEG_INF, F32), jnp.zeros((1, rows), F32), jnp.zeros((dv, rows), F32))
    s_next = scores(cd, "diag")
    for j in range(nd):
        s_cur = s_next
        if j + 1 < nd:
            s_next = scores(cd + j + 1, "diag")
        else:
            sa[...] = scores(0, "full")
        carry = update(cd + j, s_cur, carry)
    store(carry)

    def pairs(first, count):
        carry = load()
        for j in range(count):
            c = first + 2 * j
            sb[...] = scores(c + 1, "full")
            carry = update(c, sa[...], carry)
            sa[...] = scores(jnp.minimum(c + 2, last), "full")
            carry = update(c + 1, sb[...], carry)
        store(carry)

    def four(i, _):
        pairs(4 * i, 2)
        return 0

    lax.fori_loop(0, cd // 4, four, 0)

    @pl.when(cd % 4 >= 2)
    def _():
        pairs((cd // 4) * 4, 1)

    @pl.when(cd % 2 == 1)
    def _():
        store(update(cd - 1, sa[...], load()))

    o = acc_s[...] * (1.0 / l_s[...])
    for hh in range(hg):
        o_ref[:, hh * dv:(hh + 1) * dv] = o[:, hh * tq:(hh + 1) * tq].T.astype(o_ref.dtype)


def flash_attention(q, k, vt, B, S, *, mode, n_groups, hg, dqk, dv, kcol0, tq, tk, sel_t=None, expand=None):
    nq = S // tq
    nch = S // tk
    tv = vt.shape[-1]
    assert (tk % tq == 0 or tq % tk == 0) and S % tk == 0 and S % tq == 0 and tq % LANES == 0 and tk % tv == 0
    in_specs = [pl.BlockSpec((tq, hg * dqk), lambda b, g, i: (b * nq + i, g)),
                pl.BlockSpec((S, dqk), lambda b, g, i: (b, kcol0 + g)),
                pl.BlockSpec((None, None, S // tv, dv, tv), lambda b, g, i: (b, g, 0, 0, 0))]
    args = [q, k, vt]
    if mode == "sel":
        ns = S // SLC_LEN
        in_specs += [pl.BlockSpec((None, None, ns, tq), lambda b, g, i: (b, g, 0, i)),
                     pl.BlockSpec((nch, tk, ns), lambda b, g, i: (0, 0, 0))]
        args += [sel_t, expand]
    return pl.pallas_call(
        functools.partial(_flash_kernel, mode=mode, hg=hg, tq=tq, tk=tk, tv=tv, dqk=dqk, dv=dv),
        grid=(B, n_groups, nq),
        in_specs=in_specs,
        out_specs=pl.BlockSpec((tq, hg * dv), lambda b, g, i: (b * nq + i, g)),
        out_shape=jax.ShapeDtypeStruct((B * S, n_groups * hg * dv), BF16),
        scratch_shapes=[pltpu.VMEM((hg * tq, dqk), BF16),
                        pltpu.VMEM((1, hg * tq), F32),
                        pltpu.VMEM((1, hg * tq), F32),
                        pltpu.VMEM((dv, hg * tq), F32),
                        pltpu.VMEM((tk, hg * tq), F32),
                        pltpu.VMEM((tk, hg * tq), F32)],
        compiler_params=_cparams(("parallel", "parallel", "arbitrary")),
        name="flash_" + mode,
    )(*args)


def _window_kernel(q_ref, k_ref, vt_ref, oc_ref, os_ref, gate_ref, o_ref, qs, *, hg, tq, dqk, dv):
    nk = WINDOW + tq
    q0 = pl.program_id(2) * tq
    start = pl.multiple_of(jnp.maximum(q0 - WINDOW, 0), tq)
    for hh in range(hg):
        qs[hh * tq:(hh + 1) * tq, :] = q_ref[:, hh * dqk:(hh + 1) * dqk]
    s = lax.dot_general(k_ref[pl.ds(start, nk), :], qs[...], (((1,), (1,)), ((), ())),
                        preferred_element_type=F32)
    k_pos = start + lax.broadcasted_iota(jnp.int32, (nk, tq), 0)
    q_pos = q0 + lax.broadcasted_iota(jnp.int32, (nk, tq), 1)
    bias = jnp.where((k_pos <= q_pos) & (q_pos - k_pos < WINDOW), 0.0, NEG_INF)
    s = s + jnp.concatenate([bias] * hg, axis=1)
    p = jnp.exp2(s - jnp.max(s, axis=0, keepdims=True))
    l = jnp.sum(p, axis=0, keepdims=True)
    pb = p.astype(BF16)
    c0 = start // tq
    o = jnp.zeros((dv, hg * tq), F32)
    for j in range(nk // tq):
        o = o + jnp.dot(vt_ref[c0 + j], pb[j * tq:(j + 1) * tq, :], preferred_element_type=F32)
    o = o * (1.0 / l)
    first_group = pl.program_id(1) == 0
    n_heads = 2 * hg

    def gate(br, hh):
        lo = gate_ref[:, br * n_heads + hh:br * n_heads + hh + 1]
        hi = gate_ref[:, br * n_heads + hg + hh:br * n_heads + hg + hh + 1]
        return jnp.where(first_group, lo, hi)

    for hh in range(hg):
        sl = slice(hh * dv, (hh + 1) * dv)
        mixed = (gate(0, hh) * oc_ref[:, sl].astype(F32) + gate(1, hh) * os_ref[:, sl].astype(F32)
                 + gate(2, hh) * o[:, hh * tq:(hh + 1) * tq].T)
        o_ref[:, sl] = mixed.astype(o_ref.dtype)


def window_attention(q, k, vt, oc, os_, gates, B, S, *, n_groups, hg, dqk, dv, kcol0, tq):
    assert n_groups == NSA_KV_GROUPS
    nq = S // tq
    assert WINDOW % tq == 0 and S >= WINDOW + tq and tq % LANES == 0
    return pl.pallas_call(
        functools.partial(_window_kernel, hg=hg, tq=tq, dqk=dqk, dv=dv),
        grid=(B, n_groups, nq),
        in_specs=[pl.BlockSpec((tq, hg * dqk), lambda b, g, i: (b * nq + i, g)),
                  pl.BlockSpec((S, dqk), lambda b, g, i: (b, kcol0 + g)),
                  pl.BlockSpec((None, None, nq, dv, tq), lambda b, g, i: (b, g, 0, 0, 0)),
                  pl.BlockSpec((tq, hg * dv), lambda b, g, i: (b * nq + i, g)),
                  pl.BlockSpec((tq, hg * dv), lambda b, g, i: (b * nq + i, g)),
                  pl.BlockSpec((tq, LANES), lambda b, g, i: (b * nq + i, 0))],
        out_specs=pl.BlockSpec((tq, hg * dv), lambda b, g, i: (b * nq + i, g)),
        out_shape=jax.ShapeDtypeStruct((B * S, n_groups * hg * dv), BF16),
        scratch_shapes=[pltpu.VMEM((hg * tq, dqk), BF16)],
        compiler_params=_cparams(("parallel", "parallel", "arbitrary")),
        name="window_attention",
    )(q, k, vt, oc, os_, gates)


def _mla_q_kernel(z_ref, ga_ref, w_ref, gh_ref, ct_ref, s1_ref, s2_ref, o_ref):
    h = _rms(z_ref[...], ga_ref[...]).astype(BF16)
    y = jnp.dot(h, w_ref[...], preferred_element_type=F32)
    ct, s1, s2 = ct_ref[...], s1_ref[...], s2_ref[...]
    scale = LOG2E * (QK_NOPE + QK_ROPE) ** -0.5
    for hd in range(MLA_HEADS):
        c = hd * MLA_QK_PAD
        nope = _rms(y[:, c:c + QK_NOPE], gh_ref[0:1, :])
        r = y[:, c + QK_NOPE:c + MLA_QK_PAD]
        ms = jnp.sum(r * r, axis=-1, keepdims=True) * (1.0 / QK_ROPE)
        r = _rope_lanes(r * lax.rsqrt(ms + EPS) * gh_ref[1:2, :], ct, s1, s2, QK_ROPE // 2)
        o_ref[:, c:c + QK_NOPE] = (nope * scale).astype(BF16)
        o_ref[:, c + QK_NOPE:c + MLA_QK_PAD] = (r * scale).astype(BF16)


def mla_q_proj(z, ga, wq, gh, tabs, S, *, tm):
    T = z.shape[0]
    ns = S // tm
    tab_spec = pl.BlockSpec((tm, LANES), lambda i: (i % ns, 0))
    nout = MLA_HEADS * MLA_QK_PAD
    return pl.pallas_call(
        _mla_q_kernel,
        grid=(T // tm,),
        in_specs=[pl.BlockSpec((tm, Q_LORA), lambda i: (i, OFF_QA // Q_LORA)),
                  pl.BlockSpec((1, Q_LORA), lambda i: (0, 0)),
                  pl.BlockSpec((Q_LORA, nout), lambda i: (0, 0)),
                  pl.BlockSpec((8, LANES), lambda i: (0, 0)),
                  tab_spec, tab_spec, tab_spec],
        out_specs=pl.BlockSpec((tm, nout), lambda i: (i, 0)),
        out_shape=jax.ShapeDtypeStruct((T, nout), BF16),
        compiler_params=_cparams(("parallel",)),
        name="mla_q_proj",
    )(z, ga, wq, gh, *tabs)


def _mla_kv_kernel(z_ref, zr_ref, ga_ref, w_ref, gh_ref, ct_ref, s1_ref, s2_ref, k_ref, v_ref):
    h = _rms(z_ref[...], ga_ref[...]).astype(BF16)
    y = jnp.dot(h, w_ref[...], preferred_element_type=F32)
    r = zr_ref[...]
    ms = jnp.sum(r * r, axis=-1, keepdims=True) * (1.0 / QK_ROPE)
    r = _rope_lanes(r * lax.rsqrt(ms + EPS) * gh_ref[1:2, :], ct_ref[...], s1_ref[...], s2_ref[...],
                    QK_ROPE // 2).astype(BF16)
    nv = MLA_HEADS * QK_NOPE
    for hd in range(MLA_HEADS):
        c = hd * MLA_QK_PAD
        k_ref[:, c:c + QK_NOPE] = _rms(y[:, hd * QK_NOPE:(hd + 1) * QK_NOPE], gh_ref[0:1, :]).astype(BF16)
        k_ref[:, c + QK_NOPE:c + MLA_QK_PAD] = r
    for hd in range(MLA_HEADS):
        v_ref[hd] = y[:, nv + hd * V_HEAD:nv + (hd + 1) * V_HEAD].T.astype(BF16)


def mla_kv_proj(z, ga, wkv, gh, tabs, S, *, tm=VT_CHUNK):
    T = z.shape[0]
    ns = S // tm
    tab_spec = pl.BlockSpec((tm, LANES), lambda i: (i % ns, 0))
    nk = MLA_HEADS * MLA_QK_PAD
    nv = MLA_HEADS * V_HEAD
    return pl.pallas_call(
        _mla_kv_kernel,
        grid=(T // tm,),
        in_specs=[pl.BlockSpec((tm, KV_LORA), lambda i: (i, OFF_KVA // KV_LORA)),
                  pl.BlockSpec((tm, LANES), lambda i: (i, OFF_KR // LANES)),
                  pl.BlockSpec((1, KV_LORA), lambda i: (0, 0)),
                  pl.BlockSpec((KV_LORA, MLA_HEADS * (QK_NOPE + V_HEAD)), lambda i: (0, 0)),
                  pl.BlockSpec((8, LANES), lambda i: (0, 0)),
                  tab_spec, tab_spec, tab_spec],
        out_specs=[pl.BlockSpec((tm, nk), lambda i: (i, 0)),
                   pl.BlockSpec((None, MLA_HEADS, None, V_HEAD, tm), lambda i: (i // ns, 0, i % ns, 0, 0))],
        out_shape=[jax.ShapeDtypeStruct((T, nk), BF16),
                   jax.ShapeDtypeStruct((T // S, MLA_HEADS, ns, V_HEAD, tm), BF16)],
        compiler_params=_cparams(("parallel",)),
        name="mla_kv_proj",
    )(z, z, ga, wkv, gh, *tabs)


def _mix_kernel(a_ref, ob_ref, wa_ref, wb_ref, za_ref, zb_ref, o_ref):
    pa = jnp.dot(a_ref[...], wa_ref[...].astype(BF16), preferred_element_type=F32)
    pb = jnp.dot(ob_ref[...], wb_ref[...].astype(BF16), preferred_element_type=F32)
    o_ref[...] = (_sigmoid(za_ref[...]) * pa + _sigmoid(zb_ref[...]) * pb).astype(o_ref.dtype)


def gated_mix(oa, ob, wa, wb, z, *, tm, tn):
    T, K = oa.shape
    N = wa.shape[1]
    row = lambda w: pl.BlockSpec((tm, w), lambda m, n: (m, 0))
    return pl.pallas_call(
        _mix_kernel,
        grid=(T // tm, N // tn),
        in_specs=[row(K), row(K),
                  pl.BlockSpec((K, tn), lambda m, n: (0, n)),
                  pl.BlockSpec((K, tn), lambda m, n: (0, n)),
                  pl.BlockSpec((tm, tn), lambda m, n: (m, OFF_M // tn + n)),
                  pl.BlockSpec((tm, tn), lambda m, n: (m, (OFF_M + D_MODEL) // tn + n))],
        out_specs=pl.BlockSpec((tm, tn), lambda m, n: (m, n)),
        out_shape=jax.ShapeDtypeStruct((T, N), BF16),
        compiler_params=_cparams(("parallel", "parallel")),
        name="gated_mix",
    )(oa, ob, wa, wb, z, z)


def _ffn_up_kernel(x_ref, g_ref, w1_ref, w3_ref, o_ref, h_scr):
    @pl.when(pl.program_id(1) == 0)
    def _():
        h_scr[...] = _rms(x_ref[...], g_ref[...]).astype(BF16)

    h = h_scr[...]
    a = jnp.dot(h, w1_ref[...].astype(BF16), preferred_element_type=F32)
    b = jnp.dot(h, w3_ref[...].astype(BF16), preferred_element_type=F32)
    o_ref[...] = (a * _sigmoid(a) * b).astype(o_ref.dtype)


def ffn_up(x, g, w1, w3, *, tm, tn):
    T, K = x.shape
    N = w1.shape[1]
    return pl.pallas_call(
        _ffn_up_kernel,
        grid=(T // tm, N // tn),
        in_specs=[pl.BlockSpec((tm, K), lambda m, n: (m, 0)),
                  pl.BlockSpec((1, K), lambda m, n: (0, 0)),
                  pl.BlockSpec((K, tn), lambda m, n: (0, n)),
                  pl.BlockSpec((K, tn), lambda m, n: (0, n))],
        out_specs=pl.BlockSpec((tm, tn), lambda m, n: (m, n)),
        out_shape=jax.ShapeDtypeStruct((T, N), BF16),
        scratch_shapes=[pltpu.VMEM((tm, K), BF16)],
        compiler_params=_cparams(("parallel", "arbitrary")),
        name="ffn_up",
    )(x, g.reshape(1, K), w1, w3)


def _pack_bf16_pairs(h):
    k = h.shape[1] // 2
    hi = lax.bitcast_convert_type(h[:, :k].astype(jnp.bfloat16).astype(F32), jnp.uint32)
    lo = lax.bitcast_convert_type(h[:, k:].astype(jnp.bfloat16).astype(F32), jnp.uint32)
    return lax.bitcast_convert_type(hi | (lo >> 16), jnp.int32)


def _unpack_bf16_pairs(xp):
    xp = lax.bitcast_convert_type(xp, jnp.uint32)
    hi = lax.bitcast_convert_type(xp & jnp.uint32(0xFFFF0000), F32)
    lo = lax.bitcast_convert_type(xp << 16, F32)
    return hi.astype(BF16), lo.astype(BF16)


def _router_kernel(x_ref, g_ref, wr_ref, h_ref, idx_ref, gate_ref):
    h = _rms(x_ref[...], g_ref[...])
    is_pad = pl.program_id(0) == pl.num_programs(0) - 1
    h_ref[...] = jnp.where(is_pad, 0, _pack_bf16_pairs(h))
    logits = jnp.dot(h, wr_ref[...], preferred_element_type=F32, precision=lax.Precision.HIGHEST)
    lane = lax.broadcasted_iota(jnp.int32, logits.shape, 1).astype(F32)
    logits = jnp.where(lane < float(N_EXPERTS), logits, NEG_INF)
    m1 = jnp.max(logits, axis=-1, keepdims=True)
    i1 = jnp.min(jnp.where(logits == m1, lane, float(LANES)), axis=-1, keepdims=True)
    rest = jnp.where(lane == i1, NEG_INF, logits)
    m2 = jnp.max(rest, axis=-1, keepdims=True)
    i2 = jnp.min(jnp.where(rest == m2, lane, float(LANES)), axis=-1, keepdims=True)
    e = jnp.exp(m2 - m1)
    den = 1.0 + e
    idx_ref[...] = jnp.where(lane == 0.0, i1, jnp.where(lane == 1.0, i2, 0.0)).astype(jnp.int32)
    gate_ref[...] = jnp.where(lane == 0.0, 1.0 / den, jnp.where(lane == 1.0, e / den, 0.0))


def router(x, g, wr_pad, *, tm):
    T, K = x.shape
    nt = T // tm
    return pl.pallas_call(
        _router_kernel,
        grid=(nt + 1,),
        in_specs=[pl.BlockSpec((tm, K), lambda i: (jnp.minimum(i, nt - 1), 0)),
                  pl.BlockSpec((1, K), lambda i: (0, 0)),
                  pl.BlockSpec((K, LANES), lambda i: (0, 0))],
        out_specs=[pl.BlockSpec((tm, K // 2), lambda i: (i, 0)),
                   pl.BlockSpec((tm, LANES), lambda i: (jnp.minimum(i, nt - 1), 0)),
                   pl.BlockSpec((tm, LANES), lambda i: (jnp.minimum(i, nt - 1), 0))],
        out_shape=[jax.ShapeDtypeStruct((T + tm, K // 2), jnp.int32),
                   jax.ShapeDtypeStruct((T, LANES), jnp.int32),
                   jax.ShapeDtypeStruct((T, LANES), F32)],
        compiler_params=_cparams(("arbitrary",)),
        name="router",
    )(x, g.reshape(1, K), wr_pad)


def _moe_weight_stream(w_hbms, stages, casts, sems, be_ref, nu_ref, gs_ref, ne_ref, lg_ref, tn):
    n = pl.program_id(0)
    r = pl.program_id(1)
    used = r < nu_ref[0]

    def copies(e, nt):
        c0 = pl.multiple_of(nt * tn, tn)
        return [pltpu.make_async_copy(w.at[e, :, pl.ds(c0, tn)], st, sems.at[i])
                for i, (w, st) in enumerate(zip(w_hbms, stages))]

    @pl.when((n == 0) & (r == 0))
    def _():
        for c in copies(be_ref[0], 0):
            c.start()

    @pl.when(used & (gs_ref[r] == 1))
    def _():
        for c in copies(be_ref[r], n):
            c.wait()
        for st, wb in zip(stages, casts):
            wb[...] = st[...].astype(BF16)
        last = lg_ref[r] == 1

        @pl.when(jnp.logical_not(last & (n == pl.num_programs(0) - 1)))
        def _():
            for c in copies(ne_ref[r], n + last.astype(jnp.int32)):
                c.start()

    return used


def _moe_up_kernel(be_ref, nu_ref, gs_ref, ne_ref, lg_ref, x_ref, w1_hbm, w3_hbm, o_ref,
                   st1, st3, w1_s, w3_s, sems, *, tn):
    used = _moe_weight_stream((w1_hbm, w3_hbm), (st1, st3), (w1_s, w3_s), sems,
                              be_ref, nu_ref, gs_ref, ne_ref, lg_ref, tn)

    @pl.when(used)
    def _():
        xa, xb = _unpack_bf16_pairs(x_ref[...])
        k2 = xa.shape[1]
        a = (jnp.dot(xa, w1_s[:k2, :], preferred_element_type=F32)
             + jnp.dot(xb, w1_s[k2:, :], preferred_element_type=F32))
        b = (jnp.dot(xa, w3_s[:k2, :], preferred_element_type=F32)
             + jnp.dot(xb, w3_s[k2:, :], preferred_element_type=F32))
        o_ref[...] = (a * _sigmoid(a) * b).astype(o_ref.dtype)

    @pl.when(jnp.logical_not(used))
    def _():
        o_ref[...] = jnp.zeros(o_ref.shape, o_ref.dtype)


def moe_up(tables, xb, w1, w3, *, tmb, tn):
    n_slot = xb.shape[0]
    K, N = w1.shape[1], w1.shape[2]
    row = lambda r, nu: jnp.minimum(r, nu[0] - 1)
    return pl.pallas_call(
        functools.partial(_moe_up_kernel, tn=tn),
        grid_spec=pltpu.PrefetchScalarGridSpec(
            num_scalar_prefetch=5,
            grid=(N // tn, n_slot // tmb),
            in_specs=[pl.BlockSpec((tmb, K // 2), lambda n, r, be, nu, gs, ne, lg: (row(r, nu), 0)),
                      pl.BlockSpec(memory_space=pl.ANY), pl.BlockSpec(memory_space=pl.ANY)],
            out_specs=pl.BlockSpec((tmb, tn), lambda n, r, be, nu, gs, ne, lg: (r, n)),
            scratch_shapes=[pltpu.VMEM((K, tn), F32), pltpu.VMEM((K, tn), F32),
                            pltpu.VMEM((K, tn), BF16), pltpu.VMEM((K, tn), BF16),
                            pltpu.SemaphoreType.DMA((2,))]),
        out_shape=jax.ShapeDtypeStruct((n_slot, N), BF16),
        compiler_params=_cparams(("arbitrary", "arbitrary")),
        name="moe_up",
    )(*tables, xb, w1, w3)


def _moe_down_kernel(be_ref, nu_ref, gs_ref, ne_ref, lg_ref, a_ref, w2_hbm, o_ref, st2, w2_s, sems, *, tn):
    used = _moe_weight_stream((w2_hbm,), (st2,), (w2_s,), sems, be_ref, nu_ref, gs_ref, ne_ref, lg_ref, tn)

    @pl.when(used)
    def _():
        o_ref[...] = jnp.dot(a_ref[...], w2_s[...], preferred_element_type=F32)

    @pl.when(jnp.logical_not(used))
    def _():
        o_ref[...] = jnp.zeros(o_ref.shape, o_ref.dtype)


def moe_down(tables, act, w2, *, tmb, tn):
    n_slot, K = act.shape
    N = w2.shape[2]
    row = lambda r, nu: jnp.minimum(r, nu[0] - 1)
    return pl.pallas_call(
        functools.partial(_moe_down_kernel, tn=tn),
        grid_spec=pltpu.PrefetchScalarGridSpec(
            num_scalar_prefetch=5,
            grid=(N // tn, n_slot // tmb),
            in_specs=[pl.BlockSpec((tmb, K), lambda n, r, be, nu, gs, ne, lg: (row(r, nu), 0)),
                      pl.BlockSpec(memory_space=pl.ANY)],
            out_specs=pl.BlockSpec((tmb, tn), lambda n, r, be, nu, gs, ne, lg: (r, n)),
            scratch_shapes=[pltpu.VMEM((K, tn), F32), pltpu.VMEM((K, tn), BF16),
                            pltpu.SemaphoreType.DMA((1,))]),
        out_shape=jax.ShapeDtypeStruct((n_slot, N), F32),
        compiler_params=_cparams(("arbitrary", "arbitrary")),
        name="moe_down",
    )(*tables, act, w2)


def _rope_tabs(pos, rot_dim, n_rows):
    half = rot_dim // 2
    inv = 1.0 / (ROPE_THETA ** (jnp.arange(0, rot_dim, 2, dtype=F32) / rot_dim))
    ang = jnp.asarray(pos).astype(F32)[:, None] * inv[None, :]
    c, s = jnp.cos(ang), jnp.sin(ang)
    z = jnp.zeros_like(c)
    pad = lambda a, fill: jnp.pad(a, ((0, n_rows - a.shape[0]), (0, LANES - a.shape[1])), constant_values=fill)
    return pad(jnp.concatenate([c, c], 1), 1.0), pad(jnp.concatenate([-s, z], 1), 0.0), \
        pad(jnp.concatenate([z, s], 1), 0.0)


def _mla_tabs(S):
    ct, s1, s2 = _rope_tabs(jnp.arange(S), QK_ROPE, S)
    lane = jnp.arange(LANES)[None, :]
    return jnp.where(lane < QK_ROPE, ct, 0.0), s1, s2


def _pad_cols(w, n):
    return jnp.pad(w, ((0, 0), (0, n - w.shape[1])))


def _layout_w_in(w):
    sp = np.cumsum([0, NSA_HEADS * HEAD_DIM, NSA_KV_COLS, 3 * NSA_HEADS, Q_LORA, KV_LORA, QK_ROPE, 2 * D_MODEL])
    q, kv, g, qa, kva, kr, m = [w[:, sp[i]:sp[i + 1]] for i in range(7)]
    return jnp.concatenate([qa, _pad_cols(g, LANES), _pad_cols(kr, LANES), q, kva, kv, m], axis=1).astype(BF16)


def _layout_w_q_b(w):
    w = w.reshape(Q_LORA, MLA_HEADS, QK_NOPE + QK_ROPE)
    w = jnp.pad(w, ((0, 0), (0, 0), (0, MLA_QK_PAD - QK_NOPE - QK_ROPE)))
    return w.reshape(Q_LORA, MLA_HEADS * MLA_QK_PAD).astype(BF16)


def _layout_w_kv_b(w):
    w = w.reshape(KV_LORA, MLA_HEADS, QK_NOPE + V_HEAD)
    return jnp.concatenate([w[:, :, :QK_NOPE].reshape(KV_LORA, -1), w[:, :, QK_NOPE:].reshape(KV_LORA, -1)],
                           axis=1).astype(BF16)


def _layout_w_cmp(w):
    h = CMP_LEN // 2
    return jnp.concatenate([w[:h].reshape(h * HEAD_DIM, HEAD_DIM), w[h:].reshape(h * HEAD_DIM, HEAD_DIM)],
                           axis=1).astype(BF16)


def _layout_pe(pe):
    return jnp.pad(pe.reshape(2, (CMP_LEN // 2) * HEAD_DIM), ((0, 6), (0, 0))).astype(BF16)


def _overlap(nc, ns):
    n = np.arange(nc)[None, :] * CMP_STRIDE
    j = np.arange(ns)[:, None] * SLC_LEN
    ov = (n <= j + SLC_LEN - 1) & (j <= n + CMP_LEN - 1) & (np.arange(nc)[None, :] < nc - 1)
    return jnp.asarray(ov.astype(np.float32), BF16)


def _expand(ns, S, tk):
    e = ((np.arange(S // tk)[:, None, None] * tk + np.arange(tk)[None, :, None]) // SLC_LEN
         == np.arange(ns)[None, None, :])
    return jnp.asarray(e.astype(np.float32), BF16)


def _layout_layers(p):
    depth = p['w_in'].shape[0]
    lane_row = lambda g: jnp.pad(g, ((0, 0), (0, LANES - g.shape[-1])))[:, None, :]
    z4 = jnp.zeros((depth, 4, HEAD_DIM), F32)
    z6 = jnp.zeros((depth, 6, LANES), F32)
    return dict(
        gn=jnp.concatenate([p['nsa_q_norm_g'][:, None, :], p['nsa_k_norm_g'], z4], axis=1),
        w_ck=jax.vmap(_layout_w_cmp)(p['w_cmp_k']), w_cv=jax.vmap(_layout_w_cmp)(p['w_cmp_v']),
        pe_k=jax.vmap(_layout_pe)(p['cmp_pe_k']), pe_v=jax.vmap(_layout_pe)(p['cmp_pe_v']),
        gh_q=jnp.concatenate([p['mla_q_norm_g'][:, None, :], lane_row(p['mla_qr_norm_g']), z6], axis=1),
        gh_k=jnp.concatenate([p['mla_k_norm_g'][:, None, :], lane_row(p['mla_kr_norm_g']), z6], axis=1),
        w_qb=jax.vmap(_layout_w_q_b)(p['w_q_b']), w_kvb=jax.vmap(_layout_w_kv_b)(p['w_kv_b']))


def _attention_block(x2d, B, S, p, pw, l, tabs):
    T = x2d.shape[0]
    z = norm_matmul(x2d, p['attn_norm_g'][l], _layout_w_in(p['w_in'][l]), tm=min(1024, T), tn=512)
    qn, kvp, xc, vts, vtw, gates = nsa_prep(z, tabs['tok'], pw['gn'][l], S)

    nc = S // CMP_STRIDE
    x2 = xc.reshape(B, 2 * NSA_KV_GROUPS, nc, CMP_STRIDE * HEAD_DIM)
    kct, vc = nsa_compress(x2, pw['w_ck'][l], pw['w_cv'][l], pw['pe_k'][l], pw['pe_v'][l],
                           p['nsa_k_norm_g'][l][0:1], tabs['cmp'])
    ns = S // SLC_LEN
    oc, sel_t = nsa_cmp_attention(qn, kct, vc, _overlap(nc, ns), B, S, tq=min(256, S))
    nsa = dict(n_groups=NSA_KV_GROUPS, hg=NSA_HPG, dqk=HEAD_DIM, dv=HEAD_DIM)
    tks = min(512, S)
    os_ = flash_attention(qn, kvp, vts, B, S, mode="sel", kcol0=0, tq=256, tk=tks,
                          sel_t=sel_t, expand=_expand(ns, S, tks), **nsa)
    oa = window_attention(qn, kvp, vtw, oc, os_, gates, B, S, kcol0=NSA_KV_GROUPS, tq=VT_CHUNK, **nsa)

    qm = mla_q_proj(z, p['mla_qa_norm_g'][l][None], pw['w_qb'][l], pw['gh_q'][l], tabs['mla'], S, tm=256)
    km, vmt = mla_kv_proj(z, p['mla_kva_norm_g'][l][None], pw['w_kvb'][l], pw['gh_k'][l], tabs['mla'], S)
    tkm = min(512, S)
    ob = flash_attention(qm, km, vmt, B, S, mode="causal",
                         n_groups=MLA_HEADS, hg=1, dqk=MLA_QK_PAD, dv=V_HEAD, kcol0=0, tq=min(1024, S), tk=tkm)

    tm = min(2048, T)
    mix = gated_mix(oa, ob, p['w_proj_nsa'][l], p['w_proj_mla'][l], z, tm=tm, tn=512)
    return matmul_residual(mix, p['w_out'][l], x2d, tm=min(2048, T), tn=512)


def _dense_ffn(x2d, g, w1, w3, w2):
    T = x2d.shape[0]
    act = ffn_up(x2d, g, w1.astype(BF16), w3.astype(BF16), tm=min(1024, T), tn=512)
    return matmul_residual(act, w2.astype(BF16), x2d, tm=min(1024, T), tn=256)


def _cumsum_rows(oh, blk=128):
    A, E = oh.shape
    nb = A // blk
    x = oh.astype(F32).reshape(nb, blk, E)
    within = jnp.einsum('ij,bje->bie', jnp.tril(jnp.ones((blk, blk), F32)), x)
    before = jnp.tril(jnp.ones((nb, nb), F32), -1) @ within[:, -1, :]
    return (within + before[:, None, :]).astype(jnp.int32).reshape(A, E)


def _moe_ffn(x2d, g, w_router, w1, w3, w2, *, tmb=512):
    T = x2d.shape[0]
    hp, idx, gate = router(x2d, g, _pad_cols(w_router, LANES), tm=256)
    A = T * TOP_K
    e_flat = idx[:, :TOP_K].reshape(A)
    tok_flat = jnp.repeat(jnp.arange(T, dtype=jnp.int32), TOP_K)
    oh = (e_flat[:, None] == jnp.arange(N_EXPERTS)[None, :]).astype(jnp.int32)
    csum = _cumsum_rows(oh)
    rank = jnp.sum(oh * csum, axis=1) - 1
    counts = csum[-1]
    padded = (counts + tmb - 1) // tmb * tmb
    pad_end = jnp.cumsum(padded)
    dest = (pad_end - padded)[e_flat] + rank
    n_blk = -(-A // tmb) + N_EXPERTS
    n_slot = n_blk * tmb
    slot_tok = jnp.full((n_slot,), T, jnp.int32).at[dest].set(tok_flat)
    eidx = jnp.arange(N_EXPERTS, dtype=jnp.int32)
    blk = jnp.arange(n_blk, dtype=jnp.int32)
    blk_exp = jnp.minimum(jnp.sum((pad_end[None, :] <= (blk * tmb)[:, None]).astype(jnp.int32), axis=1),
                          N_EXPERTS - 1).astype(jnp.int32)
    n_used = (pad_end[-1:] // tmb).astype(jnp.int32)
    present = counts > 0
    first_e = jnp.min(jnp.where(present, eidx, N_EXPERTS))
    last_e = jnp.max(jnp.where(present, eidx, -1))
    later = jnp.where(present[None, :] & (eidx[None, :] > eidx[:, None]), eidx[None, :], N_EXPERTS)
    next_e = jnp.min(later, axis=1)
    next_e = jnp.where(next_e == N_EXPERTS, first_e, next_e).astype(jnp.int32)
    starts = ((blk == 0) | (blk_exp != jnp.roll(blk_exp, 1))) & (blk < n_used[0])
    tables = (blk_exp, n_used, starts.astype(jnp.int32), next_e[blk_exp],
              (blk_exp == last_e).astype(jnp.int32))
    xb = hp[slot_tok]
    act = moe_up(tables, xb, w1, w3, tmb=tmb, tn=1024)
    yb = moe_down(tables, act, w2, tmb=tmb, tn=512)
    d2 = dest.reshape(T, TOP_K)
    g2 = gate[:, :TOP_K]
    return x2d + g2[:, 0:1] * yb[d2[:, 0]] + g2[:, 1:2] * yb[d2[:, 1]]


def kernel(x, attn_norm_g, w_in, nsa_q_norm_g, nsa_k_norm_g, cmp_pe_k, cmp_pe_v, w_cmp_k, w_cmp_v, mla_qa_norm_g, w_q_b, mla_kva_norm_g, w_kv_b, mla_q_norm_g, mla_qr_norm_g, mla_k_norm_g, mla_kr_norm_g, w_proj_nsa, w_proj_mla, w_out, ffn_norm_g, w_ff1, w_ff3, w_ff2, w_router, w_e1, w_e3, w_e2):
    p = dict(attn_norm_g=attn_norm_g, w_in=w_in, nsa_q_norm_g=nsa_q_norm_g, nsa_k_norm_g=nsa_k_norm_g,
             cmp_pe_k=cmp_pe_k, cmp_pe_v=cmp_pe_v, w_cmp_k=w_cmp_k, w_cmp_v=w_cmp_v,
             mla_qa_norm_g=mla_qa_norm_g, w_q_b=w_q_b, mla_kva_norm_g=mla_kva_norm_g, w_kv_b=w_kv_b,
             mla_q_norm_g=mla_q_norm_g, mla_qr_norm_g=mla_qr_norm_g, mla_k_norm_g=mla_k_norm_g,
             mla_kr_norm_g=mla_kr_norm_g, w_proj_nsa=w_proj_nsa, w_proj_mla=w_proj_mla, w_out=w_out)
    B, S, D = x.shape
    depth = w_in.shape[0]
    nc = S // CMP_STRIDE
    tabs = dict(tok=_rope_tabs(jnp.arange(S), NSA_ROT, S),
                cmp=_rope_tabs(jnp.arange(nc - 1) * CMP_STRIDE + CMP_LEN - 1, NSA_ROT, nc),
                mla=_mla_tabs(S))
    pw = _layout_layers(p)
    x2d = x.reshape(B * S, D)
    for l in range(depth):
        x2d = _attention_block(x2d, B, S, p, pw, l, tabs)
        if l % 2 == 0:
            x2d = _dense_ffn(x2d, ffn_norm_g[l], w_ff1[l // 2], w_ff3[l // 2], w_ff2[l // 2])
        else:
            x2d = _moe_ffn(x2d, ffn_norm_g[l], w_router[l // 2], w_e1[l // 2], w_e3[l // 2], w_e2[l // 2])
    return x2d.reshape(B, S, D)
```

```python
import functools

import numpy as np
import jax
import jax.numpy as jnp
from jax import lax
from jax.experimental import pallas as pl
from jax.experimental.pallas import tpu as pltpu

D_MODEL = 2048
HEAD_DIM = 128
NSA_HEADS = 8
NSA_KV_GROUPS = 2
NSA_HPG = NSA_HEADS // NSA_KV_GROUPS
NSA_ROT = HEAD_DIM // 4
CMP_LEN = 32
CMP_STRIDE = 16
SLC_LEN = 64
SLC_TOPK = 16
N_LOCAL_SLC = 2
WINDOW = 512
FORCE_SCORE = 1.0e4
MLA_HEADS = 8
Q_LORA = 768
KV_LORA = 512
QK_NOPE = 128
QK_ROPE = 64
V_HEAD = 128
ROPE_THETA = 500000.0
EPS = 1e-6
D_FF = 7168
N_EXPERTS = 8
TOP_K = 2

LANES = 128
MLA_QK_PAD = 256
NEG_INF = float("-inf")
LOG2E = 1.4426950408889634
BF16 = jnp.bfloat16
F32 = jnp.float32

OFF_QA = 0
OFF_G = 768
OFF_KR = 896
OFF_Q = 1024
OFF_KVA = 2048
OFF_KV = 2560
OFF_M = 4096
D_INP = 8192
NSA_KV_COLS = 3 * 2 * NSA_KV_GROUPS * HEAD_DIM
NSA_KVP_COLS = 2 * NSA_KV_GROUPS * HEAD_DIM
VT_CHUNK = 256

VMEM_LIMIT = 56 * 1024 * 1024


def _cparams(sem):
    return pltpu.CompilerParams(dimension_semantics=sem, vmem_limit_bytes=VMEM_LIMIT)


def _rms(x, g):
    ms = jnp.mean(x * x, axis=-1, keepdims=True)
    return x * lax.rsqrt(ms + EPS) * g


def _rope_lanes(y, ct, s1, s2, half):
    return y * ct + pltpu.roll(y, LANES - half, 1) * s1 + pltpu.roll(y, half, 1) * s2


def _sigmoid(x):
    return 1.0 / (1.0 + jnp.exp(-x))


def _nmm_kernel(x_ref, g_ref, w_ref, o_ref, h_scr):
    @pl.when(pl.program_id(1) == 0)
    def _():
        h_scr[...] = _rms(x_ref[...], g_ref[...]).astype(BF16)

    o_ref[...] = jnp.dot(h_scr[...], w_ref[...].astype(BF16),
                         preferred_element_type=F32).astype(o_ref.dtype)


def norm_matmul(x, g, w, *, tm, tn, out_dtype=F32):
    T, K = x.shape
    N = w.shape[1]
    return pl.pallas_call(
        _nmm_kernel,
        grid=(T // tm, N // tn),
        in_specs=[pl.BlockSpec((tm, K), lambda m, n: (m, 0)),
                  pl.BlockSpec((1, K), lambda m, n: (0, 0)),
                  pl.BlockSpec((K, tn), lambda m, n: (0, n))],
        out_specs=pl.BlockSpec((tm, tn), lambda m, n: (m, n)),
        out_shape=jax.ShapeDtypeStruct((T, N), out_dtype),
        scratch_shapes=[pltpu.VMEM((tm, K), BF16)],
        compiler_params=_cparams(("parallel", "arbitrary")),
        name="norm_matmul",
    )(x, g.reshape(1, K), w)


def _mmres_kernel(a_ref, w_ref, r_ref, o_ref):
    o_ref[...] = r_ref[...] + jnp.dot(a_ref[...], w_ref[...].astype(BF16), preferred_element_type=F32)


def matmul_residual(a, w, res, *, tm, tn):
    T, K = a.shape
    N = w.shape[1]
    return pl.pallas_call(
        _mmres_kernel,
        grid=(T // tm, N // tn),
        in_specs=[pl.BlockSpec((tm, K), lambda m, n: (m, 0)),
                  pl.BlockSpec((K, tn), lambda m, n: (0, n)),
                  pl.BlockSpec((tm, tn), lambda m, n: (m, n))],
        out_specs=pl.BlockSpec((tm, tn), lambda m, n: (m, n)),
        out_shape=jax.ShapeDtypeStruct((T, N), F32),
        compiler_params=_cparams(("parallel", "arbitrary")),
        name="matmul_residual",
    )(a, w, res)


def _prep_kernel(zg_ref, zq_ref, zc_ref, zs_ref, zw_ref, ct_ref, s1_ref, s2_ref, gn_ref,
                 qn_ref, kvp_ref, xc_ref, vts_ref, vtw_ref, gate_ref):
    ct, s1, s2 = ct_ref[...], s1_ref[...], s2_ref[...]
    half = NSA_ROT // 2
    gq = gn_ref[0:1, :]
    for h in range(NSA_HEADS):
        y = _rope_lanes(_rms(zq_ref[:, h * HEAD_DIM:(h + 1) * HEAD_DIM], gq), ct, s1, s2, half)
        qn_ref[:, h * HEAD_DIM:(h + 1) * HEAD_DIM] = (y * (LOG2E * HEAD_DIM ** -0.5)).astype(BF16)
    for br, zb_ref in enumerate((zc_ref, zs_ref, zw_ref)):
        for kv in range(2):
            for g in range(NSA_KV_GROUPS):
                c = (kv * NSA_KV_GROUPS + g) * HEAD_DIM
                y = zb_ref[:, c:c + HEAD_DIM]
                if kv == 0 and br > 0:
                    y = _rope_lanes(_rms(y, gn_ref[1 + br:2 + br, :]), ct, s1, s2, half)
                if br == 0:
                    xc_ref[kv * NSA_KV_GROUPS + g] = y.astype(BF16)
                elif kv == 0:
                    d = (br - 1) * NSA_KV_GROUPS * HEAD_DIM + g * HEAD_DIM
                    kvp_ref[:, d:d + HEAD_DIM] = y.astype(BF16)
                else:
                    (vts_ref if br == 1 else vtw_ref)[g] = y.T.astype(BF16)
    gate_ref[...] = _sigmoid(zg_ref[...])


def nsa_prep(z, tabs, gn, S, *, tm=VT_CHUNK):
    T = z.shape[0]
    ns = S // tm
    tab_spec = pl.BlockSpec((tm, LANES), lambda i: (i % ns, 0))
    vt_spec = pl.BlockSpec((None, NSA_KV_GROUPS, None, HEAD_DIM, tm), lambda i: (i // ns, 0, i % ns, 0, 0))
    vt_shape = jax.ShapeDtypeStruct((T // S, NSA_KV_GROUPS, ns, HEAD_DIM, tm), BF16)
    wq, wb = NSA_HEADS * HEAD_DIM, 2 * NSA_KV_GROUPS * HEAD_DIM
    npl = 2 * NSA_KV_GROUPS
    return pl.pallas_call(
        _prep_kernel,
        grid=(T // tm,),
        in_specs=[pl.BlockSpec((tm, LANES), lambda i: (i, OFF_G // LANES)),
                  pl.BlockSpec((tm, wq), lambda i: (i, OFF_Q // wq)),
                  pl.BlockSpec((tm, wb), lambda i: (i, OFF_KV // wb)),
                  pl.BlockSpec((tm, wb), lambda i: (i, OFF_KV // wb + 1)),
                  pl.BlockSpec((tm, wb), lambda i: (i, OFF_KV // wb + 2)),
                  tab_spec, tab_spec, tab_spec, pl.BlockSpec((8, LANES), lambda i: (0, 0))],
        out_specs=[pl.BlockSpec((tm, NSA_HEADS * HEAD_DIM), lambda i: (i, 0)),
                   pl.BlockSpec((tm, NSA_KVP_COLS), lambda i: (i, 0)),
                   pl.BlockSpec((None, npl, tm, HEAD_DIM), lambda i: (i // ns, 0, i % ns, 0)), vt_spec, vt_spec,
                   pl.BlockSpec((tm, LANES), lambda i: (i, 0))],
        out_shape=[jax.ShapeDtypeStruct((T, NSA_HEADS * HEAD_DIM), BF16),
                   jax.ShapeDtypeStruct((T, NSA_KVP_COLS), BF16),
                   jax.ShapeDtypeStruct((T // S, npl, S, HEAD_DIM), BF16), vt_shape, vt_shape,
                   jax.ShapeDtypeStruct((T, LANES), F32)],
        compiler_params=_cparams(("parallel",)),
        name="nsa_prep",
    )(z, z, z, z, z, *tabs, gn)


def _cmp_kernel(xk_ref, xv_ref, wk_ref, wv_ref, pek_ref, pev_ref, gk_ref, ct_ref, s1_ref, s2_ref,
                kct_ref, vc_ref):
    nc = xk_ref.shape[0]
    row = lax.broadcasted_iota(jnp.int32, (nc, HEAD_DIM), 0)

    def comp(x_ref, w_ref, pe_ref):
        w = w_ref[...]
        y = jnp.dot(x_ref[...], w, preferred_element_type=F32)
        ype = jnp.dot(pe_ref[...], w, preferred_element_type=F32)
        bias = ype[0:1, :HEAD_DIM] + ype[1:2, HEAD_DIM:]
        out = y[:, :HEAD_DIM] + pltpu.roll(y[:, HEAD_DIM:], nc - 1, 0) + bias
        return jnp.where(row < nc - 1, out, 0.0)

    k = _rms(comp(xk_ref, wk_ref, pek_ref), gk_ref[...])
    k = _rope_lanes(k, ct_ref[...], s1_ref[...], s2_ref[...], NSA_ROT // 2)
    kct_ref[...] = k.T.astype(BF16)
    vc_ref[...] = comp(xv_ref, wv_ref, pev_ref).astype(BF16)


def nsa_compress(x2, wk2, wv2, pek2, pev2, gk, tabs_cmp):
    B, _, nc, kk = x2.shape
    G = NSA_KV_GROUPS
    full = lambda shape: pl.BlockSpec(shape, lambda b, g: (0,) * len(shape))
    return pl.pallas_call(
        _cmp_kernel,
        grid=(B, G),
        in_specs=[pl.BlockSpec((None, None, nc, kk), lambda b, g: (b, g, 0, 0)),
                  pl.BlockSpec((None, None, nc, kk), lambda b, g: (b, G + g, 0, 0)),
                  full((kk, 2 * HEAD_DIM)), full((kk, 2 * HEAD_DIM)),
                  full((8, kk)), full((8, kk)), full((1, HEAD_DIM)),
                  full((nc, LANES)), full((nc, LANES)), full((nc, LANES))],
        out_specs=[pl.BlockSpec((None, None, HEAD_DIM, nc), lambda b, g: (b, g, 0, 0)),
                   pl.BlockSpec((None, None, nc, HEAD_DIM), lambda b, g: (b, g, 0, 0))],
        out_shape=[jax.ShapeDtypeStruct((B, G, HEAD_DIM, nc), BF16),
                   jax.ShapeDtypeStruct((B, G, nc, HEAD_DIM), BF16)],
        compiler_params=_cparams(("parallel", "parallel")),
        name="nsa_compress",
    )(x2, x2, wk2, wv2, pek2, pev2, gk, *tabs_cmp)


def _cattn_kernel(q_ref, kct_ref, vc_ref, ov_ref, oc_ref, sel_ref, *, tq):
    nc = vc_ref.shape[0]
    ns = sel_ref.shape[0]
    t0 = pl.program_id(2) * tq
    t_pos = t0 + lax.broadcasted_iota(jnp.int32, (tq, nc), 0)
    n_idx = lax.broadcasted_iota(jnp.int32, (tq, nc), 1)
    vis = (t_pos >= n_idx * CMP_STRIDE + (CMP_LEN - 1)) & (n_idx < nc - 1)
    bias = jnp.where(vis, 0.0, NEG_INF)
    kct = kct_ref[...]
    vc = vc_ref[...]
    psum = jnp.zeros((tq, nc), F32)
    for hh in range(NSA_HPG):
        s = jnp.dot(q_ref[:, hh * HEAD_DIM:(hh + 1) * HEAD_DIM], kct, preferred_element_type=F32) + bias
        m = jnp.max(s, axis=-1, keepdims=True)
        m = jnp.where(m == NEG_INF, 0.0, m)
        p = jnp.exp2(s - m)
        den = jnp.sum(p, axis=-1, keepdims=True)
        p = p * (1.0 / jnp.where(den > 0.0, den, 1.0))
        oc_ref[:, hh * HEAD_DIM:(hh + 1) * HEAD_DIM] = jnp.dot(
            p.astype(BF16), vc, preferred_element_type=F32).astype(oc_ref.dtype)
        psum = psum + p
    pt = psum.T
    hi = pt.astype(BF16)
    lo = (pt - hi.astype(F32)).astype(BF16)
    ov = ov_ref[...]
    imp = jnp.dot(ov, hi, preferred_element_type=F32) + jnp.dot(ov, lo, preferred_element_type=F32)
    blk = lax.broadcasted_iota(jnp.int32, (ns, tq), 0)
    cur = (t0 + lax.broadcasted_iota(jnp.int32, (ns, tq), 1)) // SLC_LEN
    forced = (blk == 0) | ((blk <= cur) & (blk > cur - N_LOCAL_SLC))
    val = jnp.where(blk > cur, NEG_INF, jnp.where(forced, FORCE_SCORE, imp))
    rank = jnp.zeros((ns, tq), F32)
    for i in range(ns):
        other = val[i:i + 1, :]
        ahead = (other > val) | ((other == val) & (blk > i))
        rank = rank + jnp.where(ahead, 1.0, 0.0)
    sel_ref[...] = jnp.where(rank < float(min(SLC_TOPK, ns)), 1.0, 0.0).astype(sel_ref.dtype)


def nsa_cmp_attention(qn, kct, vc, ov, B, S, *, tq):
    G = NSA_KV_GROUPS
    nq = S // tq
    nc = vc.shape[2]
    ns = S // SLC_LEN
    gw = NSA_HPG * HEAD_DIM
    return pl.pallas_call(
        functools.partial(_cattn_kernel, tq=tq),
        grid=(B, G, nq),
        in_specs=[pl.BlockSpec((tq, gw), lambda b, g, i: (b * nq + i, g)),
                  pl.BlockSpec((None, None, HEAD_DIM, nc), lambda b, g, i: (b, g, 0, 0)),
                  pl.BlockSpec((None, None, nc, HEAD_DIM), lambda b, g, i: (b, g, 0, 0)),
                  pl.BlockSpec((ns, nc), lambda b, g, i: (0, 0))],
        out_specs=[pl.BlockSpec((tq, gw), lambda b, g, i: (b * nq + i, g)),
                   pl.BlockSpec((None, None, ns, tq), lambda b, g, i: (b, g, 0, i))],
        out_shape=[jax.ShapeDtypeStruct((B * S, NSA_HEADS * HEAD_DIM), BF16),
                   jax.ShapeDtypeStruct((B, G, ns, S), BF16)],
        compiler_params=_cparams(("parallel", "parallel", "parallel")),
        name="nsa_cmp_attention",
    )(qn, kct, vc, ov)


def _flash_kernel(*refs, mode, hg, tq, tk, tv, dqk, dv):
    if mode == "sel":
        q_ref, k_ref, vt_ref, sel_ref, ex_ref, o_ref, qs, m_s, l_s, acc_s, sa, sb = refs
    else:
        q_ref, k_ref, vt_ref, o_ref, qs, m_s, l_s, acc_s, sa, sb = refs
    rows = hg * tq
    q0 = pl.program_id(2) * tq
    cd = q0 // tk
    for hh in range(hg):
        qs[hh * tq:(hh + 1) * tq, :] = q_ref[:, hh * dqk:(hh + 1) * dqk]

    def scores(c, kind):
        start = pl.multiple_of(c * tk, tk)
        s = lax.dot_general(k_ref[pl.ds(start, tk), :], qs[...], (((1,), (1,)), ((), ())),
                            preferred_element_type=F32)
        ok = None
        if kind == "diag":
            k_pos = start + lax.broadcasted_iota(jnp.int32, (tk, tq), 0)
            q_pos = q0 + lax.broadcasted_iota(jnp.int32, (tk, tq), 1)
            ok = k_pos <= q_pos
        if mode == "sel":
            chosen = jnp.dot(ex_ref[c], sel_ref[...], preferred_element_type=F32) > 0.5
            ok = chosen if ok is None else (ok & chosen)
        if ok is not None:
            bias = jnp.where(ok, 0.0, NEG_INF)
            s = s + (jnp.concatenate([bias] * hg, axis=1) if hg > 1 else bias)
        return s

    def update(c, s, carry):
        m, l, acc = carry
        m_new = jnp.maximum(m, jnp.max(s, axis=0, keepdims=True))
        p = jnp.exp2(s - m_new)
        alpha = jnp.exp2(m - m_new)
        l = alpha * l + jnp.sum(p, axis=0, keepdims=True)
        r = tk // tv
        vt = vt_ref[c] if r == 1 else jnp.concatenate([vt_ref[c * r + j] for j in range(r)], axis=1)
        acc = alpha * acc + jnp.dot(vt, p.astype(BF16), preferred_element_type=F32)
        return m_new, l, acc

    def load():
        return m_s[...], l_s[...], acc_s[...]

    def store(carry):
        m_s[...], l_s[...], acc_s[...] = carry

    nd = max(tq // tk, 1)
    last = k_ref.shape[0] // tk - 1
    carry = (jnp.full((1, rows), NEG_INF, F32), jnp.zeros((1, rows), F32), jnp.zeros((dv, rows), F32))
    s_next = scores(cd, "diag")
    for j in range(nd):
        s_cur = s_next
        if j + 1 < nd:
            s_next = scores(cd + j + 1, "diag")
        else:
            sa[...] = scores(0, "full")
        carry = update(cd + j, s_cur, carry)
    store(carry)

    def pairs(first, count):
        carry = load()
        for j in range(count):
            c = first + 2 * j
            sb[...] = scores(c + 1, "full")
            carry = update(c, sa[...], carry)
            sa[...] = scores(jnp.minimum(c + 2, last), "full")
            carry = update(c + 1, sb[...], carry)
        store(carry)

    def four(i, _):
        pairs(4 * i, 2)
        return 0

    lax.fori_loop(0, cd // 4, four, 0)

    @pl.when(cd % 4 >= 2)
    def _():
        pairs((cd // 4) * 4, 1)

    @pl.when(cd % 2 == 1)
    def _():
        store(update(cd - 1, sa[...], load()))

    o = acc_s[...] * (1.0 / l_s[...])
    for hh in range(hg):
        o_ref[:, hh * dv:(hh + 1) * dv] = o[:, hh * tq:(hh + 1) * tq].T.astype(o_ref.dtype)


def flash_attention(q, k, vt, B, S, *, mode, n_groups, hg, dqk, dv, kcol0, tq, tk, sel_t=None, expand=None):
    nq = S // tq
    nch = S // tk
    tv = vt.shape[-1]
    assert (tk % tq == 0 or tq % tk == 0) and S % tk == 0 and S % tq == 0 and tq % LANES == 0 and tk % tv == 0
    in_specs = [pl.BlockSpec((tq, hg * dqk), lambda b, g, i: (b * nq + i, g)),
                pl.BlockSpec((S, dqk), lambda b, g, i: (b, kcol0 + g)),
                pl.BlockSpec((None, None, S // tv, dv, tv), lambda b, g, i: (b, g, 0, 0, 0))]
    args = [q, k, vt]
    if mode == "sel":
        ns = S // SLC_LEN
        in_specs += [pl.BlockSpec((None, None, ns, tq), lambda b, g, i: (b, g, 0, i)),
                     pl.BlockSpec((nch, tk, ns), lambda b, g, i: (0, 0, 0))]
        args += [sel_t, expand]
    return pl.pallas_call(
        functools.partial(_flash_kernel, mode=mode, hg=hg, tq=tq, tk=tk, tv=tv, dqk=dqk, dv=dv),
        grid=(B, n_groups, nq),
        in_specs=in_specs,
        out_specs=pl.BlockSpec((tq, hg * dv), lambda b, g, i: (b * nq + i, g)),
        out_shape=jax.ShapeDtypeStruct((B * S, n_groups * hg * dv), BF16),
        scratch_shapes=[pltpu.VMEM((hg * tq, dqk), BF16),
                        pltpu.VMEM((1, hg * tq), F32),
                        pltpu.VMEM((1, hg * tq), F32),
                        pltpu.VMEM((dv, hg * tq), F32),
                        pltpu.VMEM((tk, hg * tq), F32),
                        pltpu.VMEM((tk, hg * tq), F32)],
        compiler_params=_cparams(("parallel", "parallel", "arbitrary")),
        name="flash_" + mode,
    )(*args)


def _window_kernel(q_ref, k_ref, vt_ref, oc_ref, os_ref, gate_ref, o_ref, qs, *, hg, tq, dqk, dv):
    nk = WINDOW + tq
    q0 = pl.program_id(2) * tq
    start = pl.multiple_of(jnp.maximum(q0 - WINDOW, 0), tq)
    for hh in range(hg):
        qs[hh * tq:(hh + 1) * tq, :] = q_ref[:, hh * dqk:(hh + 1) * dqk]
    s = lax.dot_general(k_ref[pl.ds(start, nk), :], qs[...], (((1,), (1,)), ((), ())),
                        preferred_element_type=F32)
    k_pos = start + lax.broadcasted_iota(jnp.int32, (nk, tq), 0)
    q_pos = q0 + lax.broadcasted_iota(jnp.int32, (nk, tq), 1)
    bias = jnp.where((k_pos <= q_pos) & (q_pos - k_pos < WINDOW), 0.0, NEG_INF)
    s = s + jnp.concatenate([bias] * hg, axis=1)
    p = jnp.exp2(s - jnp.max(s, axis=0, keepdims=True))
    l = jnp.sum(p, axis=0, keepdims=True)
    pb = p.astype(BF16)
    c0 = start // tq
    o = jnp.zeros((dv, hg * tq), F32)
    for j in range(nk // tq):
        o = o + jnp.dot(vt_ref[c0 + j], pb[j * tq:(j + 1) * tq, :], preferred_element_type=F32)
    o = o * (1.0 / l)
    first_group = pl.program_id(1) == 0
    n_heads = 2 * hg

    def gate(br, hh):
        lo = gate_ref[:, br * n_heads + hh:br * n_heads + hh + 1]
        hi = gate_ref[:, br * n_heads + hg + hh:br * n_heads + hg + hh + 1]
        return jnp.where(first_group, lo, hi)

    for hh in range(hg):
        sl = slice(hh * dv, (hh + 1) * dv)
        mixed = (gate(0, hh) * oc_ref[:, sl].astype(F32) + gate(1, hh) * os_ref[:, sl].astype(F32)
                 + gate(2, hh) * o[:, hh * tq:(hh + 1) * tq].T)
        o_ref[:, sl] = mixed.astype(o_ref.dtype)


def window_attention(q, k, vt, oc, os_, gates, B, S, *, n_groups, hg, dqk, dv, kcol0, tq):
    assert n_groups == NSA_KV_GROUPS
    nq = S // tq
    assert WINDOW % tq == 0 and S >= WINDOW + tq and tq % LANES == 0
    return pl.pallas_call(
        functools.partial(_window_kernel, hg=hg, tq=tq, dqk=dqk, dv=dv),
        grid=(B, n_groups, nq),
        in_specs=[pl.BlockSpec((tq, hg * dqk), lambda b, g, i: (b * nq + i, g)),
                  pl.BlockSpec((S, dqk), lambda b, g, i: (b, kcol0 + g)),
                  pl.BlockSpec((None, None, nq, dv, tq), lambda b, g, i: (b, g, 0, 0, 0)),
                  pl.BlockSpec((tq, hg * dv), lambda b, g, i: (b * nq + i, g)),
                  pl.BlockSpec((tq, hg * dv), lambda b, g, i: (b * nq + i, g)),
                  pl.BlockSpec((tq, LANES), lambda b, g, i: (b * nq + i, 0))],
        out_specs=pl.BlockSpec((tq, hg * dv), lambda b, g, i: (b * nq + i, g)),
        out_shape=jax.ShapeDtypeStruct((B * S, n_groups * hg * dv), BF16),
        scratch_shapes=[pltpu.VMEM((hg * tq, dqk), BF16)],
        compiler_params=_cparams(("parallel", "parallel", "arbitrary")),
        name="window_attention",
    )(q, k, vt, oc, os_, gates)


def _mla_q_kernel(z_ref, ga_ref, w_ref, gh_ref, ct_ref, s1_ref, s2_ref, o_ref):
    h = _rms(z_ref[...], ga_ref[...]).astype(BF16)
    y = jnp.dot(h, w_ref[...], preferred_element_type=F32)
    ct, s1, s2 = ct_ref[...], s1_ref[...], s2_ref[...]
    scale = LOG2E * (QK_NOPE + QK_ROPE) ** -0.5
    for hd in range(MLA_HEADS):
        c = hd * MLA_QK_PAD
        nope = _rms(y[:, c:c + QK_NOPE], gh_ref[0:1, :])
        r = y[:, c + QK_NOPE:c + MLA_QK_PAD]
        ms = jnp.sum(r * r, axis=-1, keepdims=True) * (1.0 / QK_ROPE)
        r = _rope_lanes(r * lax.rsqrt(ms + EPS) * gh_ref[1:2, :], ct, s1, s2, QK_ROPE // 2)
        o_ref[:, c:c + QK_NOPE] = (nope * scale).astype(BF16)
        o_ref[:, c + QK_NOPE:c + MLA_QK_PAD] = (r * scale).astype(BF16)


def mla_q_proj(z, ga, wq, gh, tabs, S, *, tm):
    T = z.shape[0]
    ns = S // tm
    tab_spec = pl.BlockSpec((tm, LANES), lambda i: (i % ns, 0))
    nout = MLA_HEADS * MLA_QK_PAD
    return pl.pallas_call(
        _mla_q_kernel,
        grid=(T // tm,),
        in_specs=[pl.BlockSpec((tm, Q_LORA), lambda i: (i, OFF_QA // Q_LORA)),
                  pl.BlockSpec((1, Q_LORA), lambda i: (0, 0)),
                  pl.BlockSpec((Q_LORA, nout), lambda i: (0, 0)),
                  pl.BlockSpec((8, LANES), lambda i: (0, 0)),
                  tab_spec, tab_spec, tab_spec],
        out_specs=pl.BlockSpec((tm, nout), lambda i: (i, 0)),
        out_shape=jax.ShapeDtypeStruct((T, nout), BF16),
        compiler_params=_cparams(("parallel",)),
        name="mla_q_proj",
    )(z, ga, wq, gh, *tabs)


def _mla_kv_kernel(z_ref, zr_ref, ga_ref, w_ref, gh_ref, ct_ref, s1_ref, s2_ref, k_ref, v_ref):
    h = _rms(z_ref[...], ga_ref[...]).astype(BF16)
    y = jnp.dot(h, w_ref[...], preferred_element_type=F32)
    r = zr_ref[...]
    ms = jnp.sum(r * r, axis=-1, keepdims=True) * (1.0 / QK_ROPE)
    r = _rope_lanes(r * lax.rsqrt(ms + EPS) * gh_ref[1:2, :], ct_ref[...], s1_ref[...], s2_ref[...],
                    QK_ROPE // 2).astype(BF16)
    nv = MLA_HEADS * QK_NOPE
    for hd in range(MLA_HEADS):
        c = hd * MLA_QK_PAD
        k_ref[:, c:c + QK_NOPE] = _rms(y[:, hd * QK_NOPE:(hd + 1) * QK_NOPE], gh_ref[0:1, :]).astype(BF16)
        k_ref[:, c + QK_NOPE:c + MLA_QK_PAD] = r
    for hd in range(MLA_HEADS):
        v_ref[hd] = y[:, nv + hd * V_HEAD:nv + (hd + 1) * V_HEAD].T.astype(BF16)


def mla_kv_proj(z, ga, wkv, gh, tabs, S, *, tm=VT_CHUNK):
    T = z.shape[0]
    ns = S // tm
    tab_spec = pl.BlockSpec((tm, LANES), lambda i: (i % ns, 0))
    nk = MLA_HEADS * MLA_QK_PAD
    nv = MLA_HEADS * V_HEAD
    return pl.pallas_call(
        _mla_kv_kernel,
        grid=(T // tm,),
        in_specs=[pl.BlockSpec((tm, KV_LORA), lambda i: (i, OFF_KVA // KV_LORA)),
                  pl.BlockSpec((tm, LANES), lambda i: (i, OFF_KR // LANES)),
                  pl.BlockSpec((1, KV_LORA), lambda i: (0, 0)),
                  pl.BlockSpec((KV_LORA, MLA_HEADS * (QK_NOPE + V_HEAD)), lambda i: (0, 0)),
                  pl.BlockSpec((8, LANES), lambda i: (0, 0)),
                  tab_spec, tab_spec, tab_spec],
        out_specs=[pl.BlockSpec((tm, nk), lambda i: (i, 0)),
                   pl.BlockSpec((None, MLA_HEADS, None, V_HEAD, tm), lambda i: (i // ns, 0, i % ns, 0, 0))],
        out_shape=[jax.ShapeDtypeStruct((T, nk), BF16),
                   jax.ShapeDtypeStruct((T // S, MLA_HEADS, ns, V_HEAD, tm), BF16)],
        compiler_params=_cparams(("parallel",)),
        name="mla_kv_proj",
    )(z, z, ga, wkv, gh, *tabs)


def _mix_kernel(a_ref, ob_ref, wa_ref, wb_ref, za_ref, zb_ref, o_ref):
    pa = jnp.dot(a_ref[...], wa_ref[...].astype(BF16), preferred_element_type=F32)
    pb = jnp.dot(ob_ref[...], wb_ref[...].astype(BF16), preferred_element_type=F32)
    o_ref[...] = (_sigmoid(za_ref[...]) * pa + _sigmoid(zb_ref[...]) * pb).astype(o_ref.dtype)


def gated_mix(oa, ob, wa, wb, z, *, tm, tn):
    T, K = oa.shape
    N = wa.shape[1]
    row = lambda w: pl.BlockSpec((tm, w), lambda m, n: (m, 0))
    return pl.pallas_call(
        _mix_kernel,
        grid=(T // tm, N // tn),
        in_specs=[row(K), row(K),
                  pl.BlockSpec((K, tn), lambda m, n: (0, n)),
                  pl.BlockSpec((K, tn), lambda m, n: (0, n)),
                  pl.BlockSpec((tm, tn), lambda m, n: (m, OFF_M // tn + n)),
                  pl.BlockSpec((tm, tn), lambda m, n: (m, (OFF_M + D_MODEL) // tn + n))],
        out_specs=pl.BlockSpec((tm, tn), lambda m, n: (m, n)),
        out_shape=jax.ShapeDtypeStruct((T, N), BF16),
        compiler_params=_cparams(("parallel", "parallel")),
        name="gated_mix",
    )(oa, ob, wa, wb, z, z)


def _ffn_up_kernel(x_ref, g_ref, w1_ref, w3_ref, o_ref, h_scr):
    @pl.when(pl.program_id(1) == 0)
    def _():
        h_scr[...] = _rms(x_ref[...], g_ref[...]).astype(BF16)

    h = h_scr[...]
    a = jnp.dot(h, w1_ref[...].astype(BF16), preferred_element_type=F32)
    b = jnp.dot(h, w3_ref[...].astype(BF16), preferred_element_type=F32)
    o_ref[...] = (a * _sigmoid(a) * b).astype(o_ref.dtype)


def ffn_up(x, g, w1, w3, *, tm, tn):
    T, K = x.shape
    N = w1.shape[1]
    return pl.pallas_call(
        _ffn_up_kernel,
        grid=(T // tm, N // tn),
        in_specs=[pl.BlockSpec((tm, K), lambda m, n: (m, 0)),
                  pl.BlockSpec((1, K), lambda m, n: (0, 0)),
                  pl.BlockSpec((K, tn), lambda m, n: (0, n)),
                  pl.BlockSpec((K, tn), lambda m, n: (0, n))],
        out_specs=pl.BlockSpec((tm, tn), lambda m, n: (m, n)),
        out_shape=jax.ShapeDtypeStruct((T, N), BF16),
        scratch_shapes=[pltpu.VMEM((tm, K), BF16)],
        compiler_params=_cparams(("parallel", "arbitrary")),
        name="ffn_up",
    )(x, g.reshape(1, K), w1, w3)


def _pack_bf16_pairs(h):
    k = h.shape[1] // 2
    hi = lax.bitcast_convert_type(h[:, :k].astype(jnp.bfloat16).astype(F32), jnp.uint32)
    lo = lax.bitcast_convert_type(h[:, k:].astype(jnp.bfloat16).astype(F32), jnp.uint32)
    return lax.bitcast_convert_type(hi | (lo >> 16), jnp.int32)


def _unpack_bf16_pairs(xp):
    xp = lax.bitcast_convert_type(xp, jnp.uint32)
    hi = lax.bitcast_convert_type(xp & jnp.uint32(0xFFFF0000), F32)
    lo = lax.bitcast_convert_type(xp << 16, F32)
    return hi.astype(BF16), lo.astype(BF16)


def _router_kernel(x_ref, g_ref, wr_ref, h_ref, idx_ref, gate_ref):
    h = _rms(x_ref[...], g_ref[...])
    is_pad = pl.program_id(0) == pl.num_programs(0) - 1
    h_ref[...] = jnp.where(is_pad, 0, _pack_bf16_pairs(h))
    logits = jnp.dot(h, wr_ref[...], preferred_element_type=F32, precision=lax.Precision.HIGHEST)
    lane = lax.broadcasted_iota(jnp.int32, logits.shape, 1).astype(F32)
    logits = jnp.where(lane < float(N_EXPERTS), logits, NEG_INF)
    m1 = jnp.max(logits, axis=-1, keepdims=True)
    i1 = jnp.min(jnp.where(logits == m1, lane, float(LANES)), axis=-1, keepdims=True)
    rest = jnp.where(lane == i1, NEG_INF, logits)
    m2 = jnp.max(rest, axis=-1, keepdims=True)
    i2 = jnp.min(jnp.where(rest == m2, lane, float(LANES)), axis=-1, keepdims=True)
    e = jnp.exp(m2 - m1)
    den = 1.0 + e
    idx_ref[...] = jnp.where(lane == 0.0, i1, jnp.where(lane == 1.0, i2, 0.0)).astype(jnp.int32)
    gate_ref[...] = jnp.where(lane == 0.0, 1.0 / den, jnp.where(lane == 1.0, e / den, 0.0))


def router(x, g, wr_pad, *, tm):
    T, K = x.shape
    nt = T // tm
    return pl.pallas_call(
        _router_kernel,
        grid=(nt + 1,),
        in_specs=[pl.BlockSpec((tm, K), lambda i: (jnp.minimum(i, nt - 1), 0)),
                  pl.BlockSpec((1, K), lambda i: (0, 0)),
                  pl.BlockSpec((K, LANES), lambda i: (0, 0))],
        out_specs=[pl.BlockSpec((tm, K // 2), lambda i: (i, 0)),
                   pl.BlockSpec((tm, LANES), lambda i: (jnp.minimum(i, nt - 1), 0)),
                   pl.BlockSpec((tm, LANES), lambda i: (jnp.minimum(i, nt - 1), 0))],
        out_shape=[jax.ShapeDtypeStruct((T + tm, K // 2), jnp.int32),
                   jax.ShapeDtypeStruct((T, LANES), jnp.int32),
                   jax.ShapeDtypeStruct((T, LANES), F32)],
        compiler_params=_cparams(("arbitrary",)),
        name="router",
    )(x, g.reshape(1, K), wr_pad)


def _moe_weight_stream(w_hbms, stages, casts, sems, be_ref, nu_ref, gs_ref, ne_ref, lg_ref, tn):
    n = pl.program_id(0)
    r = pl.program_id(1)
    used = r < nu_ref[0]

    def copies(e, nt):
        c0 = pl.multiple_of(nt * tn, tn)
        return [pltpu.make_async_copy(w.at[e, :, pl.ds(c0, tn)], st, sems.at[i])
                for i, (w, st) in enumerate(zip(w_hbms, stages))]

    @pl.when((n == 0) & (r == 0))
    def _():
        for c in copies(be_ref[0], 0):
            c.start()

    @pl.when(used & (gs_ref[r] == 1))
    def _():
        for c in copies(be_ref[r], n):
            c.wait()
        for st, wb in zip(stages, casts):
            wb[...] = st[...].astype(BF16)
        last = lg_ref[r] == 1

        @pl.when(jnp.logical_not(last & (n == pl.num_programs(0) - 1)))
        def _():
            for c in copies(ne_ref[r], n + last.astype(jnp.int32)):
                c.start()

    return used


def _moe_up_kernel(be_ref, nu_ref, gs_ref, ne_ref, lg_ref, x_ref, w1_hbm, w3_hbm, o_ref,
                   st1, st3, w1_s, w3_s, sems, *, tn):
    used = _moe_weight_stream((w1_hbm, w3_hbm), (st1, st3), (w1_s, w3_s), sems,
                              be_ref, nu_ref, gs_ref, ne_ref, lg_ref, tn)

    @pl.when(used)
    def _():
        xa, xb = _unpack_bf16_pairs(x_ref[...])
        k2 = xa.shape[1]
        a = (jnp.dot(xa, w1_s[:k2, :], preferred_element_type=F32)
             + jnp.dot(xb, w1_s[k2:, :], preferred_element_type=F32))
        b = (jnp.dot(xa, w3_s[:k2, :], preferred_element_type=F32)
             + jnp.dot(xb, w3_s[k2:, :], preferred_element_type=F32))
        o_ref[...] = (a * _sigmoid(a) * b).astype(o_ref.dtype)

    @pl.when(jnp.logical_not(used))
    def _():
        o_ref[...] = jnp.zeros(o_ref.shape, o_ref.dtype)


def moe_up(tables, xb, w1, w3, *, tmb, tn):
    n_slot = xb.shape[0]
    K, N = w1.shape[1], w1.shape[2]
    row = lambda r, nu: jnp.minimum(r, nu[0] - 1)
    return pl.pallas_call(
        functools.partial(_moe_up_kernel, tn=tn),
        grid_spec=pltpu.PrefetchScalarGridSpec(
            num_scalar_prefetch=5,
            grid=(N // tn, n_slot // tmb),
            in_specs=[pl.BlockSpec((tmb, K // 2), lambda n, r, be, nu, gs, ne, lg: (row(r, nu), 0)),
                      pl.BlockSpec(memory_space=pl.ANY), pl.BlockSpec(memory_space=pl.ANY)],
            out_specs=pl.BlockSpec((tmb, tn), lambda n, r, be, nu, gs, ne, lg: (r, n)),
            scratch_shapes=[pltpu.VMEM((K, tn), F32), pltpu.VMEM((K, tn), F32),
                            pltpu.VMEM((K, tn), BF16), pltpu.VMEM((K, tn), BF16),
                            pltpu.SemaphoreType.DMA((2,))]),
        out_shape=jax.ShapeDtypeStruct((n_slot, N), BF16),
        compiler_params=_cparams(("arbitrary", "arbitrary")),
        name="moe_up",
    )(*tables, xb, w1, w3)


def _moe_down_kernel(be_ref, nu_ref, gs_ref, ne_ref, lg_ref, a_ref, w2_hbm, o_ref, st2, w2_s, sems, *, tn):
    used = _moe_weight_stream((w2_hbm,), (st2,), (w2_s,), sems, be_ref, nu_ref, gs_ref, ne_ref, lg_ref, tn)

    @pl.when(used)
    def _():
        o_ref[...] = _pack_bf16_pairs(jnp.dot(a_ref[...], w2_s[...], preferred_element_type=F32))

    @pl.when(jnp.logical_not(used))
    def _():
        o_ref[...] = jnp.zeros(o_ref.shape, o_ref.dtype)


def moe_down(tables, act, w2, *, tmb, tn):
    n_slot, K = act.shape
    N = w2.shape[2]
    row = lambda r, nu: jnp.minimum(r, nu[0] - 1)
    return pl.pallas_call(
        functools.partial(_moe_down_kernel, tn=tn),
        grid_spec=pltpu.PrefetchScalarGridSpec(
            num_scalar_prefetch=5,
            grid=(N // tn, n_slot // tmb),
            in_specs=[pl.BlockSpec((tmb, K), lambda n, r, be, nu, gs, ne, lg: (row(r, nu), 0)),
                      pl.BlockSpec(memory_space=pl.ANY)],
            out_specs=pl.BlockSpec((tmb, tn // 2), lambda n, r, be, nu, gs, ne, lg: (r, n)),
            scratch_shapes=[pltpu.VMEM((K, tn), F32), pltpu.VMEM((K, tn), BF16),
                            pltpu.SemaphoreType.DMA((1,))]),
        out_shape=jax.ShapeDtypeStruct((n_slot, N // 2), jnp.int32),
        compiler_params=_cparams(("arbitrary", "arbitrary")),
        name="moe_down",
    )(*tables, act, w2)


def _rope_tabs(pos, rot_dim, n_rows):
    half = rot_dim // 2
    inv = 1.0 / (ROPE_THETA ** (jnp.arange(0, rot_dim, 2, dtype=F32) / rot_dim))
    ang = jnp.asarray(pos).astype(F32)[:, None] * inv[None, :]
    c, s = jnp.cos(ang), jnp.sin(ang)
    z = jnp.zeros_like(c)
    pad = lambda a, fill: jnp.pad(a, ((0, n_rows - a.shape[0]), (0, LANES - a.shape[1])), constant_values=fill)
    return pad(jnp.concatenate([c, c], 1), 1.0), pad(jnp.concatenate([-s, z], 1), 0.0), \
        pad(jnp.concatenate([z, s], 1), 0.0)


def _mla_tabs(S):
    ct, s1, s2 = _rope_tabs(jnp.arange(S), QK_ROPE, S)
    lane = jnp.arange(LANES)[None, :]
    return jnp.where(lane < QK_ROPE, ct, 0.0), s1, s2


def _pad_cols(w, n):
    return jnp.pad(w, ((0, 0), (0, n - w.shape[1])))


def _layout_w_in(w):
    sp = np.cumsum([0, NSA_HEADS * HEAD_DIM, NSA_KV_COLS, 3 * NSA_HEADS, Q_LORA, KV_LORA, QK_ROPE, 2 * D_MODEL])
    q, kv, g, qa, kva, kr, m = [w[:, sp[i]:sp[i + 1]] for i in range(7)]
    return jnp.concatenate([qa, _pad_cols(g, LANES), _pad_cols(kr, LANES), q, kva, kv, m], axis=1).astype(BF16)


def _layout_w_q_b(w):
    w = w.reshape(Q_LORA, MLA_HEADS, QK_NOPE + QK_ROPE)
    w = jnp.pad(w, ((0, 0), (0, 0), (0, MLA_QK_PAD - QK_NOPE - QK_ROPE)))
    return w.reshape(Q_LORA, MLA_HEADS * MLA_QK_PAD).astype(BF16)


def _layout_w_kv_b(w):
    w = w.reshape(KV_LORA, MLA_HEADS, QK_NOPE + V_HEAD)
    return jnp.concatenate([w[:, :, :QK_NOPE].reshape(KV_LORA, -1), w[:, :, QK_NOPE:].reshape(KV_LORA, -1)],
                           axis=1).astype(BF16)


def _layout_w_cmp(w):
    h = CMP_LEN // 2
    return jnp.concatenate([w[:h].reshape(h * HEAD_DIM, HEAD_DIM), w[h:].reshape(h * HEAD_DIM, HEAD_DIM)],
                           axis=1).astype(BF16)


def _layout_pe(pe):
    return jnp.pad(pe.reshape(2, (CMP_LEN // 2) * HEAD_DIM), ((0, 6), (0, 0))).astype(BF16)


def _overlap(nc, ns):
    n = np.arange(nc)[None, :] * CMP_STRIDE
    j = np.arange(ns)[:, None] * SLC_LEN
    ov = (n <= j + SLC_LEN - 1) & (j <= n + CMP_LEN - 1) & (np.arange(nc)[None, :] < nc - 1)
    return jnp.asarray(ov.astype(np.float32), BF16)


def _expand(ns, S, tk):
    e = ((np.arange(S // tk)[:, None, None] * tk + np.arange(tk)[None, :, None]) // SLC_LEN
         == np.arange(ns)[None, None, :])
    return jnp.asarray(e.astype(np.float32), BF16)


def _layout_layers(p):
    depth = p['w_in'].shape[0]
    lane_row = lambda g: jnp.pad(g, ((0, 0), (0, LANES - g.shape[-1])))[:, None, :]
    z4 = jnp.zeros((depth, 4, HEAD_DIM), F32)
    z6 = jnp.zeros((depth, 6, LANES), F32)
    return dict(
        gn=jnp.concatenate([p['nsa_q_norm_g'][:, None, :], p['nsa_k_norm_g'], z4], axis=1),
        w_ck=jax.vmap(_layout_w_cmp)(p['w_cmp_k']), w_cv=jax.vmap(_layout_w_cmp)(p['w_cmp_v']),
        pe_k=jax.vmap(_layout_pe)(p['cmp_pe_k']), pe_v=jax.vmap(_layout_pe)(p['cmp_pe_v']),
        gh_q=jnp.concatenate([p['mla_q_norm_g'][:, None, :], lane_row(p['mla_qr_norm_g']), z6], axis=1),
        gh_k=jnp.concatenate([p['mla_k_norm_g'][:, None, :], lane_row(p['mla_kr_norm_g']), z6], axis=1),
        w_qb=jax.vmap(_layout_w_q_b)(p['w_q_b']), w_kvb=jax.vmap(_layout_w_kv_b)(p['w_kv_b']))


def _attention_block(x2d, B, S, p, pw, l, tabs):
    T = x2d.shape[0]
    z = norm_matmul(x2d, p['attn_norm_g'][l], _layout_w_in(p['w_in'][l]), tm=min(1024, T), tn=512)
    qn, kvp, xc, vts, vtw, gates = nsa_prep(z, tabs['tok'], pw['gn'][l], S)

    nc = S // CMP_STRIDE
    x2 = xc.reshape(B, 2 * NSA_KV_GROUPS, nc, CMP_STRIDE * HEAD_DIM)
    kct, vc = nsa_compress(x2, pw['w_ck'][l], pw['w_cv'][l], pw['pe_k'][l], pw['pe_v'][l],
                           p['nsa_k_norm_g'][l][0:1], tabs['cmp'])
    ns = S // SLC_LEN
    oc, sel_t = nsa_cmp_attention(qn, kct, vc, _overlap(nc, ns), B, S, tq=min(256, S))
    nsa = dict(n_groups=NSA_KV_GROUPS, hg=NSA_HPG, dqk=HEAD_DIM, dv=HEAD_DIM)
    tks = min(512, S)
    os_ = flash_attention(qn, kvp, vts, B, S, mode="sel", kcol0=0, tq=256, tk=tks,
                          sel_t=sel_t, expand=_expand(ns, S, tks), **nsa)
    oa = window_attention(qn, kvp, vtw, oc, os_, gates, B, S, kcol0=NSA_KV_GROUPS, tq=VT_CHUNK, **nsa)

    qm = mla_q_proj(z, p['mla_qa_norm_g'][l][None], pw['w_qb'][l], pw['gh_q'][l], tabs['mla'], S, tm=256)
    km, vmt = mla_kv_proj(z, p['mla_kva_norm_g'][l][None], pw['w_kvb'][l], pw['gh_k'][l], tabs['mla'], S)
    tkm = min(512, S)
    ob = flash_attention(qm, km, vmt, B, S, mode="causal",
                         n_groups=MLA_HEADS, hg=1, dqk=MLA_QK_PAD, dv=V_HEAD, kcol0=0, tq=min(1024, S), tk=tkm)

    tm = min(2048, T)
    mix = gated_mix(oa, ob, p['w_proj_nsa'][l], p['w_proj_mla'][l], z, tm=tm, tn=512)
    return matmul_residual(mix, p['w_out'][l], x2d, tm=min(2048, T), tn=512)


def _dense_ffn(x2d, g, w1, w3, w2):
    T = x2d.shape[0]
    act = ffn_up(x2d, g, w1.astype(BF16), w3.astype(BF16), tm=min(1024, T), tn=512)
    return matmul_residual(act, w2.astype(BF16), x2d, tm=min(1024, T), tn=256)


def _cumsum_rows(oh, blk=128):
    A, E = oh.shape
    nb = A // blk
    x = oh.astype(F32).reshape(nb, blk, E)
    within = jnp.einsum('ij,bje->bie', jnp.tril(jnp.ones((blk, blk), F32)), x)
    before = jnp.tril(jnp.ones((nb, nb), F32), -1) @ within[:, -1, :]
    return (within + before[:, None, :]).astype(jnp.int32).reshape(A, E)


def _moe_ffn(x2d, g, w_router, w1, w3, w2, *, tmb=512):
    T = x2d.shape[0]
    hp, idx, gate = router(x2d, g, _pad_cols(w_router, LANES), tm=256)
    A = T * TOP_K
    e_flat = idx[:, :TOP_K].reshape(A)
    tok_flat = jnp.repeat(jnp.arange(T, dtype=jnp.int32), TOP_K)
    oh = (e_flat[:, None] == jnp.arange(N_EXPERTS)[None, :]).astype(jnp.int32)
    csum = _cumsum_rows(oh)
    rank = jnp.sum(oh * csum, axis=1) - 1
    counts = csum[-1]
    padded = (counts + tmb - 1) // tmb * tmb
    pad_end = jnp.cumsum(padded)
    dest = (pad_end - padded)[e_flat] + rank
    n_blk = -(-A // tmb) + N_EXPERTS
    n_slot = n_blk * tmb
    slot_tok = jnp.full((n_slot,), T, jnp.int32).at[dest].set(tok_flat)
    eidx = jnp.arange(N_EXPERTS, dtype=jnp.int32)
    blk = jnp.arange(n_blk, dtype=jnp.int32)
    blk_exp = jnp.minimum(jnp.sum((pad_end[None, :] <= (blk * tmb)[:, None]).astype(jnp.int32), axis=1),
                          N_EXPERTS - 1).astype(jnp.int32)
    n_used = (pad_end[-1:] // tmb).astype(jnp.int32)
    present = counts > 0
    first_e = jnp.min(jnp.where(present, eidx, N_EXPERTS))
    last_e = jnp.max(jnp.where(present, eidx, -1))
    later = jnp.where(present[None, :] & (eidx[None, :] > eidx[:, None]), eidx[None, :], N_EXPERTS)
    next_e = jnp.min(later, axis=1)
    next_e = jnp.where(next_e == N_EXPERTS, first_e, next_e).astype(jnp.int32)
    starts = ((blk == 0) | (blk_exp != jnp.roll(blk_exp, 1))) & (blk < n_used[0])
    tables = (blk_exp, n_used, starts.astype(jnp.int32), next_e[blk_exp],
              (blk_exp == last_e).astype(jnp.int32))
    xb = hp[slot_tok]
    act = moe_up(tables, xb, w1, w3, tmb=tmb, tn=1024)
    tnd = 512
    yb = moe_down(tables, act, w2, tmb=tmb, tn=tnd)
    d2 = dest.reshape(T, TOP_K)
    g2 = gate[:, :TOP_K]
    return (x2d + g2[:, 0:1] * _unpack_rows(yb[d2[:, 0]], tnd) + g2[:, 1:2] * _unpack_rows(yb[d2[:, 1]], tnd))


def _unpack_rows(xp, tn):
    rows = xp.shape[0]
    u = lax.bitcast_convert_type(xp, jnp.uint32).reshape(rows, -1, tn // 2)
    hi = lax.bitcast_convert_type(u & jnp.uint32(0xFFFF0000), F32)
    lo = lax.bitcast_convert_type(u << 16, F32)
    return jnp.stack([hi, lo], axis=2).reshape(rows, -1)


def kernel(x, attn_norm_g, w_in, nsa_q_norm_g, nsa_k_norm_g, cmp_pe_k, cmp_pe_v, w_cmp_k, w_cmp_v, mla_qa_norm_g, w_q_b, mla_kva_norm_g, w_kv_b, mla_q_norm_g, mla_qr_norm_g, mla_k_norm_g, mla_kr_norm_g, w_proj_nsa, w_proj_mla, w_out, ffn_norm_g, w_ff1, w_ff3, w_ff2, w_router, w_e1, w_e3, w_e2):
    p = dict(attn_norm_g=attn_norm_g, w_in=w_in, nsa_q_norm_g=nsa_q_norm_g, nsa_k_norm_g=nsa_k_norm_g,
             cmp_pe_k=cmp_pe_k, cmp_pe_v=cmp_pe_v, w_cmp_k=w_cmp_k, w_cmp_v=w_cmp_v,
             mla_qa_norm_g=mla_qa_norm_g, w_q_b=w_q_b, mla_kva_norm_g=mla_kva_norm_g, w_kv_b=w_kv_b,
             mla_q_norm_g=mla_q_norm_g, mla_qr_norm_g=mla_qr_norm_g, mla_k_norm_g=mla_k_norm_g,
             mla_kr_norm_g=mla_kr_norm_g, w_proj_nsa=w_proj_nsa, w_proj_mla=w_proj_mla, w_out=w_out)
    B, S, D = x.shape
    depth = w_in.shape[0]
    nc = S // CMP_STRIDE
    tabs = dict(tok=_rope_tabs(jnp.arange(S), NSA_ROT, S),
                cmp=_rope_tabs(jnp.arange(nc - 1) * CMP_STRIDE + CMP_LEN - 1, NSA_ROT, nc),
                mla=_mla_tabs(S))
    pw = _layout_layers(p)
    x2d = x.reshape(B * S, D)
    for l in range(depth):
        x2d = _attention_block(x2d, B, S, p, pw, l, tabs)
        if l % 2 == 0:
            x2d = _dense_ffn(x2d, ffn_norm_g[l], w_ff1[l // 2], w_ff3[l // 2], w_ff2[l // 2])
        else:
            x2d = _moe_ffn(x2d, ffn_norm_g[l], w_router[l // 2], w_e1[l // 2], w_e3[l // 2], w_e2[l // 2])
    return x2d.reshape(B, S, D)
```

```python
import functools

import numpy as np
import jax
import jax.numpy as jnp
from jax import lax
from jax.experimental import pallas as pl
from jax.experimental.pallas import tpu as pltpu

D_MODEL = 2048
HEAD_DIM = 128
NSA_HEADS = 8
NSA_KV_GROUPS = 2
NSA_HPG = NSA_HEADS // NSA_KV_GROUPS
NSA_ROT = HEAD_DIM // 4
CMP_LEN = 32
CMP_STRIDE = 16
SLC_LEN = 64
SLC_TOPK = 16
N_LOCAL_SLC = 2
WINDOW = 512
FORCE_SCORE = 1.0e4
MLA_HEADS = 8
Q_LORA = 768
KV_LORA = 512
QK_NOPE = 128
QK_ROPE = 64
V_HEAD = 128
ROPE_THETA = 500000.0
EPS = 1e-6
D_FF = 7168
N_EXPERTS = 8
TOP_K = 2

LANES = 128
MLA_QK_PAD = 256
NEG_INF = float("-inf")
LOG2E = 1.4426950408889634
BF16 = jnp.bfloat16
F32 = jnp.float32

OFF_QA = 0
OFF_G = 768
OFF_KR = 896
OFF_Q = 1024
OFF_KVA = 2048
OFF_KV = 2560
OFF_M = 4096
D_INP = 8192
NSA_KV_COLS = 3 * 2 * NSA_KV_GROUPS * HEAD_DIM
NSA_KVP_COLS = 2 * NSA_KV_GROUPS * HEAD_DIM
VT_CHUNK = 256

VMEM_LIMIT = 56 * 1024 * 1024


def _cparams(sem):
    return pltpu.CompilerParams(dimension_semantics=sem, vmem_limit_bytes=VMEM_LIMIT)


def _rms(x, g):
    ms = jnp.mean(x * x, axis=-1, keepdims=True)
    return x * lax.rsqrt(ms + EPS) * g


def _rope_lanes(y, ct, s1, s2, half):
    return y * ct + pltpu.roll(y, LANES - half, 1) * s1 + pltpu.roll(y, half, 1) * s2


def _sigmoid(x):
    return 1.0 / (1.0 + jnp.exp(-x))


def _nmm_kernel(x_ref, g_ref, w_ref, o_ref, h_scr):
    @pl.when(pl.program_id(1) == 0)
    def _():
        h_scr[...] = _rms(x_ref[...], g_ref[...]).astype(BF16)

    o_ref[...] = jnp.dot(h_scr[...], w_ref[...].astype(BF16),
                         preferred_element_type=F32).astype(o_ref.dtype)


def norm_matmul(x, g, w, *, tm, tn, out_dtype=F32):
    T, K = x.shape
    N = w.shape[1]
    return pl.pallas_call(
        _nmm_kernel,
        grid=(T // tm, N // tn),
        in_specs=[pl.BlockSpec((tm, K), lambda m, n: (m, 0)),
                  pl.BlockSpec((1, K), lambda m, n: (0, 0)),
                  pl.BlockSpec((K, tn), lambda m, n: (0, n))],
        out_specs=pl.BlockSpec((tm, tn), lambda m, n: (m, n)),
        out_shape=jax.ShapeDtypeStruct((T, N), out_dtype),
        scratch_shapes=[pltpu.VMEM((tm, K), BF16)],
        compiler_params=_cparams(("parallel", "arbitrary")),
        name="norm_matmul",
    )(x, g.reshape(1, K), w)


def _mmres_kernel(a_ref, w_ref, r_ref, o_ref):
    o_ref[...] = r_ref[...] + jnp.dot(a_ref[...], w_ref[...].astype(BF16), preferred_element_type=F32)


def matmul_residual(a, w, res, *, tm, tn):
    T, K = a.shape
    N = w.shape[1]
    return pl.pallas_call(
        _mmres_kernel,
        grid=(T // tm, N // tn),
        in_specs=[pl.BlockSpec((tm, K), lambda m, n: (m, 0)),
                  pl.BlockSpec((K, tn), lambda m, n: (0, n)),
                  pl.BlockSpec((tm, tn), lambda m, n: (m, n))],
        out_specs=pl.BlockSpec((tm, tn), lambda m, n: (m, n)),
        out_shape=jax.ShapeDtypeStruct((T, N), F32),
        compiler_params=_cparams(("parallel", "arbitrary")),
        name="matmul_residual",
    )(a, w, res)


def _prep_kernel(zg_ref, zq_ref, zc_ref, zs_ref, zw_ref, ct_ref, s1_ref, s2_ref, gn_ref,
                 qn_ref, kvp_ref, xc_ref, vts_ref, vtw_ref, gate_ref):
    ct, s1, s2 = ct_ref[...], s1_ref[...], s2_ref[...]
    half = NSA_ROT // 2
    gq = gn_ref[0:1, :]
    for h in range(NSA_HEADS):
        y = _rope_lanes(_rms(zq_ref[:, h * HEAD_DIM:(h + 1) * HEAD_DIM], gq), ct, s1, s2, half)
        qn_ref[:, h * HEAD_DIM:(h + 1) * HEAD_DIM] = (y * (LOG2E * HEAD_DIM ** -0.5)).astype(BF16)
    for br, zb_ref in enumerate((zc_ref, zs_ref, zw_ref)):
        for kv in range(2):
            for g in range(NSA_KV_GROUPS):
                c = (kv * NSA_KV_GROUPS + g) * HEAD_DIM
                y = zb_ref[:, c:c + HEAD_DIM]
                if kv == 0 and br > 0:
                    y = _rope_lanes(_rms(y, gn_ref[1 + br:2 + br, :]), ct, s1, s2, half)
                if br == 0:
                    xc_ref[kv * NSA_KV_GROUPS + g] = y.astype(BF16)
                elif kv == 0:
                    d = (br - 1) * NSA_KV_GROUPS * HEAD_DIM + g * HEAD_DIM
                    kvp_ref[:, d:d + HEAD_DIM] = y.astype(BF16)
                else:
                    (vts_ref if br == 1 else vtw_ref)[g] = y.T.astype(BF16)
    gate_ref[...] = _sigmoid(zg_ref[...])


def nsa_prep(z, tabs, gn, S, *, tm=VT_CHUNK):
    T = z.shape[0]
    ns = S // tm
    tab_spec = pl.BlockSpec((tm, LANES), lambda i: (i % ns, 0))
    vt_spec = pl.BlockSpec((None, NSA_KV_GROUPS, None, HEAD_DIM, tm), lambda i: (i // ns, 0, i % ns, 0, 0))
    vt_shape = jax.ShapeDtypeStruct((T // S, NSA_KV_GROUPS, ns, HEAD_DIM, tm), BF16)
    wq, wb = NSA_HEADS * HEAD_DIM, 2 * NSA_KV_GROUPS * HEAD_DIM
    npl = 2 * NSA_KV_GROUPS
    return pl.pallas_call(
        _prep_kernel,
        grid=(T // tm,),
        in_specs=[pl.BlockSpec((tm, LANES), lambda i: (i, OFF_G // LANES)),
                  pl.BlockSpec((tm, wq), lambda i: (i, OFF_Q // wq)),
                  pl.BlockSpec((tm, wb), lambda i: (i, OFF_KV // wb)),
                  pl.BlockSpec((tm, wb), lambda i: (i, OFF_KV // wb + 1)),
                  pl.BlockSpec((tm, wb), lambda i: (i, OFF_KV // wb + 2)),
                  tab_spec, tab_spec, tab_spec, pl.BlockSpec((8, LANES), lambda i: (0, 0))],
        out_specs=[pl.BlockSpec((tm, NSA_HEADS * HEAD_DIM), lambda i: (i, 0)),
                   pl.BlockSpec((tm, NSA_KVP_COLS), lambda i: (i, 0)),
                   pl.BlockSpec((None, npl, tm, HEAD_DIM), lambda i: (i // ns, 0, i % ns, 0)), vt_spec, vt_spec,
                   pl.BlockSpec((tm, LANES), lambda i: (i, 0))],
        out_shape=[jax.ShapeDtypeStruct((T, NSA_HEADS * HEAD_DIM), BF16),
                   jax.ShapeDtypeStruct((T, NSA_KVP_COLS), BF16),
                   jax.ShapeDtypeStruct((T // S, npl, S, HEAD_DIM), BF16), vt_shape, vt_shape,
                   jax.ShapeDtypeStruct((T, LANES), F32)],
        compiler_params=_cparams(("parallel",)),
        name="nsa_prep",
    )(z, z, z, z, z, *tabs, gn)


def _cmp_kernel(xk_ref, xv_ref, wk_ref, wv_ref, pek_ref, pev_ref, gk_ref, ct_ref, s1_ref, s2_ref,
                kct_ref, vc_ref):
    nc = xk_ref.shape[0]
    row = lax.broadcasted_iota(jnp.int32, (nc, HEAD_DIM), 0)

    def comp(x_ref, w_ref, pe_ref):
        w = w_ref[...]
        y = jnp.dot(x_ref[...], w, preferred_element_type=F32)
        ype = jnp.dot(pe_ref[...], w, preferred_element_type=F32)
        bias = ype[0:1, :HEAD_DIM] + ype[1:2, HEAD_DIM:]
        out = y[:, :HEAD_DIM] + pltpu.roll(y[:, HEAD_DIM:], nc - 1, 0) + bias
        return jnp.where(row < nc - 1, out, 0.0)

    k = _rms(comp(xk_ref, wk_ref, pek_ref), gk_ref[...])
    k = _rope_lanes(k, ct_ref[...], s1_ref[...], s2_ref[...], NSA_ROT // 2)
    kct_ref[...] = k.T.astype(BF16)
    vc_ref[...] = comp(xv_ref, wv_ref, pev_ref).astype(BF16)


def nsa_compress(x2, wk2, wv2, pek2, pev2, gk, tabs_cmp):
    B, _, nc, kk = x2.shape
    G = NSA_KV_GROUPS
    full = lambda shape: pl.BlockSpec(shape, lambda b, g: (0,) * len(shape))
    return pl.pallas_call(
        _cmp_kernel,
        grid=(B, G),
        in_specs=[pl.BlockSpec((None, None, nc, kk), lambda b, g: (b, g, 0, 0)),
                  pl.BlockSpec((None, None, nc, kk), lambda b, g: (b, G + g, 0, 0)),
                  full((kk, 2 * HEAD_DIM)), full((kk, 2 * HEAD_DIM)),
                  full((8, kk)), full((8, kk)), full((1, HEAD_DIM)),
                  full((nc, LANES)), full((nc, LANES)), full((nc, LANES))],
        out_specs=[pl.BlockSpec((None, None, HEAD_DIM, nc), lambda b, g: (b, g, 0, 0)),
                   pl.BlockSpec((None, None, nc, HEAD_DIM), lambda b, g: (b, g, 0, 0))],
        out_shape=[jax.ShapeDtypeStruct((B, G, HEAD_DIM, nc), BF16),
                   jax.ShapeDtypeStruct((B, G, nc, HEAD_DIM), BF16)],
        compiler_params=_cparams(("parallel", "parallel")),
        name="nsa_compress",
    )(x2, x2, wk2, wv2, pek2, pev2, gk, *tabs_cmp)


def _cattn_kernel(q_ref, kct_ref, vc_ref, ov_ref, oc_ref, sel_ref, *, tq):
    nc = vc_ref.shape[0]
    ns = sel_ref.shape[0]
    t0 = pl.program_id(2) * tq
    t_pos = t0 + lax.broadcasted_iota(jnp.int32, (tq, nc), 0)
    n_idx = lax.broadcasted_iota(jnp.int32, (tq, nc), 1)
    vis = (t_pos >= n_idx * CMP_STRIDE + (CMP_LEN - 1)) & (n_idx < nc - 1)
    bias = jnp.where(vis, 0.0, NEG_INF)
    kct = kct_ref[...]
    vc = vc_ref[...]
    psum = jnp.zeros((tq, nc), F32)
    for hh in range(NSA_HPG):
        s = jnp.dot(q_ref[:, hh * HEAD_DIM:(hh + 1) * HEAD_DIM], kct, preferred_element_type=F32) + bias
        m = jnp.max(s, axis=-1, keepdims=True)
        m = jnp.where(m == NEG_INF, 0.0, m)
        p = jnp.exp2(s - m)
        den = jnp.sum(p, axis=-1, keepdims=True)
        p = p * (1.0 / jnp.where(den > 0.0, den, 1.0))
        oc_ref[:, hh * HEAD_DIM:(hh + 1) * HEAD_DIM] = jnp.dot(
            p.astype(BF16), vc, preferred_element_type=F32).astype(oc_ref.dtype)
        psum = psum + p
    pt = psum.T
    hi = pt.astype(BF16)
    lo = (pt - hi.astype(F32)).astype(BF16)
    ov = ov_ref[...]
    imp = jnp.dot(ov, hi, preferred_element_type=F32) + jnp.dot(ov, lo, preferred_element_type=F32)
    blk = lax.broadcasted_iota(jnp.int32, (ns, tq), 0)
    cur = (t0 + lax.broadcasted_iota(jnp.int32, (ns, tq), 1)) // SLC_LEN
    forced = (blk == 0) | ((blk <= cur) & (blk > cur - N_LOCAL_SLC))
    val = jnp.where(blk > cur, NEG_INF, jnp.where(forced, FORCE_SCORE, imp))
    rank = jnp.zeros((ns, tq), F32)
    for i in range(ns):
        other = val[i:i + 1, :]
        ahead = (other > val) | ((other == val) & (blk > i))
        rank = rank + jnp.where(ahead, 1.0, 0.0)
    sel_ref[...] = jnp.where(rank < float(min(SLC_TOPK, ns)), 1.0, 0.0).astype(sel_ref.dtype)


def nsa_cmp_attention(qn, kct, vc, ov, B, S, *, tq):
    G = NSA_KV_GROUPS
    nq = S // tq
    nc = vc.shape[2]
    ns = S // SLC_LEN
    gw = NSA_HPG * HEAD_DIM
    return pl.pallas_call(
        functools.partial(_cattn_kernel, tq=tq),
        grid=(B, G, nq),
        in_specs=[pl.BlockSpec((tq, gw), lambda b, g, i: (b * nq + i, g)),
                  pl.BlockSpec((None, None, HEAD_DIM, nc), lambda b, g, i: (b, g, 0, 0)),
                  pl.BlockSpec((None, None, nc, HEAD_DIM), lambda b, g, i: (b, g, 0, 0)),
                  pl.BlockSpec((ns, nc), lambda b, g, i: (0, 0))],
        out_specs=[pl.BlockSpec((tq, gw), lambda b, g, i: (b * nq + i, g)),
                   pl.BlockSpec((None, None, ns, tq), lambda b, g, i: (b, g, 0, i))],
        out_shape=[jax.ShapeDtypeStruct((B * S, NSA_HEADS * HEAD_DIM), BF16),
                   jax.ShapeDtypeStruct((B, G, ns, S), BF16)],
        compiler_params=_cparams(("parallel", "parallel", "parallel")),
        name="nsa_cmp_attention",
    )(qn, kct, vc, ov)


def _flash_kernel(*refs, mode, hg, tq, tk, tv, dqk, dv):
    if mode == "sel":
        q_ref, k_ref, vt_ref, sel_ref, ex_ref, o_ref, qs, m_s, l_s, acc_s, sa, sb = refs
    else:
        q_ref, k_ref, vt_ref, o_ref, qs, m_s, l_s, acc_s, sa, sb = refs
    rows = hg * tq
    q0 = pl.program_id(2) * tq
    cd = q0 // tk
    for hh in range(hg):
        qs[hh * tq:(hh + 1) * tq, :] = q_ref[:, hh * dqk:(hh + 1) * dqk]

    def scores(c, kind):
        start = pl.multiple_of(c * tk, tk)
        s = lax.dot_general(k_ref[pl.ds(start, tk), :], qs[...], (((1,), (1,)), ((), ())),
                            preferred_element_type=F32)
        ok = None
        if kind == "diag":
            k_pos = start + lax.broadcasted_iota(jnp.int32, (tk, tq), 0)
            q_pos = q0 + lax.broadcasted_iota(jnp.int32, (tk, tq), 1)
            ok = k_pos <= q_pos
        if mode == "sel":
            chosen = jnp.dot(ex_ref[c], sel_ref[...], preferred_element_type=F32) > 0.5
            ok = chosen if ok is None else (ok & chosen)
        if ok is not None:
            bias = jnp.where(ok, 0.0, NEG_INF)
            s = s + (jnp.concatenate([bias] * hg, axis=1) if hg > 1 else bias)
        return s

    def update(c, s, carry):
        m, l, acc = carry
        m_new = jnp.maximum(m, jnp.max(s, axis=0, keepdims=True))
        p = jnp.exp2(s - m_new)
        alpha = jnp.exp2(m - m_new)
        l = alpha * l + jnp.sum(p, axis=0, keepdims=True)
        r = tk // tv
        vt = vt_ref[c] if r == 1 else jnp.concatenate([vt_ref[c * r + j] for j in range(r)], axis=1)
        acc = alpha * acc + jnp.dot(vt, p.astype(BF16), preferred_element_type=F32)
        return m_new, l, acc

    def load():
        return m_s[...], l_s[...], acc_s[...]

    def store(carry):
        m_s[...], l_s[...], acc_s[...] = carry

    nd = max(tq // tk, 1)
    last = k_ref.shape[0] // tk - 1
    carry = (jnp.full((1, rows), NEG_INF, F32), jnp.zeros((1, rows), F32), jnp.zeros((dv, rows), F32))
    s_next = scores(cd, "diag")
    for j in range(nd):
        s_cur = s_next
        if j + 1 < nd:
            s_next = scores(cd + j + 1, "diag")
        else:
            sa[...] = scores(0, "full")
        carry = update(cd + j, s_cur, carry)
    store(carry)

    def pairs(first, count):
        carry = load()
        for j in range(count):
            c = first + 2 * j
            sb[...] = scores(c + 1, "full")
            carry = update(c, sa[...], carry)
            sa[...] = scores(jnp.minimum(c + 2, last), "full")
            carry = update(c + 1, sb[...], carry)
        store(carry)

    def four(i, _):
        pairs(4 * i, 2)
        return 0

    lax.fori_loop(0, cd // 4, four, 0)

    @pl.when(cd % 4 >= 2)
    def _():
        pairs((cd // 4) * 4, 1)

    @pl.when(cd % 2 == 1)
    def _():
        store(update(cd - 1, sa[...], load()))

    o = acc_s[...] * (1.0 / l_s[...])
    for hh in range(hg):
        o_ref[:, hh * dv:(hh + 1) * dv] = o[:, hh * tq:(hh + 1) * tq].T.astype(o_ref.dtype)


def flash_attention(q, k, vt, B, S, *, mode, n_groups, hg, dqk, dv, kcol0, tq, tk, sel_t=None, expand=None):
    nq = S // tq
    nch = S // tk
    tv = vt.shape[-1]
    assert (tk % tq == 0 or tq % tk == 0) and S % tk == 0 and S % tq == 0 and tq % LANES == 0 and tk % tv == 0
    in_specs = [pl.BlockSpec((tq, hg * dqk), lambda b, g, i: (b * nq + i, g)),
                pl.BlockSpec((S, dqk), lambda b, g, i: (b, kcol0 + g)),
                pl.BlockSpec((None, None, S // tv, dv, tv), lambda b, g, i: (b, g, 0, 0, 0))]
    args = [q, k, vt]
    if mode == "sel":
        ns = S // SLC_LEN
        in_specs += [pl.BlockSpec((None, None, ns, tq), lambda b, g, i: (b, g, 0, i)),
                     pl.BlockSpec((nch, tk, ns), lambda b, g, i: (0, 0, 0))]
        args += [sel_t, expand]
    return pl.pallas_call(
        functools.partial(_flash_kernel, mode=mode, hg=hg, tq=tq, tk=tk, tv=tv, dqk=dqk, dv=dv),
        grid=(B, n_groups, nq),
        in_specs=in_specs,
        out_specs=pl.BlockSpec((tq, hg * dv), lambda b, g, i: (b * nq + i, g)),
        out_shape=jax.ShapeDtypeStruct((B * S, n_groups * hg * dv), BF16),
        scratch_shapes=[pltpu.VMEM((hg * tq, dqk), BF16),
                        pltpu.VMEM((1, hg * tq), F32),
                        pltpu.VMEM((1, hg * tq), F32),
                        pltpu.VMEM((dv, hg * tq), F32),
                        pltpu.VMEM((tk, hg * tq), F32),
                        pltpu.VMEM((tk, hg * tq), F32)],
        compiler_params=_cparams(("parallel", "parallel", "arbitrary")),
        name="flash_" + mode,
    )(*args)


def _window_kernel(q_ref, k_ref, vt_ref, oc_ref, os_ref, gate_ref, o_ref, qs, *, hg, tq, dqk, dv):
    nk = WINDOW + tq
    q0 = pl.program_id(2) * tq
    start = pl.multiple_of(jnp.maximum(q0 - WINDOW, 0), tq)
    for hh in range(hg):
        qs[hh * tq:(hh + 1) * tq, :] = q_ref[:, hh * dqk:(hh + 1) * dqk]
    s = lax.dot_general(k_ref[pl.ds(start, nk), :], qs[...], (((1,), (1,)), ((), ())),
                        preferred_element_type=F32)
    k_pos = start + lax.broadcasted_iota(jnp.int32, (nk, tq), 0)
    q_pos = q0 + lax.broadcasted_iota(jnp.int32, (nk, tq), 1)
    bias = jnp.where((k_pos <= q_pos) & (q_pos - k_pos < WINDOW), 0.0, NEG_INF)
    s = s + jnp.concatenate([bias] * hg, axis=1)
    p = jnp.exp2(s - jnp.max(s, axis=0, keepdims=True))
    l = jnp.sum(p, axis=0, keepdims=True)
    pb = p.astype(BF16)
    c0 = start // tq
    o = jnp.zeros((dv, hg * tq), F32)
    for j in range(nk // tq):
        o = o + jnp.dot(vt_ref[c0 + j], pb[j * tq:(j + 1) * tq, :], preferred_element_type=F32)
    o = o * (1.0 / l)
    first_group = pl.program_id(1) == 0
    n_heads = 2 * hg

    def gate(br, hh):
        lo = gate_ref[:, br * n_heads + hh:br * n_heads + hh + 1]
        hi = gate_ref[:, br * n_heads + hg + hh:br * n_heads + hg + hh + 1]
        return jnp.where(first_group, lo, hi)

    for hh in range(hg):
        sl = slice(hh * dv, (hh + 1) * dv)
        mixed = (gate(0, hh) * oc_ref[:, sl].astype(F32) + gate(1, hh) * os_ref[:, sl].astype(F32)
                 + gate(2, hh) * o[:, hh * tq:(hh + 1) * tq].T)
        o_ref[:, sl] = mixed.astype(o_ref.dtype)


def window_attention(q, k, vt, oc, os_, gates, B, S, *, n_groups, hg, dqk, dv, kcol0, tq):
    assert n_groups == NSA_KV_GROUPS
    nq = S // tq
    assert WINDOW % tq == 0 and S >= WINDOW + tq and tq % LANES == 0
    return pl.pallas_call(
        functools.partial(_window_kernel, hg=hg, tq=tq, dqk=dqk, dv=dv),
        grid=(B, n_groups, nq),
        in_specs=[pl.BlockSpec((tq, hg * dqk), lambda b, g, i: (b * nq + i, g)),
                  pl.BlockSpec((S, dqk), lambda b, g, i: (b, kcol0 + g)),
                  pl.BlockSpec((None, None, nq, dv, tq), lambda b, g, i: (b, g, 0, 0, 0)),
                  pl.BlockSpec((tq, hg * dv), lambda b, g, i: (b * nq + i, g)),
                  pl.BlockSpec((tq, hg * dv), lambda b, g, i: (b * nq + i, g)),
                  pl.BlockSpec((tq, LANES), lambda b, g, i: (b * nq + i, 0))],
        out_specs=pl.BlockSpec((tq, hg * dv), lambda b, g, i: (b * nq + i, g)),
        out_shape=jax.ShapeDtypeStruct((B * S, n_groups * hg * dv), BF16),
        scratch_shapes=[pltpu.VMEM((hg * tq, dqk), BF16)],
        compiler_params=_cparams(("parallel", "parallel", "arbitrary")),
        name="window_attention",
    )(q, k, vt, oc, os_, gates)


def _mla_q_kernel(z_ref, ga_ref, w_ref, gh_ref, ct_ref, s1_ref, s2_ref, o_ref):
    h = _rms(z_ref[...], ga_ref[...]).astype(BF16)
    y = jnp.dot(h, w_ref[...], preferred_element_type=F32)
    ct, s1, s2 = ct_ref[...], s1_ref[...], s2_ref[...]
    scale = LOG2E * (QK_NOPE + QK_ROPE) ** -0.5
    for hd in range(MLA_HEADS):
        c = hd * MLA_QK_PAD
        nope = _rms(y[:, c:c + QK_NOPE], gh_ref[0:1, :])
        r = y[:, c + QK_NOPE:c + MLA_QK_PAD]
        ms = jnp.sum(r * r, axis=-1, keepdims=True) * (1.0 / QK_ROPE)
        r = _rope_lanes(r * lax.rsqrt(ms + EPS) * gh_ref[1:2, :], ct, s1, s2, QK_ROPE // 2)
        o_ref[:, c:c + QK_NOPE] = (nope * scale).astype(BF16)
        o_ref[:, c + QK_NOPE:c + MLA_QK_PAD] = (r * scale).astype(BF16)


def mla_q_proj(z, ga, wq, gh, tabs, S, *, tm):
    T = z.shape[0]
    ns = S // tm
    tab_spec = pl.BlockSpec((tm, LANES), lambda i: (i % ns, 0))
    nout = MLA_HEADS * MLA_QK_PAD
    return pl.pallas_call(
        _mla_q_kernel,
        grid=(T // tm,),
        in_specs=[pl.BlockSpec((tm, Q_LORA), lambda i: (i, OFF_QA // Q_LORA)),
                  pl.BlockSpec((1, Q_LORA), lambda i: (0, 0)),
                  pl.BlockSpec((Q_LORA, nout), lambda i: (0, 0)),
                  pl.BlockSpec((8, LANES), lambda i: (0, 0)),
                  tab_spec, tab_spec, tab_spec],
        out_specs=pl.BlockSpec((tm, nout), lambda i: (i, 0)),
        out_shape=jax.ShapeDtypeStruct((T, nout), BF16),
        compiler_params=_cparams(("parallel",)),
        name="mla_q_proj",
    )(z, ga, wq, gh, *tabs)


def _mla_kv_kernel(z_ref, zr_ref, ga_ref, w_ref, gh_ref, ct_ref, s1_ref, s2_ref, k_ref, v_ref):
    h = _rms(z_ref[...], ga_ref[...]).astype(BF16)
    y = jnp.dot(h, w_ref[...], preferred_element_type=F32)
    r = zr_ref[...]
    ms = jnp.sum(r * r, axis=-1, keepdims=True) * (1.0 / QK_ROPE)
    r = _rope_lanes(r * lax.rsqrt(ms + EPS) * gh_ref[1:2, :], ct_ref[...], s1_ref[...], s2_ref[...],
                    QK_ROPE // 2).astype(BF16)
    nv = MLA_HEADS * QK_NOPE
    for hd in range(MLA_HEADS):
        c = hd * MLA_QK_PAD
        k_ref[:, c:c + QK_NOPE] = _rms(y[:, hd * QK_NOPE:(hd + 1) * QK_NOPE], gh_ref[0:1, :]).astype(BF16)
        k_ref[:, c + QK_NOPE:c + MLA_QK_PAD] = r
    for hd in range(MLA_HEADS):
        v_ref[hd] = y[:, nv + hd * V_HEAD:nv + (hd + 1) * V_HEAD].T.astype(BF16)


def mla_kv_proj(z, ga, wkv, gh, tabs, S, *, tm=VT_CHUNK):
    T = z.shape[0]
    ns = S // tm
    tab_spec = pl.BlockSpec((tm, LANES), lambda i: (i % ns, 0))
    nk = MLA_HEADS * MLA_QK_PAD
    nv = MLA_HEADS * V_HEAD
    return pl.pallas_call(
        _mla_kv_kernel,
        grid=(T // tm,),
        in_specs=[pl.BlockSpec((tm, KV_LORA), lambda i: (i, OFF_KVA // KV_LORA)),
                  pl.BlockSpec((tm, LANES), lambda i: (i, OFF_KR // LANES)),
                  pl.BlockSpec((1, KV_LORA), lambda i: (0, 0)),
                  pl.BlockSpec((KV_LORA, MLA_HEADS * (QK_NOPE + V_HEAD)), lambda i: (0, 0)),
                  pl.BlockSpec((8, LANES), lambda i: (0, 0)),
                  tab_spec, tab_spec, tab_spec],
        out_specs=[pl.BlockSpec((tm, nk), lambda i: (i, 0)),
                   pl.BlockSpec((None, MLA_HEADS, None, V_HEAD, tm), lambda i: (i // ns, 0, i % ns, 0, 0))],
        out_shape=[jax.ShapeDtypeStruct((T, nk), BF16),
                   jax.ShapeDtypeStruct((T // S, MLA_HEADS, ns, V_HEAD, tm), BF16)],
        compiler_params=_cparams(("parallel",)),
        name="mla_kv_proj",
    )(z, z, ga, wkv, gh, *tabs)


def _mix_kernel(a_ref, ob_ref, wa_ref, wb_ref, za_ref, zb_ref, o_ref):
    pa = jnp.dot(a_ref[...], wa_ref[...].astype(BF16), preferred_element_type=F32)
    pb = jnp.dot(ob_ref[...], wb_ref[...].astype(BF16), preferred_element_type=F32)
    o_ref[...] = (_sigmoid(za_ref[...]) * pa + _sigmoid(zb_ref[...]) * pb).astype(o_ref.dtype)


def gated_mix(oa, ob, wa, wb, z, *, tm, tn):
    T, K = oa.shape
    N = wa.shape[1]
    row = lambda w: pl.BlockSpec((tm, w), lambda m, n: (m, 0))
    return pl.pallas_call(
        _mix_kernel,
        grid=(T // tm, N // tn),
        in_specs=[row(K), row(K),
                  pl.BlockSpec((K, tn), lambda m, n: (0, n)),
                  pl.BlockSpec((K, tn), lambda m, n: (0, n)),
                  pl.BlockSpec((tm, tn), lambda m, n: (m, OFF_M // tn + n)),
                  pl.BlockSpec((tm, tn), lambda m, n: (m, (OFF_M + D_MODEL) // tn + n))],
        out_specs=pl.BlockSpec((tm, tn), lambda m, n: (m, n)),
        out_shape=jax.ShapeDtypeStruct((T, N), BF16),
        compiler_params=_cparams(("parallel", "parallel")),
        name="gated_mix",
    )(oa, ob, wa, wb, z, z)


def _ffn_up_kernel(x_ref, g_ref, w1_ref, w3_ref, o_ref, h_scr):
    @pl.when(pl.program_id(1) == 0)
    def _():
        h_scr[...] = _rms(x_ref[...], g_ref[...]).astype(BF16)

    h = h_scr[...]
    a = jnp.dot(h, w1_ref[...].astype(BF16), preferred_element_type=F32)
    b = jnp.dot(h, w3_ref[...].astype(BF16), preferred_element_type=F32)
    o_ref[...] = (a * _sigmoid(a) * b).astype(o_ref.dtype)


def ffn_up(x, g, w1, w3, *, tm, tn):
    T, K = x.shape
    N = w1.shape[1]
    return pl.pallas_call(
        _ffn_up_kernel,
        grid=(T // tm, N // tn),
        in_specs=[pl.BlockSpec((tm, K), lambda m, n: (m, 0)),
                  pl.BlockSpec((1, K), lambda m, n: (0, 0)),
                  pl.BlockSpec((K, tn), lambda m, n: (0, n)),
                  pl.BlockSpec((K, tn), lambda m, n: (0, n))],
        out_specs=pl.BlockSpec((tm, tn), lambda m, n: (m, n)),
        out_shape=jax.ShapeDtypeStruct((T, N), BF16),
        scratch_shapes=[pltpu.VMEM((tm, K), BF16)],
        compiler_params=_cparams(("parallel", "arbitrary")),
        name="ffn_up",
    )(x, g.reshape(1, K), w1, w3)


def _pack_bf16_pairs(h):
    k = h.shape[1] // 2
    hi = lax.bitcast_convert_type(h[:, :k].astype(jnp.bfloat16).astype(F32), jnp.uint32)
    lo = lax.bitcast_convert_type(h[:, k:].astype(jnp.bfloat16).astype(F32), jnp.uint32)
    return lax.bitcast_convert_type(hi | (lo >> 16), F32)


def _unpack_bf16_pairs(xp):
    xp = lax.bitcast_convert_type(xp, jnp.uint32)
    hi = lax.bitcast_convert_type(xp & jnp.uint32(0xFFFF0000), F32)
    lo = lax.bitcast_convert_type(xp << 16, F32)
    return hi.astype(BF16), lo.astype(BF16)


def _router_kernel(x_ref, g_ref, wr_ref, h_ref, idx_ref, gate_ref):
    h = _rms(x_ref[...], g_ref[...])
    is_pad = pl.program_id(0) == pl.num_programs(0) - 1
    h_ref[...] = jnp.where(is_pad, 0.0, _pack_bf16_pairs(h))
    logits = jnp.dot(h, wr_ref[...], preferred_element_type=F32, precision=lax.Precision.HIGHEST)
    lane = lax.broadcasted_iota(jnp.int32, logits.shape, 1).astype(F32)
    logits = jnp.where(lane < float(N_EXPERTS), logits, NEG_INF)
    m1 = jnp.max(logits, axis=-1, keepdims=True)
    i1 = jnp.min(jnp.where(logits == m1, lane, float(LANES)), axis=-1, keepdims=True)
    rest = jnp.where(lane == i1, NEG_INF, logits)
    m2 = jnp.max(rest, axis=-1, keepdims=True)
    i2 = jnp.min(jnp.where(rest == m2, lane, float(LANES)), axis=-1, keepdims=True)
    e = jnp.exp(m2 - m1)
    den = 1.0 + e
    idx_ref[...] = jnp.where(lane == 0.0, i1, jnp.where(lane == 1.0, i2, 0.0)).astype(jnp.int32)
    gate_ref[...] = jnp.where(lane == 0.0, 1.0 / den, jnp.where(lane == 1.0, e / den, 0.0))


def router(x, g, wr_pad, *, tm):
    T, K = x.shape
    nt = T // tm
    return pl.pallas_call(
        _router_kernel,
        grid=(nt + 1,),
        in_specs=[pl.BlockSpec((tm, K), lambda i: (jnp.minimum(i, nt - 1), 0)),
                  pl.BlockSpec((1, K), lambda i: (0, 0)),
                  pl.BlockSpec((K, LANES), lambda i: (0, 0))],
        out_specs=[pl.BlockSpec((tm, K // 2), lambda i: (i, 0)),
                   pl.BlockSpec((tm, LANES), lambda i: (jnp.minimum(i, nt - 1), 0)),
                   pl.BlockSpec((tm, LANES), lambda i: (jnp.minimum(i, nt - 1), 0))],
        out_shape=[jax.ShapeDtypeStruct((T + tm, K // 2), F32),
                   jax.ShapeDtypeStruct((T, LANES), jnp.int32),
                   jax.ShapeDtypeStruct((T, LANES), F32)],
        compiler_params=_cparams(("arbitrary",)),
        name="router",
    )(x, g.reshape(1, K), wr_pad)


def _moe_weight_stream(w_hbms, stages, casts, sems, be_ref, nu_ref, gs_ref, ne_ref, lg_ref, tn):
    n = pl.program_id(0)
    r = pl.program_id(1)
    used = r < nu_ref[0]

    def copies(e, nt):
        c0 = pl.multiple_of(nt * tn, tn)
        return [pltpu.make_async_copy(w.at[e, :, pl.ds(c0, tn)], st, sems.at[i])
                for i, (w, st) in enumerate(zip(w_hbms, stages))]

    @pl.when((n == 0) & (r == 0))
    def _():
        for c in copies(be_ref[0], 0):
            c.start()

    @pl.when(used & (gs_ref[r] == 1))
    def _():
        for c in copies(be_ref[r], n):
            c.wait()
        for st, wb in zip(stages, casts):
            wb[...] = st[...].astype(BF16)
        last = lg_ref[r] == 1

        @pl.when(jnp.logical_not(last & (n == pl.num_programs(0) - 1)))
        def _():
            for c in copies(ne_ref[r], n + last.astype(jnp.int32)):
                c.start()

    return used


def _moe_up_kernel(be_ref, nu_ref, gs_ref, ne_ref, lg_ref, x_ref, w1_hbm, w3_hbm, o_ref,
                   st1, st3, w1_s, w3_s, sems, *, tn):
    used = _moe_weight_stream((w1_hbm, w3_hbm), (st1, st3), (w1_s, w3_s), sems,
                              be_ref, nu_ref, gs_ref, ne_ref, lg_ref, tn)

    @pl.when(used)
    def _():
        xa, xb = _unpack_bf16_pairs(x_ref[...])
        k2 = xa.shape[1]
        a = (jnp.dot(xa, w1_s[:k2, :], preferred_element_type=F32)
             + jnp.dot(xb, w1_s[k2:, :], preferred_element_type=F32))
        b = (jnp.dot(xa, w3_s[:k2, :], preferred_element_type=F32)
             + jnp.dot(xb, w3_s[k2:, :], preferred_element_type=F32))
        o_ref[...] = (a * _sigmoid(a) * b).astype(o_ref.dtype)

    @pl.when(jnp.logical_not(used))
    def _():
        o_ref[...] = jnp.zeros(o_ref.shape, o_ref.dtype)


def moe_up(tables, xb, w1, w3, *, tmb, tn):
    n_slot = xb.shape[0]
    K, N = w1.shape[1], w1.shape[2]
    row = lambda r, nu: jnp.minimum(r, nu[0] - 1)
    return pl.pallas_call(
        functools.partial(_moe_up_kernel, tn=tn),
        grid_spec=pltpu.PrefetchScalarGridSpec(
            num_scalar_prefetch=5,
            grid=(N // tn, n_slot // tmb),
            in_specs=[pl.BlockSpec((tmb, K // 2), lambda n, r, be, nu, gs, ne, lg: (row(r, nu), 0)),
                      pl.BlockSpec(memory_space=pl.ANY), pl.BlockSpec(memory_space=pl.ANY)],
            out_specs=pl.BlockSpec((tmb, tn), lambda n, r, be, nu, gs, ne, lg: (r, n)),
            scratch_shapes=[pltpu.VMEM((K, tn), F32), pltpu.VMEM((K, tn), F32),
                            pltpu.VMEM((K, tn), BF16), pltpu.VMEM((K, tn), BF16),
                            pltpu.SemaphoreType.DMA((2,))]),
        out_shape=jax.ShapeDtypeStruct((n_slot, N), BF16),
        compiler_params=_cparams(("arbitrary", "arbitrary")),
        name="moe_up",
    )(*tables, xb, w1, w3)


def _moe_down_kernel(be_ref, nu_ref, gs_ref, ne_ref, lg_ref, a_ref, w2_hbm, o_ref, st2, w2_s, sems, *, tn):
    used = _moe_weight_stream((w2_hbm,), (st2,), (w2_s,), sems, be_ref, nu_ref, gs_ref, ne_ref, lg_ref, tn)

    @pl.when(used)
    def _():
        o_ref[...] = _pack_bf16_pairs(jnp.dot(a_ref[...], w2_s[...], preferred_element_type=F32))

    @pl.when(jnp.logical_not(used))
    def _():
        o_ref[...] = jnp.zeros(o_ref.shape, o_ref.dtype)


def moe_down(tables, act, w2, *, tmb, tn):
    n_slot, K = act.shape
    N = w2.shape[2]
    row = lambda r, nu: jnp.minimum(r, nu[0] - 1)
    return pl.pallas_call(
        functools.partial(_moe_down_kernel, tn=tn),
        grid_spec=pltpu.PrefetchScalarGridSpec(
            num_scalar_prefetch=5,
            grid=(N // tn, n_slot // tmb),
            in_specs=[pl.BlockSpec((tmb, K), lambda n, r, be, nu, gs, ne, lg: (row(r, nu), 0)),
                      pl.BlockSpec(memory_space=pl.ANY)],
            out_specs=pl.BlockSpec((tmb, tn // 2), lambda n, r, be, nu, gs, ne, lg: (r, n)),
            scratch_shapes=[pltpu.VMEM((K, tn), F32), pltpu.VMEM((K, tn), BF16),
                            pltpu.SemaphoreType.DMA((1,))]),
        out_shape=jax.ShapeDtypeStruct((n_slot, N // 2), F32),
        compiler_params=_cparams(("arbitrary", "arbitrary")),
        name="moe_down",
    )(*tables, act, w2)


def _combine_kernel(x_ref, g_ref, y0_ref, y1_ref, o_ref, *, tn):
    g0, g1 = g_ref[:, 0:1], g_ref[:, 1:2]
    h = tn // 2

    def halves(y_ref, n):
        u = lax.bitcast_convert_type(y_ref[:, n * h:(n + 1) * h], jnp.uint32)
        return (lax.bitcast_convert_type(u & jnp.uint32(0xFFFF0000), F32),
                lax.bitcast_convert_type(u << 16, F32))

    for n in range(x_ref.shape[1] // tn):
        a_hi, a_lo = halves(y0_ref, n)
        b_hi, b_lo = halves(y1_ref, n)
        lo_cols = slice(n * tn, n * tn + h)
        hi_cols = slice(n * tn + h, (n + 1) * tn)
        o_ref[:, lo_cols] = x_ref[:, lo_cols] + g0 * a_hi + g1 * b_hi
        o_ref[:, hi_cols] = x_ref[:, hi_cols] + g0 * a_lo + g1 * b_lo


def moe_combine(x, gate, y0, y1, *, tn, tm):
    T, N = x.shape
    return pl.pallas_call(
        functools.partial(_combine_kernel, tn=tn),
        grid=(T // tm,),
        in_specs=[pl.BlockSpec((tm, N), lambda i: (i, 0)),
                  pl.BlockSpec((tm, LANES), lambda i: (i, 0)),
                  pl.BlockSpec((tm, N // 2), lambda i: (i, 0)),
                  pl.BlockSpec((tm, N // 2), lambda i: (i, 0))],
        out_specs=pl.BlockSpec((tm, N), lambda i: (i, 0)),
        out_shape=jax.ShapeDtypeStruct((T, N), F32),
        compiler_params=_cparams(("parallel",)),
        name="moe_combine",
    )(x, gate, y0, y1)


def _rope_tabs(pos, rot_dim, n_rows):
    half = rot_dim // 2
    inv = 1.0 / (ROPE_THETA ** (jnp.arange(0, rot_dim, 2, dtype=F32) / rot_dim))
    ang = jnp.asarray(pos).astype(F32)[:, None] * inv[None, :]
    c, s = jnp.cos(ang), jnp.sin(ang)
    z = jnp.zeros_like(c)
    pad = lambda a, fill: jnp.pad(a, ((0, n_rows - a.shape[0]), (0, LANES - a.shape[1])), constant_values=fill)
    return pad(jnp.concatenate([c, c], 1), 1.0), pad(jnp.concatenate([-s, z], 1), 0.0), \
        pad(jnp.concatenate([z, s], 1), 0.0)


def _mla_tabs(S):
    ct, s1, s2 = _rope_tabs(jnp.arange(S), QK_ROPE, S)
    lane = jnp.arange(LANES)[None, :]
    return jnp.where(lane < QK_ROPE, ct, 0.0), s1, s2


def _pad_cols(w, n):
    return jnp.pad(w, ((0, 0), (0, n - w.shape[1])))


def _layout_w_in(w):
    sp = np.cumsum([0, NSA_HEADS * HEAD_DIM, NSA_KV_COLS, 3 * NSA_HEADS, Q_LORA, KV_LORA, QK_ROPE, 2 * D_MODEL])
    q, kv, g, qa, kva, kr, m = [w[:, sp[i]:sp[i + 1]] for i in range(7)]
    return jnp.concatenate([qa, _pad_cols(g, LANES), _pad_cols(kr, LANES), q, kva, kv, m], axis=1).astype(BF16)


def _layout_w_q_b(w):
    w = w.reshape(Q_LORA, MLA_HEADS, QK_NOPE + QK_ROPE)
    w = jnp.pad(w, ((0, 0), (0, 0), (0, MLA_QK_PAD - QK_NOPE - QK_ROPE)))
    return w.reshape(Q_LORA, MLA_HEADS * MLA_QK_PAD).astype(BF16)


def _layout_w_kv_b(w):
    w = w.reshape(KV_LORA, MLA_HEADS, QK_NOPE + V_HEAD)
    return jnp.concatenate([w[:, :, :QK_NOPE].reshape(KV_LORA, -1), w[:, :, QK_NOPE:].reshape(KV_LORA, -1)],
                           axis=1).astype(BF16)


def _layout_w_cmp(w):
    h = CMP_LEN // 2
    return jnp.concatenate([w[:h].reshape(h * HEAD_DIM, HEAD_DIM), w[h:].reshape(h * HEAD_DIM, HEAD_DIM)],
                           axis=1).astype(BF16)


def _layout_pe(pe):
    return jnp.pad(pe.reshape(2, (CMP_LEN // 2) * HEAD_DIM), ((0, 6), (0, 0))).astype(BF16)


def _overlap(nc, ns):
    n = np.arange(nc)[None, :] * CMP_STRIDE
    j = np.arange(ns)[:, None] * SLC_LEN
    ov = (n <= j + SLC_LEN - 1) & (j <= n + CMP_LEN - 1) & (np.arange(nc)[None, :] < nc - 1)
    return jnp.asarray(ov.astype(np.float32), BF16)


def _expand(ns, S, tk):
    e = ((np.arange(S // tk)[:, None, None] * tk + np.arange(tk)[None, :, None]) // SLC_LEN
         == np.arange(ns)[None, None, :])
    return jnp.asarray(e.astype(np.float32), BF16)


def _layout_layers(p):
    depth = p['w_in'].shape[0]
    lane_row = lambda g: jnp.pad(g, ((0, 0), (0, LANES - g.shape[-1])))[:, None, :]
    z4 = jnp.zeros((depth, 4, HEAD_DIM), F32)
    z6 = jnp.zeros((depth, 6, LANES), F32)
    return dict(
        gn=jnp.concatenate([p['nsa_q_norm_g'][:, None, :], p['nsa_k_norm_g'], z4], axis=1),
        w_ck=jax.vmap(_layout_w_cmp)(p['w_cmp_k']), w_cv=jax.vmap(_layout_w_cmp)(p['w_cmp_v']),
        pe_k=jax.vmap(_layout_pe)(p['cmp_pe_k']), pe_v=jax.vmap(_layout_pe)(p['cmp_pe_v']),
        gh_q=jnp.concatenate([p['mla_q_norm_g'][:, None, :], lane_row(p['mla_qr_norm_g']), z6], axis=1),
        gh_k=jnp.concatenate([p['mla_k_norm_g'][:, None, :], lane_row(p['mla_kr_norm_g']), z6], axis=1),
        w_qb=jax.vmap(_layout_w_q_b)(p['w_q_b']), w_kvb=jax.vmap(_layout_w_kv_b)(p['w_kv_b']))


def _attention_block(x2d, B, S, p, pw, l, tabs):
    T = x2d.shape[0]
    z = norm_matmul(x2d, p['attn_norm_g'][l], _layout_w_in(p['w_in'][l]), tm=min(1024, T), tn=512)
    qn, kvp, xc, vts, vtw, gates = nsa_prep(z, tabs['tok'], pw['gn'][l], S)

    nc = S // CMP_STRIDE
    x2 = xc.reshape(B, 2 * NSA_KV_GROUPS, nc, CMP_STRIDE * HEAD_DIM)
    kct, vc = nsa_compress(x2, pw['w_ck'][l], pw['w_cv'][l], pw['pe_k'][l], pw['pe_v'][l],
                           p['nsa_k_norm_g'][l][0:1], tabs['cmp'])
    ns = S // SLC_LEN
    oc, sel_t = nsa_cmp_attention(qn, kct, vc, _overlap(nc, ns), B, S, tq=min(256, S))
    nsa = dict(n_groups=NSA_KV_GROUPS, hg=NSA_HPG, dqk=HEAD_DIM, dv=HEAD_DIM)
    tks = min(512, S)
    os_ = flash_attention(qn, kvp, vts, B, S, mode="sel", kcol0=0, tq=256, tk=tks,
                          sel_t=sel_t, expand=_expand(ns, S, tks), **nsa)
    oa = window_attention(qn, kvp, vtw, oc, os_, gates, B, S, kcol0=NSA_KV_GROUPS, tq=VT_CHUNK, **nsa)

    qm = mla_q_proj(z, p['mla_qa_norm_g'][l][None], pw['w_qb'][l], pw['gh_q'][l], tabs['mla'], S, tm=256)
    km, vmt = mla_kv_proj(z, p['mla_kva_norm_g'][l][None], pw['w_kvb'][l], pw['gh_k'][l], tabs['mla'], S)
    tkm = min(512, S)
    ob = flash_attention(qm, km, vmt, B, S, mode="causal",
                         n_groups=MLA_HEADS, hg=1, dqk=MLA_QK_PAD, dv=V_HEAD, kcol0=0, tq=min(1024, S), tk=tkm)

    tm = min(2048, T)
    mix = gated_mix(oa, ob, p['w_proj_nsa'][l], p['w_proj_mla'][l], z, tm=tm, tn=512)
    return matmul_residual(mix, p['w_out'][l], x2d, tm=min(2048, T), tn=512)


def _dense_ffn(x2d, g, w1, w3, w2):
    T = x2d.shape[0]
    act = ffn_up(x2d, g, w1.astype(BF16), w3.astype(BF16), tm=min(1024, T), tn=512)
    return matmul_residual(act, w2.astype(BF16), x2d, tm=min(1024, T), tn=256)


def _cumsum_rows(oh, blk=128):
    A, E = oh.shape
    nb = A // blk
    x = oh.astype(F32).reshape(nb, blk, E)
    within = jnp.einsum('ij,bje->bie', jnp.tril(jnp.ones((blk, blk), F32)), x)
    before = jnp.tril(jnp.ones((nb, nb), F32), -1) @ within[:, -1, :]
    return (within + before[:, None, :]).astype(jnp.int32).reshape(A, E)


def _moe_ffn(x2d, g, w_router, w1, w3, w2, *, tmb=512):
    T = x2d.shape[0]
    hp, idx, gate = router(x2d, g, _pad_cols(w_router, LANES), tm=256)
    A = T * TOP_K
    e_flat = idx[:, :TOP_K].reshape(A)
    tok_flat = jnp.repeat(jnp.arange(T, dtype=jnp.int32), TOP_K)
    oh = (e_flat[:, None] == jnp.arange(N_EXPERTS)[None, :]).astype(jnp.int32)
    csum = _cumsum_rows(oh)
    rank = jnp.sum(oh * csum, axis=1) - 1
    counts = csum[-1]
    padded = (counts + tmb - 1) // tmb * tmb
    pad_end = jnp.cumsum(padded)
    dest = (pad_end - padded)[e_flat] + rank
    n_blk = -(-A // tmb) + N_EXPERTS
    n_slot = n_blk * tmb
    slot_tok = jnp.full((n_slot,), T, jnp.int32).at[dest].set(tok_flat)
    eidx = jnp.arange(N_EXPERTS, dtype=jnp.int32)
    blk = jnp.arange(n_blk, dtype=jnp.int32)
    blk_exp = jnp.minimum(jnp.sum((pad_end[None, :] <= (blk * tmb)[:, None]).astype(jnp.int32), axis=1),
                          N_EXPERTS - 1).astype(jnp.int32)
    n_used = (pad_end[-1:] // tmb).astype(jnp.int32)
    present = counts > 0
    first_e = jnp.min(jnp.where(present, eidx, N_EXPERTS))
    last_e = jnp.max(jnp.where(present, eidx, -1))
    later = jnp.where(present[None, :] & (eidx[None, :] > eidx[:, None]), eidx[None, :], N_EXPERTS)
    next_e = jnp.min(later, axis=1)
    next_e = jnp.where(next_e == N_EXPERTS, first_e, next_e).astype(jnp.int32)
    starts = ((blk == 0) | (blk_exp != jnp.roll(blk_exp, 1))) & (blk < n_used[0])
    tables = (blk_exp, n_used, starts.astype(jnp.int32), next_e[blk_exp],
              (blk_exp == last_e).astype(jnp.int32))
    xb = hp[slot_tok]
    act = moe_up(tables, xb, w1, w3, tmb=tmb, tn=1024)
    tnd = 512
    yb = moe_down(tables, act, w2, tmb=tmb, tn=tnd)
    d2 = dest.reshape(T, TOP_K)
    return moe_combine(x2d, gate, yb[d2[:, 0]], yb[d2[:, 1]], tn=tnd, tm=512)


def kernel(x, attn_norm_g, w_in, nsa_q_norm_g, nsa_k_norm_g, cmp_pe_k, cmp_pe_v, w_cmp_k, w_cmp_v, mla_qa_norm_g, w_q_b, mla_kva_norm_g, w_kv_b, mla_q_norm_g, mla_qr_norm_g, mla_k_norm_g, mla_kr_norm_g, w_proj_nsa, w_proj_mla, w_out, ffn_norm_g, w_ff1, w_ff3, w_ff2, w_router, w_e1, w_e3, w_e2):
    p = dict(attn_norm_g=attn_norm_g, w_in=w_in, nsa_q_norm_g=nsa_q_norm_g, nsa_k_norm_g=nsa_k_norm_g,
             cmp_pe_k=cmp_pe_k, cmp_pe_v=cmp_pe_v, w_cmp_k=w_cmp_k, w_cmp_v=w_cmp_v,
             mla_qa_norm_g=mla_qa_norm_g, w_q_b=w_q_b, mla_kva_norm_g=mla_kva_norm_g, w_kv_b=w_kv_b,
             mla_q_norm_g=mla_q_norm_g, mla_qr_norm_g=mla_qr_norm_g, mla_k_norm_g=mla_k_norm_g,
             mla_kr_norm_g=mla_kr_norm_g, w_proj_nsa=w_proj_nsa, w_proj_mla=w_proj_mla, w_out=w_out)
    B, S, D = x.shape
    depth = w_in.shape[0]
    nc = S // CMP_STRIDE
    tabs = dict(tok=_rope_tabs(jnp.arange(S), NSA_ROT, S),
                cmp=_rope_tabs(jnp.arange(nc - 1) * CMP_STRIDE + CMP_LEN - 1, NSA_ROT, nc),
                mla=_mla_tabs(S))
    pw = _layout_layers(p)
    x2d = x.reshape(B * S, D)
    for l in range(depth):
        x2d = _attention_block(x2d, B, S, p, pw, l, tabs)
        if l % 2 == 0:
            x2d = _dense_ffn(x2d, ffn_norm_g[l], w_ff1[l // 2], w_ff3[l // 2], w_ff2[l // 2])
        else:
            x2d = _moe_ffn(x2d, ffn_norm_g[l], w_router[l // 2], w_e1[l // 2], w_e3[l // 2], w_e2[l // 2])
    return x2d.reshape(B, S, D)
```

```python
import functools

import numpy as np
import jax
import jax.numpy as jnp
from jax import lax
from jax.experimental import pallas as pl
from jax.experimental.pallas import tpu as pltpu

D_MODEL = 2048
HEAD_DIM = 128
NSA_HEADS = 8
NSA_KV_GROUPS = 2
NSA_HPG = NSA_HEADS // NSA_KV_GROUPS
NSA_ROT = HEAD_DIM // 4
CMP_LEN = 32
CMP_STRIDE = 16
SLC_LEN = 64
SLC_TOPK = 16
N_LOCAL_SLC = 2
WINDOW = 512
FORCE_SCORE = 1.0e4
MLA_HEADS = 8
Q_LORA = 768
KV_LORA = 512
QK_NOPE = 128
QK_ROPE = 64
V_HEAD = 128
ROPE_THETA = 500000.0
EPS = 1e-6
D_FF = 7168
N_EXPERTS = 8
TOP_K = 2

LANES = 128
MLA_QK_PAD = 256
NEG_INF = float("-inf")
LOG2E = 1.4426950408889634
BF16 = jnp.bfloat16
F32 = jnp.float32

OFF_QA = 0
OFF_G = 768
OFF_KR = 896
OFF_Q = 1024
OFF_KVA = 2048
OFF_KV = 2560
OFF_M = 4096
D_INP = 8192
NSA_KV_COLS = 3 * 2 * NSA_KV_GROUPS * HEAD_DIM
NSA_KVP_COLS = 2 * NSA_KV_GROUPS * HEAD_DIM
VT_CHUNK = 256

VMEM_LIMIT = 56 * 1024 * 1024


def _cparams(sem):
    return pltpu.CompilerParams(dimension_semantics=sem, vmem_limit_bytes=VMEM_LIMIT)


def _rms(x, g):
    ms = jnp.mean(x * x, axis=-1, keepdims=True)
    return x * lax.rsqrt(ms + EPS) * g


def _rope_lanes(y, ct, s1, s2, half):
    return y * ct + pltpu.roll(y, LANES - half, 1) * s1 + pltpu.roll(y, half, 1) * s2


def _sigmoid(x):
    return 1.0 / (1.0 + jnp.exp(-x))


def _nmm_kernel(x_ref, g_ref, w_ref, o_ref, h_scr):
    @pl.when(pl.program_id(1) == 0)
    def _():
        h_scr[...] = _rms(x_ref[...], g_ref[...]).astype(BF16)

    o_ref[...] = jnp.dot(h_scr[...], w_ref[...].astype(BF16),
                         preferred_element_type=F32).astype(o_ref.dtype)


def norm_matmul(x, g, w, *, tm, tn, out_dtype=F32):
    T, K = x.shape
    N = w.shape[1]
    return pl.pallas_call(
        _nmm_kernel,
        grid=(T // tm, N // tn),
        in_specs=[pl.BlockSpec((tm, K), lambda m, n: (m, 0)),
                  pl.BlockSpec((1, K), lambda m, n: (0, 0)),
                  pl.BlockSpec((K, tn), lambda m, n: (0, n))],
        out_specs=pl.BlockSpec((tm, tn), lambda m, n: (m, n)),
        out_shape=jax.ShapeDtypeStruct((T, N), out_dtype),
        scratch_shapes=[pltpu.VMEM((tm, K), BF16)],
        compiler_params=_cparams(("parallel", "arbitrary")),
        name="norm_matmul",
    )(x, g.reshape(1, K), w)


def _mmres_kernel(a_ref, w_ref, r_ref, o_ref):
    o_ref[...] = r_ref[...] + jnp.dot(a_ref[...], w_ref[...].astype(BF16), preferred_element_type=F32)


def matmul_residual(a, w, res, *, tm, tn):
    T, K = a.shape
    N = w.shape[1]
    return pl.pallas_call(
        _mmres_kernel,
        grid=(T // tm, N // tn),
        in_specs=[pl.BlockSpec((tm, K), lambda m, n: (m, 0)),
                  pl.BlockSpec((K, tn), lambda m, n: (0, n)),
                  pl.BlockSpec((tm, tn), lambda m, n: (m, n))],
        out_specs=pl.BlockSpec((tm, tn), lambda m, n: (m, n)),
        out_shape=jax.ShapeDtypeStruct((T, N), F32),
        compiler_params=_cparams(("parallel", "arbitrary")),
        name="matmul_residual",
    )(a, w, res)


def _prep_kernel(zg_ref, zq_ref, zc_ref, zs_ref, zw_ref, ct_ref, s1_ref, s2_ref, gn_ref,
                 qn_ref, kvp_ref, xc_ref, vts_ref, vtw_ref, gate_ref):
    ct, s1, s2 = ct_ref[...], s1_ref[...], s2_ref[...]
    half = NSA_ROT // 2
    gq = gn_ref[0:1, :]
    for h in range(NSA_HEADS):
        y = _rope_lanes(_rms(zq_ref[:, h * HEAD_DIM:(h + 1) * HEAD_DIM], gq), ct, s1, s2, half)
        qn_ref[:, h * HEAD_DIM:(h + 1) * HEAD_DIM] = (y * (LOG2E * HEAD_DIM ** -0.5)).astype(BF16)
    for br, zb_ref in enumerate((zc_ref, zs_ref, zw_ref)):
        for kv in range(2):
            for g in range(NSA_KV_GROUPS):
                c = (kv * NSA_KV_GROUPS + g) * HEAD_DIM
                y = zb_ref[:, c:c + HEAD_DIM]
                if kv == 0 and br > 0:
                    y = _rope_lanes(_rms(y, gn_ref[1 + br:2 + br, :]), ct, s1, s2, half)
                if br == 0:
                    xc_ref[kv * NSA_KV_GROUPS + g] = y.astype(BF16)
                elif kv == 0:
                    d = (br - 1) * NSA_KV_GROUPS * HEAD_DIM + g * HEAD_DIM
                    kvp_ref[:, d:d + HEAD_DIM] = y.astype(BF16)
                else:
                    (vts_ref if br == 1 else vtw_ref)[g] = y.T.astype(BF16)
    gate_ref[...] = _sigmoid(zg_ref[...])


def nsa_prep(z, tabs, gn, S, *, tm=VT_CHUNK):
    T = z.shape[0]
    ns = S // tm
    tab_spec = pl.BlockSpec((tm, LANES), lambda i: (i % ns, 0))
    vt_spec = pl.BlockSpec((None, NSA_KV_GROUPS, None, HEAD_DIM, tm), lambda i: (i // ns, 0, i % ns, 0, 0))
    vt_shape = jax.ShapeDtypeStruct((T // S, NSA_KV_GROUPS, ns, HEAD_DIM, tm), BF16)
    wq, wb = NSA_HEADS * HEAD_DIM, 2 * NSA_KV_GROUPS * HEAD_DIM
    npl = 2 * NSA_KV_GROUPS
    return pl.pallas_call(
        _prep_kernel,
        grid=(T // tm,),
        in_specs=[pl.BlockSpec((tm, LANES), lambda i: (i, OFF_G // LANES)),
                  pl.BlockSpec((tm, wq), lambda i: (i, OFF_Q // wq)),
                  pl.BlockSpec((tm, wb), lambda i: (i, OFF_KV // wb)),
                  pl.BlockSpec((tm, wb), lambda i: (i, OFF_KV // wb + 1)),
                  pl.BlockSpec((tm, wb), lambda i: (i, OFF_KV // wb + 2)),
                  tab_spec, tab_spec, tab_spec, pl.BlockSpec((8, LANES), lambda i: (0, 0))],
        out_specs=[pl.BlockSpec((tm, NSA_HEADS * HEAD_DIM), lambda i: (i, 0)),
                   pl.BlockSpec((tm, NSA_KVP_COLS), lambda i: (i, 0)),
                   pl.BlockSpec((None, npl, tm, HEAD_DIM), lambda i: (i // ns, 0, i % ns, 0)), vt_spec, vt_spec,
                   pl.BlockSpec((tm, LANES), lambda i: (i, 0))],
        out_shape=[jax.ShapeDtypeStruct((T, NSA_HEADS * HEAD_DIM), BF16),
                   jax.ShapeDtypeStruct((T, NSA_KVP_COLS), BF16),
                   jax.ShapeDtypeStruct((T // S, npl, S, HEAD_DIM), BF16), vt_shape, vt_shape,
                   jax.ShapeDtypeStruct((T, LANES), F32)],
        compiler_params=_cparams(("parallel",)),
        name="nsa_prep",
    )(z, z, z, z, z, *tabs, gn)


def _cmp_kernel(xk_ref, xv_ref, wk_ref, wv_ref, pek_ref, pev_ref, gk_ref, ct_ref, s1_ref, s2_ref,
                kct_ref, vc_ref):
    nc = xk_ref.shape[0]
    row = lax.broadcasted_iota(jnp.int32, (nc, HEAD_DIM), 0)

    def comp(x_ref, w_ref, pe_ref):
        w = w_ref[...]
        y = jnp.dot(x_ref[...], w, preferred_element_type=F32)
        ype = jnp.dot(pe_ref[...], w, preferred_element_type=F32)
        bias = ype[0:1, :HEAD_DIM] + ype[1:2, HEAD_DIM:]
        out = y[:, :HEAD_DIM] + pltpu.roll(y[:, HEAD_DIM:], nc - 1, 0) + bias
        return jnp.where(row < nc - 1, out, 0.0)

    k = _rms(comp(xk_ref, wk_ref, pek_ref), gk_ref[...])
    k = _rope_lanes(k, ct_ref[...], s1_ref[...], s2_ref[...], NSA_ROT // 2)
    kct_ref[...] = k.T.astype(BF16)
    vc_ref[...] = comp(xv_ref, wv_ref, pev_ref).astype(BF16)


def nsa_compress(x2, wk2, wv2, pek2, pev2, gk, tabs_cmp):
    B, _, nc, kk = x2.shape
    G = NSA_KV_GROUPS
    full = lambda shape: pl.BlockSpec(shape, lambda b, g: (0,) * len(shape))
    return pl.pallas_call(
        _cmp_kernel,
        grid=(B, G),
        in_specs=[pl.BlockSpec((None, None, nc, kk), lambda b, g: (b, g, 0, 0)),
                  pl.BlockSpec((None, None, nc, kk), lambda b, g: (b, G + g, 0, 0)),
                  full((kk, 2 * HEAD_DIM)), full((kk, 2 * HEAD_DIM)),
                  full((8, kk)), full((8, kk)), full((1, HEAD_DIM)),
                  full((nc, LANES)), full((nc, LANES)), full((nc, LANES))],
        out_specs=[pl.BlockSpec((None, None, HEAD_DIM, nc), lambda b, g: (b, g, 0, 0)),
                   pl.BlockSpec((None, None, nc, HEAD_DIM), lambda b, g: (b, g, 0, 0))],
        out_shape=[jax.ShapeDtypeStruct((B, G, HEAD_DIM, nc), BF16),
                   jax.ShapeDtypeStruct((B, G, nc, HEAD_DIM), BF16)],
        compiler_params=_cparams(("parallel", "parallel")),
        name="nsa_compress",
    )(x2, x2, wk2, wv2, pek2, pev2, gk, *tabs_cmp)


def _cattn_kernel(q_ref, kct_ref, vc_ref, ov_ref, oc_ref, sel_ref, *, tq):
    nc = vc_ref.shape[0]
    ns = sel_ref.shape[0]
    t0 = pl.program_id(2) * tq
    t_pos = t0 + lax.broadcasted_iota(jnp.int32, (tq, nc), 0)
    n_idx = lax.broadcasted_iota(jnp.int32, (tq, nc), 1)
    vis = (t_pos >= n_idx * CMP_STRIDE + (CMP_LEN - 1)) & (n_idx < nc - 1)
    bias = jnp.where(vis, 0.0, NEG_INF)
    kct = kct_ref[...]
    vc = vc_ref[...]
    psum = jnp.zeros((tq, nc), F32)
    for hh in range(NSA_HPG):
        s = jnp.dot(q_ref[:, hh * HEAD_DIM:(hh + 1) * HEAD_DIM], kct, preferred_element_type=F32) + bias
        m = jnp.max(s, axis=-1, keepdims=True)
        m = jnp.where(m == NEG_INF, 0.0, m)
        p = jnp.exp2(s - m)
        den = jnp.sum(p, axis=-1, keepdims=True)
        p = p * (1.0 / jnp.where(den > 0.0, den, 1.0))
        oc_ref[:, hh * HEAD_DIM:(hh + 1) * HEAD_DIM] = jnp.dot(
            p.astype(BF16), vc, preferred_element_type=F32).astype(oc_ref.dtype)
        psum = psum + p
    pt = psum.T
    hi = pt.astype(BF16)
    lo = (pt - hi.astype(F32)).astype(BF16)
    ov = ov_ref[...]
    imp = jnp.dot(ov, hi, preferred_element_type=F32) + jnp.dot(ov, lo, preferred_element_type=F32)
    blk = lax.broadcasted_iota(jnp.int32, (ns, tq), 0)
    cur = (t0 + lax.broadcasted_iota(jnp.int32, (ns, tq), 1)) // SLC_LEN
    forced = (blk == 0) | ((blk <= cur) & (blk > cur - N_LOCAL_SLC))
    val = jnp.where(blk > cur, NEG_INF, jnp.where(forced, FORCE_SCORE, imp))
    rank = jnp.zeros((ns, tq), F32)
    for i in range(ns):
        other = val[i:i + 1, :]
        ahead = (other > val) | ((other == val) & (blk > i))
        rank = rank + jnp.where(ahead, 1.0, 0.0)
    sel_ref[...] = jnp.where(rank < float(min(SLC_TOPK, ns)), 1.0, 0.0).astype(sel_ref.dtype)


def nsa_cmp_attention(qn, kct, vc, ov, B, S, *, tq):
    G = NSA_KV_GROUPS
    nq = S // tq
    nc = vc.shape[2]
    ns = S // SLC_LEN
    gw = NSA_HPG * HEAD_DIM
    return pl.pallas_call(
        functools.partial(_cattn_kernel, tq=tq),
        grid=(B, G, nq),
        in_specs=[pl.BlockSpec((tq, gw), lambda b, g, i: (b * nq + i, g)),
                  pl.BlockSpec((None, None, HEAD_DIM, nc), lambda b, g, i: (b, g, 0, 0)),
                  pl.BlockSpec((None, None, nc, HEAD_DIM), lambda b, g, i: (b, g, 0, 0)),
                  pl.BlockSpec((ns, nc), lambda b, g, i: (0, 0))],
        out_specs=[pl.BlockSpec((tq, gw), lambda b, g, i: (b * nq + i, g)),
                   pl.BlockSpec((None, None, ns, tq), lambda b, g, i: (b, g, 0, i))],
        out_shape=[jax.ShapeDtypeStruct((B * S, NSA_HEADS * HEAD_DIM), BF16),
                   jax.ShapeDtypeStruct((B, G, ns, S), BF16)],
        compiler_params=_cparams(("parallel", "parallel", "parallel")),
        name="nsa_cmp_attention",
    )(qn, kct, vc, ov)


def _flash_kernel(*refs, mode, hg, tq, tk, tv, dqk, dv):
    if mode == "sel":
        q_ref, k_ref, vt_ref, sel_ref, ex_ref, o_ref, qs, m_s, l_s, acc_s, sa, sb = refs
    else:
        q_ref, k_ref, vt_ref, o_ref, qs, m_s, l_s, acc_s, sa, sb = refs
    rows = hg * tq
    q0 = pl.program_id(2) * tq
    cd = q0 // tk
    for hh in range(hg):
        qs[hh * tq:(hh + 1) * tq, :] = q_ref[:, hh * dqk:(hh + 1) * dqk]

    def scores(c, kind):
        start = pl.multiple_of(c * tk, tk)
        s = lax.dot_general(k_ref[pl.ds(start, tk), :], qs[...], (((1,), (1,)), ((), ())),
                            preferred_element_type=F32)
        ok = None
        if kind == "diag":
            k_pos = start + lax.broadcasted_iota(jnp.int32, (tk, tq), 0)
            q_pos = q0 + lax.broadcasted_iota(jnp.int32, (tk, tq), 1)
            ok = k_pos <= q_pos
        if mode == "sel":
            chosen = jnp.dot(ex_ref[c], sel_ref[...], preferred_element_type=F32) > 0.5
            ok = chosen if ok is None else (ok & chosen)
        if ok is not None:
            bias = jnp.where(ok, 0.0, NEG_INF)
            s = s + (jnp.concatenate([bias] * hg, axis=1) if hg > 1 else bias)
        return s

    def update(c, s, carry):
        m, l, acc = carry
        m_new = jnp.maximum(m, jnp.max(s, axis=0, keepdims=True))
        p = jnp.exp2(s - m_new)
        alpha = jnp.exp2(m - m_new)
        l = alpha * l + jnp.sum(p, axis=0, keepdims=True)
        r = tk // tv
        vt = vt_ref[c] if r == 1 else jnp.concatenate([vt_ref[c * r + j] for j in range(r)], axis=1)
        acc = alpha * acc + jnp.dot(vt, p.astype(BF16), preferred_element_type=F32)
        return m_new, l, acc

    def load():
        return m_s[...], l_s[...], acc_s[...]

    def store(carry):
        m_s[...], l_s[...], acc_s[...] = carry

    nd = max(tq // tk, 1)
    last = k_ref.shape[0] // tk - 1
    carry = (jnp.full((1, rows), NEG_INF, F32), jnp.zeros((1, rows), F32), jnp.zeros((dv, rows), F32))
    s_next = scores(cd, "diag")
    for j in range(nd):
        s_cur = s_next
        if j + 1 < nd:
            s_next = scores(cd + j + 1, "diag")
        else:
            sa[...] = scores(0, "full")
        carry = update(cd + j, s_cur, carry)
    store(carry)

    def pairs(first, count):
        carry = load()
        for j in range(count):
            c = first + 2 * j
            sb[...] = scores(c + 1, "full")
            carry = update(c, sa[...], carry)
            sa[...] = scores(jnp.minimum(c + 2, last), "full")
            carry = update(c + 1, sb[...], carry)
        store(carry)

    def four(i, _):
        pairs(4 * i, 2)
        return 0

    lax.fori_loop(0, cd // 4, four, 0)

    @pl.when(cd % 4 >= 2)
    def _():
        pairs((cd // 4) * 4, 1)

    @pl.when(cd % 2 == 1)
    def _():
        store(update(cd - 1, sa[...], load()))

    o = acc_s[...] * (1.0 / l_s[...])
    for hh in range(hg):
        o_ref[:, hh * dv:(hh + 1) * dv] = o[:, hh * tq:(hh + 1) * tq].T.astype(o_ref.dtype)


def flash_attention(q, k, vt, B, S, *, mode, n_groups, hg, dqk, dv, kcol0, tq, tk, sel_t=None, expand=None):
    nq = S // tq
    nch = S // tk
    tv = vt.shape[-1]
    assert (tk % tq == 0 or tq % tk == 0) and S % tk == 0 and S % tq == 0 and tq % LANES == 0 and tk % tv == 0
    in_specs = [pl.BlockSpec((tq, hg * dqk), lambda b, g, i: (b * nq + i, g)),
                pl.BlockSpec((S, dqk), lambda b, g, i: (b, kcol0 + g)),
                pl.BlockSpec((None, None, S // tv, dv, tv), lambda b, g, i: (b, g, 0, 0, 0))]
    args = [q, k, vt]
    if mode == "sel":
        ns = S // SLC_LEN
        in_specs += [pl.BlockSpec((None, None, ns, tq), lambda b, g, i: (b, g, 0, i)),
                     pl.BlockSpec((nch, tk, ns), lambda b, g, i: (0, 0, 0))]
        args += [sel_t, expand]
    return pl.pallas_call(
        functools.partial(_flash_kernel, mode=mode, hg=hg, tq=tq, tk=tk, tv=tv, dqk=dqk, dv=dv),
        grid=(B, n_groups, nq),
        in_specs=in_specs,
        out_specs=pl.BlockSpec((tq, hg * dv), lambda b, g, i: (b * nq + i, g)),
        out_shape=jax.ShapeDtypeStruct((B * S, n_groups * hg * dv), BF16),
        scratch_shapes=[pltpu.VMEM((hg * tq, dqk), BF16),
                        pltpu.VMEM((1, hg * tq), F32),
                        pltpu.VMEM((1, hg * tq), F32),
                        pltpu.VMEM((dv, hg * tq), F32),
                        pltpu.VMEM((tk, hg * tq), F32),
                        pltpu.VMEM((tk, hg * tq), F32)],
        compiler_params=_cparams(("parallel", "parallel", "arbitrary")),
        name="flash_" + mode,
    )(*args)


def _window_kernel(q_ref, k_ref, vt_ref, oc_ref, os_ref, gate_ref, o_ref, qs, *, hg, tq, dqk, dv):
    nk = WINDOW + tq
    q0 = pl.program_id(2) * tq
    start = pl.multiple_of(jnp.maximum(q0 - WINDOW, 0), tq)
    for hh in range(hg):
        qs[hh * tq:(hh + 1) * tq, :] = q_ref[:, hh * dqk:(hh + 1) * dqk]
    s = lax.dot_general(k_ref[pl.ds(start, nk), :], qs[...], (((1,), (1,)), ((), ())),
                        preferred_element_type=F32)
    k_pos = start + lax.broadcasted_iota(jnp.int32, (nk, tq), 0)
    q_pos = q0 + lax.broadcasted_iota(jnp.int32, (nk, tq), 1)
    bias = jnp.where((k_pos <= q_pos) & (q_pos - k_pos < WINDOW), 0.0, NEG_INF)
    s = s + jnp.concatenate([bias] * hg, axis=1)
    p = jnp.exp2(s - jnp.max(s, axis=0, keepdims=True))
    l = jnp.sum(p, axis=0, keepdims=True)
    pb = p.astype(BF16)
    c0 = start // tq
    o = jnp.zeros((dv, hg * tq), F32)
    for j in range(nk // tq):
        o = o + jnp.dot(vt_ref[c0 + j], pb[j * tq:(j + 1) * tq, :], preferred_element_type=F32)
    o = o * (1.0 / l)
    first_group = pl.program_id(1) == 0
    n_heads = 2 * hg

    def gate(br, hh):
        lo = gate_ref[:, br * n_heads + hh:br * n_heads + hh + 1]
        hi = gate_ref[:, br * n_heads + hg + hh:br * n_heads + hg + hh + 1]
        return jnp.where(first_group, lo, hi)

    for hh in range(hg):
        sl = slice(hh * dv, (hh + 1) * dv)
        mixed = (gate(0, hh) * oc_ref[:, sl].astype(F32) + gate(1, hh) * os_ref[:, sl].astype(F32)
                 + gate(2, hh) * o[:, hh * tq:(hh + 1) * tq].T)
        o_ref[:, sl] = mixed.astype(o_ref.dtype)


def window_attention(q, k, vt, oc, os_, gates, B, S, *, n_groups, hg, dqk, dv, kcol0, tq):
    assert n_groups == NSA_KV_GROUPS
    nq = S // tq
    assert WINDOW % tq == 0 and S >= WINDOW + tq and tq % LANES == 0
    return pl.pallas_call(
        functools.partial(_window_kernel, hg=hg, tq=tq, dqk=dqk, dv=dv),
        grid=(B, n_groups, nq),
        in_specs=[pl.BlockSpec((tq, hg * dqk), lambda b, g, i: (b * nq + i, g)),
                  pl.BlockSpec((S, dqk), lambda b, g, i: (b, kcol0 + g)),
                  pl.BlockSpec((None, None, nq, dv, tq), lambda b, g, i: (b, g, 0, 0, 0)),
                  pl.BlockSpec((tq, hg * dv), lambda b, g, i: (b * nq + i, g)),
                  pl.BlockSpec((tq, hg * dv), lambda b, g, i: (b * nq + i, g)),
                  pl.BlockSpec((tq, LANES), lambda b, g, i: (b * nq + i, 0))],
        out_specs=pl.BlockSpec((tq, hg * dv), lambda b, g, i: (b * nq + i, g)),
        out_shape=jax.ShapeDtypeStruct((B * S, n_groups * hg * dv), BF16),
        scratch_shapes=[pltpu.VMEM((hg * tq, dqk), BF16)],
        compiler_params=_cparams(("parallel", "parallel", "arbitrary")),
        name="window_attention",
    )(q, k, vt, oc, os_, gates)


def _mla_q_kernel(z_ref, ga_ref, w_ref, gh_ref, ct_ref, s1_ref, s2_ref, o_ref):
    h = _rms(z_ref[...], ga_ref[...]).astype(BF16)
    y = jnp.dot(h, w_ref[...], preferred_element_type=F32)
    ct, s1, s2 = ct_ref[...], s1_ref[...], s2_ref[...]
    scale = LOG2E * (QK_NOPE + QK_ROPE) ** -0.5
    for hd in range(MLA_HEADS):
        c = hd * MLA_QK_PAD
        nope = _rms(y[:, c:c + QK_NOPE], gh_ref[0:1, :])
        r = y[:, c + QK_NOPE:c + MLA_QK_PAD]
        ms = jnp.sum(r * r, axis=-1, keepdims=True) * (1.0 / QK_ROPE)
        r = _rope_lanes(r * lax.rsqrt(ms + EPS) * gh_ref[1:2, :], ct, s1, s2, QK_ROPE // 2)
        o_ref[:, c:c + QK_NOPE] = (nope * scale).astype(BF16)
        o_ref[:, c + QK_NOPE:c + MLA_QK_PAD] = (r * scale).astype(BF16)


def mla_q_proj(z, ga, wq, gh, tabs, S, *, tm):
    T = z.shape[0]
    ns = S // tm
    tab_spec = pl.BlockSpec((tm, LANES), lambda i: (i % ns, 0))
    nout = MLA_HEADS * MLA_QK_PAD
    return pl.pallas_call(
        _mla_q_kernel,
        grid=(T // tm,),
        in_specs=[pl.BlockSpec((tm, Q_LORA), lambda i: (i, OFF_QA // Q_LORA)),
                  pl.BlockSpec((1, Q_LORA), lambda i: (0, 0)),
                  pl.BlockSpec((Q_LORA, nout), lambda i: (0, 0)),
                  pl.BlockSpec((8, LANES), lambda i: (0, 0)),
                  tab_spec, tab_spec, tab_spec],
        out_specs=pl.BlockSpec((tm, nout), lambda i: (i, 0)),
        out_shape=jax.ShapeDtypeStruct((T, nout), BF16),
        compiler_params=_cparams(("parallel",)),
        name="mla_q_proj",
    )(z, ga, wq, gh, *tabs)


def _mla_kv_kernel(z_ref, zr_ref, ga_ref, w_ref, gh_ref, ct_ref, s1_ref, s2_ref, k_ref, v_ref):
    h = _rms(z_ref[...], ga_ref[...]).astype(BF16)
    y = jnp.dot(h, w_ref[...], preferred_element_type=F32)
    r = zr_ref[...]
    ms = jnp.sum(r * r, axis=-1, keepdims=True) * (1.0 / QK_ROPE)
    r = _rope_lanes(r * lax.rsqrt(ms + EPS) * gh_ref[1:2, :], ct_ref[...], s1_ref[...], s2_ref[...],
                    QK_ROPE // 2).astype(BF16)
    nv = MLA_HEADS * QK_NOPE
    for hd in range(MLA_HEADS):
        c = hd * MLA_QK_PAD
        k_ref[:, c:c + QK_NOPE] = _rms(y[:, hd * QK_NOPE:(hd + 1) * QK_NOPE], gh_ref[0:1, :]).astype(BF16)
        k_ref[:, c + QK_NOPE:c + MLA_QK_PAD] = r
    for hd in range(MLA_HEADS):
        v_ref[hd] = y[:, nv + hd * V_HEAD:nv + (hd + 1) * V_HEAD].T.astype(BF16)


def mla_kv_proj(z, ga, wkv, gh, tabs, S, *, tm=VT_CHUNK):
    T = z.shape[0]
    ns = S // tm
    tab_spec = pl.BlockSpec((tm, LANES), lambda i: (i % ns, 0))
    nk = MLA_HEADS * MLA_QK_PAD
    nv = MLA_HEADS * V_HEAD
    return pl.pallas_call(
        _mla_kv_kernel,
        grid=(T // tm,),
        in_specs=[pl.BlockSpec((tm, KV_LORA), lambda i: (i, OFF_KVA // KV_LORA)),
                  pl.BlockSpec((tm, LANES), lambda i: (i, OFF_KR // LANES)),
                  pl.BlockSpec((1, KV_LORA), lambda i: (0, 0)),
                  pl.BlockSpec((KV_LORA, MLA_HEADS * (QK_NOPE + V_HEAD)), lambda i: (0, 0)),
                  pl.BlockSpec((8, LANES), lambda i: (0, 0)),
                  tab_spec, tab_spec, tab_spec],
        out_specs=[pl.BlockSpec((tm, nk), lambda i: (i, 0)),
                   pl.BlockSpec((None, MLA_HEADS, None, V_HEAD, tm), lambda i: (i // ns, 0, i % ns, 0, 0))],
        out_shape=[jax.ShapeDtypeStruct((T, nk), BF16),
                   jax.ShapeDtypeStruct((T // S, MLA_HEADS, ns, V_HEAD, tm), BF16)],
        compiler_params=_cparams(("parallel",)),
        name="mla_kv_proj",
    )(z, z, ga, wkv, gh, *tabs)


def _mix_kernel(a_ref, ob_ref, wa_ref, wb_ref, za_ref, zb_ref, o_ref):
    pa = jnp.dot(a_ref[...], wa_ref[...].astype(BF16), preferred_element_type=F32)
    pb = jnp.dot(ob_ref[...], wb_ref[...].astype(BF16), preferred_element_type=F32)
    o_ref[...] = (_sigmoid(za_ref[...]) * pa + _sigmoid(zb_ref[...]) * pb).astype(o_ref.dtype)


def gated_mix(oa, ob, wa, wb, z, *, tm, tn):
    T, K = oa.shape
    N = wa.shape[1]
    row = lambda w: pl.BlockSpec((tm, w), lambda m, n: (m, 0))
    return pl.pallas_call(
        _mix_kernel,
        grid=(T // tm, N // tn),
        in_specs=[row(K), row(K),
                  pl.BlockSpec((K, tn), lambda m, n: (0, n)),
                  pl.BlockSpec((K, tn), lambda m, n: (0, n)),
                  pl.BlockSpec((tm, tn), lambda m, n: (m, OFF_M // tn + n)),
                  pl.BlockSpec((tm, tn), lambda m, n: (m, (OFF_M + D_MODEL) // tn + n))],
        out_specs=pl.BlockSpec((tm, tn), lambda m, n: (m, n)),
        out_shape=jax.ShapeDtypeStruct((T, N), BF16),
        compiler_params=_cparams(("parallel", "parallel")),
        name="gated_mix",
    )(oa, ob, wa, wb, z, z)


def _ffn_up_kernel(x_ref, g_ref, w1_ref, w3_ref, o_ref, h_scr):
    @pl.when(pl.program_id(1) == 0)
    def _():
        h_scr[...] = _rms(x_ref[...], g_ref[...]).astype(BF16)

    h = h_scr[...]
    a = jnp.dot(h, w1_ref[...].astype(BF16), preferred_element_type=F32)
    b = jnp.dot(h, w3_ref[...].astype(BF16), preferred_element_type=F32)
    o_ref[...] = (a * _sigmoid(a) * b).astype(o_ref.dtype)


def ffn_up(x, g, w1, w3, *, tm, tn):
    T, K = x.shape
    N = w1.shape[1]
    return pl.pallas_call(
        _ffn_up_kernel,
        grid=(T // tm, N // tn),
        in_specs=[pl.BlockSpec((tm, K), lambda m, n: (m, 0)),
                  pl.BlockSpec((1, K), lambda m, n: (0, 0)),
                  pl.BlockSpec((K, tn), lambda m, n: (0, n)),
                  pl.BlockSpec((K, tn), lambda m, n: (0, n))],
        out_specs=pl.BlockSpec((tm, tn), lambda m, n: (m, n)),
        out_shape=jax.ShapeDtypeStruct((T, N), BF16),
        scratch_shapes=[pltpu.VMEM((tm, K), BF16)],
        compiler_params=_cparams(("parallel", "arbitrary")),
        name="ffn_up",
    )(x, g.reshape(1, K), w1, w3)


def _pack_bf16_pairs(h):
    k = h.shape[1] // 2
    hi = lax.bitcast_convert_type(h[:, :k].astype(jnp.bfloat16).astype(F32), jnp.uint32)
    lo = lax.bitcast_convert_type(h[:, k:].astype(jnp.bfloat16).astype(F32), jnp.uint32)
    return lax.bitcast_convert_type(hi | (lo >> 16), F32)


def _unpack_bf16_pairs(xp):
    xp = lax.bitcast_convert_type(xp, jnp.uint32)
    hi = lax.bitcast_convert_type(xp & jnp.uint32(0xFFFF0000), F32)
    lo = lax.bitcast_convert_type(xp << 16, F32)
    return hi.astype(BF16), lo.astype(BF16)


def _router_kernel(x_ref, g_ref, wr_ref, h_ref, idx_ref, gate_ref):
    h = _rms(x_ref[...], g_ref[...])
    is_pad = pl.program_id(0) == pl.num_programs(0) - 1
    h_ref[...] = jnp.where(is_pad, 0.0, _pack_bf16_pairs(h))
    logits = jnp.dot(h, wr_ref[...], preferred_element_type=F32, precision=lax.Precision.HIGHEST)
    lane = lax.broadcasted_iota(jnp.int32, logits.shape, 1).astype(F32)
    logits = jnp.where(lane < float(N_EXPERTS), logits, NEG_INF)
    m1 = jnp.max(logits, axis=-1, keepdims=True)
    i1 = jnp.min(jnp.where(logits == m1, lane, float(LANES)), axis=-1, keepdims=True)
    rest = jnp.where(lane == i1, NEG_INF, logits)
    m2 = jnp.max(rest, axis=-1, keepdims=True)
    i2 = jnp.min(jnp.where(rest == m2, lane, float(LANES)), axis=-1, keepdims=True)
    e = jnp.exp(m2 - m1)
    den = 1.0 + e
    idx_ref[...] = jnp.where(lane == 0.0, i1, jnp.where(lane == 1.0, i2, 0.0)).astype(jnp.int32)
    gate_ref[...] = jnp.where(lane == 0.0, 1.0 / den, jnp.where(lane == 1.0, e / den, 0.0))


def router(x, g, wr_pad, *, tm):
    T, K = x.shape
    nt = T // tm
    return pl.pallas_call(
        _router_kernel,
        grid=(nt + 1,),
        in_specs=[pl.BlockSpec((tm, K), lambda i: (jnp.minimum(i, nt - 1), 0)),
                  pl.BlockSpec((1, K), lambda i: (0, 0)),
                  pl.BlockSpec((K, LANES), lambda i: (0, 0))],
        out_specs=[pl.BlockSpec((tm, K // 2), lambda i: (i, 0)),
                   pl.BlockSpec((tm, LANES), lambda i: (jnp.minimum(i, nt - 1), 0)),
                   pl.BlockSpec((tm, LANES), lambda i: (jnp.minimum(i, nt - 1), 0))],
        out_shape=[jax.ShapeDtypeStruct((T + tm, K // 2), F32),
                   jax.ShapeDtypeStruct((T, LANES), jnp.int32),
                   jax.ShapeDtypeStruct((T, LANES), F32)],
        compiler_params=_cparams(("arbitrary",)),
        name="router",
    )(x, g.reshape(1, K), wr_pad)


def _moe_weight_stream(w_hbms, stages, casts, sems, be_ref, nu_ref, gs_ref, ne_ref, lg_ref, tn):
    n = pl.program_id(0)
    r = pl.program_id(1)
    used = r < nu_ref[0]

    def copies(e, nt):
        c0 = pl.multiple_of(nt * tn, tn)
        return [pltpu.make_async_copy(w.at[e, :, pl.ds(c0, tn)], st, sems.at[i])
                for i, (w, st) in enumerate(zip(w_hbms, stages))]

    @pl.when((n == 0) & (r == 0))
    def _():
        for c in copies(be_ref[0], 0):
            c.start()

    @pl.when(used & (gs_ref[r] == 1))
    def _():
        for c in copies(be_ref[r], n):
            c.wait()
        for st, wb in zip(stages, casts):
            wb[...] = st[...].astype(BF16)
        last = lg_ref[r] == 1

        @pl.when(jnp.logical_not(last & (n == pl.num_programs(0) - 1)))
        def _():
            for c in copies(ne_ref[r], n + last.astype(jnp.int32)):
                c.start()

    return used


def _moe_up_kernel(be_ref, nu_ref, gs_ref, ne_ref, lg_ref, x_ref, w1_hbm, w3_hbm, o_ref,
                   st1, st3, w1_s, w3_s, sems, *, tn):
    used = _moe_weight_stream((w1_hbm, w3_hbm), (st1, st3), (w1_s, w3_s), sems,
                              be_ref, nu_ref, gs_ref, ne_ref, lg_ref, tn)

    @pl.when(used)
    def _():
        xa, xb = _unpack_bf16_pairs(x_ref[...])
        k2 = xa.shape[1]
        a = (jnp.dot(xa, w1_s[:k2, :], preferred_element_type=F32)
             + jnp.dot(xb, w1_s[k2:, :], preferred_element_type=F32))
        b = (jnp.dot(xa, w3_s[:k2, :], preferred_element_type=F32)
             + jnp.dot(xb, w3_s[k2:, :], preferred_element_type=F32))
        o_ref[...] = (a * _sigmoid(a) * b).astype(o_ref.dtype)

    @pl.when(jnp.logical_not(used))
    def _():
        o_ref[...] = jnp.zeros(o_ref.shape, o_ref.dtype)


def moe_up(tables, xb, w1, w3, *, tmb, tn):
    n_slot = xb.shape[0]
    K, N = w1.shape[1], w1.shape[2]
    row = lambda r, nu: jnp.minimum(r, nu[0] - 1)
    return pl.pallas_call(
        functools.partial(_moe_up_kernel, tn=tn),
        grid_spec=pltpu.PrefetchScalarGridSpec(
            num_scalar_prefetch=5,
            grid=(N // tn, n_slot // tmb),
            in_specs=[pl.BlockSpec((tmb, K // 2), lambda n, r, be, nu, gs, ne, lg: (row(r, nu), 0)),
                      pl.BlockSpec(memory_space=pl.ANY), pl.BlockSpec(memory_space=pl.ANY)],
            out_specs=pl.BlockSpec((tmb, tn), lambda n, r, be, nu, gs, ne, lg: (r, n)),
            scratch_shapes=[pltpu.VMEM((K, tn), F32), pltpu.VMEM((K, tn), F32),
                            pltpu.VMEM((K, tn), BF16), pltpu.VMEM((K, tn), BF16),
                            pltpu.SemaphoreType.DMA((2,))]),
        out_shape=jax.ShapeDtypeStruct((n_slot, N), BF16),
        compiler_params=_cparams(("arbitrary", "arbitrary")),
        name="moe_up",
    )(*tables, xb, w1, w3)


def _moe_down_kernel(be_ref, nu_ref, gs_ref, ne_ref, lg_ref, a_ref, w2_hbm, o_ref, st2, w2_s, sems, *, tn):
    used = _moe_weight_stream((w2_hbm,), (st2,), (w2_s,), sems, be_ref, nu_ref, gs_ref, ne_ref, lg_ref, tn)

    @pl.when(used)
    def _():
        o_ref[...] = _pack_bf16_pairs(jnp.dot(a_ref[...], w2_s[...], preferred_element_type=F32))

    @pl.when(jnp.logical_not(used))
    def _():
        o_ref[...] = jnp.zeros(o_ref.shape, o_ref.dtype)


def moe_down(tables, act, w2, *, tmb, tn):
    n_slot, K = act.shape
    N = w2.shape[2]
    row = lambda r, nu: jnp.minimum(r, nu[0] - 1)
    return pl.pallas_call(
        functools.partial(_moe_down_kernel, tn=tn),
        grid_spec=pltpu.PrefetchScalarGridSpec(
            num_scalar_prefetch=5,
            grid=(N // tn, n_slot // tmb),
            in_specs=[pl.BlockSpec((tmb, K), lambda n, r, be, nu, gs, ne, lg: (row(r, nu), 0)),
                      pl.BlockSpec(memory_space=pl.ANY)],
            out_specs=pl.BlockSpec((tmb, tn // 2), lambda n, r, be, nu, gs, ne, lg: (r, n)),
            scratch_shapes=[pltpu.VMEM((K, tn), F32), pltpu.VMEM((K, tn), BF16),
                            pltpu.SemaphoreType.DMA((1,))]),
        out_shape=jax.ShapeDtypeStruct((n_slot, N // 2), F32),
        compiler_params=_cparams(("arbitrary", "arbitrary")),
        name="moe_down",
    )(*tables, act, w2)


def _combine_kernel(x_ref, g_ref, y0_ref, y1_ref, o_ref, *, tn):
    g0, g1 = g_ref[:, 0:1], g_ref[:, 1:2]
    h = tn // 2

    def halves(y_ref, n):
        u = lax.bitcast_convert_type(y_ref[:, n * h:(n + 1) * h], jnp.uint32)
        return (lax.bitcast_convert_type(u & jnp.uint32(0xFFFF0000), F32),
                lax.bitcast_convert_type(u << 16, F32))

    for n in range(x_ref.shape[1] // tn):
        a_hi, a_lo = halves(y0_ref, n)
        b_hi, b_lo = halves(y1_ref, n)
        lo_cols = slice(n * tn, n * tn + h)
        hi_cols = slice(n * tn + h, (n + 1) * tn)
        o_ref[:, lo_cols] = x_ref[:, lo_cols] + g0 * a_hi + g1 * b_hi
        o_ref[:, hi_cols] = x_ref[:, hi_cols] + g0 * a_lo + g1 * b_lo


def moe_combine(x, gate, y0, y1, *, tn, tm):
    T, N = x.shape
    return pl.pallas_call(
        functools.partial(_combine_kernel, tn=tn),
        grid=(T // tm,),
        in_specs=[pl.BlockSpec((tm, N), lambda i: (i, 0)),
                  pl.BlockSpec((tm, LANES), lambda i: (i, 0)),
                  pl.BlockSpec((tm, N // 2), lambda i: (i, 0)),
                  pl.BlockSpec((tm, N // 2), lambda i: (i, 0))],
        out_specs=pl.BlockSpec((tm, N), lambda i: (i, 0)),
        out_shape=jax.ShapeDtypeStruct((T, N), F32),
        compiler_params=_cparams(("parallel",)),
        name="moe_combine",
    )(x, gate, y0, y1)


def _rope_tabs(pos, rot_dim, n_rows):
    half = rot_dim // 2
    inv = 1.0 / (ROPE_THETA ** (jnp.arange(0, rot_dim, 2, dtype=F32) / rot_dim))
    ang = jnp.asarray(pos).astype(F32)[:, None] * inv[None, :]
    c, s = jnp.cos(ang), jnp.sin(ang)
    z = jnp.zeros_like(c)
    pad = lambda a, fill: jnp.pad(a, ((0, n_rows - a.shape[0]), (0, LANES - a.shape[1])), constant_values=fill)
    return pad(jnp.concatenate([c, c], 1), 1.0), pad(jnp.concatenate([-s, z], 1), 0.0), \
        pad(jnp.concatenate([z, s], 1), 0.0)


def _mla_tabs(S):
    ct, s1, s2 = _rope_tabs(jnp.arange(S), QK_ROPE, S)
    lane = jnp.arange(LANES)[None, :]
    return jnp.where(lane < QK_ROPE, ct, 0.0), s1, s2


def _pad_cols(w, n):
    return jnp.pad(w, ((0, 0), (0, n - w.shape[1])))


def _layout_w_in(w):
    sp = np.cumsum([0, NSA_HEADS * HEAD_DIM, NSA_KV_COLS, 3 * NSA_HEADS, Q_LORA, KV_LORA, QK_ROPE, 2 * D_MODEL])
    q, kv, g, qa, kva, kr, m = [w[:, sp[i]:sp[i + 1]] for i in range(7)]
    return jnp.concatenate([qa, _pad_cols(g, LANES), _pad_cols(kr, LANES), q, kva, kv, m], axis=1).astype(BF16)


def _layout_w_q_b(w):
    w = w.reshape(Q_LORA, MLA_HEADS, QK_NOPE + QK_ROPE)
    w = jnp.pad(w, ((0, 0), (0, 0), (0, MLA_QK_PAD - QK_NOPE - QK_ROPE)))
    return w.reshape(Q_LORA, MLA_HEADS * MLA_QK_PAD).astype(BF16)


def _layout_w_kv_b(w):
    w = w.reshape(KV_LORA, MLA_HEADS, QK_NOPE + V_HEAD)
    return jnp.concatenate([w[:, :, :QK_NOPE].reshape(KV_LORA, -1), w[:, :, QK_NOPE:].reshape(KV_LORA, -1)],
                           axis=1).astype(BF16)


def _layout_w_cmp(w):
    h = CMP_LEN // 2
    return jnp.concatenate([w[:h].reshape(h * HEAD_DIM, HEAD_DIM), w[h:].reshape(h * HEAD_DIM, HEAD_DIM)],
                           axis=1).astype(BF16)


def _layout_pe(pe):
    return jnp.pad(pe.reshape(2, (CMP_LEN // 2) * HEAD_DIM), ((0, 6), (0, 0))).astype(BF16)


def _overlap(nc, ns):
    n = np.arange(nc)[None, :] * CMP_STRIDE
    j = np.arange(ns)[:, None] * SLC_LEN
    ov = (n <= j + SLC_LEN - 1) & (j <= n + CMP_LEN - 1) & (np.arange(nc)[None, :] < nc - 1)
    return jnp.asarray(ov.astype(np.float32), BF16)


def _expand(ns, S, tk):
    e = ((np.arange(S // tk)[:, None, None] * tk + np.arange(tk)[None, :, None]) // SLC_LEN
         == np.arange(ns)[None, None, :])
    return jnp.asarray(e.astype(np.float32), BF16)


def _layout_layers(p):
    depth = p['w_in'].shape[0]
    lane_row = lambda g: jnp.pad(g, ((0, 0), (0, LANES - g.shape[-1])))[:, None, :]
    z4 = jnp.zeros((depth, 4, HEAD_DIM), F32)
    z6 = jnp.zeros((depth, 6, LANES), F32)
    return dict(
        gn=jnp.concatenate([p['nsa_q_norm_g'][:, None, :], p['nsa_k_norm_g'], z4], axis=1),
        w_ck=jax.vmap(_layout_w_cmp)(p['w_cmp_k']), w_cv=jax.vmap(_layout_w_cmp)(p['w_cmp_v']),
        pe_k=jax.vmap(_layout_pe)(p['cmp_pe_k']), pe_v=jax.vmap(_layout_pe)(p['cmp_pe_v']),
        gh_q=jnp.concatenate([p['mla_q_norm_g'][:, None, :], lane_row(p['mla_qr_norm_g']), z6], axis=1),
        gh_k=jnp.concatenate([p['mla_k_norm_g'][:, None, :], lane_row(p['mla_kr_norm_g']), z6], axis=1),
        w_qb=jax.vmap(_layout_w_q_b)(p['w_q_b']), w_kvb=jax.vmap(_layout_w_kv_b)(p['w_kv_b']))


def _attention_block(x2d, B, S, p, pw, l, tabs):
    T = x2d.shape[0]
    z = norm_matmul(x2d, p['attn_norm_g'][l], _layout_w_in(p['w_in'][l]), tm=min(1024, T), tn=1024)
    qn, kvp, xc, vts, vtw, gates = nsa_prep(z, tabs['tok'], pw['gn'][l], S)

    nc = S // CMP_STRIDE
    x2 = xc.reshape(B, 2 * NSA_KV_GROUPS, nc, CMP_STRIDE * HEAD_DIM)
    kct, vc = nsa_compress(x2, pw['w_ck'][l], pw['w_cv'][l], pw['pe_k'][l], pw['pe_v'][l],
                           p['nsa_k_norm_g'][l][0:1], tabs['cmp'])
    ns = S // SLC_LEN
    oc, sel_t = nsa_cmp_attention(qn, kct, vc, _overlap(nc, ns), B, S, tq=min(256, S))
    nsa = dict(n_groups=NSA_KV_GROUPS, hg=NSA_HPG, dqk=HEAD_DIM, dv=HEAD_DIM)
    tks = min(512, S)
    os_ = flash_attention(qn, kvp, vts, B, S, mode="sel", kcol0=0, tq=256, tk=tks,
                          sel_t=sel_t, expand=_expand(ns, S, tks), **nsa)
    oa = window_attention(qn, kvp, vtw, oc, os_, gates, B, S, kcol0=NSA_KV_GROUPS, tq=VT_CHUNK, **nsa)

    qm = mla_q_proj(z, p['mla_qa_norm_g'][l][None], pw['w_qb'][l], pw['gh_q'][l], tabs['mla'], S, tm=256)
    km, vmt = mla_kv_proj(z, p['mla_kva_norm_g'][l][None], pw['w_kvb'][l], pw['gh_k'][l], tabs['mla'], S)
    tkm = min(512, S)
    ob = flash_attention(qm, km, vmt, B, S, mode="causal",
                         n_groups=MLA_HEADS, hg=1, dqk=MLA_QK_PAD, dv=V_HEAD, kcol0=0, tq=min(1024, S), tk=tkm)

    tm = min(2048, T)
    mix = gated_mix(oa, ob, p['w_proj_nsa'][l], p['w_proj_mla'][l], z, tm=tm, tn=512)
    return matmul_residual(mix, p['w_out'][l], x2d, tm=min(2048, T), tn=512)


def _dense_ffn(x2d, g, w1, w3, w2):
    T = x2d.shape[0]
    act = ffn_up(x2d, g, w1.astype(BF16), w3.astype(BF16), tm=min(1024, T), tn=1024)
    return matmul_residual(act, w2.astype(BF16), x2d, tm=min(1024, T), tn=256)


def _cumsum_rows(oh, blk=128):
    A, E = oh.shape
    nb = A // blk
    x = oh.astype(F32).reshape(nb, blk, E)
    within = jnp.einsum('ij,bje->bie', jnp.tril(jnp.ones((blk, blk), F32)), x)
    before = jnp.tril(jnp.ones((nb, nb), F32), -1) @ within[:, -1, :]
    return (within + before[:, None, :]).astype(jnp.int32).reshape(A, E)


def _moe_ffn(x2d, g, w_router, w1, w3, w2, *, tmb=512):
    T = x2d.shape[0]
    hp, idx, gate = router(x2d, g, _pad_cols(w_router, LANES), tm=256)
    A = T * TOP_K
    e_flat = idx[:, :TOP_K].reshape(A)
    tok_flat = jnp.repeat(jnp.arange(T, dtype=jnp.int32), TOP_K)
    oh = (e_flat[:, None] == jnp.arange(N_EXPERTS)[None, :]).astype(jnp.int32)
    csum = _cumsum_rows(oh)
    rank = jnp.sum(oh * csum, axis=1) - 1
    counts = csum[-1]
    padded = (counts + tmb - 1) // tmb * tmb
    pad_end = jnp.cumsum(padded)
    dest = (pad_end - padded)[e_flat] + rank
    n_blk = -(-A // tmb) + N_EXPERTS
    n_slot = n_blk * tmb
    slot_tok = jnp.full((n_slot,), T, jnp.int32).at[dest].set(tok_flat)
    eidx = jnp.arange(N_EXPERTS, dtype=jnp.int32)
    blk = jnp.arange(n_blk, dtype=jnp.int32)
    blk_exp = jnp.minimum(jnp.sum((pad_end[None, :] <= (blk * tmb)[:, None]).astype(jnp.int32), axis=1),
                          N_EXPERTS - 1).astype(jnp.int32)
    n_used = (pad_end[-1:] // tmb).astype(jnp.int32)
    present = counts > 0
    first_e = jnp.min(jnp.where(present, eidx, N_EXPERTS))
    last_e = jnp.max(jnp.where(present, eidx, -1))
    later = jnp.where(present[None, :] & (eidx[None, :] > eidx[:, None]), eidx[None, :], N_EXPERTS)
    next_e = jnp.min(later, axis=1)
    next_e = jnp.where(next_e == N_EXPERTS, first_e, next_e).astype(jnp.int32)
    starts = ((blk == 0) | (blk_exp != jnp.roll(blk_exp, 1))) & (blk < n_used[0])
    tables = (blk_exp, n_used, starts.astype(jnp.int32), next_e[blk_exp],
              (blk_exp == last_e).astype(jnp.int32))
    xb = hp[slot_tok]
    act = moe_up(tables, xb, w1, w3, tmb=tmb, tn=1024)
    tnd = 1024
    be, nu, gs, ne, lg = tables
    halved = (jnp.repeat(be, 2), nu * 2, jnp.stack([gs, jnp.zeros_like(gs)], axis=1).reshape(-1),
              jnp.repeat(ne, 2), jnp.repeat(lg, 2))
    yb = moe_down(halved, act, w2, tmb=tmb // 2, tn=tnd)
    d2 = dest.reshape(T, TOP_K)
    return moe_combine(x2d, gate, yb[d2[:, 0]], yb[d2[:, 1]], tn=tnd, tm=512)


def kernel(x, attn_norm_g, w_in, nsa_q_norm_g, nsa_k_norm_g, cmp_pe_k, cmp_pe_v, w_cmp_k, w_cmp_v, mla_qa_norm_g, w_q_b, mla_kva_norm_g, w_kv_b, mla_q_norm_g, mla_qr_norm_g, mla_k_norm_g, mla_kr_norm_g, w_proj_nsa, w_proj_mla, w_out, ffn_norm_g, w_ff1, w_ff3, w_ff2, w_router, w_e1, w_e3, w_e2):
    p = dict(attn_norm_g=attn_norm_g, w_in=w_in, nsa_q_norm_g=nsa_q_norm_g, nsa_k_norm_g=nsa_k_norm_g,
             cmp_pe_k=cmp_pe_k, cmp_pe_v=cmp_pe_v, w_cmp_k=w_cmp_k, w_cmp_v=w_cmp_v,
             mla_qa_norm_g=mla_qa_norm_g, w_q_b=w_q_b, mla_kva_norm_g=mla_kva_norm_g, w_kv_b=w_kv_b,
             mla_q_norm_g=mla_q_norm_g, mla_qr_norm_g=mla_qr_norm_g, mla_k_norm_g=mla_k_norm_g,
             mla_kr_norm_g=mla_kr_norm_g, w_proj_nsa=w_proj_nsa, w_proj_mla=w_proj_mla, w_out=w_out)
    B, S, D = x.shape
    depth = w_in.shape[0]
    nc = S // CMP_STRIDE
    tabs = dict(tok=_rope_tabs(jnp.arange(S), NSA_ROT, S),
                cmp=_rope_tabs(jnp.arange(nc - 1) * CMP_STRIDE + CMP_LEN - 1, NSA_ROT, nc),
                mla=_mla_tabs(S))
    pw = _layout_layers(p)
    x2d = x.reshape(B * S, D)
    for l in range(depth):
        x2d = _attention_block(x2d, B, S, p, pw, l, tabs)
        if l % 2 == 0:
            x2d = _dense_ffn(x2d, ffn_norm_g[l], w_ff1[l // 2], w_ff3[l // 2], w_ff2[l // 2])
        else:
            x2d = _moe_ffn(x2d, ffn_norm_g[l], w_router[l // 2], w_e1[l // 2], w_e3[l // 2], w_e2[l // 2])
    return x2d.reshape(B, S, D)
```

```python
import functools

import numpy as np
import jax
import jax.numpy as jnp
from jax import lax
from jax.experimental import pallas as pl
from jax.experimental.pallas import tpu as pltpu

D_MODEL = 2048
HEAD_DIM = 128
NSA_HEADS = 8
NSA_KV_GROUPS = 2
NSA_HPG = NSA_HEADS // NSA_KV_GROUPS
NSA_ROT = HEAD_DIM // 4
CMP_LEN = 32
CMP_STRIDE = 16
SLC_LEN = 64
SLC_TOPK = 16
N_LOCAL_SLC = 2
WINDOW = 512
FORCE_SCORE = 1.0e4
MLA_HEADS = 8
Q_LORA = 768
KV_LORA = 512
QK_NOPE = 128
QK_ROPE = 64
V_HEAD = 128
ROPE_THETA = 500000.0
EPS = 1e-6
D_FF = 7168
N_EXPERTS = 8
TOP_K = 2

LANES = 128
MLA_QK_PAD = 256
NEG_INF = float("-inf")
LOG2E = 1.4426950408889634
BF16 = jnp.bfloat16
F32 = jnp.float32

OFF_QA = 0
OFF_G = 768
OFF_KR = 896
OFF_Q = 1024
OFF_KVA = 2048
OFF_KV = 2560
OFF_M = 4096
D_INP = 8192
NSA_KV_COLS = 3 * 2 * NSA_KV_GROUPS * HEAD_DIM
NSA_KVP_COLS = 2 * NSA_KV_GROUPS * HEAD_DIM
VT_CHUNK = 256

VMEM_LIMIT = 56 * 1024 * 1024


def _cparams(sem):
    return pltpu.CompilerParams(dimension_semantics=sem, vmem_limit_bytes=VMEM_LIMIT)


def _rms(x, g):
    ms = jnp.mean(x * x, axis=-1, keepdims=True)
    return x * lax.rsqrt(ms + EPS) * g


def _rope_lanes(y, ct, s1, s2, half):
    return y * ct + pltpu.roll(y, LANES - half, 1) * s1 + pltpu.roll(y, half, 1) * s2


def _sigmoid(x):
    return 1.0 / (1.0 + jnp.exp(-x))


def _nmm_kernel(x_ref, g_ref, w_ref, o_ref, h_scr):
    @pl.when(pl.program_id(1) == 0)
    def _():
        h_scr[...] = _rms(x_ref[...], g_ref[...]).astype(BF16)

    o_ref[...] = jnp.dot(h_scr[...], w_ref[...].astype(BF16),
                         preferred_element_type=F32).astype(o_ref.dtype)


def norm_matmul(x, g, w, *, tm, tn, out_dtype=F32):
    T, K = x.shape
    N = w.shape[1]
    return pl.pallas_call(
        _nmm_kernel,
        grid=(T // tm, N // tn),
        in_specs=[pl.BlockSpec((tm, K), lambda m, n: (m, 0)),
                  pl.BlockSpec((1, K), lambda m, n: (0, 0)),
                  pl.BlockSpec((K, tn), lambda m, n: (0, n))],
        out_specs=pl.BlockSpec((tm, tn), lambda m, n: (m, n)),
        out_shape=jax.ShapeDtypeStruct((T, N), out_dtype),
        scratch_shapes=[pltpu.VMEM((tm, K), BF16)],
        compiler_params=_cparams(("parallel", "arbitrary")),
        name="norm_matmul",
    )(x, g.reshape(1, K), w)


def _mmres_kernel(a_ref, w_ref, r_ref, o_ref):
    o_ref[...] = r_ref[...] + jnp.dot(a_ref[...], w_ref[...].astype(BF16), preferred_element_type=F32)


def matmul_residual(a, w, res, *, tm, tn):
    T, K = a.shape
    N = w.shape[1]
    return pl.pallas_call(
        _mmres_kernel,
        grid=(T // tm, N // tn),
        in_specs=[pl.BlockSpec((tm, K), lambda m, n: (m, 0)),
                  pl.BlockSpec((K, tn), lambda m, n: (0, n)),
                  pl.BlockSpec((tm, tn), lambda m, n: (m, n))],
        out_specs=pl.BlockSpec((tm, tn), lambda m, n: (m, n)),
        out_shape=jax.ShapeDtypeStruct((T, N), F32),
        compiler_params=_cparams(("parallel", "arbitrary")),
        name="matmul_residual",
    )(a, w, res)


def _prep_kernel(zg_ref, zq_ref, zc_ref, zs_ref, zw_ref, ct_ref, s1_ref, s2_ref, gn_ref,
                 qn_ref, kvp_ref, xc_ref, vts_ref, vtw_ref, gate_ref):
    ct, s1, s2 = ct_ref[...], s1_ref[...], s2_ref[...]
    half = NSA_ROT // 2
    gq = gn_ref[0:1, :]
    for h in range(NSA_HEADS):
        y = _rope_lanes(_rms(zq_ref[:, h * HEAD_DIM:(h + 1) * HEAD_DIM], gq), ct, s1, s2, half)
        qn_ref[:, h * HEAD_DIM:(h + 1) * HEAD_DIM] = (y * (LOG2E * HEAD_DIM ** -0.5)).astype(BF16)
    for br, zb_ref in enumerate((zc_ref, zs_ref, zw_ref)):
        for kv in range(2):
            for g in range(NSA_KV_GROUPS):
                c = (kv * NSA_KV_GROUPS + g) * HEAD_DIM
                y = zb_ref[:, c:c + HEAD_DIM]
                if kv == 0 and br > 0:
                    y = _rope_lanes(_rms(y, gn_ref[1 + br:2 + br, :]), ct, s1, s2, half)
                if br == 0:
                    xc_ref[kv * NSA_KV_GROUPS + g] = y.astype(BF16)
                elif kv == 0:
                    d = (br - 1) * NSA_KV_GROUPS * HEAD_DIM + g * HEAD_DIM
                    kvp_ref[:, d:d + HEAD_DIM] = y.astype(BF16)
                else:
                    (vts_ref if br == 1 else vtw_ref)[g] = y.T.astype(BF16)
    gate_ref[...] = _sigmoid(zg_ref[...])


def nsa_prep(z, tabs, gn, S, *, tm=VT_CHUNK):
    T = z.shape[0]
    ns = S // tm
    tab_spec = pl.BlockSpec((tm, LANES), lambda i: (i % ns, 0))
    vt_spec = pl.BlockSpec((None, NSA_KV_GROUPS, None, HEAD_DIM, tm), lambda i: (i // ns, 0, i % ns, 0, 0))
    vt_shape = jax.ShapeDtypeStruct((T // S, NSA_KV_GROUPS, ns, HEAD_DIM, tm), BF16)
    wq, wb = NSA_HEADS * HEAD_DIM, 2 * NSA_KV_GROUPS * HEAD_DIM
    npl = 2 * NSA_KV_GROUPS
    return pl.pallas_call(
        _prep_kernel,
        grid=(T // tm,),
        in_specs=[pl.BlockSpec((tm, LANES), lambda i: (i, OFF_G // LANES)),
                  pl.BlockSpec((tm, wq), lambda i: (i, OFF_Q // wq)),
                  pl.BlockSpec((tm, wb), lambda i: (i, OFF_KV // wb)),
                  pl.BlockSpec((tm, wb), lambda i: (i, OFF_KV // wb + 1)),
                  pl.BlockSpec((tm, wb), lambda i: (i, OFF_KV // wb + 2)),
                  tab_spec, tab_spec, tab_spec, pl.BlockSpec((8, LANES), lambda i: (0, 0))],
        out_specs=[pl.BlockSpec((tm, NSA_HEADS * HEAD_DIM), lambda i: (i, 0)),
                   pl.BlockSpec((tm, NSA_KVP_COLS), lambda i: (i, 0)),
                   pl.BlockSpec((None, npl, tm, HEAD_DIM), lambda i: (i // ns, 0, i % ns, 0)), vt_spec, vt_spec,
                   pl.BlockSpec((tm, LANES), lambda i: (i, 0))],
        out_shape=[jax.ShapeDtypeStruct((T, NSA_HEADS * HEAD_DIM), BF16),
                   jax.ShapeDtypeStruct((T, NSA_KVP_COLS), BF16),
                   jax.ShapeDtypeStruct((T // S, npl, S, HEAD_DIM), BF16), vt_shape, vt_shape,
                   jax.ShapeDtypeStruct((T, LANES), F32)],
        compiler_params=_cparams(("parallel",)),
        name="nsa_prep",
    )(z, z, z, z, z, *tabs, gn)


def _cmp_kernel(xk_ref, xv_ref, wk_ref, wv_ref, pek_ref, pev_ref, gk_ref, ct_ref, s1_ref, s2_ref,
                kct_ref, vc_ref):
    nc = xk_ref.shape[0]
    row = lax.broadcasted_iota(jnp.int32, (nc, HEAD_DIM), 0)

    def comp(x_ref, w_ref, pe_ref):
        w = w_ref[...]
        y = jnp.dot(x_ref[...], w, preferred_element_type=F32)
        ype = jnp.dot(pe_ref[...], w, preferred_element_type=F32)
        bias = ype[0:1, :HEAD_DIM] + ype[1:2, HEAD_DIM:]
        out = y[:, :HEAD_DIM] + pltpu.roll(y[:, HEAD_DIM:], nc - 1, 0) + bias
        return jnp.where(row < nc - 1, out, 0.0)

    k = _rms(comp(xk_ref, wk_ref, pek_ref), gk_ref[...])
    k = _rope_lanes(k, ct_ref[...], s1_ref[...], s2_ref[...], NSA_ROT // 2)
    kct_ref[...] = k.T.astype(BF16)
    vc_ref[...] = comp(xv_ref, wv_ref, pev_ref).astype(BF16)


def nsa_compress(x2, wk2, wv2, pek2, pev2, gk, tabs_cmp):
    B, _, nc, kk = x2.shape
    G = NSA_KV_GROUPS
    full = lambda shape: pl.BlockSpec(shape, lambda b, g: (0,) * len(shape))
    return pl.pallas_call(
        _cmp_kernel,
        grid=(B, G),
        in_specs=[pl.BlockSpec((None, None, nc, kk), lambda b, g: (b, g, 0, 0)),
                  pl.BlockSpec((None, None, nc, kk), lambda b, g: (b, G + g, 0, 0)),
                  full((kk, 2 * HEAD_DIM)), full((kk, 2 * HEAD_DIM)),
                  full((8, kk)), full((8, kk)), full((1, HEAD_DIM)),
                  full((nc, LANES)), full((nc, LANES)), full((nc, LANES))],
        out_specs=[pl.BlockSpec((None, None, HEAD_DIM, nc), lambda b, g: (b, g, 0, 0)),
                   pl.BlockSpec((None, None, nc, HEAD_DIM), lambda b, g: (b, g, 0, 0))],
        out_shape=[jax.ShapeDtypeStruct((B, G, HEAD_DIM, nc), BF16),
                   jax.ShapeDtypeStruct((B, G, nc, HEAD_DIM), BF16)],
        compiler_params=_cparams(("parallel", "parallel")),
        name="nsa_compress",
    )(x2, x2, wk2, wv2, pek2, pev2, gk, *tabs_cmp)


def _cattn_kernel(q_ref, kct_ref, vc_ref, ov_ref, oc_ref, sel_ref, *, tq):
    nc = vc_ref.shape[0]
    ns = sel_ref.shape[0]
    t0 = pl.program_id(2) * tq
    t_pos = t0 + lax.broadcasted_iota(jnp.int32, (tq, nc), 0)
    n_idx = lax.broadcasted_iota(jnp.int32, (tq, nc), 1)
    vis = (t_pos >= n_idx * CMP_STRIDE + (CMP_LEN - 1)) & (n_idx < nc - 1)
    bias = jnp.where(vis, 0.0, NEG_INF)
    kct = kct_ref[...]
    vc = vc_ref[...]
    psum = jnp.zeros((tq, nc), F32)
    for hh in range(NSA_HPG):
        s = jnp.dot(q_ref[:, hh * HEAD_DIM:(hh + 1) * HEAD_DIM], kct, preferred_element_type=F32) + bias
        m = jnp.max(s, axis=-1, keepdims=True)
        m = jnp.where(m == NEG_INF, 0.0, m)
        p = jnp.exp2(s - m)
        den = jnp.sum(p, axis=-1, keepdims=True)
        p = p * (1.0 / jnp.where(den > 0.0, den, 1.0))
        oc_ref[:, hh * HEAD_DIM:(hh + 1) * HEAD_DIM] = jnp.dot(
            p.astype(BF16), vc, preferred_element_type=F32).astype(oc_ref.dtype)
        psum = psum + p
    pt = psum.T
    hi = pt.astype(BF16)
    lo = (pt - hi.astype(F32)).astype(BF16)
    ov = ov_ref[...]
    imp = jnp.dot(ov, hi, preferred_element_type=F32) + jnp.dot(ov, lo, preferred_element_type=F32)
    blk = lax.broadcasted_iota(jnp.int32, (ns, tq), 0)
    cur = (t0 + lax.broadcasted_iota(jnp.int32, (ns, tq), 1)) // SLC_LEN
    forced = (blk == 0) | ((blk <= cur) & (blk > cur - N_LOCAL_SLC))
    val = jnp.where(blk > cur, NEG_INF, jnp.where(forced, FORCE_SCORE, imp))
    rank = jnp.zeros((ns, tq), F32)
    for i in range(ns):
        other = val[i:i + 1, :]
        ahead = (other > val) | ((other == val) & (blk > i))
        rank = rank + jnp.where(ahead, 1.0, 0.0)
    sel_ref[...] = jnp.where(rank < float(min(SLC_TOPK, ns)), 1.0, 0.0).astype(sel_ref.dtype)


def nsa_cmp_attention(qn, kct, vc, ov, B, S, *, tq):
    G = NSA_KV_GROUPS
    nq = S // tq
    nc = vc.shape[2]
    ns = S // SLC_LEN
    gw = NSA_HPG * HEAD_DIM
    return pl.pallas_call(
        functools.partial(_cattn_kernel, tq=tq),
        grid=(B, G, nq),
        in_specs=[pl.BlockSpec((tq, gw), lambda b, g, i: (b * nq + i, g)),
                  pl.BlockSpec((None, None, HEAD_DIM, nc), lambda b, g, i: (b, g, 0, 0)),
                  pl.BlockSpec((None, None, nc, HEAD_DIM), lambda b, g, i: (b, g, 0, 0)),
                  pl.BlockSpec((ns, nc), lambda b, g, i: (0, 0))],
        out_specs=[pl.BlockSpec((tq, gw), lambda b, g, i: (b * nq + i, g)),
                   pl.BlockSpec((None, None, ns, tq), lambda b, g, i: (b, g, 0, i))],
        out_shape=[jax.ShapeDtypeStruct((B * S, NSA_HEADS * HEAD_DIM), BF16),
                   jax.ShapeDtypeStruct((B, G, ns, S), BF16)],
        compiler_params=_cparams(("parallel", "parallel", "parallel")),
        name="nsa_cmp_attention",
    )(qn, kct, vc, ov)


def _flash_kernel(*refs, mode, hg, tq, tk, tv, dqk, dv):
    if mode == "sel":
        q_ref, k_ref, vt_ref, sel_ref, ex_ref, o_ref, qs, m_s, l_s, acc_s, sa, sb = refs
    else:
        q_ref, k_ref, vt_ref, o_ref, qs, m_s, l_s, acc_s, sa, sb = refs
    rows = hg * tq
    q0 = pl.program_id(2) * tq
    cd = q0 // tk
    for hh in range(hg):
        qs[hh * tq:(hh + 1) * tq, :] = q_ref[:, hh * dqk:(hh + 1) * dqk]

    def scores(c, kind):
        start = pl.multiple_of(c * tk, tk)
        s = lax.dot_general(k_ref[pl.ds(start, tk), :], qs[...], (((1,), (1,)), ((), ())),
                            preferred_element_type=F32)
        ok = None
        if kind == "diag":
            k_pos = start + lax.broadcasted_iota(jnp.int32, (tk, tq), 0)
            q_pos = q0 + lax.broadcasted_iota(jnp.int32, (tk, tq), 1)
            ok = k_pos <= q_pos
        if mode == "sel":
            chosen = jnp.dot(ex_ref[c], sel_ref[...], preferred_element_type=F32) > 0.5
            ok = chosen if ok is None else (ok & chosen)
        if ok is not None:
            bias = jnp.where(ok, 0.0, NEG_INF)
            s = s + (jnp.concatenate([bias] * hg, axis=1) if hg > 1 else bias)
        return s

    def update(c, s, carry):
        m, l, acc = carry
        m_new = jnp.maximum(m, jnp.max(s, axis=0, keepdims=True))
        p = jnp.exp2(s - m_new)
        alpha = jnp.exp2(m - m_new)
        l = alpha * l + jnp.sum(p, axis=0, keepdims=True)
        r = tk // tv
        vt = vt_ref[c] if r == 1 else jnp.concatenate([vt_ref[c * r + j] for j in range(r)], axis=1)
        acc = alpha * acc + jnp.dot(vt, p.astype(BF16), preferred_element_type=F32)
        return m_new, l, acc

    def load():
        return m_s[...], l_s[...], acc_s[...]

    def store(carry):
        m_s[...], l_s[...], acc_s[...] = carry

    nd = max(tq // tk, 1)
    last = k_ref.shape[0] // tk - 1
    carry = (jnp.full((1, rows), NEG_INF, F32), jnp.zeros((1, rows), F32), jnp.zeros((dv, rows), F32))
    s_next = scores(cd, "diag")
    for j in range(nd):
        s_cur = s_next
        if j + 1 < nd:
            s_next = scores(cd + j + 1, "diag")
        else:
            sa[...] = scores(0, "full")
        carry = update(cd + j, s_cur, carry)
    store(carry)

    def pairs(first, count):
        carry = load()
        for j in range(count):
            c = first + 2 * j
            sb[...] = scores(c + 1, "full")
            carry = update(c, sa[...], carry)
            sa[...] = scores(jnp.minimum(c + 2, last), "full")
            carry = update(c + 1, sb[...], carry)
        store(carry)

    def four(i, _):
        pairs(4 * i, 2)
        return 0

    lax.fori_loop(0, cd // 4, four, 0)

    @pl.when(cd % 4 >= 2)
    def _():
        pairs((cd // 4) * 4, 1)

    @pl.when(cd % 2 == 1)
    def _():
        store(update(cd - 1, sa[...], load()))

    o = acc_s[...] * (1.0 / l_s[...])
    for hh in range(hg):
        o_ref[:, hh * dv:(hh + 1) * dv] = o[:, hh * tq:(hh + 1) * tq].T.astype(o_ref.dtype)


def flash_attention(q, k, vt, B, S, *, mode, n_groups, hg, dqk, dv, kcol0, tq, tk, sel_t=None, expand=None):
    nq = S // tq
    nch = S // tk
    tv = vt.shape[-1]
    assert (tk % tq == 0 or tq % tk == 0) and S % tk == 0 and S % tq == 0 and tq % LANES == 0 and tk % tv == 0
    in_specs = [pl.BlockSpec((tq, hg * dqk), lambda b, g, i: (b * nq + i, g)),
                pl.BlockSpec((S, dqk), lambda b, g, i: (b, kcol0 + g)),
                pl.BlockSpec((None, None, S // tv, dv, tv), lambda b, g, i: (b, g, 0, 0, 0))]
    args = [q, k, vt]
    if mode == "sel":
        ns = S // SLC_LEN
        in_specs += [pl.BlockSpec((None, None, ns, tq), lambda b, g, i: (b, g, 0, i)),
                     pl.BlockSpec((nch, tk, ns), lambda b, g, i: (0, 0, 0))]
        args += [sel_t, expand]
    return pl.pallas_call(
        functools.partial(_flash_kernel, mode=mode, hg=hg, tq=tq, tk=tk, tv=tv, dqk=dqk, dv=dv),
        grid=(B, n_groups, nq),
        in_specs=in_specs,
        out_specs=pl.BlockSpec((tq, hg * dv), lambda b, g, i: (b * nq + i, g)),
        out_shape=jax.ShapeDtypeStruct((B * S, n_groups * hg * dv), BF16),
        scratch_shapes=[pltpu.VMEM((hg * tq, dqk), BF16),
                        pltpu.VMEM((1, hg * tq), F32),
                        pltpu.VMEM((1, hg * tq), F32),
                        pltpu.VMEM((dv, hg * tq), F32),
                        pltpu.VMEM((tk, hg * tq), F32),
                        pltpu.VMEM((tk, hg * tq), F32)],
        compiler_params=_cparams(("parallel", "parallel", "arbitrary")),
        name="flash_" + mode,
    )(*args)


def _window_kernel(q_ref, k_ref, vt_ref, oc_ref, os_ref, gate_ref, o_ref, qs, *, hg, tq, dqk, dv):
    nk = WINDOW + tq
    q0 = pl.program_id(2) * tq
    start = pl.multiple_of(jnp.maximum(q0 - WINDOW, 0), tq)
    for hh in range(hg):
        qs[hh * tq:(hh + 1) * tq, :] = q_ref[:, hh * dqk:(hh + 1) * dqk]
    s = lax.dot_general(k_ref[pl.ds(start, nk), :], qs[...], (((1,), (1,)), ((), ())),
                        preferred_element_type=F32)
    k_pos = start + lax.broadcasted_iota(jnp.int32, (nk, tq), 0)
    q_pos = q0 + lax.broadcasted_iota(jnp.int32, (nk, tq), 1)
    bias = jnp.where((k_pos <= q_pos) & (q_pos - k_pos < WINDOW), 0.0, NEG_INF)
    s = s + jnp.concatenate([bias] * hg, axis=1)
    p = jnp.exp2(s - jnp.max(s, axis=0, keepdims=True))
    l = jnp.sum(p, axis=0, keepdims=True)
    pb = p.astype(BF16)
    c0 = start // tq
    o = jnp.zeros((dv, hg * tq), F32)
    for j in range(nk // tq):
        o = o + jnp.dot(vt_ref[c0 + j], pb[j * tq:(j + 1) * tq, :], preferred_element_type=F32)
    o = o * (1.0 / l)
    first_group = pl.program_id(1) == 0
    n_heads = 2 * hg

    def gate(br, hh):
        lo = gate_ref[:, br * n_heads + hh:br * n_heads + hh + 1]
        hi = gate_ref[:, br * n_heads + hg + hh:br * n_heads + hg + hh + 1]
        return jnp.where(first_group, lo, hi)

    for hh in range(hg):
        sl = slice(hh * dv, (hh + 1) * dv)
        mixed = (gate(0, hh) * oc_ref[:, sl].astype(F32) + gate(1, hh) * os_ref[:, sl].astype(F32)
                 + gate(2, hh) * o[:, hh * tq:(hh + 1) * tq].T)
        o_ref[:, sl] = mixed.astype(o_ref.dtype)


def window_attention(q, k, vt, oc, os_, gates, B, S, *, n_groups, hg, dqk, dv, kcol0, tq):
    assert n_groups == NSA_KV_GROUPS
    nq = S // tq
    assert WINDOW % tq == 0 and S >= WINDOW + tq and tq % LANES == 0
    return pl.pallas_call(
        functools.partial(_window_kernel, hg=hg, tq=tq, dqk=dqk, dv=dv),
        grid=(B, n_groups, nq),
        in_specs=[pl.BlockSpec((tq, hg * dqk), lambda b, g, i: (b * nq + i, g)),
                  pl.BlockSpec((S, dqk), lambda b, g, i: (b, kcol0 + g)),
                  pl.BlockSpec((None, None, nq, dv, tq), lambda b, g, i: (b, g, 0, 0, 0)),
                  pl.BlockSpec((tq, hg * dv), lambda b, g, i: (b * nq + i, g)),
                  pl.BlockSpec((tq, hg * dv), lambda b, g, i: (b * nq + i, g)),
                  pl.BlockSpec((tq, LANES), lambda b, g, i: (b * nq + i, 0))],
        out_specs=pl.BlockSpec((tq, hg * dv), lambda b, g, i: (b * nq + i, g)),
        out_shape=jax.ShapeDtypeStruct((B * S, n_groups * hg * dv), BF16),
        scratch_shapes=[pltpu.VMEM((hg * tq, dqk), BF16)],
        compiler_params=_cparams(("parallel", "parallel", "arbitrary")),
        name="window_attention",
    )(q, k, vt, oc, os_, gates)


def _mla_q_kernel(z_ref, ga_ref, w_ref, gh_ref, ct_ref, s1_ref, s2_ref, o_ref):
    h = _rms(z_ref[...], ga_ref[...]).astype(BF16)
    y = jnp.dot(h, w_ref[...], preferred_element_type=F32)
    ct, s1, s2 = ct_ref[...], s1_ref[...], s2_ref[...]
    scale = LOG2E * (QK_NOPE + QK_ROPE) ** -0.5
    for hd in range(MLA_HEADS):
        c = hd * MLA_QK_PAD
        nope = _rms(y[:, c:c + QK_NOPE], gh_ref[0:1, :])
        r = y[:, c + QK_NOPE:c + MLA_QK_PAD]
        ms = jnp.sum(r * r, axis=-1, keepdims=True) * (1.0 / QK_ROPE)
        r = _rope_lanes(r * lax.rsqrt(ms + EPS) * gh_ref[1:2, :], ct, s1, s2, QK_ROPE // 2)
        o_ref[:, c:c + QK_NOPE] = (nope * scale).astype(BF16)
        o_ref[:, c + QK_NOPE:c + MLA_QK_PAD] = (r * scale).astype(BF16)


def mla_q_proj(z, ga, wq, gh, tabs, S, *, tm):
    T = z.shape[0]
    ns = S // tm
    tab_spec = pl.BlockSpec((tm, LANES), lambda i: (i % ns, 0))
    nout = MLA_HEADS * MLA_QK_PAD
    return pl.pallas_call(
        _mla_q_kernel,
        grid=(T // tm,),
        in_specs=[pl.BlockSpec((tm, Q_LORA), lambda i: (i, OFF_QA // Q_LORA)),
                  pl.BlockSpec((1, Q_LORA), lambda i: (0, 0)),
                  pl.BlockSpec((Q_LORA, nout), lambda i: (0, 0)),
                  pl.BlockSpec((8, LANES), lambda i: (0, 0)),
                  tab_spec, tab_spec, tab_spec],
        out_specs=pl.BlockSpec((tm, nout), lambda i: (i, 0)),
        out_shape=jax.ShapeDtypeStruct((T, nout), BF16),
        compiler_params=_cparams(("parallel",)),
        name="mla_q_proj",
    )(z, ga, wq, gh, *tabs)


def _mla_kv_kernel(z_ref, zr_ref, ga_ref, w_ref, gh_ref, ct_ref, s1_ref, s2_ref, k_ref, v_ref):
    h = _rms(z_ref[...], ga_ref[...]).astype(BF16)
    y = jnp.dot(h, w_ref[...], preferred_element_type=F32)
    r = zr_ref[...]
    ms = jnp.sum(r * r, axis=-1, keepdims=True) * (1.0 / QK_ROPE)
    r = _rope_lanes(r * lax.rsqrt(ms + EPS) * gh_ref[1:2, :], ct_ref[...], s1_ref[...], s2_ref[...],
                    QK_ROPE // 2).astype(BF16)
    nv = MLA_HEADS * QK_NOPE
    for hd in range(MLA_HEADS):
        c = hd * MLA_QK_PAD
        k_ref[:, c:c + QK_NOPE] = _rms(y[:, hd * QK_NOPE:(hd + 1) * QK_NOPE], gh_ref[0:1, :]).astype(BF16)
        k_ref[:, c + QK_NOPE:c + MLA_QK_PAD] = r
    for hd in range(MLA_HEADS):
        v_ref[hd] = y[:, nv + hd * V_HEAD:nv + (hd + 1) * V_HEAD].T.astype(BF16)


def mla_kv_proj(z, ga, wkv, gh, tabs, S, *, tm=VT_CHUNK):
    T = z.shape[0]
    ns = S // tm
    tab_spec = pl.BlockSpec((tm, LANES), lambda i: (i % ns, 0))
    nk = MLA_HEADS * MLA_QK_PAD
    nv = MLA_HEADS * V_HEAD
    return pl.pallas_call(
        _mla_kv_kernel,
        grid=(T // tm,),
        in_specs=[pl.BlockSpec((tm, KV_LORA), lambda i: (i, OFF_KVA // KV_LORA)),
                  pl.BlockSpec((tm, LANES), lambda i: (i, OFF_KR // LANES)),
                  pl.BlockSpec((1, KV_LORA), lambda i: (0, 0)),
                  pl.BlockSpec((KV_LORA, MLA_HEADS * (QK_NOPE + V_HEAD)), lambda i: (0, 0)),
                  pl.BlockSpec((8, LANES), lambda i: (0, 0)),
                  tab_spec, tab_spec, tab_spec],
        out_specs=[pl.BlockSpec((tm, nk), lambda i: (i, 0)),
                   pl.BlockSpec((None, MLA_HEADS, None, V_HEAD, tm), lambda i: (i // ns, 0, i % ns, 0, 0))],
        out_shape=[jax.ShapeDtypeStruct((T, nk), BF16),
                   jax.ShapeDtypeStruct((T // S, MLA_HEADS, ns, V_HEAD, tm), BF16)],
        compiler_params=_cparams(("parallel",)),
        name="mla_kv_proj",
    )(z, z, ga, wkv, gh, *tabs)


def _mix_kernel(a_ref, ob_ref, wa_ref, wb_ref, za_ref, zb_ref, o_ref):
    pa = jnp.dot(a_ref[...], wa_ref[...].astype(BF16), preferred_element_type=F32)
    pb = jnp.dot(ob_ref[...], wb_ref[...].astype(BF16), preferred_element_type=F32)
    o_ref[...] = (_sigmoid(za_ref[...]) * pa + _sigmoid(zb_ref[...]) * pb).astype(o_ref.dtype)


def gated_mix(oa, ob, wa, wb, z, *, tm, tn):
    T, K = oa.shape
    N = wa.shape[1]
    row = lambda w: pl.BlockSpec((tm, w), lambda m, n: (m, 0))
    return pl.pallas_call(
        _mix_kernel,
        grid=(T // tm, N // tn),
        in_specs=[row(K), row(K),
                  pl.BlockSpec((K, tn), lambda m, n: (0, n)),
                  pl.BlockSpec((K, tn), lambda m, n: (0, n)),
                  pl.BlockSpec((tm, tn), lambda m, n: (m, OFF_M // tn + n)),
                  pl.BlockSpec((tm, tn), lambda m, n: (m, (OFF_M + D_MODEL) // tn + n))],
        out_specs=pl.BlockSpec((tm, tn), lambda m, n: (m, n)),
        out_shape=jax.ShapeDtypeStruct((T, N), BF16),
        compiler_params=_cparams(("parallel", "parallel")),
        name="gated_mix",
    )(oa, ob, wa, wb, z, z)


def _ffn_up_kernel(x_ref, g_ref, w1_ref, w3_ref, o_ref, h_scr):
    @pl.when(pl.program_id(1) == 0)
    def _():
        h_scr[...] = _rms(x_ref[...], g_ref[...]).astype(BF16)

    h = h_scr[...]
    a = jnp.dot(h, w1_ref[...].astype(BF16), preferred_element_type=F32)
    b = jnp.dot(h, w3_ref[...].astype(BF16), preferred_element_type=F32)
    o_ref[...] = (a * _sigmoid(a) * b).astype(o_ref.dtype)


def ffn_up(x, g, w1, w3, *, tm, tn):
    T, K = x.shape
    N = w1.shape[1]
    return pl.pallas_call(
        _ffn_up_kernel,
        grid=(T // tm, N // tn),
        in_specs=[pl.BlockSpec((tm, K), lambda m, n: (m, 0)),
                  pl.BlockSpec((1, K), lambda m, n: (0, 0)),
                  pl.BlockSpec((K, tn), lambda m, n: (0, n)),
                  pl.BlockSpec((K, tn), lambda m, n: (0, n))],
        out_specs=pl.BlockSpec((tm, tn), lambda m, n: (m, n)),
        out_shape=jax.ShapeDtypeStruct((T, N), BF16),
        scratch_shapes=[pltpu.VMEM((tm, K), BF16)],
        compiler_params=_cparams(("parallel", "arbitrary")),
        name="ffn_up",
    )(x, g.reshape(1, K), w1, w3)


def _pack_bf16_pairs(h):
    k = h.shape[1] // 2
    hi = lax.bitcast_convert_type(h[:, :k].astype(jnp.bfloat16).astype(F32), jnp.uint32)
    lo = lax.bitcast_convert_type(h[:, k:].astype(jnp.bfloat16).astype(F32), jnp.uint32)
    return lax.bitcast_convert_type(hi | (lo >> 16), F32)


def _unpack_bf16_pairs(xp):
    xp = lax.bitcast_convert_type(xp, jnp.uint32)
    hi = lax.bitcast_convert_type(xp & jnp.uint32(0xFFFF0000), F32)
    lo = lax.bitcast_convert_type(xp << 16, F32)
    return hi.astype(BF16), lo.astype(BF16)


def _router_kernel(x_ref, g_ref, wr_ref, h_ref, idx_ref, gate_ref):
    h = _rms(x_ref[...], g_ref[...])
    is_pad = pl.program_id(0) == pl.num_programs(0) - 1
    h_ref[...] = jnp.where(is_pad, 0.0, _pack_bf16_pairs(h))
    logits = jnp.dot(h, wr_ref[...], preferred_element_type=F32, precision=lax.Precision.HIGHEST)
    lane = lax.broadcasted_iota(jnp.int32, logits.shape, 1).astype(F32)
    logits = jnp.where(lane < float(N_EXPERTS), logits, NEG_INF)
    m1 = jnp.max(logits, axis=-1, keepdims=True)
    i1 = jnp.min(jnp.where(logits == m1, lane, float(LANES)), axis=-1, keepdims=True)
    rest = jnp.where(lane == i1, NEG_INF, logits)
    m2 = jnp.max(rest, axis=-1, keepdims=True)
    i2 = jnp.min(jnp.where(rest == m2, lane, float(LANES)), axis=-1, keepdims=True)
    e = jnp.exp(m2 - m1)
    den = 1.0 + e
    idx_ref[...] = jnp.where(lane == 0.0, i1, jnp.where(lane == 1.0, i2, 0.0)).astype(jnp.int32)
    gate_ref[...] = jnp.where(lane == 0.0, 1.0 / den, jnp.where(lane == 1.0, e / den, 0.0))


def router(x, g, wr_pad, *, tm):
    T, K = x.shape
    nt = T // tm
    return pl.pallas_call(
        _router_kernel,
        grid=(nt + 1,),
        in_specs=[pl.BlockSpec((tm, K), lambda i: (jnp.minimum(i, nt - 1), 0)),
                  pl.BlockSpec((1, K), lambda i: (0, 0)),
                  pl.BlockSpec((K, LANES), lambda i: (0, 0))],
        out_specs=[pl.BlockSpec((tm, K // 2), lambda i: (i, 0)),
                   pl.BlockSpec((tm, LANES), lambda i: (jnp.minimum(i, nt - 1), 0)),
                   pl.BlockSpec((tm, LANES), lambda i: (jnp.minimum(i, nt - 1), 0))],
        out_shape=[jax.ShapeDtypeStruct((T + tm, K // 2), F32),
                   jax.ShapeDtypeStruct((T, LANES), jnp.int32),
                   jax.ShapeDtypeStruct((T, LANES), F32)],
        compiler_params=_cparams(("arbitrary",)),
        name="router",
    )(x, g.reshape(1, K), wr_pad)


def _moe_weight_stream(w_hbms, stages, casts, sems, be_ref, nu_ref, gs_ref, ne_ref, lg_ref, tn, r):
    n = pl.program_id(0)
    used = r < nu_ref[0]

    def copies(e, nt):
        c0 = pl.multiple_of(nt * tn, tn)
        return [pltpu.make_async_copy(w.at[e, :, pl.ds(c0, tn)], st, sems.at[i])
                for i, (w, st) in enumerate(zip(w_hbms, stages))]

    @pl.when((n == 0) & (r == 0))
    def _():
        for c in copies(be_ref[0], 0):
            c.start()

    @pl.when(used & (gs_ref[r] == 1))
    def _():
        for c in copies(be_ref[r], n):
            c.wait()
        for st, wb in zip(stages, casts):
            wb[...] = st[...].astype(BF16)
        last = lg_ref[r] == 1

        @pl.when(jnp.logical_not(last & (n == pl.num_programs(0) - 1)))
        def _():
            for c in copies(ne_ref[r], n + last.astype(jnp.int32)):
                c.start()

    return used


MOE_SUB = 2


def _moe_up_kernel(be_ref, nu_ref, gs_ref, ne_ref, lg_ref, x_ref, w1_hbm, w3_hbm, o_ref,
                   st1, st3, w1_s, w3_s, sems, *, tn, tmb):
    for sub in range(MOE_SUB):
        rows = slice(sub * tmb, (sub + 1) * tmb)
        used = _moe_weight_stream((w1_hbm, w3_hbm), (st1, st3), (w1_s, w3_s), sems,
                                  be_ref, nu_ref, gs_ref, ne_ref, lg_ref, tn, pl.program_id(1) * MOE_SUB + sub)

        @pl.when(used)
        def _():
            xa, xb = _unpack_bf16_pairs(x_ref[rows, :])
            k2 = xa.shape[1]
            a = (jnp.dot(xa, w1_s[:k2, :], preferred_element_type=F32)
                 + jnp.dot(xb, w1_s[k2:, :], preferred_element_type=F32))
            b = (jnp.dot(xa, w3_s[:k2, :], preferred_element_type=F32)
                 + jnp.dot(xb, w3_s[k2:, :], preferred_element_type=F32))
            o_ref[rows, :] = (a * _sigmoid(a) * b).astype(o_ref.dtype)

        @pl.when(jnp.logical_not(used))
        def _():
            o_ref[rows, :] = jnp.zeros((tmb, o_ref.shape[1]), o_ref.dtype)


def _moe_specs(tmb, kx, wout):
    step = lambda r, nu: jnp.minimum(r, (nu[0] - 1) // MOE_SUB)
    return (pl.BlockSpec((MOE_SUB * tmb, kx), lambda n, r, be, nu, gs, ne, lg: (step(r, nu), 0)),
            pl.BlockSpec((MOE_SUB * tmb, wout), lambda n, r, be, nu, gs, ne, lg: (r, n)))


def moe_up(tables, xb, w1, w3, *, tmb, tn):
    n_slot = xb.shape[0]
    K, N = w1.shape[1], w1.shape[2]
    x_spec, o_spec = _moe_specs(tmb, K // 2, tn)
    return pl.pallas_call(
        functools.partial(_moe_up_kernel, tn=tn, tmb=tmb),
        grid_spec=pltpu.PrefetchScalarGridSpec(
            num_scalar_prefetch=5,
            grid=(N // tn, n_slot // (MOE_SUB * tmb)),
            in_specs=[x_spec, pl.BlockSpec(memory_space=pl.ANY), pl.BlockSpec(memory_space=pl.ANY)],
            out_specs=o_spec,
            scratch_shapes=[pltpu.VMEM((K, tn), F32), pltpu.VMEM((K, tn), F32),
                            pltpu.VMEM((K, tn), BF16), pltpu.VMEM((K, tn), BF16),
                            pltpu.SemaphoreType.DMA((2,))]),
        out_shape=jax.ShapeDtypeStruct((n_slot, N), BF16),
        compiler_params=_cparams(("arbitrary", "arbitrary")),
        name="moe_up",
    )(*tables, xb, w1, w3)


def _moe_down_kernel(be_ref, nu_ref, gs_ref, ne_ref, lg_ref, a_ref, w2_hbm, o_ref, st2, w2_s, sems, *, tn, tmb):
    for sub in range(MOE_SUB):
        rows = slice(sub * tmb, (sub + 1) * tmb)
        used = _moe_weight_stream((w2_hbm,), (st2,), (w2_s,), sems, be_ref, nu_ref, gs_ref, ne_ref, lg_ref, tn,
                                  pl.program_id(1) * MOE_SUB + sub)

        @pl.when(used)
        def _():
            o_ref[rows, :] = _pack_bf16_pairs(jnp.dot(a_ref[rows, :], w2_s[...], preferred_element_type=F32))

        @pl.when(jnp.logical_not(used))
        def _():
            o_ref[rows, :] = jnp.zeros((tmb, o_ref.shape[1]), o_ref.dtype)


def moe_down(tables, act, w2, *, tmb, tn):
    n_slot, K = act.shape
    N = w2.shape[2]
    a_spec, o_spec = _moe_specs(tmb, K, tn // 2)
    return pl.pallas_call(
        functools.partial(_moe_down_kernel, tn=tn, tmb=tmb),
        grid_spec=pltpu.PrefetchScalarGridSpec(
            num_scalar_prefetch=5,
            grid=(N // tn, n_slot // (MOE_SUB * tmb)),
            in_specs=[a_spec, pl.BlockSpec(memory_space=pl.ANY)],
            out_specs=o_spec,
            scratch_shapes=[pltpu.VMEM((K, tn), F32), pltpu.VMEM((K, tn), BF16),
                            pltpu.SemaphoreType.DMA((1,))]),
        out_shape=jax.ShapeDtypeStruct((n_slot, N // 2), F32),
        compiler_params=_cparams(("arbitrary", "arbitrary")),
        name="moe_down",
    )(*tables, act, w2)


def _combine_kernel(x_ref, g_ref, y0_ref, y1_ref, o_ref, *, tn):
    g0, g1 = g_ref[:, 0:1], g_ref[:, 1:2]
    h = tn // 2

    def halves(y_ref, n):
        u = lax.bitcast_convert_type(y_ref[:, n * h:(n + 1) * h], jnp.uint32)
        return (lax.bitcast_convert_type(u & jnp.uint32(0xFFFF0000), F32),
                lax.bitcast_convert_type(u << 16, F32))

    for n in range(x_ref.shape[1] // tn):
        a_hi, a_lo = halves(y0_ref, n)
        b_hi, b_lo = halves(y1_ref, n)
        lo_cols = slice(n * tn, n * tn + h)
        hi_cols = slice(n * tn + h, (n + 1) * tn)
        o_ref[:, lo_cols] = x_ref[:, lo_cols] + g0 * a_hi + g1 * b_hi
        o_ref[:, hi_cols] = x_ref[:, hi_cols] + g0 * a_lo + g1 * b_lo


def moe_combine(x, gate, y0, y1, *, tn, tm):
    T, N = x.shape
    return pl.pallas_call(
        functools.partial(_combine_kernel, tn=tn),
        grid=(T // tm,),
        in_specs=[pl.BlockSpec((tm, N), lambda i: (i, 0)),
                  pl.BlockSpec((tm, LANES), lambda i: (i, 0)),
                  pl.BlockSpec((tm, N // 2), lambda i: (i, 0)),
                  pl.BlockSpec((tm, N // 2), lambda i: (i, 0))],
        out_specs=pl.BlockSpec((tm, N), lambda i: (i, 0)),
        out_shape=jax.ShapeDtypeStruct((T, N), F32),
        compiler_params=_cparams(("parallel",)),
        name="moe_combine",
    )(x, gate, y0, y1)


def _rope_tabs(pos, rot_dim, n_rows):
    half = rot_dim // 2
    inv = 1.0 / (ROPE_THETA ** (jnp.arange(0, rot_dim, 2, dtype=F32) / rot_dim))
    ang = jnp.asarray(pos).astype(F32)[:, None] * inv[None, :]
    c, s = jnp.cos(ang), jnp.sin(ang)
    z = jnp.zeros_like(c)
    pad = lambda a, fill: jnp.pad(a, ((0, n_rows - a.shape[0]), (0, LANES - a.shape[1])), constant_values=fill)
    return pad(jnp.concatenate([c, c], 1), 1.0), pad(jnp.concatenate([-s, z], 1), 0.0), \
        pad(jnp.concatenate([z, s], 1), 0.0)


def _mla_tabs(S):
    ct, s1, s2 = _rope_tabs(jnp.arange(S), QK_ROPE, S)
    lane = jnp.arange(LANES)[None, :]
    return jnp.where(lane < QK_ROPE, ct, 0.0), s1, s2


def _pad_cols(w, n):
    return jnp.pad(w, ((0, 0), (0, n - w.shape[1])))


def _layout_w_in(w):
    sp = np.cumsum([0, NSA_HEADS * HEAD_DIM, NSA_KV_COLS, 3 * NSA_HEADS, Q_LORA, KV_LORA, QK_ROPE, 2 * D_MODEL])
    q, kv, g, qa, kva, kr, m = [w[:, sp[i]:sp[i + 1]] for i in range(7)]
    return jnp.concatenate([qa, _pad_cols(g, LANES), _pad_cols(kr, LANES), q, kva, kv, m], axis=1).astype(BF16)


def _layout_w_q_b(w):
    w = w.reshape(Q_LORA, MLA_HEADS, QK_NOPE + QK_ROPE)
    w = jnp.pad(w, ((0, 0), (0, 0), (0, MLA_QK_PAD - QK_NOPE - QK_ROPE)))
    return w.reshape(Q_LORA, MLA_HEADS * MLA_QK_PAD).astype(BF16)


def _layout_w_kv_b(w):
    w = w.reshape(KV_LORA, MLA_HEADS, QK_NOPE + V_HEAD)
    return jnp.concatenate([w[:, :, :QK_NOPE].reshape(KV_LORA, -1), w[:, :, QK_NOPE:].reshape(KV_LORA, -1)],
                           axis=1).astype(BF16)


def _layout_w_cmp(w):
    h = CMP_LEN // 2
    return jnp.concatenate([w[:h].reshape(h * HEAD_DIM, HEAD_DIM), w[h:].reshape(h * HEAD_DIM, HEAD_DIM)],
                           axis=1).astype(BF16)


def _layout_pe(pe):
    return jnp.pad(pe.reshape(2, (CMP_LEN // 2) * HEAD_DIM), ((0, 6), (0, 0))).astype(BF16)


def _overlap(nc, ns):
    n = np.arange(nc)[None, :] * CMP_STRIDE
    j = np.arange(ns)[:, None] * SLC_LEN
    ov = (n <= j + SLC_LEN - 1) & (j <= n + CMP_LEN - 1) & (np.arange(nc)[None, :] < nc - 1)
    return jnp.asarray(ov.astype(np.float32), BF16)


def _expand(ns, S, tk):
    e = ((np.arange(S // tk)[:, None, None] * tk + np.arange(tk)[None, :, None]) // SLC_LEN
         == np.arange(ns)[None, None, :])
    return jnp.asarray(e.astype(np.float32), BF16)


def _layout_layers(p):
    depth = p['w_in'].shape[0]
    lane_row = lambda g: jnp.pad(g, ((0, 0), (0, LANES - g.shape[-1])))[:, None, :]
    z4 = jnp.zeros((depth, 4, HEAD_DIM), F32)
    z6 = jnp.zeros((depth, 6, LANES), F32)
    return dict(
        gn=jnp.concatenate([p['nsa_q_norm_g'][:, None, :], p['nsa_k_norm_g'], z4], axis=1),
        w_ck=jax.vmap(_layout_w_cmp)(p['w_cmp_k']), w_cv=jax.vmap(_layout_w_cmp)(p['w_cmp_v']),
        pe_k=jax.vmap(_layout_pe)(p['cmp_pe_k']), pe_v=jax.vmap(_layout_pe)(p['cmp_pe_v']),
        gh_q=jnp.concatenate([p['mla_q_norm_g'][:, None, :], lane_row(p['mla_qr_norm_g']), z6], axis=1),
        gh_k=jnp.concatenate([p['mla_k_norm_g'][:, None, :], lane_row(p['mla_kr_norm_g']), z6], axis=1),
        w_qb=jax.vmap(_layout_w_q_b)(p['w_q_b']), w_kvb=jax.vmap(_layout_w_kv_b)(p['w_kv_b']))


def _attention_block(x2d, B, S, p, pw, l, tabs):
    T = x2d.shape[0]
    z = norm_matmul(x2d, p['attn_norm_g'][l], _layout_w_in(p['w_in'][l]), tm=min(1024, T), tn=1024)
    qn, kvp, xc, vts, vtw, gates = nsa_prep(z, tabs['tok'], pw['gn'][l], S)

    nc = S // CMP_STRIDE
    x2 = xc.reshape(B, 2 * NSA_KV_GROUPS, nc, CMP_STRIDE * HEAD_DIM)
    kct, vc = nsa_compress(x2, pw['w_ck'][l], pw['w_cv'][l], pw['pe_k'][l], pw['pe_v'][l],
                           p['nsa_k_norm_g'][l][0:1], tabs['cmp'])
    ns = S // SLC_LEN
    oc, sel_t = nsa_cmp_attention(qn, kct, vc, _overlap(nc, ns), B, S, tq=min(256, S))
    nsa = dict(n_groups=NSA_KV_GROUPS, hg=NSA_HPG, dqk=HEAD_DIM, dv=HEAD_DIM)
    tks = min(512, S)
    os_ = flash_attention(qn, kvp, vts, B, S, mode="sel", kcol0=0, tq=256, tk=tks,
                          sel_t=sel_t, expand=_expand(ns, S, tks), **nsa)
    oa = window_attention(qn, kvp, vtw, oc, os_, gates, B, S, kcol0=NSA_KV_GROUPS, tq=VT_CHUNK, **nsa)

    qm = mla_q_proj(z, p['mla_qa_norm_g'][l][None], pw['w_qb'][l], pw['gh_q'][l], tabs['mla'], S, tm=256)
    km, vmt = mla_kv_proj(z, p['mla_kva_norm_g'][l][None], pw['w_kvb'][l], pw['gh_k'][l], tabs['mla'], S)
    tkm = min(512, S)
    ob = flash_attention(qm, km, vmt, B, S, mode="causal",
                         n_groups=MLA_HEADS, hg=1, dqk=MLA_QK_PAD, dv=V_HEAD, kcol0=0, tq=min(1024, S), tk=tkm)

    tm = min(2048, T)
    mix = gated_mix(oa, ob, p['w_proj_nsa'][l], p['w_proj_mla'][l], z, tm=tm, tn=512)
    return matmul_residual(mix, p['w_out'][l], x2d, tm=min(2048, T), tn=512)


def _dense_ffn(x2d, g, w1, w3, w2):
    T = x2d.shape[0]
    act = ffn_up(x2d, g, w1.astype(BF16), w3.astype(BF16), tm=min(1024, T), tn=1024)
    return matmul_residual(act, w2.astype(BF16), x2d, tm=min(1024, T), tn=256)


def _cumsum_rows(oh, blk=128):
    A, E = oh.shape
    nb = A // blk
    x = oh.astype(F32).reshape(nb, blk, E)
    within = jnp.einsum('ij,bje->bie', jnp.tril(jnp.ones((blk, blk), F32)), x)
    before = jnp.tril(jnp.ones((nb, nb), F32), -1) @ within[:, -1, :]
    return (within + before[:, None, :]).astype(jnp.int32).reshape(A, E)


def _moe_ffn(x2d, g, w_router, w1, w3, w2, *, tmb=512):
    T = x2d.shape[0]
    hp, idx, gate = router(x2d, g, _pad_cols(w_router, LANES), tm=256)
    A = T * TOP_K
    e_flat = idx[:, :TOP_K].reshape(A)
    tok_flat = jnp.repeat(jnp.arange(T, dtype=jnp.int32), TOP_K)
    oh = (e_flat[:, None] == jnp.arange(N_EXPERTS)[None, :]).astype(jnp.int32)
    csum = _cumsum_rows(oh)
    rank = jnp.sum(oh * csum, axis=1) - 1
    counts = csum[-1]
    padded = (counts + tmb - 1) // tmb * tmb
    pad_end = jnp.cumsum(padded)
    dest = (pad_end - padded)[e_flat] + rank
    n_blk = -(-(-(-A // tmb) + N_EXPERTS) // MOE_SUB) * MOE_SUB
    n_slot = n_blk * tmb
    slot_tok = jnp.full((n_slot,), T, jnp.int32).at[dest].set(tok_flat)
    eidx = jnp.arange(N_EXPERTS, dtype=jnp.int32)
    blk = jnp.arange(n_blk, dtype=jnp.int32)
    blk_exp = jnp.minimum(jnp.sum((pad_end[None, :] <= (blk * tmb)[:, None]).astype(jnp.int32), axis=1),
                          N_EXPERTS - 1).astype(jnp.int32)
    n_used = (pad_end[-1:] // tmb).astype(jnp.int32)
    present = counts > 0
    first_e = jnp.min(jnp.where(present, eidx, N_EXPERTS))
    last_e = jnp.max(jnp.where(present, eidx, -1))
    later = jnp.where(present[None, :] & (eidx[None, :] > eidx[:, None]), eidx[None, :], N_EXPERTS)
    next_e = jnp.min(later, axis=1)
    next_e = jnp.where(next_e == N_EXPERTS, first_e, next_e).astype(jnp.int32)
    starts = ((blk == 0) | (blk_exp != jnp.roll(blk_exp, 1))) & (blk < n_used[0])
    tables = (blk_exp, n_used, starts.astype(jnp.int32), next_e[blk_exp],
              (blk_exp == last_e).astype(jnp.int32))
    xb = hp[slot_tok]
    act = moe_up(tables, xb, w1, w3, tmb=tmb, tn=1024)
    tnd = 512
    yb = moe_down(tables, act, w2, tmb=tmb, tn=tnd)
    d2 = dest.reshape(T, TOP_K)
    return moe_combine(x2d, gate, yb[d2[:, 0]], yb[d2[:, 1]], tn=tnd, tm=512)


def kernel(x, attn_norm_g, w_in, nsa_q_norm_g, nsa_k_norm_g, cmp_pe_k, cmp_pe_v, w_cmp_k, w_cmp_v, mla_qa_norm_g, w_q_b, mla_kva_norm_g, w_kv_b, mla_q_norm_g, mla_qr_norm_g, mla_k_norm_g, mla_kr_norm_g, w_proj_nsa, w_proj_mla, w_out, ffn_norm_g, w_ff1, w_ff3, w_ff2, w_router, w_e1, w_e3, w_e2):
    p = dict(attn_norm_g=attn_norm_g, w_in=w_in, nsa_q_norm_g=nsa_q_norm_g, nsa_k_norm_g=nsa_k_norm_g,
             cmp_pe_k=cmp_pe_k, cmp_pe_v=cmp_pe_v, w_cmp_k=w_cmp_k, w_cmp_v=w_cmp_v,
             mla_qa_norm_g=mla_qa_norm_g, w_q_b=w_q_b, mla_kva_norm_g=mla_kva_norm_g, w_kv_b=w_kv_b,
             mla_q_norm_g=mla_q_norm_g, mla_qr_norm_g=mla_qr_norm_g, mla_k_norm_g=mla_k_norm_g,
             mla_kr_norm_g=mla_kr_norm_g, w_proj_nsa=w_proj_nsa, w_proj_mla=w_proj_mla, w_out=w_out)
    B, S, D = x.shape
    depth = w_in.shape[0]
    nc = S // CMP_STRIDE
    tabs = dict(tok=_rope_tabs(jnp.arange(S), NSA_ROT, S),
                cmp=_rope_tabs(jnp.arange(nc - 1) * CMP_STRIDE + CMP_LEN - 1, NSA_ROT, nc),
                mla=_mla_tabs(S))
    pw = _layout_layers(p)
    x2d = x.reshape(B * S, D)
    for l in range(depth):
        x2d = _attention_block(x2d, B, S, p, pw, l, tabs)
        if l % 2 == 0:
            x2d = _dense_ffn(x2d, ffn_norm_g[l], w_ff1[l // 2], w_ff3[l // 2], w_ff2[l // 2])
        else:
            x2d = _moe_ffn(x2d, ffn_norm_g[l], w_router[l // 2], w_e1[l // 2], w_e3[l // 2], w_e2[l // 2])
    return x2d.reshape(B, S, D)
```

```python
import functools

import numpy as np
import jax
import jax.numpy as jnp
from jax import lax
from jax.experimental import pallas as pl
from jax.experimental.pallas import tpu as pltpu

D_MODEL = 2048
HEAD_DIM = 128
NSA_HEADS = 8
NSA_KV_GROUPS = 2
NSA_HPG = NSA_HEADS // NSA_KV_GROUPS
NSA_ROT = HEAD_DIM // 4
CMP_LEN = 32
CMP_STRIDE = 16
SLC_LEN = 64
SLC_TOPK = 16
N_LOCAL_SLC = 2
WINDOW = 512
FORCE_SCORE = 1.0e4
MLA_HEADS = 8
Q_LORA = 768
KV_LORA = 512
QK_NOPE = 128
QK_ROPE = 64
V_HEAD = 128
ROPE_THETA = 500000.0
EPS = 1e-6
D_FF = 7168
N_EXPERTS = 8
TOP_K = 2

LANES = 128
MLA_QK_PAD = 256
NEG_INF = float("-inf")
LOG2E = 1.4426950408889634
BF16 = jnp.bfloat16
F32 = jnp.float32

OFF_QA = 0
OFF_G = 768
OFF_KR = 896
OFF_Q = 1024
OFF_KVA = 2048
OFF_KV = 2560
OFF_M = 4096
D_INP = 8192
NSA_KV_COLS = 3 * 2 * NSA_KV_GROUPS * HEAD_DIM
NSA_KVP_COLS = 2 * NSA_KV_GROUPS * HEAD_DIM
VT_CHUNK = 256

VMEM_LIMIT = 56 * 1024 * 1024


def _cparams(sem):
    return pltpu.CompilerParams(dimension_semantics=sem, vmem_limit_bytes=VMEM_LIMIT)


def _rms(x, g):
    ms = jnp.mean(x * x, axis=-1, keepdims=True)
    return x * lax.rsqrt(ms + EPS) * g


def _rope_lanes(y, ct, s1, s2, half):
    return y * ct + pltpu.roll(y, LANES - half, 1) * s1 + pltpu.roll(y, half, 1) * s2


def _sigmoid(x):
    return 1.0 / (1.0 + jnp.exp(-x))


def _nmm_kernel(x_ref, g_ref, w_ref, o_ref, h_scr):
    @pl.when(pl.program_id(1) == 0)
    def _():
        h_scr[...] = _rms(x_ref[...], g_ref[...]).astype(BF16)

    o_ref[...] = jnp.dot(h_scr[...], w_ref[...].astype(BF16),
                         preferred_element_type=F32).astype(o_ref.dtype)


def norm_matmul(x, g, w, *, tm, tn, out_dtype=F32):
    T, K = x.shape
    N = w.shape[1]
    return pl.pallas_call(
        _nmm_kernel,
        grid=(T // tm, N // tn),
        in_specs=[pl.BlockSpec((tm, K), lambda m, n: (m, 0)),
                  pl.BlockSpec((1, K), lambda m, n: (0, 0)),
                  pl.BlockSpec((K, tn), lambda m, n: (0, n))],
        out_specs=pl.BlockSpec((tm, tn), lambda m, n: (m, n)),
        out_shape=jax.ShapeDtypeStruct((T, N), out_dtype),
        scratch_shapes=[pltpu.VMEM((tm, K), BF16)],
        compiler_params=_cparams(("parallel", "arbitrary")),
        name="norm_matmul",
    )(x, g.reshape(1, K), w)


def _mmres_kernel(a_ref, w_ref, r_ref, o_ref):
    o_ref[...] = r_ref[...] + jnp.dot(a_ref[...], w_ref[...].astype(BF16), preferred_element_type=F32)


def matmul_residual(a, w, res, *, tm, tn):
    T, K = a.shape
    N = w.shape[1]
    return pl.pallas_call(
        _mmres_kernel,
        grid=(T // tm, N // tn),
        in_specs=[pl.BlockSpec((tm, K), lambda m, n: (m, 0)),
                  pl.BlockSpec((K, tn), lambda m, n: (0, n)),
                  pl.BlockSpec((tm, tn), lambda m, n: (m, n))],
        out_specs=pl.BlockSpec((tm, tn), lambda m, n: (m, n)),
        out_shape=jax.ShapeDtypeStruct((T, N), F32),
        compiler_params=_cparams(("parallel", "arbitrary")),
        name="matmul_residual",
    )(a, w, res)


def _prep_kernel(zg_ref, zq_ref, zc_ref, zs_ref, zw_ref, ct_ref, s1_ref, s2_ref, gn_ref,
                 qn_ref, kvp_ref, xc_ref, vts_ref, vtw_ref, gate_ref):
    ct, s1, s2 = ct_ref[...], s1_ref[...], s2_ref[...]
    half = NSA_ROT // 2
    gq = gn_ref[0:1, :]
    for h in range(NSA_HEADS):
        y = _rope_lanes(_rms(zq_ref[:, h * HEAD_DIM:(h + 1) * HEAD_DIM], gq), ct, s1, s2, half)
        qn_ref[:, h * HEAD_DIM:(h + 1) * HEAD_DIM] = (y * (LOG2E * HEAD_DIM ** -0.5)).astype(BF16)
    for br, zb_ref in enumerate((zc_ref, zs_ref, zw_ref)):
        for kv in range(2):
            for g in range(NSA_KV_GROUPS):
                c = (kv * NSA_KV_GROUPS + g) * HEAD_DIM
                y = zb_ref[:, c:c + HEAD_DIM]
                if kv == 0 and br > 0:
                    y = _rope_lanes(_rms(y, gn_ref[1 + br:2 + br, :]), ct, s1, s2, half)
                if br == 0:
                    xc_ref[kv * NSA_KV_GROUPS + g] = y.astype(BF16)
                elif kv == 0:
                    d = (br - 1) * NSA_KV_GROUPS * HEAD_DIM + g * HEAD_DIM
                    kvp_ref[:, d:d + HEAD_DIM] = y.astype(BF16)
                else:
                    (vts_ref if br == 1 else vtw_ref)[g] = y.T.astype(BF16)
    gate_ref[...] = _sigmoid(zg_ref[...])


def nsa_prep(z, tabs, gn, S, *, tm=VT_CHUNK):
    T = z.shape[0]
    ns = S // tm
    tab_spec = pl.BlockSpec((tm, LANES), lambda i: (i % ns, 0))
    vt_spec = pl.BlockSpec((None, NSA_KV_GROUPS, None, HEAD_DIM, tm), lambda i: (i // ns, 0, i % ns, 0, 0))
    vt_shape = jax.ShapeDtypeStruct((T // S, NSA_KV_GROUPS, ns, HEAD_DIM, tm), BF16)
    wq, wb = NSA_HEADS * HEAD_DIM, 2 * NSA_KV_GROUPS * HEAD_DIM
    npl = 2 * NSA_KV_GROUPS
    return pl.pallas_call(
        _prep_kernel,
        grid=(T // tm,),
        in_specs=[pl.BlockSpec((tm, LANES), lambda i: (i, OFF_G // LANES)),
                  pl.BlockSpec((tm, wq), lambda i: (i, OFF_Q // wq)),
                  pl.BlockSpec((tm, wb), lambda i: (i, OFF_KV // wb)),
                  pl.BlockSpec((tm, wb), lambda i: (i, OFF_KV // wb + 1)),
                  pl.BlockSpec((tm, wb), lambda i: (i, OFF_KV // wb + 2)),
                  tab_spec, tab_spec, tab_spec, pl.BlockSpec((8, LANES), lambda i: (0, 0))],
        out_specs=[pl.BlockSpec((tm, NSA_HEADS * HEAD_DIM), lambda i: (i, 0)),
                   pl.BlockSpec((tm, NSA_KVP_COLS), lambda i: (i, 0)),
                   pl.BlockSpec((None, npl, tm, HEAD_DIM), lambda i: (i // ns, 0, i % ns, 0)), vt_spec, vt_spec,
                   pl.BlockSpec((tm, LANES), lambda i: (i, 0))],
        out_shape=[jax.ShapeDtypeStruct((T, NSA_HEADS * HEAD_DIM), BF16),
                   jax.ShapeDtypeStruct((T, NSA_KVP_COLS), BF16),
                   jax.ShapeDtypeStruct((T // S, npl, S, HEAD_DIM), BF16), vt_shape, vt_shape,
                   jax.ShapeDtypeStruct((T, LANES), F32)],
        compiler_params=_cparams(("parallel",)),
        name="nsa_prep",
    )(z, z, z, z, z, *tabs, gn)


def _cmp_kernel(xk_ref, xv_ref, wk_ref, wv_ref, pek_ref, pev_ref, gk_ref, ct_ref, s1_ref, s2_ref,
                kct_ref, vc_ref):
    nc = xk_ref.shape[0]
    row = lax.broadcasted_iota(jnp.int32, (nc, HEAD_DIM), 0)

    def comp(x_ref, w_ref, pe_ref):
        w = w_ref[...]
        y = jnp.dot(x_ref[...], w, preferred_element_type=F32)
        ype = jnp.dot(pe_ref[...], w, preferred_element_type=F32)
        bias = ype[0:1, :HEAD_DIM] + ype[1:2, HEAD_DIM:]
        out = y[:, :HEAD_DIM] + pltpu.roll(y[:, HEAD_DIM:], nc - 1, 0) + bias
        return jnp.where(row < nc - 1, out, 0.0)

    k = _rms(comp(xk_ref, wk_ref, pek_ref), gk_ref[...])
    k = _rope_lanes(k, ct_ref[...], s1_ref[...], s2_ref[...], NSA_ROT // 2)
    kct_ref[...] = k.T.astype(BF16)
    vc_ref[...] = comp(xv_ref, wv_ref, pev_ref).astype(BF16)


def nsa_compress(x2, wk2, wv2, pek2, pev2, gk, tabs_cmp):
    B, _, nc, kk = x2.shape
    G = NSA_KV_GROUPS
    full = lambda shape: pl.BlockSpec(shape, lambda b, g: (0,) * len(shape))
    return pl.pallas_call(
        _cmp_kernel,
        grid=(B, G),
        in_specs=[pl.BlockSpec((None, None, nc, kk), lambda b, g: (b, g, 0, 0)),
                  pl.BlockSpec((None, None, nc, kk), lambda b, g: (b, G + g, 0, 0)),
                  full((kk, 2 * HEAD_DIM)), full((kk, 2 * HEAD_DIM)),
                  full((8, kk)), full((8, kk)), full((1, HEAD_DIM)),
                  full((nc, LANES)), full((nc, LANES)), full((nc, LANES))],
        out_specs=[pl.BlockSpec((None, None, HEAD_DIM, nc), lambda b, g: (b, g, 0, 0)),
                   pl.BlockSpec((None, None, nc, HEAD_DIM), lambda b, g: (b, g, 0, 0))],
        out_shape=[jax.ShapeDtypeStruct((B, G, HEAD_DIM, nc), BF16),
                   jax.ShapeDtypeStruct((B, G, nc, HEAD_DIM), BF16)],
        compiler_params=_cparams(("parallel", "parallel")),
        name="nsa_compress",
    )(x2, x2, wk2, wv2, pek2, pev2, gk, *tabs_cmp)


def _cattn_kernel(q_ref, kct_ref, vc_ref, ov_ref, oc_ref, sel_ref, *, tq):
    nc = vc_ref.shape[0]
    ns = sel_ref.shape[0]
    t0 = pl.program_id(2) * tq
    t_pos = t0 + lax.broadcasted_iota(jnp.int32, (tq, nc), 0)
    n_idx = lax.broadcasted_iota(jnp.int32, (tq, nc), 1)
    vis = (t_pos >= n_idx * CMP_STRIDE + (CMP_LEN - 1)) & (n_idx < nc - 1)
    bias = jnp.where(vis, 0.0, NEG_INF)
    kct = kct_ref[...]
    vc = vc_ref[...]
    psum = jnp.zeros((tq, nc), F32)
    for hh in range(NSA_HPG):
        s = jnp.dot(q_ref[:, hh * HEAD_DIM:(hh + 1) * HEAD_DIM], kct, preferred_element_type=F32) + bias
        m = jnp.max(s, axis=-1, keepdims=True)
        m = jnp.where(m == NEG_INF, 0.0, m)
        p = jnp.exp2(s - m)
        den = jnp.sum(p, axis=-1, keepdims=True)
        p = p * (1.0 / jnp.where(den > 0.0, den, 1.0))
        oc_ref[:, hh * HEAD_DIM:(hh + 1) * HEAD_DIM] = jnp.dot(
            p.astype(BF16), vc, preferred_element_type=F32).astype(oc_ref.dtype)
        psum = psum + p
    pt = psum.T
    hi = pt.astype(BF16)
    lo = (pt - hi.astype(F32)).astype(BF16)
    ov = ov_ref[...]
    imp = jnp.dot(ov, hi, preferred_element_type=F32) + jnp.dot(ov, lo, preferred_element_type=F32)
    blk = lax.broadcasted_iota(jnp.int32, (ns, tq), 0)
    cur = (t0 + lax.broadcasted_iota(jnp.int32, (ns, tq), 1)) // SLC_LEN
    forced = (blk == 0) | ((blk <= cur) & (blk > cur - N_LOCAL_SLC))
    val = jnp.where(blk > cur, NEG_INF, jnp.where(forced, FORCE_SCORE, imp))
    rank = jnp.zeros((ns, tq), F32)
    for i in range(ns):
        other = val[i:i + 1, :]
        ahead = (other > val) | ((other == val) & (blk > i))
        rank = rank + jnp.where(ahead, 1.0, 0.0)
    sel_ref[...] = jnp.where(rank < float(min(SLC_TOPK, ns)), 1.0, 0.0).astype(sel_ref.dtype)


def nsa_cmp_attention(qn, kct, vc, ov, B, S, *, tq):
    G = NSA_KV_GROUPS
    nq = S // tq
    nc = vc.shape[2]
    ns = S // SLC_LEN
    gw = NSA_HPG * HEAD_DIM
    return pl.pallas_call(
        functools.partial(_cattn_kernel, tq=tq),
        grid=(B, G, nq),
        in_specs=[pl.BlockSpec((tq, gw), lambda b, g, i: (b * nq + i, g)),
                  pl.BlockSpec((None, None, HEAD_DIM, nc), lambda b, g, i: (b, g, 0, 0)),
                  pl.BlockSpec((None, None, nc, HEAD_DIM), lambda b, g, i: (b, g, 0, 0)),
                  pl.BlockSpec((ns, nc), lambda b, g, i: (0, 0))],
        out_specs=[pl.BlockSpec((tq, gw), lambda b, g, i: (b * nq + i, g)),
                   pl.BlockSpec((None, None, ns, tq), lambda b, g, i: (b, g, 0, i))],
        out_shape=[jax.ShapeDtypeStruct((B * S, NSA_HEADS * HEAD_DIM), BF16),
                   jax.ShapeDtypeStruct((B, G, ns, S), BF16)],
        compiler_params=_cparams(("parallel", "parallel", "parallel")),
        name="nsa_cmp_attention",
    )(qn, kct, vc, ov)


def _flash_kernel(*refs, mode, hg, tq, tk, tv, dqk, dv):
    if mode == "sel":
        q_ref, k_ref, vt_ref, sel_ref, ex_ref, o_ref, qs, m_s, l_s, acc_s, sa, sb = refs
    else:
        q_ref, k_ref, vt_ref, o_ref, qs, m_s, l_s, acc_s, sa, sb = refs
    rows = hg * tq
    q0 = pl.program_id(2) * tq
    cd = q0 // tk
    for hh in range(hg):
        qs[hh * tq:(hh + 1) * tq, :] = q_ref[:, hh * dqk:(hh + 1) * dqk]

    def scores(c, kind):
        start = pl.multiple_of(c * tk, tk)
        s = lax.dot_general(k_ref[pl.ds(start, tk), :], qs[...], (((1,), (1,)), ((), ())),
                            preferred_element_type=F32)
        ok = None
        if kind == "diag":
            k_pos = start + lax.broadcasted_iota(jnp.int32, (tk, tq), 0)
            q_pos = q0 + lax.broadcasted_iota(jnp.int32, (tk, tq), 1)
            ok = k_pos <= q_pos
        if mode == "sel":
            chosen = jnp.dot(ex_ref[c], sel_ref[...], preferred_element_type=F32) > 0.5
            ok = chosen if ok is None else (ok & chosen)
        if ok is not None:
            bias = jnp.where(ok, 0.0, NEG_INF)
            s = s + (jnp.concatenate([bias] * hg, axis=1) if hg > 1 else bias)
        return s

    def update(c, s, carry):
        m, l, acc = carry
        m_new = jnp.maximum(m, jnp.max(s, axis=0, keepdims=True))
        p = jnp.exp2(s - m_new)
        alpha = jnp.exp2(m - m_new)
        l = alpha * l + jnp.sum(p, axis=0, keepdims=True)
        r = tk // tv
        vt = vt_ref[c] if r == 1 else jnp.concatenate([vt_ref[c * r + j] for j in range(r)], axis=1)
        acc = alpha * acc + jnp.dot(vt, p.astype(BF16), preferred_element_type=F32)
        return m_new, l, acc

    def load():
        return m_s[...], l_s[...], acc_s[...]

    def store(carry):
        m_s[...], l_s[...], acc_s[...] = carry

    nd = max(tq // tk, 1)
    last = k_ref.shape[0] // tk - 1
    carry = (jnp.full((1, rows), NEG_INF, F32), jnp.zeros((1, rows), F32), jnp.zeros((dv, rows), F32))
    s_next = scores(cd, "diag")
    for j in range(nd):
        s_cur = s_next
        if j + 1 < nd:
            s_next = scores(cd + j + 1, "diag")
        else:
            sa[...] = scores(0, "full")
        carry = update(cd + j, s_cur, carry)
    store(carry)

    def pairs(first, count):
        carry = load()
        for j in range(count):
            c = first + 2 * j
            sb[...] = scores(c + 1, "full")
            carry = update(c, sa[...], carry)
            sa[...] = scores(jnp.minimum(c + 2, last), "full")
            carry = update(c + 1, sb[...], carry)
        store(carry)

    def four(i, _):
        pairs(4 * i, 2)
        return 0

    lax.fori_loop(0, cd // 4, four, 0)

    @pl.when(cd % 4 >= 2)
    def _():
        pairs((cd // 4) * 4, 1)

    @pl.when(cd % 2 == 1)
    def _():
        store(update(cd - 1, sa[...], load()))

    o = acc_s[...] * (1.0 / l_s[...])
    for hh in range(hg):
        o_ref[:, hh * dv:(hh + 1) * dv] = o[:, hh * tq:(hh + 1) * tq].T.astype(o_ref.dtype)


def flash_attention(q, k, vt, B, S, *, mode, n_groups, hg, dqk, dv, kcol0, tq, tk, sel_t=None, expand=None):
    nq = S // tq
    nch = S // tk
    tv = vt.shape[-1]
    assert (tk % tq == 0 or tq % tk == 0) and S % tk == 0 and S % tq == 0 and tq % LANES == 0 and tk % tv == 0
    in_specs = [pl.BlockSpec((tq, hg * dqk), lambda b, g, i: (b * nq + i, g)),
                pl.BlockSpec((S, dqk), lambda b, g, i: (b, kcol0 + g)),
                pl.BlockSpec((None, None, S // tv, dv, tv), lambda b, g, i: (b, g, 0, 0, 0))]
    args = [q, k, vt]
    if mode == "sel":
        ns = S // SLC_LEN
        in_specs += [pl.BlockSpec((None, None, ns, tq), lambda b, g, i: (b, g, 0, i)),
                     pl.BlockSpec((nch, tk, ns), lambda b, g, i: (0, 0, 0))]
        args += [sel_t, expand]
    return pl.pallas_call(
        functools.partial(_flash_kernel, mode=mode, hg=hg, tq=tq, tk=tk, tv=tv, dqk=dqk, dv=dv),
        grid=(B, n_groups, nq),
        in_specs=in_specs,
        out_specs=pl.BlockSpec((tq, hg * dv), lambda b, g, i: (b * nq + i, g)),
        out_shape=jax.ShapeDtypeStruct((B * S, n_groups * hg * dv), BF16),
        scratch_shapes=[pltpu.VMEM((hg * tq, dqk), BF16),
                        pltpu.VMEM((1, hg * tq), F32),
                        pltpu.VMEM((1, hg * tq), F32),
                        pltpu.VMEM((dv, hg * tq), F32),
                        pltpu.VMEM((tk, hg * tq), F32),
                        pltpu.VMEM((tk, hg * tq), F32)],
        compiler_params=_cparams(("parallel", "parallel", "arbitrary")),
        name="flash_" + mode,
    )(*args)


def _window_kernel(q_ref, k_ref, vt_ref, oc_ref, os_ref, gate_ref, o_ref, qs, *, hg, tq, dqk, dv):
    nk = WINDOW + tq
    q0 = pl.program_id(2) * tq
    start = pl.multiple_of(jnp.maximum(q0 - WINDOW, 0), tq)
    for hh in range(hg):
        qs[hh * tq:(hh + 1) * tq, :] = q_ref[:, hh * dqk:(hh + 1) * dqk]
    s = lax.dot_general(k_ref[pl.ds(start, nk), :], qs[...], (((1,), (1,)), ((), ())),
                        preferred_element_type=F32)
    k_pos = start + lax.broadcasted_iota(jnp.int32, (nk, tq), 0)
    q_pos = q0 + lax.broadcasted_iota(jnp.int32, (nk, tq), 1)
    bias = jnp.where((k_pos <= q_pos) & (q_pos - k_pos < WINDOW), 0.0, NEG_INF)
    s = s + jnp.concatenate([bias] * hg, axis=1)
    p = jnp.exp2(s - jnp.max(s, axis=0, keepdims=True))
    l = jnp.sum(p, axis=0, keepdims=True)
    pb = p.astype(BF16)
    c0 = start // tq
    o = jnp.zeros((dv, hg * tq), F32)
    for j in range(nk // tq):
        o = o + jnp.dot(vt_ref[c0 + j], pb[j * tq:(j + 1) * tq, :], preferred_element_type=F32)
    o = o * (1.0 / l)
    first_group = pl.program_id(1) == 0
    n_heads = 2 * hg

    def gate(br, hh):
        lo = gate_ref[:, br * n_heads + hh:br * n_heads + hh + 1]
        hi = gate_ref[:, br * n_heads + hg + hh:br * n_heads + hg + hh + 1]
        return jnp.where(first_group, lo, hi)

    for hh in range(hg):
        sl = slice(hh * dv, (hh + 1) * dv)
        mixed = (gate(0, hh) * oc_ref[:, sl].astype(F32) + gate(1, hh) * os_ref[:, sl].astype(F32)
                 + gate(2, hh) * o[:, hh * tq:(hh + 1) * tq].T)
        o_ref[:, sl] = mixed.astype(o_ref.dtype)


def window_attention(q, k, vt, oc, os_, gates, B, S, *, n_groups, hg, dqk, dv, kcol0, tq):
    assert n_groups == NSA_KV_GROUPS
    nq = S // tq
    assert WINDOW % tq == 0 and S >= WINDOW + tq and tq % LANES == 0
    return pl.pallas_call(
        functools.partial(_window_kernel, hg=hg, tq=tq, dqk=dqk, dv=dv),
        grid=(B, n_groups, nq),
        in_specs=[pl.BlockSpec((tq, hg * dqk), lambda b, g, i: (b * nq + i, g)),
                  pl.BlockSpec((S, dqk), lambda b, g, i: (b, kcol0 + g)),
                  pl.BlockSpec((None, None, nq, dv, tq), lambda b, g, i: (b, g, 0, 0, 0)),
                  pl.BlockSpec((tq, hg * dv), lambda b, g, i: (b * nq + i, g)),
                  pl.BlockSpec((tq, hg * dv), lambda b, g, i: (b * nq + i, g)),
                  pl.BlockSpec((tq, LANES), lambda b, g, i: (b * nq + i, 0))],
        out_specs=pl.BlockSpec((tq, hg * dv), lambda b, g, i: (b * nq + i, g)),
        out_shape=jax.ShapeDtypeStruct((B * S, n_groups * hg * dv), BF16),
        scratch_shapes=[pltpu.VMEM((hg * tq, dqk), BF16)],
        compiler_params=_cparams(("parallel", "parallel", "arbitrary")),
        name="window_attention",
    )(q, k, vt, oc, os_, gates)


def _mla_q_kernel(z_ref, ga_ref, w_ref, gh_ref, ct_ref, s1_ref, s2_ref, o_ref):
    h = _rms(z_ref[...], ga_ref[...]).astype(BF16)
    y = jnp.dot(h, w_ref[...], preferred_element_type=F32)
    ct, s1, s2 = ct_ref[...], s1_ref[...], s2_ref[...]
    scale = LOG2E * (QK_NOPE + QK_ROPE) ** -0.5
    for hd in range(MLA_HEADS):
        c = hd * MLA_QK_PAD
        nope = _rms(y[:, c:c + QK_NOPE], gh_ref[0:1, :])
        r = y[:, c + QK_NOPE:c + MLA_QK_PAD]
        ms = jnp.sum(r * r, axis=-1, keepdims=True) * (1.0 / QK_ROPE)
        r = _rope_lanes(r * lax.rsqrt(ms + EPS) * gh_ref[1:2, :], ct, s1, s2, QK_ROPE // 2)
        o_ref[:, c:c + QK_NOPE] = (nope * scale).astype(BF16)
        o_ref[:, c + QK_NOPE:c + MLA_QK_PAD] = (r * scale).astype(BF16)


def mla_q_proj(z, ga, wq, gh, tabs, S, *, tm):
    T = z.shape[0]
    ns = S // tm
    tab_spec = pl.BlockSpec((tm, LANES), lambda i: (i % ns, 0))
    nout = MLA_HEADS * MLA_QK_PAD
    return pl.pallas_call(
        _mla_q_kernel,
        grid=(T // tm,),
        in_specs=[pl.BlockSpec((tm, Q_LORA), lambda i: (i, OFF_QA // Q_LORA)),
                  pl.BlockSpec((1, Q_LORA), lambda i: (0, 0)),
                  pl.BlockSpec((Q_LORA, nout), lambda i: (0, 0)),
                  pl.BlockSpec((8, LANES), lambda i: (0, 0)),
                  tab_spec, tab_spec, tab_spec],
        out_specs=pl.BlockSpec((tm, nout), lambda i: (i, 0)),
        out_shape=jax.ShapeDtypeStruct((T, nout), BF16),
        compiler_params=_cparams(("parallel",)),
        name="mla_q_proj",
    )(z, ga, wq, gh, *tabs)


def _mla_kv_kernel(z_ref, zr_ref, ga_ref, w_ref, gh_ref, ct_ref, s1_ref, s2_ref, k_ref, v_ref):
    h = _rms(z_ref[...], ga_ref[...]).astype(BF16)
    y = jnp.dot(h, w_ref[...], preferred_element_type=F32)
    r = zr_ref[...]
    ms = jnp.sum(r * r, axis=-1, keepdims=True) * (1.0 / QK_ROPE)
    r = _rope_lanes(r * lax.rsqrt(ms + EPS) * gh_ref[1:2, :], ct_ref[...], s1_ref[...], s2_ref[...],
                    QK_ROPE // 2).astype(BF16)
    nv = MLA_HEADS * QK_NOPE
    for hd in range(MLA_HEADS):
        c = hd * MLA_QK_PAD
        k_ref[:, c:c + QK_NOPE] = _rms(y[:, hd * QK_NOPE:(hd + 1) * QK_NOPE], gh_ref[0:1, :]).astype(BF16)
        k_ref[:, c + QK_NOPE:c + MLA_QK_PAD] = r
    for hd in range(MLA_HEADS):
        v_ref[hd] = y[:, nv + hd * V_HEAD:nv + (hd + 1) * V_HEAD].T.astype(BF16)


def mla_kv_proj(z, ga, wkv, gh, tabs, S, *, tm=VT_CHUNK):
    T = z.shape[0]
    ns = S // tm
    tab_spec = pl.BlockSpec((tm, LANES), lambda i: (i % ns, 0))
    nk = MLA_HEADS * MLA_QK_PAD
    nv = MLA_HEADS * V_HEAD
    return pl.pallas_call(
        _mla_kv_kernel,
        grid=(T // tm,),
        in_specs=[pl.BlockSpec((tm, KV_LORA), lambda i: (i, OFF_KVA // KV_LORA)),
                  pl.BlockSpec((tm, LANES), lambda i: (i, OFF_KR // LANES)),
                  pl.BlockSpec((1, KV_LORA), lambda i: (0, 0)),
                  pl.BlockSpec((KV_LORA, MLA_HEADS * (QK_NOPE + V_HEAD)), lambda i: (0, 0)),
                  pl.BlockSpec((8, LANES), lambda i: (0, 0)),
                  tab_spec, tab_spec, tab_spec],
        out_specs=[pl.BlockSpec((tm, nk), lambda i: (i, 0)),
                   pl.BlockSpec((None, MLA_HEADS, None, V_HEAD, tm), lambda i: (i // ns, 0, i % ns, 0, 0))],
        out_shape=[jax.ShapeDtypeStruct((T, nk), BF16),
                   jax.ShapeDtypeStruct((T // S, MLA_HEADS, ns, V_HEAD, tm), BF16)],
        compiler_params=_cparams(("parallel",)),
        name="mla_kv_proj",
    )(z, z, ga, wkv, gh, *tabs)


def _mix_kernel(a_ref, ob_ref, wa_ref, wb_ref, za_ref, zb_ref, o_ref):
    pa = jnp.dot(a_ref[...], wa_ref[...].astype(BF16), preferred_element_type=F32)
    pb = jnp.dot(ob_ref[...], wb_ref[...].astype(BF16), preferred_element_type=F32)
    o_ref[...] = (_sigmoid(za_ref[...]) * pa + _sigmoid(zb_ref[...]) * pb).astype(o_ref.dtype)


def gated_mix(oa, ob, wa, wb, z, *, tm, tn):
    T, K = oa.shape
    N = wa.shape[1]
    row = lambda w: pl.BlockSpec((tm, w), lambda m, n: (m, 0))
    return pl.pallas_call(
        _mix_kernel,
        grid=(T // tm, N // tn),
        in_specs=[row(K), row(K),
                  pl.BlockSpec((K, tn), lambda m, n: (0, n)),
                  pl.BlockSpec((K, tn), lambda m, n: (0, n)),
                  pl.BlockSpec((tm, tn), lambda m, n: (m, OFF_M // tn + n)),
                  pl.BlockSpec((tm, tn), lambda m, n: (m, (OFF_M + D_MODEL) // tn + n))],
        out_specs=pl.BlockSpec((tm, tn), lambda m, n: (m, n)),
        out_shape=jax.ShapeDtypeStruct((T, N), BF16),
        compiler_params=_cparams(("parallel", "parallel")),
        name="gated_mix",
    )(oa, ob, wa, wb, z, z)


def _ffn_up_kernel(x_ref, g_ref, w1_ref, w3_ref, o_ref, h_scr):
    @pl.when(pl.program_id(1) == 0)
    def _():
        h_scr[...] = _rms(x_ref[...], g_ref[...]).astype(BF16)

    h = h_scr[...]
    a = jnp.dot(h, w1_ref[...].astype(BF16), preferred_element_type=F32)
    b = jnp.dot(h, w3_ref[...].astype(BF16), preferred_element_type=F32)
    o_ref[...] = (a * _sigmoid(a) * b).astype(o_ref.dtype)


def ffn_up(x, g, w1, w3, *, tm, tn):
    T, K = x.shape
    N = w1.shape[1]
    return pl.pallas_call(
        _ffn_up_kernel,
        grid=(T // tm, N // tn),
        in_specs=[pl.BlockSpec((tm, K), lambda m, n: (m, 0)),
                  pl.BlockSpec((1, K), lambda m, n: (0, 0)),
                  pl.BlockSpec((K, tn), lambda m, n: (0, n)),
                  pl.BlockSpec((K, tn), lambda m, n: (0, n))],
        out_specs=pl.BlockSpec((tm, tn), lambda m, n: (m, n)),
        out_shape=jax.ShapeDtypeStruct((T, N), BF16),
        scratch_shapes=[pltpu.VMEM((tm, K), BF16)],
        compiler_params=_cparams(("parallel", "arbitrary")),
        name="ffn_up",
    )(x, g.reshape(1, K), w1, w3)


def _pack_bf16_pairs(h):
    k = h.shape[1] // 2
    hi = lax.bitcast_convert_type(h[:, :k].astype(jnp.bfloat16).astype(F32), jnp.uint32)
    lo = lax.bitcast_convert_type(h[:, k:].astype(jnp.bfloat16).astype(F32), jnp.uint32)
    return lax.bitcast_convert_type(hi | (lo >> 16), F32)


def _unpack_bf16_pairs(xp):
    xp = lax.bitcast_convert_type(xp, jnp.uint32)
    hi = lax.bitcast_convert_type(xp & jnp.uint32(0xFFFF0000), F32)
    lo = lax.bitcast_convert_type(xp << 16, F32)
    return hi.astype(BF16), lo.astype(BF16)


def _router_kernel(x_ref, g_ref, wr_ref, h_ref, idx_ref, gate_ref):
    h = _rms(x_ref[...], g_ref[...])
    is_pad = pl.program_id(0) == pl.num_programs(0) - 1
    h_ref[...] = jnp.where(is_pad, 0.0, _pack_bf16_pairs(h))
    logits = jnp.dot(h, wr_ref[...], preferred_element_type=F32, precision=lax.Precision.HIGHEST)
    lane = lax.broadcasted_iota(jnp.int32, logits.shape, 1).astype(F32)
    logits = jnp.where(lane < float(N_EXPERTS), logits, NEG_INF)
    m1 = jnp.max(logits, axis=-1, keepdims=True)
    i1 = jnp.min(jnp.where(logits == m1, lane, float(LANES)), axis=-1, keepdims=True)
    rest = jnp.where(lane == i1, NEG_INF, logits)
    m2 = jnp.max(rest, axis=-1, keepdims=True)
    i2 = jnp.min(jnp.where(rest == m2, lane, float(LANES)), axis=-1, keepdims=True)
    e = jnp.exp(m2 - m1)
    den = 1.0 + e
    idx_ref[...] = jnp.where(lane == 0.0, i1, jnp.where(lane == 1.0, i2, 0.0)).astype(jnp.int32)
    gate_ref[...] = jnp.where(lane == 0.0, 1.0 / den, jnp.where(lane == 1.0, e / den, 0.0))


def router(x, g, wr_pad, *, tm):
    T, K = x.shape
    nt = T // tm
    return pl.pallas_call(
        _router_kernel,
        grid=(nt + 1,),
        in_specs=[pl.BlockSpec((tm, K), lambda i: (jnp.minimum(i, nt - 1), 0)),
                  pl.BlockSpec((1, K), lambda i: (0, 0)),
                  pl.BlockSpec((K, LANES), lambda i: (0, 0))],
        out_specs=[pl.BlockSpec((tm, K // 2), lambda i: (i, 0)),
                   pl.BlockSpec((tm, LANES), lambda i: (jnp.minimum(i, nt - 1), 0)),
                   pl.BlockSpec((tm, LANES), lambda i: (jnp.minimum(i, nt - 1), 0))],
        out_shape=[jax.ShapeDtypeStruct((T + tm, K // 2), F32),
                   jax.ShapeDtypeStruct((T, LANES), jnp.int32),
                   jax.ShapeDtypeStruct((T, LANES), F32)],
        compiler_params=_cparams(("arbitrary",)),
        name="router",
    )(x, g.reshape(1, K), wr_pad)


def _moe_weight_stream(w_hbms, stages, casts, sems, be_ref, nu_ref, gs_ref, ne_ref, lg_ref, tn, r):
    n = pl.program_id(0)
    used = r < nu_ref[0]

    def copies(e, nt):
        c0 = pl.multiple_of(nt * tn, tn)
        return [pltpu.make_async_copy(w.at[e, :, pl.ds(c0, tn)], st, sems.at[i])
                for i, (w, st) in enumerate(zip(w_hbms, stages))]

    @pl.when((n == 0) & (r == 0))
    def _():
        for c in copies(be_ref[0], 0):
            c.start()

    @pl.when(used & (gs_ref[r] == 1))
    def _():
        for c in copies(be_ref[r], n):
            c.wait()
        for st, wb in zip(stages, casts):
            wb[...] = st[...].astype(BF16)
        last = lg_ref[r] == 1

        @pl.when(jnp.logical_not(last & (n == pl.num_programs(0) - 1)))
        def _():
            for c in copies(ne_ref[r], n + last.astype(jnp.int32)):
                c.start()

    return used


MOE_SUB = 2


def _moe_up_kernel(be_ref, nu_ref, gs_ref, ne_ref, lg_ref, x_ref, w1_hbm, w3_hbm, o_ref,
                   st1, st3, w1_s, w3_s, sems, *, tn, tmb):
    for sub in range(MOE_SUB):
        rows = slice(sub * tmb, (sub + 1) * tmb)
        used = _moe_weight_stream((w1_hbm, w3_hbm), (st1, st3), (w1_s, w3_s), sems,
                                  be_ref, nu_ref, gs_ref, ne_ref, lg_ref, tn, pl.program_id(1) * MOE_SUB + sub)

        @pl.when(used)
        def _():
            xa, xb = _unpack_bf16_pairs(x_ref[rows, :])
            k2 = xa.shape[1]
            a = (jnp.dot(xa, w1_s[:k2, :], preferred_element_type=F32)
                 + jnp.dot(xb, w1_s[k2:, :], preferred_element_type=F32))
            b = (jnp.dot(xa, w3_s[:k2, :], preferred_element_type=F32)
                 + jnp.dot(xb, w3_s[k2:, :], preferred_element_type=F32))
            o_ref[rows, :] = (a * _sigmoid(a) * b).astype(o_ref.dtype)

        @pl.when(jnp.logical_not(used))
        def _():
            o_ref[rows, :] = jnp.zeros((tmb, o_ref.shape[1]), o_ref.dtype)


def _moe_specs(tmb, kx, wout):
    step = lambda r, nu: jnp.minimum(r, (nu[0] - 1) // MOE_SUB)
    return (pl.BlockSpec((MOE_SUB * tmb, kx), lambda n, r, be, nu, gs, ne, lg: (step(r, nu), 0)),
            pl.BlockSpec((MOE_SUB * tmb, wout), lambda n, r, be, nu, gs, ne, lg: (r, n)))


def moe_up(tables, xb, w1, w3, *, tmb, tn):
    n_slot = xb.shape[0]
    K, N = w1.shape[1], w1.shape[2]
    x_spec, o_spec = _moe_specs(tmb, K // 2, tn)
    return pl.pallas_call(
        functools.partial(_moe_up_kernel, tn=tn, tmb=tmb),
        grid_spec=pltpu.PrefetchScalarGridSpec(
            num_scalar_prefetch=5,
            grid=(N // tn, n_slot // (MOE_SUB * tmb)),
            in_specs=[x_spec, pl.BlockSpec(memory_space=pl.ANY), pl.BlockSpec(memory_space=pl.ANY)],
            out_specs=o_spec,
            scratch_shapes=[pltpu.VMEM((K, tn), F32), pltpu.VMEM((K, tn), F32),
                            pltpu.VMEM((K, tn), BF16), pltpu.VMEM((K, tn), BF16),
                            pltpu.SemaphoreType.DMA((2,))]),
        out_shape=jax.ShapeDtypeStruct((n_slot, N), BF16),
        compiler_params=_cparams(("arbitrary", "arbitrary")),
        name="moe_up",
    )(*tables, xb, w1, w3)


def _moe_down_kernel(be_ref, nu_ref, gs_ref, ne_ref, lg_ref, a_ref, w2_hbm, o_ref, st2, w2_s, sems, *, tn, tmb):
    for sub in range(MOE_SUB):
        rows = slice(sub * tmb, (sub + 1) * tmb)
        used = _moe_weight_stream((w2_hbm,), (st2,), (w2_s,), sems, be_ref, nu_ref, gs_ref, ne_ref, lg_ref, tn,
                                  pl.program_id(1) * MOE_SUB + sub)

        @pl.when(used)
        def _():
            o_ref[rows, :] = _pack_bf16_pairs(jnp.dot(a_ref[rows, :], w2_s[...], preferred_element_type=F32))

        @pl.when(jnp.logical_not(used))
        def _():
            o_ref[rows, :] = jnp.zeros((tmb, o_ref.shape[1]), o_ref.dtype)


def moe_down(tables, act, w2, *, tmb, tn):
    n_slot, K = act.shape
    N = w2.shape[2]
    a_spec, o_spec = _moe_specs(tmb, K, tn // 2)
    return pl.pallas_call(
        functools.partial(_moe_down_kernel, tn=tn, tmb=tmb),
        grid_spec=pltpu.PrefetchScalarGridSpec(
            num_scalar_prefetch=5,
            grid=(N // tn, n_slot // (MOE_SUB * tmb)),
            in_specs=[a_spec, pl.BlockSpec(memory_space=pl.ANY)],
            out_specs=o_spec,
            scratch_shapes=[pltpu.VMEM((K, tn), F32), pltpu.VMEM((K, tn), BF16),
                            pltpu.SemaphoreType.DMA((1,))]),
        out_shape=jax.ShapeDtypeStruct((n_slot, N // 2), F32),
        compiler_params=_cparams(("arbitrary", "arbitrary")),
        name="moe_down",
    )(*tables, act, w2)


def _combine_kernel(x_ref, g_ref, y0_ref, y1_ref, o_ref, *, tn):
    g0, g1 = g_ref[:, 0:1], g_ref[:, 1:2]
    h = tn // 2

    def halves(y_ref, n):
        u = lax.bitcast_convert_type(y_ref[:, n * h:(n + 1) * h], jnp.uint32)
        return (lax.bitcast_convert_type(u & jnp.uint32(0xFFFF0000), F32),
                lax.bitcast_convert_type(u << 16, F32))

    for n in range(x_ref.shape[1] // tn):
        a_hi, a_lo = halves(y0_ref, n)
        b_hi, b_lo = halves(y1_ref, n)
        lo_cols = slice(n * tn, n * tn + h)
        hi_cols = slice(n * tn + h, (n + 1) * tn)
        o_ref[:, lo_cols] = x_ref[:, lo_cols] + g0 * a_hi + g1 * b_hi
        o_ref[:, hi_cols] = x_ref[:, hi_cols] + g0 * a_lo + g1 * b_lo


def moe_combine(x, gate, y0, y1, *, tn, tm):
    T, N = x.shape
    return pl.pallas_call(
        functools.partial(_combine_kernel, tn=tn),
        grid=(T // tm,),
        in_specs=[pl.BlockSpec((tm, N), lambda i: (i, 0)),
                  pl.BlockSpec((tm, LANES), lambda i: (i, 0)),
                  pl.BlockSpec((tm, N // 2), lambda i: (i, 0)),
                  pl.BlockSpec((tm, N // 2), lambda i: (i, 0))],
        out_specs=pl.BlockSpec((tm, N), lambda i: (i, 0)),
        out_shape=jax.ShapeDtypeStruct((T, N), F32),
        compiler_params=_cparams(("parallel",)),
        name="moe_combine",
    )(x, gate, y0, y1)


def _rope_tabs(pos, rot_dim, n_rows):
    half = rot_dim // 2
    inv = 1.0 / (ROPE_THETA ** (jnp.arange(0, rot_dim, 2, dtype=F32) / rot_dim))
    ang = jnp.asarray(pos).astype(F32)[:, None] * inv[None, :]
    c, s = jnp.cos(ang), jnp.sin(ang)
    z = jnp.zeros_like(c)
    pad = lambda a, fill: jnp.pad(a, ((0, n_rows - a.shape[0]), (0, LANES - a.shape[1])), constant_values=fill)
    return pad(jnp.concatenate([c, c], 1), 1.0), pad(jnp.concatenate([-s, z], 1), 0.0), \
        pad(jnp.concatenate([z, s], 1), 0.0)


def _mla_tabs(S):
    ct, s1, s2 = _rope_tabs(jnp.arange(S), QK_ROPE, S)
    lane = jnp.arange(LANES)[None, :]
    return jnp.where(lane < QK_ROPE, ct, 0.0), s1, s2


def _pad_cols(w, n):
    return jnp.pad(w, ((0, 0), (0, n - w.shape[1])))


def _layout_w_in(w):
    sp = np.cumsum([0, NSA_HEADS * HEAD_DIM, NSA_KV_COLS, 3 * NSA_HEADS, Q_LORA, KV_LORA, QK_ROPE, 2 * D_MODEL])
    q, kv, g, qa, kva, kr, m = [w[:, sp[i]:sp[i + 1]] for i in range(7)]
    return jnp.concatenate([qa, _pad_cols(g, LANES), _pad_cols(kr, LANES), q, kva, kv, m], axis=1).astype(BF16)


def _layout_w_q_b(w):
    w = w.reshape(Q_LORA, MLA_HEADS, QK_NOPE + QK_ROPE)
    w = jnp.pad(w, ((0, 0), (0, 0), (0, MLA_QK_PAD - QK_NOPE - QK_ROPE)))
    return w.reshape(Q_LORA, MLA_HEADS * MLA_QK_PAD).astype(BF16)


def _layout_w_kv_b(w):
    w = w.reshape(KV_LORA, MLA_HEADS, QK_NOPE + V_HEAD)
    return jnp.concatenate([w[:, :, :QK_NOPE].reshape(KV_LORA, -1), w[:, :, QK_NOPE:].reshape(KV_LORA, -1)],
                           axis=1).astype(BF16)


def _layout_w_cmp(w):
    h = CMP_LEN // 2
    return jnp.concatenate([w[:h].reshape(h * HEAD_DIM, HEAD_DIM), w[h:].reshape(h * HEAD_DIM, HEAD_DIM)],
                           axis=1).astype(BF16)


def _layout_pe(pe):
    return jnp.pad(pe.reshape(2, (CMP_LEN // 2) * HEAD_DIM), ((0, 6), (0, 0))).astype(BF16)


def _overlap(nc, ns):
    n = np.arange(nc)[None, :] * CMP_STRIDE
    j = np.arange(ns)[:, None] * SLC_LEN
    ov = (n <= j + SLC_LEN - 1) & (j <= n + CMP_LEN - 1) & (np.arange(nc)[None, :] < nc - 1)
    return jnp.asarray(ov.astype(np.float32), BF16)


def _expand(ns, S, tk):
    e = ((np.arange(S // tk)[:, None, None] * tk + np.arange(tk)[None, :, None]) // SLC_LEN
         == np.arange(ns)[None, None, :])
    return jnp.asarray(e.astype(np.float32), BF16)


def _layout_layers(p):
    depth = p['w_in'].shape[0]
    lane_row = lambda g: jnp.pad(g, ((0, 0), (0, LANES - g.shape[-1])))[:, None, :]
    z4 = jnp.zeros((depth, 4, HEAD_DIM), F32)
    z6 = jnp.zeros((depth, 6, LANES), F32)
    return dict(
        gn=jnp.concatenate([p['nsa_q_norm_g'][:, None, :], p['nsa_k_norm_g'], z4], axis=1),
        w_ck=jax.vmap(_layout_w_cmp)(p['w_cmp_k']), w_cv=jax.vmap(_layout_w_cmp)(p['w_cmp_v']),
        pe_k=jax.vmap(_layout_pe)(p['cmp_pe_k']), pe_v=jax.vmap(_layout_pe)(p['cmp_pe_v']),
        gh_q=jnp.concatenate([p['mla_q_norm_g'][:, None, :], lane_row(p['mla_qr_norm_g']), z6], axis=1),
        gh_k=jnp.concatenate([p['mla_k_norm_g'][:, None, :], lane_row(p['mla_kr_norm_g']), z6], axis=1),
        w_qb=jax.vmap(_layout_w_q_b)(p['w_q_b']), w_kvb=jax.vmap(_layout_w_kv_b)(p['w_kv_b']))


def _attention_block(x2d, B, S, p, pw, l, tabs):
    T = x2d.shape[0]
    z = norm_matmul(x2d, p['attn_norm_g'][l], _layout_w_in(p['w_in'][l]), tm=min(1024, T), tn=1024)
    qn, kvp, xc, vts, vtw, gates = nsa_prep(z, tabs['tok'], pw['gn'][l], S)

    nc = S // CMP_STRIDE
    x2 = xc.reshape(B, 2 * NSA_KV_GROUPS, nc, CMP_STRIDE * HEAD_DIM)
    kct, vc = nsa_compress(x2, pw['w_ck'][l], pw['w_cv'][l], pw['pe_k'][l], pw['pe_v'][l],
                           p['nsa_k_norm_g'][l][0:1], tabs['cmp'])
    ns = S // SLC_LEN
    oc, sel_t = nsa_cmp_attention(qn, kct, vc, _overlap(nc, ns), B, S, tq=min(512, S))
    nsa = dict(n_groups=NSA_KV_GROUPS, hg=NSA_HPG, dqk=HEAD_DIM, dv=HEAD_DIM)
    tks = min(512, S)
    os_ = flash_attention(qn, kvp, vts, B, S, mode="sel", kcol0=0, tq=256, tk=tks,
                          sel_t=sel_t, expand=_expand(ns, S, tks), **nsa)
    oa = window_attention(qn, kvp, vtw, oc, os_, gates, B, S, kcol0=NSA_KV_GROUPS, tq=VT_CHUNK, **nsa)

    qm = mla_q_proj(z, p['mla_qa_norm_g'][l][None], pw['w_qb'][l], pw['gh_q'][l], tabs['mla'], S, tm=512)
    km, vmt = mla_kv_proj(z, p['mla_kva_norm_g'][l][None], pw['w_kvb'][l], pw['gh_k'][l], tabs['mla'], S)
    tkm = min(512, S)
    ob = flash_attention(qm, km, vmt, B, S, mode="causal",
                         n_groups=MLA_HEADS, hg=1, dqk=MLA_QK_PAD, dv=V_HEAD, kcol0=0, tq=min(1024, S), tk=tkm)

    tm = min(2048, T)
    mix = gated_mix(oa, ob, p['w_proj_nsa'][l], p['w_proj_mla'][l], z, tm=tm, tn=512)
    return matmul_residual(mix, p['w_out'][l], x2d, tm=min(2048, T), tn=512)


def _dense_ffn(x2d, g, w1, w3, w2):
    T = x2d.shape[0]
    act = ffn_up(x2d, g, w1.astype(BF16), w3.astype(BF16), tm=min(1024, T), tn=1024)
    return matmul_residual(act, w2.astype(BF16), x2d, tm=min(1024, T), tn=256)


def _cumsum_rows(oh, blk=128):
    A, E = oh.shape
    nb = A // blk
    x = oh.astype(F32).reshape(nb, blk, E)
    within = jnp.einsum('ij,bje->bie', jnp.tril(jnp.ones((blk, blk), F32)), x)
    before = jnp.tril(jnp.ones((nb, nb), F32), -1) @ within[:, -1, :]
    return (within + before[:, None, :]).astype(jnp.int32).reshape(A, E)


def _moe_ffn(x2d, g, w_router, w1, w3, w2, *, tmb=512):
    T = x2d.shape[0]
    hp, idx, gate = router(x2d, g, _pad_cols(w_router, LANES), tm=512)
    A = T * TOP_K
    e_flat = idx[:, :TOP_K].reshape(A)
    tok_flat = jnp.repeat(jnp.arange(T, dtype=jnp.int32), TOP_K)
    oh = (e_flat[:, None] == jnp.arange(N_EXPERTS)[None, :]).astype(jnp.int32)
    csum = _cumsum_rows(oh)
    rank = jnp.sum(oh * csum, axis=1) - 1
    counts = csum[-1]
    padded = (counts + tmb - 1) // tmb * tmb
    pad_end = jnp.cumsum(padded)
    dest = (pad_end - padded)[e_flat] + rank
    n_blk = -(-(-(-A // tmb) + N_EXPERTS) // MOE_SUB) * MOE_SUB
    n_slot = n_blk * tmb
    slot_tok = jnp.full((n_slot,), T, jnp.int32).at[dest].set(tok_flat)
    eidx = jnp.arange(N_EXPERTS, dtype=jnp.int32)
    blk = jnp.arange(n_blk, dtype=jnp.int32)
    blk_exp = jnp.minimum(jnp.sum((pad_end[None, :] <= (blk * tmb)[:, None]).astype(jnp.int32), axis=1),
                          N_EXPERTS - 1).astype(jnp.int32)
    n_used = (pad_end[-1:] // tmb).astype(jnp.int32)
    present = counts > 0
    first_e = jnp.min(jnp.where(present, eidx, N_EXPERTS))
    last_e = jnp.max(jnp.where(present, eidx, -1))
    later = jnp.where(present[None, :] & (eidx[None, :] > eidx[:, None]), eidx[None, :], N_EXPERTS)
    next_e = jnp.min(later, axis=1)
    next_e = jnp.where(next_e == N_EXPERTS, first_e, next_e).astype(jnp.int32)
    starts = ((blk == 0) | (blk_exp != jnp.roll(blk_exp, 1))) & (blk < n_used[0])
    tables = (blk_exp, n_used, starts.astype(jnp.int32), next_e[blk_exp],
              (blk_exp == last_e).astype(jnp.int32))
    xb = hp[slot_tok]
    act = moe_up(tables, xb, w1, w3, tmb=tmb, tn=1024)
    tnd = 512
    yb = moe_down(tables, act, w2, tmb=tmb, tn=tnd)
    d2 = dest.reshape(T, TOP_K)
    return moe_combine(x2d, gate, yb[d2[:, 0]], yb[d2[:, 1]], tn=tnd, tm=512)


def kernel(x, attn_norm_g, w_in, nsa_q_norm_g, nsa_k_norm_g, cmp_pe_k, cmp_pe_v, w_cmp_k, w_cmp_v, mla_qa_norm_g, w_q_b, mla_kva_norm_g, w_kv_b, mla_q_norm_g, mla_qr_norm_g, mla_k_norm_g, mla_kr_norm_g, w_proj_nsa, w_proj_mla, w_out, ffn_norm_g, w_ff1, w_ff3, w_ff2, w_router, w_e1, w_e3, w_e2):
    p = dict(attn_norm_g=attn_norm_g, w_in=w_in, nsa_q_norm_g=nsa_q_norm_g, nsa_k_norm_g=nsa_k_norm_g,
             cmp_pe_k=cmp_pe_k, cmp_pe_v=cmp_pe_v, w_cmp_k=w_cmp_k, w_cmp_v=w_cmp_v,
             mla_qa_norm_g=mla_qa_norm_g, w_q_b=w_q_b, mla_kva_norm_g=mla_kva_norm_g, w_kv_b=w_kv_b,
             mla_q_norm_g=mla_q_norm_g, mla_qr_norm_g=mla_qr_norm_g, mla_k_norm_g=mla_k_norm_g,
             mla_kr_norm_g=mla_kr_norm_g, w_proj_nsa=w_proj_nsa, w_proj_mla=w_proj_mla, w_out=w_out)
    B, S, D = x.shape
    depth = w_in.shape[0]
    nc = S // CMP_STRIDE
    tabs = dict(tok=_rope_tabs(jnp.arange(S), NSA_ROT, S),
                cmp=_rope_tabs(jnp.arange(nc - 1) * CMP_STRIDE + CMP_LEN - 1, NSA_ROT, nc),
                mla=_mla_tabs(S))
    pw = _layout_layers(p)
    x2d = x.reshape(B * S, D)
    for l in range(depth):
        x2d = _attention_block(x2d, B, S, p, pw, l, tabs)
        if l % 2 == 0:
            x2d = _dense_ffn(x2d, ffn_norm_g[l], w_ff1[l // 2], w_ff3[l // 2], w_ff2[l // 2])
        else:
            x2d = _moe_ffn(x2d, ffn_norm_g[l], w_router[l // 2], w_e1[l // 2], w_e3[l // 2], w_e2[l // 2])
    return x2d.reshape(B, S, D)
```

```python
import functools

import numpy as np
import jax
import jax.numpy as jnp
from jax import lax
from jax.experimental import pallas as pl
from jax.experimental.pallas import tpu as pltpu

D_MODEL = 2048
HEAD_DIM = 128
NSA_HEADS = 8
NSA_KV_GROUPS = 2
NSA_HPG = NSA_HEADS // NSA_KV_GROUPS
NSA_ROT = HEAD_DIM // 4
CMP_LEN = 32
CMP_STRIDE = 16
SLC_LEN = 64
SLC_TOPK = 16
N_LOCAL_SLC = 2
WINDOW = 512
FORCE_SCORE = 1.0e4
MLA_HEADS = 8
Q_LORA = 768
KV_LORA = 512
QK_NOPE = 128
QK_ROPE = 64
V_HEAD = 128
ROPE_THETA = 500000.0
EPS = 1e-6
D_FF = 7168
N_EXPERTS = 8
TOP_K = 2

LANES = 128
MLA_QK_PAD = 256
NEG_INF = float("-inf")
LOG2E = 1.4426950408889634
BF16 = jnp.bfloat16
F32 = jnp.float32

OFF_QA = 0
OFF_G = 768
OFF_KR = 896
OFF_Q = 1024
OFF_KVA = 2048
OFF_KV = 2560
OFF_M = 4096
D_INP = 8192
NSA_KV_COLS = 3 * 2 * NSA_KV_GROUPS * HEAD_DIM
NSA_KVP_COLS = 2 * NSA_KV_GROUPS * HEAD_DIM
VT_CHUNK = 256

VMEM_LIMIT = 56 * 1024 * 1024


def _cparams(sem):
    return pltpu.CompilerParams(dimension_semantics=sem, vmem_limit_bytes=VMEM_LIMIT)


def _rms(x, g):
    ms = jnp.mean(x * x, axis=-1, keepdims=True)
    return x * lax.rsqrt(ms + EPS) * g


def _rope_lanes(y, ct, s1, s2, half):
    return y * ct + pltpu.roll(y, LANES - half, 1) * s1 + pltpu.roll(y, half, 1) * s2


def _sigmoid(x):
    return 1.0 / (1.0 + jnp.exp(-x))


def _nmm_kernel(x_ref, g_ref, w_ref, o_ref, h_scr):
    @pl.when(pl.program_id(1) == 0)
    def _():
        h_scr[...] = _rms(x_ref[...], g_ref[...]).astype(BF16)

    o_ref[...] = jnp.dot(h_scr[...], w_ref[...].astype(BF16),
                         preferred_element_type=F32).astype(o_ref.dtype)


def norm_matmul(x, g, w, *, tm, tn, out_dtype=F32):
    T, K = x.shape
    N = w.shape[1]
    return pl.pallas_call(
        _nmm_kernel,
        grid=(T // tm, N // tn),
        in_specs=[pl.BlockSpec((tm, K), lambda m, n: (m, 0)),
                  pl.BlockSpec((1, K), lambda m, n: (0, 0)),
                  pl.BlockSpec((K, tn), lambda m, n: (0, n))],
        out_specs=pl.BlockSpec((tm, tn), lambda m, n: (m, n)),
        out_shape=jax.ShapeDtypeStruct((T, N), out_dtype),
        scratch_shapes=[pltpu.VMEM((tm, K), BF16)],
        compiler_params=_cparams(("parallel", "arbitrary")),
        name="norm_matmul",
    )(x, g.reshape(1, K), w)


def _mmres_kernel(a_ref, w_ref, r_ref, o_ref):
    o_ref[...] = r_ref[...] + jnp.dot(a_ref[...], w_ref[...].astype(BF16), preferred_element_type=F32)


def matmul_residual(a, w, res, *, tm, tn):
    T, K = a.shape
    N = w.shape[1]
    return pl.pallas_call(
        _mmres_kernel,
        grid=(T // tm, N // tn),
        in_specs=[pl.BlockSpec((tm, K), lambda m, n: (m, 0)),
                  pl.BlockSpec((K, tn), lambda m, n: (0, n)),
                  pl.BlockSpec((tm, tn), lambda m, n: (m, n))],
        out_specs=pl.BlockSpec((tm, tn), lambda m, n: (m, n)),
        out_shape=jax.ShapeDtypeStruct((T, N), F32),
        compiler_params=_cparams(("parallel", "arbitrary")),
        name="matmul_residual",
    )(a, w, res)


def _prep_kernel(zg_ref, zq_ref, zc_ref, zs_ref, zw_ref, ct_ref, s1_ref, s2_ref, gn_ref,
                 qn_ref, kvp_ref, xc_ref, vts_ref, vtw_ref, gate_ref):
    ct, s1, s2 = ct_ref[...], s1_ref[...], s2_ref[...]
    half = NSA_ROT // 2
    gq = gn_ref[0:1, :]
    for h in range(NSA_HEADS):
        y = _rope_lanes(_rms(zq_ref[:, h * HEAD_DIM:(h + 1) * HEAD_DIM], gq), ct, s1, s2, half)
        qn_ref[:, h * HEAD_DIM:(h + 1) * HEAD_DIM] = (y * (LOG2E * HEAD_DIM ** -0.5)).astype(BF16)
    for br, zb_ref in enumerate((zc_ref, zs_ref, zw_ref)):
        for kv in range(2):
            for g in range(NSA_KV_GROUPS):
                c = (kv * NSA_KV_GROUPS + g) * HEAD_DIM
                y = zb_ref[:, c:c + HEAD_DIM]
                if kv == 0 and br > 0:
                    y = _rope_lanes(_rms(y, gn_ref[1 + br:2 + br, :]), ct, s1, s2, half)
                if br == 0:
                    xc_ref[kv * NSA_KV_GROUPS + g] = y.astype(BF16)
                elif kv == 0:
                    d = (br - 1) * NSA_KV_GROUPS * HEAD_DIM + g * HEAD_DIM
                    kvp_ref[:, d:d + HEAD_DIM] = y.astype(BF16)
                else:
                    (vts_ref if br == 1 else vtw_ref)[g] = y.T.astype(BF16)
    gate_ref[...] = _sigmoid(zg_ref[...])


def nsa_prep(z, tabs, gn, S, *, tm=VT_CHUNK):
    T = z.shape[0]
    ns = S // tm
    tab_spec = pl.BlockSpec((tm, LANES), lambda i: (i % ns, 0))
    vt_spec = pl.BlockSpec((None, NSA_KV_GROUPS, None, HEAD_DIM, tm), lambda i: (i // ns, 0, i % ns, 0, 0))
    vt_shape = jax.ShapeDtypeStruct((T // S, NSA_KV_GROUPS, ns, HEAD_DIM, tm), BF16)
    wq, wb = NSA_HEADS * HEAD_DIM, 2 * NSA_KV_GROUPS * HEAD_DIM
    npl = 2 * NSA_KV_GROUPS
    return pl.pallas_call(
        _prep_kernel,
        grid=(T // tm,),
        in_specs=[pl.BlockSpec((tm, LANES), lambda i: (i, OFF_G // LANES)),
                  pl.BlockSpec((tm, wq), lambda i: (i, OFF_Q // wq)),
                  pl.BlockSpec((tm, wb), lambda i: (i, OFF_KV // wb)),
                  pl.BlockSpec((tm, wb), lambda i: (i, OFF_KV // wb + 1)),
                  pl.BlockSpec((tm, wb), lambda i: (i, OFF_KV // wb + 2)),
                  tab_spec, tab_spec, tab_spec, pl.BlockSpec((8, LANES), lambda i: (0, 0))],
        out_specs=[pl.BlockSpec((tm, NSA_HEADS * HEAD_DIM), lambda i: (i, 0)),
                   pl.BlockSpec((tm, NSA_KVP_COLS), lambda i: (i, 0)),
                   pl.BlockSpec((None, npl, tm, HEAD_DIM), lambda i: (i // ns, 0, i % ns, 0)), vt_spec, vt_spec,
                   pl.BlockSpec((tm, LANES), lambda i: (i, 0))],
        out_shape=[jax.ShapeDtypeStruct((T, NSA_HEADS * HEAD_DIM), BF16),
                   jax.ShapeDtypeStruct((T, NSA_KVP_COLS), BF16),
                   jax.ShapeDtypeStruct((T // S, npl, S, HEAD_DIM), BF16), vt_shape, vt_shape,
                   jax.ShapeDtypeStruct((T, LANES), F32)],
        compiler_params=_cparams(("parallel",)),
        name="nsa_prep",
    )(z, z, z, z, z, *tabs, gn)


def _cmp_kernel(xk_ref, xv_ref, wk_ref, wv_ref, pek_ref, pev_ref, gk_ref, ct_ref, s1_ref, s2_ref,
                kct_ref, vc_ref):
    nc = xk_ref.shape[0]
    row = lax.broadcasted_iota(jnp.int32, (nc, HEAD_DIM), 0)

    def comp(x_ref, w_ref, pe_ref):
        w = w_ref[...]
        y = jnp.dot(x_ref[...], w, preferred_element_type=F32)
        ype = jnp.dot(pe_ref[...], w, preferred_element_type=F32)
        bias = ype[0:1, :HEAD_DIM] + ype[1:2, HEAD_DIM:]
        out = y[:, :HEAD_DIM] + pltpu.roll(y[:, HEAD_DIM:], nc - 1, 0) + bias
        return jnp.where(row < nc - 1, out, 0.0)

    k = _rms(comp(xk_ref, wk_ref, pek_ref), gk_ref[...])
    k = _rope_lanes(k, ct_ref[...], s1_ref[...], s2_ref[...], NSA_ROT // 2)
    kct_ref[...] = k.T.astype(BF16)
    vc_ref[...] = comp(xv_ref, wv_ref, pev_ref).astype(BF16)


def nsa_compress(x2, wk2, wv2, pek2, pev2, gk, tabs_cmp):
    B, _, nc, kk = x2.shape
    G = NSA_KV_GROUPS
    full = lambda shape: pl.BlockSpec(shape, lambda b, g: (0,) * len(shape))
    return pl.pallas_call(
        _cmp_kernel,
        grid=(B, G),
        in_specs=[pl.BlockSpec((None, None, nc, kk), lambda b, g: (b, g, 0, 0)),
                  pl.BlockSpec((None, None, nc, kk), lambda b, g: (b, G + g, 0, 0)),
                  full((kk, 2 * HEAD_DIM)), full((kk, 2 * HEAD_DIM)),
                  full((8, kk)), full((8, kk)), full((1, HEAD_DIM)),
                  full((nc, LANES)), full((nc, LANES)), full((nc, LANES))],
        out_specs=[pl.BlockSpec((None, None, HEAD_DIM, nc), lambda b, g: (b, g, 0, 0)),
                   pl.BlockSpec((None, None, nc, HEAD_DIM), lambda b, g: (b, g, 0, 0))],
        out_shape=[jax.ShapeDtypeStruct((B, G, HEAD_DIM, nc), BF16),
                   jax.ShapeDtypeStruct((B, G, nc, HEAD_DIM), BF16)],
        compiler_params=_cparams(("parallel", "parallel")),
        name="nsa_compress",
    )(x2, x2, wk2, wv2, pek2, pev2, gk, *tabs_cmp)


def _cattn_kernel(q_ref, kct_ref, vc_ref, ov_ref, oc_ref, sel_ref, *, tq):
    nc = vc_ref.shape[0]
    ns = sel_ref.shape[0]
    t0 = pl.program_id(2) * tq
    t_pos = t0 + lax.broadcasted_iota(jnp.int32, (tq, nc), 0)
    n_idx = lax.broadcasted_iota(jnp.int32, (tq, nc), 1)
    vis = (t_pos >= n_idx * CMP_STRIDE + (CMP_LEN - 1)) & (n_idx < nc - 1)
    bias = jnp.where(vis, 0.0, NEG_INF)
    kct = kct_ref[...]
    vc = vc_ref[...]
    psum = jnp.zeros((tq, nc), F32)
    for hh in range(NSA_HPG):
        s = jnp.dot(q_ref[:, hh * HEAD_DIM:(hh + 1) * HEAD_DIM], kct, preferred_element_type=F32) + bias
        m = jnp.max(s, axis=-1, keepdims=True)
        m = jnp.where(m == NEG_INF, 0.0, m)
        p = jnp.exp2(s - m)
        den = jnp.sum(p, axis=-1, keepdims=True)
        p = p * (1.0 / jnp.where(den > 0.0, den, 1.0))
        oc_ref[:, hh * HEAD_DIM:(hh + 1) * HEAD_DIM] = jnp.dot(
            p.astype(BF16), vc, preferred_element_type=F32).astype(oc_ref.dtype)
        psum = psum + p
    pt = psum.T
    hi = pt.astype(BF16)
    lo = (pt - hi.astype(F32)).astype(BF16)
    ov = ov_ref[...]
    imp = jnp.dot(ov, hi, preferred_element_type=F32) + jnp.dot(ov, lo, preferred_element_type=F32)
    blk = lax.broadcasted_iota(jnp.int32, (ns, tq), 0)
    cur = (t0 + lax.broadcasted_iota(jnp.int32, (ns, tq), 1)) // SLC_LEN
    forced = (blk == 0) | ((blk <= cur) & (blk > cur - N_LOCAL_SLC))
    val = jnp.where(blk > cur, NEG_INF, jnp.where(forced, FORCE_SCORE, imp))
    rank = jnp.zeros((ns, tq), F32)
    for i in range(ns):
        other = val[i:i + 1, :]
        ahead = (other > val) | ((other == val) & (blk > i))
        rank = rank + jnp.where(ahead, 1.0, 0.0)
    sel_ref[...] = jnp.where(rank < float(min(SLC_TOPK, ns)), 1.0, 0.0).astype(sel_ref.dtype)


def nsa_cmp_attention(qn, kct, vc, ov, B, S, *, tq):
    G = NSA_KV_GROUPS
    nq = S // tq
    nc = vc.shape[2]
    ns = S // SLC_LEN
    gw = NSA_HPG * HEAD_DIM
    return pl.pallas_call(
        functools.partial(_cattn_kernel, tq=tq),
        grid=(B, G, nq),
        in_specs=[pl.BlockSpec((tq, gw), lambda b, g, i: (b * nq + i, g)),
                  pl.BlockSpec((None, None, HEAD_DIM, nc), lambda b, g, i: (b, g, 0, 0)),
                  pl.BlockSpec((None, None, nc, HEAD_DIM), lambda b, g, i: (b, g, 0, 0)),
                  pl.BlockSpec((ns, nc), lambda b, g, i: (0, 0))],
        out_specs=[pl.BlockSpec((tq, gw), lambda b, g, i: (b * nq + i, g)),
                   pl.BlockSpec((None, None, ns, tq), lambda b, g, i: (b, g, 0, i))],
        out_shape=[jax.ShapeDtypeStruct((B * S, NSA_HEADS * HEAD_DIM), BF16),
                   jax.ShapeDtypeStruct((B, G, ns, S), BF16)],
        compiler_params=_cparams(("parallel", "parallel", "parallel")),
        name="nsa_cmp_attention",
    )(qn, kct, vc, ov)


def _flash_kernel(*refs, mode, hg, tq, tk, tv, dqk, dv):
    if mode == "sel":
        q_ref, k_ref, vt_ref, sel_ref, ex_ref, o_ref, qs, m_s, l_s, acc_s, sa, sb = refs
    else:
        q_ref, k_ref, vt_ref, o_ref, qs, m_s, l_s, acc_s, sa, sb = refs
    rows = hg * tq
    q0 = pl.program_id(2) * tq
    cd = q0 // tk
    for hh in range(hg):
        qs[hh * tq:(hh + 1) * tq, :] = q_ref[:, hh * dqk:(hh + 1) * dqk]

    def scores(c, kind):
        start = pl.multiple_of(c * tk, tk)
        s = lax.dot_general(k_ref[pl.ds(start, tk), :], qs[...], (((1,), (1,)), ((), ())),
                            preferred_element_type=F32)
        ok = None
        if kind == "diag":
            k_pos = start + lax.broadcasted_iota(jnp.int32, (tk, tq), 0)
            q_pos = q0 + lax.broadcasted_iota(jnp.int32, (tk, tq), 1)
            ok = k_pos <= q_pos
        if mode == "sel":
            chosen = jnp.dot(ex_ref[c], sel_ref[...], preferred_element_type=F32) > 0.5
            ok = chosen if ok is None else (ok & chosen)
        if ok is not None:
            bias = jnp.where(ok, 0.0, NEG_INF)
            s = s + (jnp.concatenate([bias] * hg, axis=1) if hg > 1 else bias)
        return s

    def update(c, s, carry):
        m, l, acc = carry
        m_new = jnp.maximum(m, jnp.max(s, axis=0, keepdims=True))
        p = jnp.exp2(s - m_new)
        alpha = jnp.exp2(m - m_new)
        l = alpha * l + jnp.sum(p, axis=0, keepdims=True)
        r = tk // tv
        vt = vt_ref[c] if r == 1 else jnp.concatenate([vt_ref[c * r + j] for j in range(r)], axis=1)
        acc = alpha * acc + jnp.dot(vt, p.astype(BF16), preferred_element_type=F32)
        return m_new, l, acc

    def load():
        return m_s[...], l_s[...], acc_s[...]

    def store(carry):
        m_s[...], l_s[...], acc_s[...] = carry

    nd = max(tq // tk, 1)
    last = k_ref.shape[0] // tk - 1
    carry = (jnp.full((1, rows), NEG_INF, F32), jnp.zeros((1, rows), F32), jnp.zeros((dv, rows), F32))
    s_next = scores(cd, "diag")
    for j in range(nd):
        s_cur = s_next
        if j + 1 < nd:
            s_next = scores(cd + j + 1, "diag")
        else:
            sa[...] = scores(0, "full")
        carry = update(cd + j, s_cur, carry)
    store(carry)

    def pairs(first, count):
        carry = load()
        for j in range(count):
            c = first + 2 * j
            sb[...] = scores(c + 1, "full")
            carry = update(c, sa[...], carry)
            sa[...] = scores(jnp.minimum(c + 2, last), "full")
            carry = update(c + 1, sb[...], carry)
        store(carry)

    def four(i, _):
        pairs(4 * i, 2)
        return 0

    lax.fori_loop(0, cd // 4, four, 0)

    @pl.when(cd % 4 >= 2)
    def _():
        pairs((cd // 4) * 4, 1)

    @pl.when(cd % 2 == 1)
    def _():
        store(update(cd - 1, sa[...], load()))

    o = acc_s[...] * (1.0 / l_s[...])
    for hh in range(hg):
        o_ref[:, hh * dv:(hh + 1) * dv] = o[:, hh * tq:(hh + 1) * tq].T.astype(o_ref.dtype)


def flash_attention(q, k, vt, B, S, *, mode, n_groups, hg, dqk, dv, kcol0, tq, tk, sel_t=None, expand=None):
    nq = S // tq
    nch = S // tk
    tv = vt.shape[-1]
    assert (tk % tq == 0 or tq % tk == 0) and S % tk == 0 and S % tq == 0 and tq % LANES == 0 and tk % tv == 0
    in_specs = [pl.BlockSpec((tq, hg * dqk), lambda b, g, i: (b * nq + i, g)),
                pl.BlockSpec((S, dqk), lambda b, g, i: (b, kcol0 + g)),
                pl.BlockSpec((None, None, S // tv, dv, tv), lambda b, g, i: (b, g, 0, 0, 0))]
    args = [q, k, vt]
    if mode == "sel":
        ns = S // SLC_LEN
        in_specs += [pl.BlockSpec((None, None, ns, tq), lambda b, g, i: (b, g, 0, i)),
                     pl.BlockSpec((nch, tk, ns), lambda b, g, i: (0, 0, 0))]
        args += [sel_t, expand]
    return pl.pallas_call(
        functools.partial(_flash_kernel, mode=mode, hg=hg, tq=tq, tk=tk, tv=tv, dqk=dqk, dv=dv),
        grid=(B, n_groups, nq),
        in_specs=in_specs,
        out_specs=pl.BlockSpec((tq, hg * dv), lambda b, g, i: (b * nq + i, g)),
        out_shape=jax.ShapeDtypeStruct((B * S, n_groups * hg * dv), BF16),
        scratch_shapes=[pltpu.VMEM((hg * tq, dqk), BF16),
                        pltpu.VMEM((1, hg * tq), F32),
                        pltpu.VMEM((1, hg * tq), F32),
                        pltpu.VMEM((dv, hg * tq), F32),
                        pltpu.VMEM((tk, hg * tq), F32),
                        pltpu.VMEM((tk, hg * tq), F32)],
        compiler_params=_cparams(("parallel", "parallel", "arbitrary")),
        name="flash_" + mode,
    )(*args)


def _window_kernel(q_ref, k_ref, vt_ref, oc_ref, os_ref, gate_ref, o_ref, qs, *, hg, tq, dqk, dv):
    nk = WINDOW + tq
    q0 = pl.program_id(2) * tq
    start = pl.multiple_of(jnp.maximum(q0 - WINDOW, 0), tq)
    for hh in range(hg):
        qs[hh * tq:(hh + 1) * tq, :] = q_ref[:, hh * dqk:(hh + 1) * dqk]
    s = lax.dot_general(k_ref[pl.ds(start, nk), :], qs[...], (((1,), (1,)), ((), ())),
                        preferred_element_type=F32)
    k_pos = start + lax.broadcasted_iota(jnp.int32, (nk, tq), 0)
    q_pos = q0 + lax.broadcasted_iota(jnp.int32, (nk, tq), 1)
    bias = jnp.where((k_pos <= q_pos) & (q_pos - k_pos < WINDOW), 0.0, NEG_INF)
    s = s + jnp.concatenate([bias] * hg, axis=1)
    p = jnp.exp2(s - jnp.max(s, axis=0, keepdims=True))
    l = jnp.sum(p, axis=0, keepdims=True)
    pb = p.astype(BF16)
    c0 = start // tq
    o = jnp.zeros((dv, hg * tq), F32)
    for j in range(nk // tq):
        o = o + jnp.dot(vt_ref[c0 + j], pb[j * tq:(j + 1) * tq, :], preferred_element_type=F32)
    o = o * (1.0 / l)
    first_group = pl.program_id(1) == 0
    n_heads = 2 * hg

    def gate(br, hh):
        lo = gate_ref[:, br * n_heads + hh:br * n_heads + hh + 1]
        hi = gate_ref[:, br * n_heads + hg + hh:br * n_heads + hg + hh + 1]
        return jnp.where(first_group, lo, hi)

    for hh in range(hg):
        sl = slice(hh * dv, (hh + 1) * dv)
        mixed = (gate(0, hh) * oc_ref[:, sl].astype(F32) + gate(1, hh) * os_ref[:, sl].astype(F32)
                 + gate(2, hh) * o[:, hh * tq:(hh + 1) * tq].T)
        o_ref[:, sl] = mixed.astype(o_ref.dtype)


def window_attention(q, k, vt, oc, os_, gates, B, S, *, n_groups, hg, dqk, dv, kcol0, tq):
    assert n_groups == NSA_KV_GROUPS
    nq = S // tq
    assert WINDOW % tq == 0 and S >= WINDOW + tq and tq % LANES == 0
    return pl.pallas_call(
        functools.partial(_window_kernel, hg=hg, tq=tq, dqk=dqk, dv=dv),
        grid=(B, n_groups, nq),
        in_specs=[pl.BlockSpec((tq, hg * dqk), lambda b, g, i: (b * nq + i, g)),
                  pl.BlockSpec((S, dqk), lambda b, g, i: (b, kcol0 + g)),
                  pl.BlockSpec((None, None, nq, dv, tq), lambda b, g, i: (b, g, 0, 0, 0)),
                  pl.BlockSpec((tq, hg * dv), lambda b, g, i: (b * nq + i, g)),
                  pl.BlockSpec((tq, hg * dv), lambda b, g, i: (b * nq + i, g)),
                  pl.BlockSpec((tq, LANES), lambda b, g, i: (b * nq + i, 0))],
        out_specs=pl.BlockSpec((tq, hg * dv), lambda b, g, i: (b * nq + i, g)),
        out_shape=jax.ShapeDtypeStruct((B * S, n_groups * hg * dv), BF16),
        scratch_shapes=[pltpu.VMEM((hg * tq, dqk), BF16)],
        compiler_params=_cparams(("parallel", "parallel", "arbitrary")),
        name="window_attention",
    )(q, k, vt, oc, os_, gates)


def _mla_q_kernel(z_ref, ga_ref, w_ref, gh_ref, ct_ref, s1_ref, s2_ref, o_ref):
    h = _rms(z_ref[...], ga_ref[...]).astype(BF16)
    y = jnp.dot(h, w_ref[...], preferred_element_type=F32)
    ct, s1, s2 = ct_ref[...], s1_ref[...], s2_ref[...]
    scale = LOG2E * (QK_NOPE + QK_ROPE) ** -0.5
    for hd in range(MLA_HEADS):
        c = hd * MLA_QK_PAD
        nope = _rms(y[:, c:c + QK_NOPE], gh_ref[0:1, :])
        r = y[:, c + QK_NOPE:c + MLA_QK_PAD]
        ms = jnp.sum(r * r, axis=-1, keepdims=True) * (1.0 / QK_ROPE)
        r = _rope_lanes(r * lax.rsqrt(ms + EPS) * gh_ref[1:2, :], ct, s1, s2, QK_ROPE // 2)
        o_ref[:, c:c + QK_NOPE] = (nope * scale).astype(BF16)
        o_ref[:, c + QK_NOPE:c + MLA_QK_PAD] = (r * scale).astype(BF16)


def mla_q_proj(z, ga, wq, gh, tabs, S, *, tm):
    T = z.shape[0]
    ns = S // tm
    tab_spec = pl.BlockSpec((tm, LANES), lambda i: (i % ns, 0))
    nout = MLA_HEADS * MLA_QK_PAD
    return pl.pallas_call(
        _mla_q_kernel,
        grid=(T // tm,),
        in_specs=[pl.BlockSpec((tm, Q_LORA), lambda i: (i, OFF_QA // Q_LORA)),
                  pl.BlockSpec((1, Q_LORA), lambda i: (0, 0)),
                  pl.BlockSpec((Q_LORA, nout), lambda i: (0, 0)),
                  pl.BlockSpec((8, LANES), lambda i: (0, 0)),
                  tab_spec, tab_spec, tab_spec],
        out_specs=pl.BlockSpec((tm, nout), lambda i: (i, 0)),
        out_shape=jax.ShapeDtypeStruct((T, nout), BF16),
        compiler_params=_cparams(("parallel",)),
        name="mla_q_proj",
    )(z, ga, wq, gh, *tabs)


def _mla_kv_kernel(z_ref, zr_ref, ga_ref, w_ref, gh_ref, ct_ref, s1_ref, s2_ref, k_ref, v_ref):
    h = _rms(z_ref[...], ga_ref[...]).astype(BF16)
    y = jnp.dot(h, w_ref[...], preferred_element_type=F32)
    r = zr_ref[...]
    ms = jnp.sum(r * r, axis=-1, keepdims=True) * (1.0 / QK_ROPE)
    r = _rope_lanes(r * lax.rsqrt(ms + EPS) * gh_ref[1:2, :], ct_ref[...], s1_ref[...], s2_ref[...],
                    QK_ROPE // 2).astype(BF16)
    nv = MLA_HEADS * QK_NOPE
    for hd in range(MLA_HEADS):
        c = hd * MLA_QK_PAD
        k_ref[:, c:c + QK_NOPE] = _rms(y[:, hd * QK_NOPE:(hd + 1) * QK_NOPE], gh_ref[0:1, :]).astype(BF16)
        k_ref[:, c + QK_NOPE:c + MLA_QK_PAD] = r
    for hd in range(MLA_HEADS):
        v_ref[hd] = y[:, nv + hd * V_HEAD:nv + (hd + 1) * V_HEAD].T.astype(BF16)


def mla_kv_proj(z, ga, wkv, gh, tabs, S, *, tm=VT_CHUNK):
    T = z.shape[0]
    ns = S // tm
    tab_spec = pl.BlockSpec((tm, LANES), lambda i: (i % ns, 0))
    nk = MLA_HEADS * MLA_QK_PAD
    nv = MLA_HEADS * V_HEAD
    return pl.pallas_call(
        _mla_kv_kernel,
        grid=(T // tm,),
        in_specs=[pl.BlockSpec((tm, KV_LORA), lambda i: (i, OFF_KVA // KV_LORA)),
                  pl.BlockSpec((tm, LANES), lambda i: (i, OFF_KR // LANES)),
                  pl.BlockSpec((1, KV_LORA), lambda i: (0, 0)),
                  pl.BlockSpec((KV_LORA, MLA_HEADS * (QK_NOPE + V_HEAD)), lambda i: (0, 0)),
                  pl.BlockSpec((8, LANES), lambda i: (0, 0)),
                  tab_spec, tab_spec, tab_spec],
        out_specs=[pl.BlockSpec((tm, nk), lambda i: (i, 0)),
                   pl.BlockSpec((None, MLA_HEADS, None, V_HEAD, tm), lambda i: (i // ns, 0, i % ns, 0, 0))],
        out_shape=[jax.ShapeDtypeStruct((T, nk), BF16),
                   jax.ShapeDtypeStruct((T // S, MLA_HEADS, ns, V_HEAD, tm), BF16)],
        compiler_params=_cparams(("parallel",)),
        name="mla_kv_proj",
    )(z, z, ga, wkv, gh, *tabs)


def _mix_kernel(a_ref, ob_ref, wa_ref, wb_ref, za_ref, zb_ref, o_ref):
    pa = jnp.dot(a_ref[...], wa_ref[...].astype(BF16), preferred_element_type=F32)
    pb = jnp.dot(ob_ref[...], wb_ref[...].astype(BF16), preferred_element_type=F32)
    o_ref[...] = (_sigmoid(za_ref[...]) * pa + _sigmoid(zb_ref[...]) * pb).astype(o_ref.dtype)


def gated_mix(oa, ob, wa, wb, z, *, tm, tn):
    T, K = oa.shape
    N = wa.shape[1]
    row = lambda w: pl.BlockSpec((tm, w), lambda m, n: (m, 0))
    return pl.pallas_call(
        _mix_kernel,
        grid=(T // tm, N // tn),
        in_specs=[row(K), row(K),
                  pl.BlockSpec((K, tn), lambda m, n: (0, n)),
                  pl.BlockSpec((K, tn), lambda m, n: (0, n)),
                  pl.BlockSpec((tm, tn), lambda m, n: (m, OFF_M // tn + n)),
                  pl.BlockSpec((tm, tn), lambda m, n: (m, (OFF_M + D_MODEL) // tn + n))],
        out_specs=pl.BlockSpec((tm, tn), lambda m, n: (m, n)),
        out_shape=jax.ShapeDtypeStruct((T, N), BF16),
        compiler_params=_cparams(("parallel", "parallel")),
        name="gated_mix",
    )(oa, ob, wa, wb, z, z)


def _ffn_up_kernel(x_ref, g_ref, w1_ref, w3_ref, o_ref, h_scr):
    @pl.when(pl.program_id(1) == 0)
    def _():
        h_scr[...] = _rms(x_ref[...], g_ref[...]).astype(BF16)

    h = h_scr[...]
    a = jnp.dot(h, w1_ref[...].astype(BF16), preferred_element_type=F32)
    b = jnp.dot(h, w3_ref[...].astype(BF16), preferred_element_type=F32)
    o_ref[...] = (a * _sigmoid(a) * b).astype(o_ref.dtype)


def ffn_up(x, g, w1, w3, *, tm, tn):
    T, K = x.shape
    N = w1.shape[1]
    return pl.pallas_call(
        _ffn_up_kernel,
        grid=(T // tm, N // tn),
        in_specs=[pl.BlockSpec((tm, K), lambda m, n: (m, 0)),
                  pl.BlockSpec((1, K), lambda m, n: (0, 0)),
                  pl.BlockSpec((K, tn), lambda m, n: (0, n)),
                  pl.BlockSpec((K, tn), lambda m, n: (0, n))],
        out_specs=pl.BlockSpec((tm, tn), lambda m, n: (m, n)),
        out_shape=jax.ShapeDtypeStruct((T, N), BF16),
        scratch_shapes=[pltpu.VMEM((tm, K), BF16)],
        compiler_params=_cparams(("parallel", "arbitrary")),
        name="ffn_up",
    )(x, g.reshape(1, K), w1, w3)


def _pack_bf16_pairs(h):
    k = h.shape[1] // 2
    hi = lax.bitcast_convert_type(h[:, :k].astype(jnp.bfloat16).astype(F32), jnp.uint32)
    lo = lax.bitcast_convert_type(h[:, k:].astype(jnp.bfloat16).astype(F32), jnp.uint32)
    return lax.bitcast_convert_type(hi | (lo >> 16), F32)


def _unpack_bf16_pairs(xp):
    xp = lax.bitcast_convert_type(xp, jnp.uint32)
    hi = lax.bitcast_convert_type(xp & jnp.uint32(0xFFFF0000), F32)
    lo = lax.bitcast_convert_type(xp << 16, F32)
    return hi.astype(BF16), lo.astype(BF16)


def _router_kernel(x_ref, g_ref, wr_ref, h_ref, idx_ref, gate_ref):
    h = _rms(x_ref[...], g_ref[...])
    is_pad = pl.program_id(0) == pl.num_programs(0) - 1
    h_ref[...] = jnp.where(is_pad, 0.0, _pack_bf16_pairs(h))
    logits = jnp.dot(h, wr_ref[...], preferred_element_type=F32, precision=lax.Precision.HIGHEST)
    lane = lax.broadcasted_iota(jnp.int32, logits.shape, 1).astype(F32)
    logits = jnp.where(lane < float(N_EXPERTS), logits, NEG_INF)
    m1 = jnp.max(logits, axis=-1, keepdims=True)
    i1 = jnp.min(jnp.where(logits == m1, lane, float(LANES)), axis=-1, keepdims=True)
    rest = jnp.where(lane == i1, NEG_INF, logits)
    m2 = jnp.max(rest, axis=-1, keepdims=True)
    i2 = jnp.min(jnp.where(rest == m2, lane, float(LANES)), axis=-1, keepdims=True)
    e = jnp.exp(m2 - m1)
    den = 1.0 + e
    idx_ref[...] = jnp.where(lane == 0.0, i1, jnp.where(lane == 1.0, i2, 0.0)).astype(jnp.int32)
    gate_ref[...] = jnp.where(lane == 0.0, 1.0 / den, jnp.where(lane == 1.0, e / den, 0.0))


def router(x, g, wr_pad, *, tm):
    T, K = x.shape
    nt = T // tm
    return pl.pallas_call(
        _router_kernel,
        grid=(nt + 1,),
        in_specs=[pl.BlockSpec((tm, K), lambda i: (jnp.minimum(i, nt - 1), 0)),
                  pl.BlockSpec((1, K), lambda i: (0, 0)),
                  pl.BlockSpec((K, LANES), lambda i: (0, 0))],
        out_specs=[pl.BlockSpec((tm, K // 2), lambda i: (i, 0)),
                   pl.BlockSpec((tm, LANES), lambda i: (jnp.minimum(i, nt - 1), 0)),
                   pl.BlockSpec((tm, LANES), lambda i: (jnp.minimum(i, nt - 1), 0))],
        out_shape=[jax.ShapeDtypeStruct((T + tm, K // 2), F32),
                   jax.ShapeDtypeStruct((T, LANES), jnp.int32),
                   jax.ShapeDtypeStruct((T, LANES), F32)],
        compiler_params=_cparams(("arbitrary",)),
        name="router",
    )(x, g.reshape(1, K), wr_pad)


def _moe_weight_stream(w_hbms, stages, casts, sems, be_ref, nu_ref, gs_ref, ne_ref, lg_ref, tn, r):
    n = pl.program_id(0)
    used = r < nu_ref[0]

    def copies(e, nt):
        c0 = pl.multiple_of(nt * tn, tn)
        return [pltpu.make_async_copy(w.at[e, :, pl.ds(c0, tn)], st, sems.at[i])
                for i, (w, st) in enumerate(zip(w_hbms, stages))]

    @pl.when((n == 0) & (r == 0))
    def _():
        for c in copies(be_ref[0], 0):
            c.start()

    @pl.when(used & (gs_ref[r] == 1))
    def _():
        for c in copies(be_ref[r], n):
            c.wait()
        for st, wb in zip(stages, casts):
            wb[...] = st[...].astype(BF16)
        last = lg_ref[r] == 1

        @pl.when(jnp.logical_not(last & (n == pl.num_programs(0) - 1)))
        def _():
            for c in copies(ne_ref[r], n + last.astype(jnp.int32)):
                c.start()

    return used


MOE_SUB = 2


def _moe_up_kernel(be_ref, nu_ref, gs_ref, ne_ref, lg_ref, x_ref, w1_hbm, w3_hbm, o_ref,
                   st1, st3, w1_s, w3_s, sems, *, tn, tmb):
    for sub in range(MOE_SUB):
        rows = slice(sub * tmb, (sub + 1) * tmb)
        used = _moe_weight_stream((w1_hbm, w3_hbm), (st1, st3), (w1_s, w3_s), sems,
                                  be_ref, nu_ref, gs_ref, ne_ref, lg_ref, tn, pl.program_id(1) * MOE_SUB + sub)

        @pl.when(used)
        def _():
            xa, xb = _unpack_bf16_pairs(x_ref[rows, :])
            k2 = xa.shape[1]
            a = (jnp.dot(xa, w1_s[:k2, :], preferred_element_type=F32)
                 + jnp.dot(xb, w1_s[k2:, :], preferred_element_type=F32))
            b = (jnp.dot(xa, w3_s[:k2, :], preferred_element_type=F32)
                 + jnp.dot(xb, w3_s[k2:, :], preferred_element_type=F32))
            o_ref[rows, :] = (a * _sigmoid(a) * b).astype(o_ref.dtype)

        @pl.when(jnp.logical_not(used))
        def _():
            o_ref[rows, :] = jnp.zeros((tmb, o_ref.shape[1]), o_ref.dtype)


def _moe_specs(tmb, kx, wout):
    step = lambda r, nu: jnp.minimum(r, (nu[0] - 1) // MOE_SUB)
    return (pl.BlockSpec((MOE_SUB * tmb, kx), lambda n, r, be, nu, gs, ne, lg: (step(r, nu), 0)),
            pl.BlockSpec((MOE_SUB * tmb, wout), lambda n, r, be, nu, gs, ne, lg: (r, n)))


def moe_up(tables, xb, w1, w3, *, tmb, tn):
    n_slot = xb.shape[0]
    K, N = w1.shape[1], w1.shape[2]
    x_spec, o_spec = _moe_specs(tmb, K // 2, tn)
    return pl.pallas_call(
        functools.partial(_moe_up_kernel, tn=tn, tmb=tmb),
        grid_spec=pltpu.PrefetchScalarGridSpec(
            num_scalar_prefetch=5,
            grid=(N // tn, n_slot // (MOE_SUB * tmb)),
            in_specs=[x_spec, pl.BlockSpec(memory_space=pl.ANY), pl.BlockSpec(memory_space=pl.ANY)],
            out_specs=o_spec,
            scratch_shapes=[pltpu.VMEM((K, tn), F32), pltpu.VMEM((K, tn), F32),
                            pltpu.VMEM((K, tn), BF16), pltpu.VMEM((K, tn), BF16),
                            pltpu.SemaphoreType.DMA((2,))]),
        out_shape=jax.ShapeDtypeStruct((n_slot, N), BF16),
        compiler_params=_cparams(("arbitrary", "arbitrary")),
        name="moe_up",
    )(*tables, xb, w1, w3)


def _moe_down_kernel(be_ref, nu_ref, gs_ref, ne_ref, lg_ref, a_ref, w2_hbm, o_ref, st2, w2_s, sems, *, tn, tmb):
    for sub in range(MOE_SUB):
        rows = slice(sub * tmb, (sub + 1) * tmb)
        used = _moe_weight_stream((w2_hbm,), (st2,), (w2_s,), sems, be_ref, nu_ref, gs_ref, ne_ref, lg_ref, tn,
                                  pl.program_id(1) * MOE_SUB + sub)

        @pl.when(used)
        def _():
            o_ref[rows, :] = _pack_bf16_pairs(jnp.dot(a_ref[rows, :], w2_s[...], preferred_element_type=F32))

        @pl.when(jnp.logical_not(used))
        def _():
            o_ref[rows, :] = jnp.zeros((tmb, o_ref.shape[1]), o_ref.dtype)


def moe_down(tables, act, w2, *, tmb, tn):
    n_slot, K = act.shape
    N = w2.shape[2]
    a_spec, o_spec = _moe_specs(tmb, K, tn // 2)
    return pl.pallas_call(
        functools.partial(_moe_down_kernel, tn=tn, tmb=tmb),
        grid_spec=pltpu.PrefetchScalarGridSpec(
            num_scalar_prefetch=5,
            grid=(N // tn, n_slot // (MOE_SUB * tmb)),
            in_specs=[a_spec, pl.BlockSpec(memory_space=pl.ANY)],
            out_specs=o_spec,
            scratch_shapes=[pltpu.VMEM((K, tn), F32), pltpu.VMEM((K, tn), BF16),
                            pltpu.SemaphoreType.DMA((1,))]),
        out_shape=jax.ShapeDtypeStruct((n_slot, N // 2), F32),
        compiler_params=_cparams(("arbitrary", "arbitrary")),
        name="moe_down",
    )(*tables, act, w2)


def _combine_kernel(x_ref, g_ref, y0_ref, y1_ref, o_ref, *, tn):
    g0, g1 = g_ref[:, 0:1], g_ref[:, 1:2]
    h = tn // 2

    def halves(y_ref, n):
        u = lax.bitcast_convert_type(y_ref[:, n * h:(n + 1) * h], jnp.uint32)
        return (lax.bitcast_convert_type(u & jnp.uint32(0xFFFF0000), F32),
                lax.bitcast_convert_type(u << 16, F32))

    for n in range(x_ref.shape[1] // tn):
        a_hi, a_lo = halves(y0_ref, n)
        b_hi, b_lo = halves(y1_ref, n)
        lo_cols = slice(n * tn, n * tn + h)
        hi_cols = slice(n * tn + h, (n + 1) * tn)
        o_ref[:, lo_cols] = x_ref[:, lo_cols] + g0 * a_hi + g1 * b_hi
        o_ref[:, hi_cols] = x_ref[:, hi_cols] + g0 * a_lo + g1 * b_lo


def moe_combine(x, gate, y0, y1, *, tn, tm):
    T, N = x.shape
    return pl.pallas_call(
        functools.partial(_combine_kernel, tn=tn),
        grid=(T // tm,),
        in_specs=[pl.BlockSpec((tm, N), lambda i: (i, 0)),
                  pl.BlockSpec((tm, LANES), lambda i: (i, 0)),
                  pl.BlockSpec((tm, N // 2), lambda i: (i, 0)),
                  pl.BlockSpec((tm, N // 2), lambda i: (i, 0))],
        out_specs=pl.BlockSpec((tm, N), lambda i: (i, 0)),
        out_shape=jax.ShapeDtypeStruct((T, N), F32),
        compiler_params=_cparams(("parallel",)),
        name="moe_combine",
    )(x, gate, y0, y1)


def _rope_tabs(pos, rot_dim, n_rows):
    half = rot_dim // 2
    inv = 1.0 / (ROPE_THETA ** (jnp.arange(0, rot_dim, 2, dtype=F32) / rot_dim))
    ang = jnp.asarray(pos).astype(F32)[:, None] * inv[None, :]
    c, s = jnp.cos(ang), jnp.sin(ang)
    z = jnp.zeros_like(c)
    pad = lambda a, fill: jnp.pad(a, ((0, n_rows - a.shape[0]), (0, LANES - a.shape[1])), constant_values=fill)
    return pad(jnp.concatenate([c, c], 1), 1.0), pad(jnp.concatenate([-s, z], 1), 0.0), \
        pad(jnp.concatenate([z, s], 1), 0.0)


def _mla_tabs(S):
    ct, s1, s2 = _rope_tabs(jnp.arange(S), QK_ROPE, S)
    lane = jnp.arange(LANES)[None, :]
    return jnp.where(lane < QK_ROPE, ct, 0.0), s1, s2


def _pad_cols(w, n):
    return jnp.pad(w, ((0, 0), (0, n - w.shape[1])))


def _layout_w_in(w):
    sp = np.cumsum([0, NSA_HEADS * HEAD_DIM, NSA_KV_COLS, 3 * NSA_HEADS, Q_LORA, KV_LORA, QK_ROPE, 2 * D_MODEL])
    q, kv, g, qa, kva, kr, m = [w[:, sp[i]:sp[i + 1]] for i in range(7)]
    return jnp.concatenate([qa, _pad_cols(g, LANES), _pad_cols(kr, LANES), q, kva, kv, m], axis=1).astype(BF16)


def _layout_w_q_b(w):
    w = w.reshape(Q_LORA, MLA_HEADS, QK_NOPE + QK_ROPE)
    w = jnp.pad(w, ((0, 0), (0, 0), (0, MLA_QK_PAD - QK_NOPE - QK_ROPE)))
    return w.reshape(Q_LORA, MLA_HEADS * MLA_QK_PAD).astype(BF16)


def _layout_w_kv_b(w):
    w = w.reshape(KV_LORA, MLA_HEADS, QK_NOPE + V_HEAD)
    return jnp.concatenate([w[:, :, :QK_NOPE].reshape(KV_LORA, -1), w[:, :, QK_NOPE:].reshape(KV_LORA, -1)],
                           axis=1).astype(BF16)


def _layout_w_cmp(w):
    h = CMP_LEN // 2
    return jnp.concatenate([w[:h].reshape(h * HEAD_DIM, HEAD_DIM), w[h:].reshape(h * HEAD_DIM, HEAD_DIM)],
                           axis=1).astype(BF16)


def _layout_pe(pe):
    return jnp.pad(pe.reshape(2, (CMP_LEN // 2) * HEAD_DIM), ((0, 6), (0, 0))).astype(BF16)


def _overlap(nc, ns):
    n = np.arange(nc)[None, :] * CMP_STRIDE
    j = np.arange(ns)[:, None] * SLC_LEN
    ov = (n <= j + SLC_LEN - 1) & (j <= n + CMP_LEN - 1) & (np.arange(nc)[None, :] < nc - 1)
    return jnp.asarray(ov.astype(np.float32), BF16)


def _expand(ns, S, tk):
    e = ((np.arange(S // tk)[:, None, None] * tk + np.arange(tk)[None, :, None]) // SLC_LEN
         == np.arange(ns)[None, None, :])
    return jnp.asarray(e.astype(np.float32), BF16)


def _layout_layers(p):
    depth = p['w_in'].shape[0]
    lane_row = lambda g: jnp.pad(g, ((0, 0), (0, LANES - g.shape[-1])))[:, None, :]
    z4 = jnp.zeros((depth, 4, HEAD_DIM), F32)
    z6 = jnp.zeros((depth, 6, LANES), F32)
    return dict(
        gn=jnp.concatenate([p['nsa_q_norm_g'][:, None, :], p['nsa_k_norm_g'], z4], axis=1),
        w_ck=jax.vmap(_layout_w_cmp)(p['w_cmp_k']), w_cv=jax.vmap(_layout_w_cmp)(p['w_cmp_v']),
        pe_k=jax.vmap(_layout_pe)(p['cmp_pe_k']), pe_v=jax.vmap(_layout_pe)(p['cmp_pe_v']),
        gh_q=jnp.concatenate([p['mla_q_norm_g'][:, None, :], lane_row(p['mla_qr_norm_g']), z6], axis=1),
        gh_k=jnp.concatenate([p['mla_k_norm_g'][:, None, :], lane_row(p['mla_kr_norm_g']), z6], axis=1),
        w_qb=jax.vmap(_layout_w_q_b)(p['w_q_b']), w_kvb=jax.vmap(_layout_w_kv_b)(p['w_kv_b']))


def _attention_block(x2d, B, S, p, pw, l, tabs):
    T = x2d.shape[0]
    z = norm_matmul(x2d, p['attn_norm_g'][l], _layout_w_in(p['w_in'][l]), tm=min(1024, T), tn=1024)
    qn, kvp, xc, vts, vtw, gates = nsa_prep(z, tabs['tok'], pw['gn'][l], S)

    nc = S // CMP_STRIDE
    x2 = xc.reshape(B, 2 * NSA_KV_GROUPS, nc, CMP_STRIDE * HEAD_DIM)
    kct, vc = nsa_compress(x2, pw['w_ck'][l], pw['w_cv'][l], pw['pe_k'][l], pw['pe_v'][l],
                           p['nsa_k_norm_g'][l][0:1], tabs['cmp'])
    ns = S // SLC_LEN
    oc, sel_t = nsa_cmp_attention(qn, kct, vc, _overlap(nc, ns), B, S, tq=min(512, S))
    nsa = dict(n_groups=NSA_KV_GROUPS, hg=NSA_HPG, dqk=HEAD_DIM, dv=HEAD_DIM)
    tks = min(512, S)
    os_ = flash_attention(qn, kvp, vts, B, S, mode="sel", kcol0=0, tq=256, tk=tks,
                          sel_t=sel_t, expand=_expand(ns, S, tks), **nsa)
    oa = window_attention(qn, kvp, vtw, oc, os_, gates, B, S, kcol0=NSA_KV_GROUPS, tq=VT_CHUNK, **nsa)

    qm = mla_q_proj(z, p['mla_qa_norm_g'][l][None], pw['w_qb'][l], pw['gh_q'][l], tabs['mla'], S, tm=256)
    km, vmt = mla_kv_proj(z, p['mla_kva_norm_g'][l][None], pw['w_kvb'][l], pw['gh_k'][l], tabs['mla'], S)
    tkm = min(512, S)
    ob = flash_attention(qm, km, vmt, B, S, mode="causal",
                         n_groups=MLA_HEADS, hg=1, dqk=MLA_QK_PAD, dv=V_HEAD, kcol0=0, tq=min(1024, S), tk=tkm)

    tm = min(2048, T)
    mix = gated_mix(oa, ob, p['w_proj_nsa'][l], p['w_proj_mla'][l], z, tm=tm, tn=512)
    return matmul_residual(mix, p['w_out'][l], x2d, tm=min(2048, T), tn=512)


def _dense_ffn(x2d, g, w1, w3, w2):
    T = x2d.shape[0]
    act = ffn_up(x2d, g, w1.astype(BF16), w3.astype(BF16), tm=min(1024, T), tn=1024)
    return matmul_residual(act, w2.astype(BF16), x2d, tm=min(1024, T), tn=256)


def _cumsum_rows(oh, blk=128):
    A, E = oh.shape
    nb = A // blk
    x = oh.astype(F32).reshape(nb, blk, E)
    within = jnp.einsum('ij,bje->bie', jnp.tril(jnp.ones((blk, blk), F32)), x)
    before = jnp.tril(jnp.ones((nb, nb), F32), -1) @ within[:, -1, :]
    return (within + before[:, None, :]).astype(jnp.int32).reshape(A, E)


def _moe_ffn(x2d, g, w_router, w1, w3, w2, *, tmb=512):
    T = x2d.shape[0]
    hp, idx, gate = router(x2d, g, _pad_cols(w_router, LANES), tm=512)
    A = T * TOP_K
    e_flat = idx[:, :TOP_K].reshape(A)
    tok_flat = jnp.repeat(jnp.arange(T, dtype=jnp.int32), TOP_K)
    oh = (e_flat[:, None] == jnp.arange(N_EXPERTS)[None, :]).astype(jnp.int32)
    csum = _cumsum_rows(oh)
    rank = jnp.sum(oh * csum, axis=1) - 1
    counts = csum[-1]
    padded = (counts + tmb - 1) // tmb * tmb
    pad_end = jnp.cumsum(padded)
    dest = (pad_end - padded)[e_flat] + rank
    n_blk = -(-(-(-A // tmb) + N_EXPERTS) // MOE_SUB) * MOE_SUB
    n_slot = n_blk * tmb
    slot_tok = jnp.full((n_slot,), T, jnp.int32).at[dest].set(tok_flat)
    eidx = jnp.arange(N_EXPERTS, dtype=jnp.int32)
    blk = jnp.arange(n_blk, dtype=jnp.int32)
    blk_exp = jnp.minimum(jnp.sum((pad_end[None, :] <= (blk * tmb)[:, None]).astype(jnp.int32), axis=1),
                          N_EXPERTS - 1).astype(jnp.int32)
    n_used = (pad_end[-1:] // tmb).astype(jnp.int32)
    present = counts > 0
    first_e = jnp.min(jnp.where(present, eidx, N_EXPERTS))
    last_e = jnp.max(jnp.where(present, eidx, -1))
    later = jnp.where(present[None, :] & (eidx[None, :] > eidx[:, None]), eidx[None, :], N_EXPERTS)
    next_e = jnp.min(later, axis=1)
    next_e = jnp.where(next_e == N_EXPERTS, first_e, next_e).astype(jnp.int32)
    starts = ((blk == 0) | (blk_exp != jnp.roll(blk_exp, 1))) & (blk < n_used[0])
    tables = (blk_exp, n_used, starts.astype(jnp.int32), next_e[blk_exp],
              (blk_exp == last_e).astype(jnp.int32))
    xb = hp[slot_tok]
    act = moe_up(tables, xb, w1, w3, tmb=tmb, tn=1024)
    tnd = 512
    yb = moe_down(tables, act, w2, tmb=tmb, tn=tnd)
    d2 = dest.reshape(T, TOP_K)
    return moe_combine(x2d, gate, yb[d2[:, 0]], yb[d2[:, 1]], tn=tnd, tm=512)


def kernel(x, attn_norm_g, w_in, nsa_q_norm_g, nsa_k_norm_g, cmp_pe_k, cmp_pe_v, w_cmp_k, w_cmp_v, mla_qa_norm_g, w_q_b, mla_kva_norm_g, w_kv_b, mla_q_norm_g, mla_qr_norm_g, mla_k_norm_g, mla_kr_norm_g, w_proj_nsa, w_proj_mla, w_out, ffn_norm_g, w_ff1, w_ff3, w_ff2, w_router, w_e1, w_e3, w_e2):
    p = dict(attn_norm_g=attn_norm_g, w_in=w_in, nsa_q_norm_g=nsa_q_norm_g, nsa_k_norm_g=nsa_k_norm_g,
             cmp_pe_k=cmp_pe_k, cmp_pe_v=cmp_pe_v, w_cmp_k=w_cmp_k, w_cmp_v=w_cmp_v,
             mla_qa_norm_g=mla_qa_norm_g, w_q_b=w_q_b, mla_kva_norm_g=mla_kva_norm_g, w_kv_b=w_kv_b,
             mla_q_norm_g=mla_q_norm_g, mla_qr_norm_g=mla_qr_norm_g, mla_k_norm_g=mla_k_norm_g,
             mla_kr_norm_g=mla_kr_norm_g, w_proj_nsa=w_proj_nsa, w_proj_mla=w_proj_mla, w_out=w_out)
    B, S, D = x.shape
    depth = w_in.shape[0]
    nc = S // CMP_STRIDE
    tabs = dict(tok=_rope_tabs(jnp.arange(S), NSA_ROT, S),
                cmp=_rope_tabs(jnp.arange(nc - 1) * CMP_STRIDE + CMP_LEN - 1, NSA_ROT, nc),
                mla=_mla_tabs(S))
    pw = _layout_layers(p)
    x2d = x.reshape(B * S, D)
    for l in range(depth):
        x2d = _attention_block(x2d, B, S, p, pw, l, tabs)
        if l % 2 == 0:
            x2d = _dense_ffn(x2d, ffn_norm_g[l], w_ff1[l // 2], w_ff3[l // 2], w_ff2[l // 2])
        else:
            x2d = _moe_ffn(x2d, ffn_norm_g[l], w_router[l // 2], w_e1[l // 2], w_e3[l // 2], w_e2[l // 2])
    return x2d.reshape(B, S, D)
```
